```python
import jax, jax.numpy as jnp
from jax import lax
import numpy as np

D_MODEL = 1024
BATCH = 8
SEQ = 8192
DEPTH = 4

N_META = 16
ATT_BLOCK = 128
ATT_HEADS = 16
ATT_HEAD_DIM = 64
DN_HEADS = 8
DN_HEAD_DIM = 128
DN_CONV = 4
DN_CHUNK = 64
FFN_DIM = 2816
FFN_CONV = 3
N_ATTN_LAYERS = (DEPTH + 1) // 2
N_DN_LAYERS = DEPTH // 2
ATT_HD = ATT_HEADS * ATT_HEAD_DIM
DN_HD = DN_HEADS * DN_HEAD_DIM
ATT_IN = 4 * ATT_HD + ATT_HEADS
DN_IN = 4 * DN_HD + 2 * DN_HEADS
EPS = 1e-6
NEG = -1e30

kernel_name = "fox_gdn_hybrid_trunk"


def rmsnorm(x, g):
    xf = x.astype(jnp.float32)
    y = xf * lax.rsqrt(jnp.mean(xf * xf, axis=-1, keepdims=True) + EPS)
    return (y * g.astype(jnp.float32)).astype(x.dtype)


def l2norm(x):
    xf = x.astype(jnp.float32)
    return xf * lax.rsqrt(jnp.sum(xf * xf, axis=-1, keepdims=True) + EPS)


def causal_dwconv(x, w):
    K, C = w.shape
    return lax.conv_general_dilated(
        x, w[:, None, :].astype(x.dtype), window_strides=(1,), padding=[(K - 1, 0)],
        dimension_numbers=('NWC', 'WIO', 'NWC'), feature_group_count=C)


def forgetting_attention(h, w_in, b_forget, q_gain, k_gain, w_out):
    B, L, _ = h.shape
    pad = (-L) % ATT_BLOCK
    hp = jnp.pad(h, ((0, 0), (pad, 0), (0, 0)))
    Lp = L + pad
    proj = hp @ w_in
    q, k, v, og, fl = jnp.split(proj, [ATT_HD, 2 * ATT_HD, 3 * ATT_HD, 4 * ATT_HD], axis=-1)
    q = rmsnorm(q.reshape(B, Lp, ATT_HEADS, ATT_HEAD_DIM), q_gain).transpose(0, 2, 1, 3)
    k = rmsnorm(k.reshape(B, Lp, ATT_HEADS, ATT_HEAD_DIM), k_gain).transpose(0, 2, 1, 3)
    v = v.reshape(B, Lp, ATT_HEADS, ATT_HEAD_DIM).transpose(0, 2, 1, 3)
    logf = jax.nn.log_sigmoid(fl.astype(jnp.float32) + b_forget.astype(jnp.float32))
    c = jnp.cumsum(logf, axis=1).transpose(0, 2, 1)
    pos = jnp.arange(Lp)
    n_blk = Lp // ATT_BLOCK
    qb = q.reshape(B, ATT_HEADS, n_blk, ATT_BLOCK, ATT_HEAD_DIM).transpose(2, 0, 1, 3, 4)
    cb = c.reshape(B, ATT_HEADS, n_blk, ATT_BLOCK).transpose(2, 0, 1, 3)
    pb = pos.reshape(n_blk, ATT_BLOCK)
    scale = ATT_HEAD_DIM ** -0.5

    def one_block(args):
        q_i, c_i, p_i = args
        s = jnp.einsum('bhqd,bhkd->bhqk', q_i, k).astype(jnp.float32) * scale
        s = s + c_i[..., :, None] - c[:, :, None, :]
        mask = (pos[None, :] <= p_i[:, None]) & (pos[None, :] >= pad)
        p = jax.nn.softmax(jnp.where(mask, s, NEG), axis=-1)
        return jnp.einsum('bhqk,bhkd->bhqd', p.astype(v.dtype), v)

    o = lax.map(one_block, (qb, cb, pb))
    o = o.transpose(1, 0, 3, 2, 4).reshape(B, Lp, ATT_HD)
    o = o * jax.nn.sigmoid(og)
    return (o @ w_out)[:, pad:]


def gated_deltanet(h, w_in, conv_w, a_log, dt_bias, o_gain, w_out):
    B, L, _ = h.shape
    H, Dh, C = DN_HEADS, DN_HEAD_DIM, DN_CHUNK
    pad = (-L) % C
    hp = jnp.pad(h, ((0, 0), (pad, 0), (0, 0)))
    Lp = L + pad
    N = Lp // C
    proj = hp @ w_in
    qkv, og, b_logit, a_logit = jnp.split(proj, [3 * DN_HD, 4 * DN_HD, 4 * DN_HD + H], axis=-1)
    qkv = jax.nn.silu(causal_dwconv(qkv, conv_w))
    q, k, v = jnp.split(qkv, [DN_HD, 2 * DN_HD], axis=-1)
    q = l2norm(q.reshape(B, Lp, H, Dh)) * (Dh ** -0.5)
    k = l2norm(k.reshape(B, Lp, H, Dh))
    v = v.reshape(B, Lp, H, Dh).astype(jnp.float32)
    beta = jax.nn.sigmoid(b_logit.astype(jnp.float32))
    g = -jnp.exp(a_log.astype(jnp.float32)) * jax.nn.softplus(
        a_logit.astype(jnp.float32) + dt_bias.astype(jnp.float32))

    def chunk(t):
        return t.reshape(B, N, C, H, Dh).transpose(0, 3, 1, 2, 4)
    qc, kc, vc = chunk(q), chunk(k), chunk(v)
    bc = beta.reshape(B, N, C, H).transpose(0, 3, 1, 2)
    gc = jnp.cumsum(g.reshape(B, N, C, H).transpose(0, 3, 1, 2), axis=-1)
    idx = jnp.arange(C)
    strict = idx[:, None] > idx[None, :]
    incl = idx[:, None] >= idx[None, :]
    dec = jnp.exp(jnp.where(incl, gc[..., :, None] - gc[..., None, :], -jnp.inf))
    kk = jnp.einsum('bhnid,bhnjd->bhnij', kc, kc)
    A = jnp.where(strict, kk * dec * bc[..., :, None], 0.0)
    lhs = A + jnp.eye(C, dtype=jnp.float32)
    rhs = jnp.concatenate([kc * (bc * jnp.exp(gc))[..., None], vc * bc[..., None]], axis=-1)
    sol = lax.linalg.triangular_solve(lhs, rhs, left_side=True, lower=True, unit_diagonal=True)
    W, U0 = sol[..., :Dh], sol[..., Dh:]
    qk = jnp.einsum('bhnid,bhnjd->bhnij', qc, kc) * dec
    q_dec = qc * jnp.exp(gc)[..., None]
    k_dec = kc * jnp.exp(gc[..., -1:] - gc)[..., None]
    g_last = jnp.exp(gc[..., -1])

    def to_n(t):
        return jnp.moveaxis(t, 2, 0)

    def step(S, xs):
        W_n, U0_n, qk_n, qd_n, kd_n, gl_n = xs
        U = U0_n - jnp.einsum('bhcd,bhvd->bhcv', W_n, S)
        O = jnp.einsum('bhcd,bhvd->bhcv', qd_n, S) + jnp.einsum('bhij,bhjv->bhiv', qk_n, U)
        S = gl_n[..., None, None] * S + jnp.einsum('bhcv,bhcd->bhvd', U, kd_n)
        return S, O

    S0 = jnp.zeros((B, H, Dh, Dh), jnp.float32)
    _, O = lax.scan(step, S0, (to_n(W), to_n(U0), to_n(qk), to_n(q_dec), to_n(k_dec), to_n(g_last)))
    O = O.transpose(1, 0, 3, 2, 4).reshape(B, Lp, H, Dh)
    O = rmsnorm(O, o_gain) * jax.nn.silu(og.reshape(B, Lp, H, Dh).astype(jnp.float32))
    return (O.reshape(B, Lp, DN_HD).astype(h.dtype) @ w_out)[:, pad:]


def conv_ffn(h, w_up, conv_w, w_down):
    u = causal_dwconv(h @ w_up, conv_w)
    gate, up = jnp.split(u, [FFN_DIM], axis=-1)
    return (jax.nn.gelu(gate, approximate=True) * up) @ w_down


def _fwd_setup_inputs(seed: int = 0) -> dict:
    key = jax.random.key(seed)
    ks = jax.random.split(key, 24)
    D = D_MODEL
    nrm = lambda k, shape, s: jax.random.normal(k, shape, jnp.float32) * s
    gain = lambda k, shape: 1.0 + 0.1 * jax.random.normal(k, shape, jnp.float32)
    dt = jnp.exp(jax.random.uniform(ks[12], (N_DN_LAYERS, DN_HEADS), jnp.float32,
                                    np.log(1e-3), np.log(1e-1)))
    return {
        "x": nrm(ks[0], (BATCH, SEQ, D), 1.0),
        "meta_tokens": nrm(ks[1], (N_META, D), 1.0),
        "norm_mix_pre": gain(ks[2], (DEPTH, D)),
        "norm_mix_post": gain(ks[3], (DEPTH, D)),
        "norm_ffn_pre": gain(ks[4], (DEPTH, D)),
        "norm_ffn_post": gain(ks[5], (DEPTH, D)),
        "attn_w_in": nrm(ks[6], (N_ATTN_LAYERS, D, ATT_IN), D ** -0.5),
        "attn_b_forget": jax.random.uniform(ks[7], (N_ATTN_LAYERS, ATT_HEADS), jnp.float32, 2.0, 6.0),
        "attn_q_norm": gain(ks[8], (N_ATTN_LAYERS, ATT_HEAD_DIM)),
        "attn_k_norm": gain(ks[9], (N_ATTN_LAYERS, ATT_HEAD_DIM)),
        "attn_w_out": nrm(ks[10], (N_ATTN_LAYERS, ATT_HD, D), ATT_HD ** -0.5),
        "dn_w_in": nrm(ks[11], (N_DN_LAYERS, D, DN_IN), D ** -0.5),
        "dn_conv": nrm(ks[13], (N_DN_LAYERS, DN_CONV, 3 * DN_HD), DN_CONV ** -0.5),
        "dn_a_log": jnp.log(jax.random.uniform(ks[14], (N_DN_LAYERS, DN_HEADS), jnp.float32, 1.0, 16.0)),
        "dn_dt_bias": dt + jnp.log(-jnp.expm1(-dt)),
        "dn_o_norm": gain(ks[15], (N_DN_LAYERS, DN_HEAD_DIM)),
        "dn_w_out": nrm(ks[16], (N_DN_LAYERS, DN_HD, D), DN_HD ** -0.5),
        "ffn_w_up": nrm(ks[17], (DEPTH, D, 2 * FFN_DIM), D ** -0.5),
        "ffn_conv": nrm(ks[18], (DEPTH, FFN_CONV, 2 * FFN_DIM), FFN_CONV ** -0.5),
        "ffn_w_down": nrm(ks[19], (DEPTH, FFN_DIM, D), FFN_DIM ** -0.5),
    }


def _fwd_reference(x, meta_tokens, norm_mix_pre, norm_mix_post, norm_ffn_pre, norm_ffn_post,
              attn_w_in, attn_b_forget, attn_q_norm, attn_k_norm, attn_w_out,
              dn_w_in, dn_conv, dn_a_log, dn_dt_bias, dn_o_norm, dn_w_out,
              ffn_w_up, ffn_conv, ffn_w_down):
    B = x.shape[0]
    meta = jnp.broadcast_to(meta_tokens[None].astype(x.dtype), (B, N_META, x.shape[-1]))
    h = jnp.concatenate([meta, x], axis=1)
    for i in range(DEPTH):
        j = i // 2
        a = rmsnorm(h, norm_mix_pre[i])
        if i % 2 == 0:
            m = forgetting_attention(a, attn_w_in[j], attn_b_forget[j], attn_q_norm[j],
                                     attn_k_norm[j], attn_w_out[j])
        else:
            m = gated_deltanet(a, dn_w_in[j], dn_conv[j], dn_a_log[j], dn_dt_bias[j],
                               dn_o_norm[j], dn_w_out[j])
        h = h + rmsnorm(m, norm_mix_post[i])
        f = conv_ffn(rmsnorm(h, norm_ffn_pre[i]), ffn_w_up[i], ffn_conv[i], ffn_w_down[i])
        h = h + rmsnorm(f, norm_ffn_post[i])
    return h[:, N_META:]


import jax as _jax
import jax.numpy as _jnp

TWIN_FORMAT = 'train_step'
FWD_PARAMS = ['x', 'meta_tokens', 'norm_mix_pre', 'norm_mix_post', 'norm_ffn_pre', 'norm_ffn_post', 'attn_w_in', 'attn_b_forget', 'attn_q_norm', 'attn_k_norm', 'attn_w_out', 'dn_w_in', 'dn_conv', 'dn_a_log', 'dn_dt_bias', 'dn_o_norm', 'dn_w_out', 'ffn_w_up', 'ffn_conv', 'ffn_w_down']
TWIN_WEIGHTS = ['meta_tokens', 'norm_mix_pre', 'norm_mix_post', 'norm_ffn_pre', 'norm_ffn_post', 'attn_w_in', 'attn_b_forget', 'attn_q_norm', 'attn_k_norm', 'attn_w_out', 'dn_w_in', 'dn_conv', 'dn_a_log', 'dn_dt_bias', 'dn_o_norm', 'dn_w_out', 'ffn_w_up', 'ffn_conv', 'ffn_w_down']
TWIN_DIFF_INPUT = 'x'
TWIN_INPUTS = ['x', 'meta_tokens', 'norm_mix_pre', 'norm_mix_post', 'norm_ffn_pre', 'norm_ffn_post', 'attn_w_in', 'attn_b_forget', 'attn_q_norm', 'attn_k_norm', 'attn_w_out', 'dn_w_in', 'dn_conv', 'dn_a_log', 'dn_dt_bias', 'dn_o_norm', 'dn_w_out', 'ffn_w_up', 'ffn_conv', 'ffn_w_down', 'loss_target', 'm_meta_tokens', 'm_norm_mix_pre', 'm_norm_mix_post', 'm_norm_ffn_pre', 'm_norm_ffn_post', 'm_attn_w_in', 'm_attn_b_forget', 'm_attn_q_norm', 'm_attn_k_norm', 'm_attn_w_out', 'm_dn_w_in', 'm_dn_conv', 'm_dn_a_log', 'm_dn_dt_bias', 'm_dn_o_norm', 'm_dn_w_out', 'm_ffn_w_up', 'm_ffn_conv', 'm_ffn_w_down', 'v_meta_tokens', 'v_norm_mix_pre', 'v_norm_mix_post', 'v_norm_ffn_pre', 'v_norm_ffn_post', 'v_attn_w_in', 'v_attn_b_forget', 'v_attn_q_norm', 'v_attn_k_norm', 'v_attn_w_out', 'v_dn_w_in', 'v_dn_conv', 'v_dn_a_log', 'v_dn_dt_bias', 'v_dn_o_norm', 'v_dn_w_out', 'v_ffn_w_up', 'v_ffn_conv', 'v_ffn_w_down']
TWIN_OUTPUTS = ['loss', 'grad_x', 'grad_meta_tokens', 'grad_norm_mix_pre', 'grad_norm_mix_post', 'grad_norm_ffn_pre', 'grad_norm_ffn_post', 'grad_attn_w_in', 'grad_attn_b_forget', 'grad_attn_q_norm', 'grad_attn_k_norm', 'grad_attn_w_out', 'grad_dn_w_in', 'grad_dn_conv', 'grad_dn_a_log', 'grad_dn_dt_bias', 'grad_dn_o_norm', 'grad_dn_w_out', 'grad_ffn_w_up', 'grad_ffn_conv', 'grad_ffn_w_down', 'delta_meta_tokens', 'delta_norm_mix_pre', 'delta_norm_mix_post', 'delta_norm_ffn_pre', 'delta_norm_ffn_post', 'delta_attn_w_in', 'delta_attn_b_forget', 'delta_attn_q_norm', 'delta_attn_k_norm', 'delta_attn_w_out', 'delta_dn_w_in', 'delta_dn_conv', 'delta_dn_a_log', 'delta_dn_dt_bias', 'delta_dn_o_norm', 'delta_dn_w_out', 'delta_ffn_w_up', 'delta_ffn_conv', 'delta_ffn_w_down', 'new_m_meta_tokens', 'new_m_norm_mix_pre', 'new_m_norm_mix_post', 'new_m_norm_ffn_pre', 'new_m_norm_ffn_post', 'new_m_attn_w_in', 'new_m_attn_b_forget', 'new_m_attn_q_norm', 'new_m_attn_k_norm', 'new_m_attn_w_out', 'new_m_dn_w_in', 'new_m_dn_conv', 'new_m_dn_a_log', 'new_m_dn_dt_bias', 'new_m_dn_o_norm', 'new_m_dn_w_out', 'new_m_ffn_w_up', 'new_m_ffn_conv', 'new_m_ffn_w_down', 'new_v_meta_tokens', 'new_v_norm_mix_pre', 'new_v_norm_mix_post', 'new_v_norm_ffn_pre', 'new_v_norm_ffn_post', 'new_v_attn_w_in', 'new_v_attn_b_forget', 'new_v_attn_q_norm', 'new_v_attn_k_norm', 'new_v_attn_w_out', 'new_v_dn_w_in', 'new_v_dn_conv', 'new_v_dn_a_log', 'new_v_dn_dt_bias', 'new_v_dn_o_norm', 'new_v_dn_w_out', 'new_v_ffn_w_up', 'new_v_ffn_conv', 'new_v_ffn_w_down']
TWIN_LEAF_KINDS = {'loss': 'loss', 'grad_x': 'grad_x', 'grad_meta_tokens': 'grad_w', 'grad_norm_mix_pre': 'grad_w', 'grad_norm_mix_post': 'grad_w', 'grad_norm_ffn_pre': 'grad_w', 'grad_norm_ffn_post': 'grad_w', 'grad_attn_w_in': 'grad_w', 'grad_attn_b_forget': 'grad_w', 'grad_attn_q_norm': 'grad_w', 'grad_attn_k_norm': 'grad_w', 'grad_attn_w_out': 'grad_w', 'grad_dn_w_in': 'grad_w', 'grad_dn_conv': 'grad_w', 'grad_dn_a_log': 'grad_w', 'grad_dn_dt_bias': 'grad_w', 'grad_dn_o_norm': 'grad_w', 'grad_dn_w_out': 'grad_w', 'grad_ffn_w_up': 'grad_w', 'grad_ffn_conv': 'grad_w', 'grad_ffn_w_down': 'grad_w', 'delta_meta_tokens': 'delta_w', 'delta_norm_mix_pre': 'delta_w', 'delta_norm_mix_post': 'delta_w', 'delta_norm_ffn_pre': 'delta_w', 'delta_norm_ffn_post': 'delta_w', 'delta_attn_w_in': 'delta_w', 'delta_attn_b_forget': 'delta_w', 'delta_attn_q_norm': 'delta_w', 'delta_attn_k_norm': 'delta_w', 'delta_attn_w_out': 'delta_w', 'delta_dn_w_in': 'delta_w', 'delta_dn_conv': 'delta_w', 'delta_dn_a_log': 'delta_w', 'delta_dn_dt_bias': 'delta_w', 'delta_dn_o_norm': 'delta_w', 'delta_dn_w_out': 'delta_w', 'delta_ffn_w_up': 'delta_w', 'delta_ffn_conv': 'delta_w', 'delta_ffn_w_down': 'delta_w', 'new_m_meta_tokens': 'new_m', 'new_m_norm_mix_pre': 'new_m', 'new_m_norm_mix_post': 'new_m', 'new_m_norm_ffn_pre': 'new_m', 'new_m_norm_ffn_post': 'new_m', 'new_m_attn_w_in': 'new_m', 'new_m_attn_b_forget': 'new_m', 'new_m_attn_q_norm': 'new_m', 'new_m_attn_k_norm': 'new_m', 'new_m_attn_w_out': 'new_m', 'new_m_dn_w_in': 'new_m', 'new_m_dn_conv': 'new_m', 'new_m_dn_a_log': 'new_m', 'new_m_dn_dt_bias': 'new_m', 'new_m_dn_o_norm': 'new_m', 'new_m_dn_w_out': 'new_m', 'new_m_ffn_w_up': 'new_m', 'new_m_ffn_conv': 'new_m', 'new_m_ffn_w_down': 'new_m', 'new_v_meta_tokens': 'new_v', 'new_v_norm_mix_pre': 'new_v', 'new_v_norm_mix_post': 'new_v', 'new_v_norm_ffn_pre': 'new_v', 'new_v_norm_ffn_post': 'new_v', 'new_v_attn_w_in': 'new_v', 'new_v_attn_b_forget': 'new_v', 'new_v_attn_q_norm': 'new_v', 'new_v_attn_k_norm': 'new_v', 'new_v_attn_w_out': 'new_v', 'new_v_dn_w_in': 'new_v', 'new_v_dn_conv': 'new_v', 'new_v_dn_a_log': 'new_v', 'new_v_dn_dt_bias': 'new_v', 'new_v_dn_o_norm': 'new_v', 'new_v_dn_w_out': 'new_v', 'new_v_ffn_w_up': 'new_v', 'new_v_ffn_conv': 'new_v', 'new_v_ffn_w_down': 'new_v'}


def _forward(args):
    return _fwd_reference(*[args[k] for k in FWD_PARAMS])


def _output_shape():
    def fwd():
        inp = _fwd_setup_inputs(0)
        return _fwd_reference(*[inp[k] for k in FWD_PARAMS])
    out = _jax.eval_shape(fwd)
    return out.shape, out.dtype

N_MICROBATCH = 1
ADAM_LR = 0.001
ADAM_B1 = 0.9
ADAM_B2 = 0.999
ADAM_EPS = 1e-08
ADAM_WD = 0.01
ADAM_STEP = 10
PER_EXAMPLE_BATCH_AXIS = {'x': 0, 'loss_target': 0}
SHARED_INPUTS = []
_WEIGHT_DTYPES = {'meta_tokens': _jnp.float32, 'norm_mix_pre': _jnp.float32, 'norm_mix_post': _jnp.float32, 'norm_ffn_pre': _jnp.float32, 'norm_ffn_post': _jnp.float32, 'attn_w_in': _jnp.float32, 'attn_b_forget': _jnp.float32, 'attn_q_norm': _jnp.float32, 'attn_k_norm': _jnp.float32, 'attn_w_out': _jnp.float32, 'dn_w_in': _jnp.float32, 'dn_conv': _jnp.float32, 'dn_a_log': _jnp.float32, 'dn_dt_bias': _jnp.float32, 'dn_o_norm': _jnp.float32, 'dn_w_out': _jnp.float32, 'ffn_w_up': _jnp.float32, 'ffn_conv': _jnp.float32, 'ffn_w_down': _jnp.float32}
MOMENT_SCALE = {'meta_tokens': 4.960316e-01, 'norm_mix_pre': 3.972154e+00, 'norm_mix_post': 6.347333e+01, 'norm_ffn_pre': 2.749320e+00, 'norm_ffn_post': 6.357099e+01, 'attn_w_in': 2.504953e+00, 'attn_b_forget': 1.900064e+01, 'attn_q_norm': 9.729748e+00, 'attn_k_norm': 9.941544e+00, 'attn_w_out': 4.012023e+00, 'dn_w_in': 1.477355e+00, 'dn_conv': 2.071095e+00, 'dn_a_log': 1.055073e+01, 'dn_dt_bias': 1.019632e+01, 'dn_o_norm': 1.378905e+01, 'dn_w_out': 5.459851e+00, 'ffn_w_up': 1.123036e+00, 'ffn_conv': 1.186402e+00, 'ffn_w_down': 2.112524e+00}


def _to_microbatches(a, axis):
    t = _jnp.moveaxis(a, axis, 0)
    t = t.reshape((N_MICROBATCH, t.shape[0] // N_MICROBATCH) + t.shape[1:])
    return _jnp.moveaxis(t, 1, axis + 1)


def setup_inputs(seed: int = 0) -> dict:
    inp = _fwd_setup_inputs(seed)
    key = _jax.random.fold_in(_jax.random.key(seed), 7919)
    shape, _ = _output_shape()
    out = dict(inp)
    out["loss_target"] = _jax.random.normal(_jax.random.fold_in(key, 0), shape, _jnp.float32)
    for i, name in enumerate(TWIN_WEIGHTS):
        w = inp[name].astype(_jnp.float32)
        if MOMENT_SCALE is None:
            s = _jnp.sqrt(_jnp.mean(_jnp.square(w)) + 1e-30)
        else:
            s = MOMENT_SCALE[name]
        km, kv = _jax.random.split(_jax.random.fold_in(key, i + 1))
        out[name] = w
        out["m_" + name] = s * _jax.random.normal(km, w.shape, _jnp.float32)
        out["v_" + name] = (s * s) * _jax.random.uniform(kv, w.shape, _jnp.float32, 0.5, 1.5)
    if N_MICROBATCH > 1:
        for name, axis in PER_EXAMPLE_BATCH_AXIS.items():
            out[name] = _to_microbatches(out[name], axis)
    return {'x': out['x'], 'meta_tokens': out['meta_tokens'], 'norm_mix_pre': out['norm_mix_pre'], 'norm_mix_post': out['norm_mix_post'], 'norm_ffn_pre': out['norm_ffn_pre'], 'norm_ffn_post': out['norm_ffn_post'], 'attn_w_in': out['attn_w_in'], 'attn_b_forget': out['attn_b_forget'], 'attn_q_norm': out['attn_q_norm'], 'attn_k_norm': out['attn_k_norm'], 'attn_w_out': out['attn_w_out'], 'dn_w_in': out['dn_w_in'], 'dn_conv': out['dn_conv'], 'dn_a_log': out['dn_a_log'], 'dn_dt_bias': out['dn_dt_bias'], 'dn_o_norm': out['dn_o_norm'], 'dn_w_out': out['dn_w_out'], 'ffn_w_up': out['ffn_w_up'], 'ffn_conv': out['ffn_conv'], 'ffn_w_down': out['ffn_w_down'], 'loss_target': out['loss_target'], 'm_meta_tokens': out['m_meta_tokens'], 'm_norm_mix_pre': out['m_norm_mix_pre'], 'm_norm_mix_post': out['m_norm_mix_post'], 'm_norm_ffn_pre': out['m_norm_ffn_pre'], 'm_norm_ffn_post': out['m_norm_ffn_post'], 'm_attn_w_in': out['m_attn_w_in'], 'm_attn_b_forget': out['m_attn_b_forget'], 'm_attn_q_norm': out['m_attn_q_norm'], 'm_attn_k_norm': out['m_attn_k_norm'], 'm_attn_w_out': out['m_attn_w_out'], 'm_dn_w_in': out['m_dn_w_in'], 'm_dn_conv': out['m_dn_conv'], 'm_dn_a_log': out['m_dn_a_log'], 'm_dn_dt_bias': out['m_dn_dt_bias'], 'm_dn_o_norm': out['m_dn_o_norm'], 'm_dn_w_out': out['m_dn_w_out'], 'm_ffn_w_up': out['m_ffn_w_up'], 'm_ffn_conv': out['m_ffn_conv'], 'm_ffn_w_down': out['m_ffn_w_down'], 'v_meta_tokens': out['v_meta_tokens'], 'v_norm_mix_pre': out['v_norm_mix_pre'], 'v_norm_mix_post': out['v_norm_mix_post'], 'v_norm_ffn_pre': out['v_norm_ffn_pre'], 'v_norm_ffn_post': out['v_norm_ffn_post'], 'v_attn_w_in': out['v_attn_w_in'], 'v_attn_b_forget': out['v_attn_b_forget'], 'v_attn_q_norm': out['v_attn_q_norm'], 'v_attn_k_norm': out['v_attn_k_norm'], 'v_attn_w_out': out['v_attn_w_out'], 'v_dn_w_in': out['v_dn_w_in'], 'v_dn_conv': out['v_dn_conv'], 'v_dn_a_log': out['v_dn_a_log'], 'v_dn_dt_bias': out['v_dn_dt_bias'], 'v_dn_o_norm': out['v_dn_o_norm'], 'v_dn_w_out': out['v_dn_w_out'], 'v_ffn_w_up': out['v_ffn_w_up'], 'v_ffn_conv': out['v_ffn_conv'], 'v_ffn_w_down': out['v_ffn_w_down']}


def _loss(weights, diff, rest, loss_target):
    with _jax.named_scope("forward"):
        args = {**rest, TWIN_DIFF_INPUT: diff, **{k: w.astype(_WEIGHT_DTYPES[k]) for k, w in weights.items()}}
        y = _forward(args)
    with _jax.named_scope("loss_head"):
        err = _jnp.square(y.astype(_jnp.float32) - loss_target)
        return 0.5 * _jnp.sum(_jnp.mean(err, axis=-1)) if err.ndim else 0.5 * err


def _adamw(w, g, m, v):
    m = ADAM_B1 * m + (1.0 - ADAM_B1) * g
    v = ADAM_B2 * v + (1.0 - ADAM_B2) * _jnp.square(g)
    m_hat = m / (1.0 - ADAM_B1 ** ADAM_STEP)
    v_hat = v / (1.0 - ADAM_B2 ** ADAM_STEP)
    delta = -ADAM_LR * (m_hat / (_jnp.sqrt(v_hat) + ADAM_EPS) + ADAM_WD * w)
    return delta, m, v


def reference(x, meta_tokens, norm_mix_pre, norm_mix_post, norm_ffn_pre, norm_ffn_post, attn_w_in, attn_b_forget, attn_q_norm, attn_k_norm, attn_w_out, dn_w_in, dn_conv, dn_a_log, dn_dt_bias, dn_o_norm, dn_w_out, ffn_w_up, ffn_conv, ffn_w_down, loss_target, m_meta_tokens, m_norm_mix_pre, m_norm_mix_post, m_norm_ffn_pre, m_norm_ffn_post, m_attn_w_in, m_attn_b_forget, m_attn_q_norm, m_attn_k_norm, m_attn_w_out, m_dn_w_in, m_dn_conv, m_dn_a_log, m_dn_dt_bias, m_dn_o_norm, m_dn_w_out, m_ffn_w_up, m_ffn_conv, m_ffn_w_down, v_meta_tokens, v_norm_mix_pre, v_norm_mix_post, v_norm_ffn_pre, v_norm_ffn_post, v_attn_w_in, v_attn_b_forget, v_attn_q_norm, v_attn_k_norm, v_attn_w_out, v_dn_w_in, v_dn_conv, v_dn_a_log, v_dn_dt_bias, v_dn_o_norm, v_dn_w_out, v_ffn_w_up, v_ffn_conv, v_ffn_w_down):
    given = dict(x=x, meta_tokens=meta_tokens, norm_mix_pre=norm_mix_pre, norm_mix_post=norm_mix_post, norm_ffn_pre=norm_ffn_pre, norm_ffn_post=norm_ffn_post, attn_w_in=attn_w_in, attn_b_forget=attn_b_forget, attn_q_norm=attn_q_norm, attn_k_norm=attn_k_norm, attn_w_out=attn_w_out, dn_w_in=dn_w_in, dn_conv=dn_conv, dn_a_log=dn_a_log, dn_dt_bias=dn_dt_bias, dn_o_norm=dn_o_norm, dn_w_out=dn_w_out, ffn_w_up=ffn_w_up, ffn_conv=ffn_conv, ffn_w_down=ffn_w_down, loss_target=loss_target, m_meta_tokens=m_meta_tokens, m_norm_mix_pre=m_norm_mix_pre, m_norm_mix_post=m_norm_mix_post, m_norm_ffn_pre=m_norm_ffn_pre, m_norm_ffn_post=m_norm_ffn_post, m_attn_w_in=m_attn_w_in, m_attn_b_forget=m_attn_b_forget, m_attn_q_norm=m_attn_q_norm, m_attn_k_norm=m_attn_k_norm, m_attn_w_out=m_attn_w_out, m_dn_w_in=m_dn_w_in, m_dn_conv=m_dn_conv, m_dn_a_log=m_dn_a_log, m_dn_dt_bias=m_dn_dt_bias, m_dn_o_norm=m_dn_o_norm, m_dn_w_out=m_dn_w_out, m_ffn_w_up=m_ffn_w_up, m_ffn_conv=m_ffn_conv, m_ffn_w_down=m_ffn_w_down, v_meta_tokens=v_meta_tokens, v_norm_mix_pre=v_norm_mix_pre, v_norm_mix_post=v_norm_mix_post, v_norm_ffn_pre=v_norm_ffn_pre, v_norm_ffn_post=v_norm_ffn_post, v_attn_w_in=v_attn_w_in, v_attn_b_forget=v_attn_b_forget, v_attn_q_norm=v_attn_q_norm, v_attn_k_norm=v_attn_k_norm, v_attn_w_out=v_attn_w_out, v_dn_w_in=v_dn_w_in, v_dn_conv=v_dn_conv, v_dn_a_log=v_dn_a_log, v_dn_dt_bias=v_dn_dt_bias, v_dn_o_norm=v_dn_o_norm, v_dn_w_out=v_dn_w_out, v_ffn_w_up=v_ffn_w_up, v_ffn_conv=v_ffn_conv, v_ffn_w_down=v_ffn_w_down)
    weights = {n: given[n] for n in TWIN_WEIGHTS}
    shared = {n: given[n] for n in SHARED_INPUTS}
    per_example = {n: given[n] for n in ['x']}
    grad_fn = _jax.value_and_grad(_loss, argnums=(0, 1))

    def one_microbatch(ex, loss_target):
        ex = dict(ex)
        diff = ex.pop(TWIN_DIFF_INPUT)
        return grad_fn(weights, diff, {**shared, **ex}, loss_target)

    if N_MICROBATCH == 1:
        loss, (grad_w, grad_x) = one_microbatch(per_example, given["loss_target"])
    else:
        def body(carry, xs):
            loss_sum, grad_sum = carry
            l_k, (gw_k, gx_k) = one_microbatch(xs[0], xs[1])
            with _jax.named_scope("update"):
                return (loss_sum + l_k, _jax.tree.map(_jnp.add, grad_sum, gw_k)), gx_k

        init = (_jnp.zeros((), _jnp.float32), _jax.tree.map(_jnp.zeros_like, weights))
        (loss, grad_w), grad_x = _jax.lax.scan(body, init, (per_example, given["loss_target"]))
    with _jax.named_scope("update"):
        delta_w, new_m, new_v = {}, {}, {}
        for n in TWIN_WEIGHTS:
            delta_w[n], new_m[n], new_v[n] = _adamw(weights[n], grad_w[n], given["m_" + n], given["v_" + n])
    return (loss, grad_x, *[grad_w[n] for n in TWIN_WEIGHTS], *[delta_w[n] for n in TWIN_WEIGHTS],
            *[new_m[n] for n in TWIN_WEIGHTS], *[new_v[n] for n in TWIN_WEIGHTS])
```

```python
import functools

import jax
import jax.numpy as jnp
from jax import lax
from jax.experimental import pallas as pl
from jax.experimental.pallas import tpu as pltpu

F32, BF16 = jnp.float32, jnp.bfloat16

D_MODEL = 1024
N_META = 16
ATT_HEADS, ATT_DH = 16, 64
ATT_HD = ATT_HEADS * ATT_DH
AUG = 128
ATT_AUG = ATT_HEADS * AUG
DN_HEADS, DN_DH = 8, 128
DN_HD = DN_HEADS * DN_DH
DN_CHUNK = 64
FFN_DIM = 2816
DEPTH = 4
EPS = 1e-6
NEG = -1e30

ADAM_LR, ADAM_B1, ADAM_B2, ADAM_EPS, ADAM_WD, ADAM_STEP = 0.001, 0.9, 0.999, 1e-08, 0.01, 10

TM = 640
HALO = 8
VMEM_LIMIT = 52 * 1024 * 1024
MESH_ID = pl.DeviceIdType.MESH


def _cp(sem):
    return pltpu.CompilerParams(dimension_semantics=sem, vmem_limit_bytes=VMEM_LIMIT)


def _pick(n, cands):
    for c in cands:
        if n % c == 0:
            return c
    return n


def matmul(a, b, mode, out_dtype, name, add=None):
    if mode == "nn":
        (M, K), (_, N) = a.shape, b.shape
    elif mode == "nt":
        (M, K), (N, _) = a.shape, b.shape
    else:
        (K, M), (_, N) = a.shape, b.shape
    tm = _pick(M, (TM, 1024, 1408, 512, 256, 128))
    tn = _pick(N, (1024, 1408, 512, 256, 128))
    tk = _pick(K, (TM, 1024, 1408, 512, 256, 128))
    nk = K // tk
    if mode == "nn":
        a_spec = pl.BlockSpec((tm, tk), lambda i, j, k: (i, k))
        b_spec = pl.BlockSpec((tk, tn), lambda i, j, k: (k, j))
        dims = (((1,), (0,)), ((), ()))
    elif mode == "nt":
        a_spec = pl.BlockSpec((tm, tk), lambda i, j, k: (i, k))
        b_spec = pl.BlockSpec((tn, tk), lambda i, j, k: (j, k))
        dims = (((1,), (1,)), ((), ()))
    else:
        a_spec = pl.BlockSpec((tk, tm), lambda i, j, k: (k, i))
        b_spec = pl.BlockSpec((tk, tn), lambda i, j, k: (k, j))
        dims = (((0,), (0,)), ((), ()))
    o_spec = pl.BlockSpec((tm, tn), lambda i, j, k: (i, j))
    has_add = add is not None

    def body(*refs):
        if has_add:
            a_ref, b_ref, add_ref, o_ref, acc_ref = refs
        else:
            a_ref, b_ref, o_ref, acc_ref = refs
        k = pl.program_id(2)
        part = lax.dot_general(a_ref[...], b_ref[...], dims, preferred_element_type=F32)

        @pl.when(k == 0)
        def _():
            acc_ref[...] = part

        @pl.when(k > 0)
        def _():
            acc_ref[...] += part

        @pl.when(k == nk - 1)
        def _():
            r = acc_ref[...]
            if has_add:
                r = r + add_ref[...].astype(F32)
            o_ref[...] = r.astype(out_dtype)

    in_specs = [a_spec, b_spec] + ([o_spec] if has_add else [])
    args = (a, b) + ((add,) if has_add else ())
    return pl.pallas_call(
        body, name=name, grid=(M // tm, N // tn, nk), in_specs=in_specs, out_specs=o_spec,
        out_shape=jax.ShapeDtypeStruct((M, N), out_dtype),
        scratch_shapes=[pltpu.VMEM((tm, tn), F32)],
        compiler_params=_cp(("parallel", "parallel", "arbitrary")),
    )(*args)


def tok(arr, width=None, col=0, conv=None, grad=None, fixed=False):
    return dict(arr=arr, w=arr.shape[1] if width is None else width, col=col, conv=conv, grad=grad, step=0 if fixed else 1)


def par(arr, width=None, col=0):
    return dict(arr=arr, w=arr.shape[1] if width is None else width, col=col)


def _conv_apply(x, halo, w, ext_ref):
    K = w.shape[0]
    rows = x.shape[0]
    ext_ref[0:HALO, :] = halo
    ext_ref[HALO:, :] = x
    y = w[K - 1:K, :] * x
    for k in range(K - 1):
        y = y + w[k:k + 1, :] * ext_ref[pl.ds(HALO - (K - 1) + k, rows), :]
    return y


def _row_specs(toks, pars):
    specs, args = [], []
    for t in toks:
        specs.append(pl.BlockSpec((TM, t["w"]), functools.partial(lambda i, j, c, st: (i, c + st * j), c=t["col"], st=t["step"])))
        args.append(t["arr"])
        if t["conv"] is not None:
            specs.append(pl.BlockSpec((HALO, t["w"]), functools.partial(
                lambda i, j, c: (jnp.maximum(i * (TM // HALO) - 1, 0), c + j), c=t["col"])))
            args.append(t["arr"])
            cw, ccol = t["conv"]
            specs.append(pl.BlockSpec((cw.shape[0], t["w"]), functools.partial(lambda i, j, c: (0, c + j), c=ccol)))
            args.append(cw)
    for p in pars:
        specs.append(pl.BlockSpec((p["arr"].shape[0], p["w"]), functools.partial(lambda i, j, c: (0, c), c=p["col"])))
        args.append(p["arr"])
    return specs, args


def _row_load(toks, pars, refs, ext_refs):
    i = pl.program_id(0)
    vals, n, e = [], 0, 0
    for t in toks:
        x = refs[n][...].astype(F32)
        n += 1
        if t["conv"] is not None:
            halo = jnp.where(i == 0, 0.0, refs[n][...].astype(F32))
            w = refs[n + 1][...]
            n += 2
            x = _conv_apply(x, halo, w, ext_refs[e])
            e += 1
        vals.append(x)
    for _ in pars:
        vals.append(refs[n][...])
        n += 1
    return vals, n


def rowwise(fn, toks, pars, outs, name, nsplit=1, with_j=False):
    Tp = toks[0]["arr"].shape[0]
    specs, args = _row_specs(toks, pars)
    n_ext = sum(t["conv"] is not None for t in toks)

    def body(*refs):
        ext_refs = refs[len(refs) - n_ext:]
        vals, n = _row_load(toks, pars, refs, ext_refs)
        res = fn(pl.program_id(1), *vals) if with_j else fn(*vals)
        for r, o_ref in zip(res, refs[n:n + len(outs)], strict=True):
            o_ref[...] = r.astype(o_ref.dtype)

    return pl.pallas_call(
        body, name=name, grid=(Tp // TM, nsplit), in_specs=specs,
        out_specs=[pl.BlockSpec((TM, w), lambda i, j: (i, j)) for w, _ in outs],
        out_shape=[jax.ShapeDtypeStruct((Tp, w * nsplit), dt) for w, dt in outs],
        scratch_shapes=[pltpu.VMEM((TM + HALO, t["w"]), F32) for t in toks if t["conv"] is not None],
        compiler_params=_cp(("parallel", "parallel")),
    )(*args)


def rowwise_bwd(fn, toks, pars, cts, name, groups=None, par_grads=None, nsplit=1, with_j=False):
    Tp = toks[0]["arr"].shape[0]
    specs, args = _row_specs(toks, pars)
    n_ext = sum(t["conv"] is not None for t in toks)
    gidx = [k for k, t in enumerate(toks) if t["grad"] is not None]
    if groups is None:
        groups = [[k] for k in range(len(gidx))]
    par_grads = list(range(len(pars))) if par_grads is None else par_grads
    ct_specs, ct_args = [], []
    for c in cts:
        if c is not None:
            ct_specs.append(pl.BlockSpec((TM, c.shape[1] // nsplit), lambda i, j: (i, j)))
            ct_args.append(c)
    out_shapes, out_specs = [], []
    for g in groups:
        w = sum(toks[gidx[k]]["w"] for k in g)
        out_shapes.append(jax.ShapeDtypeStruct((Tp, w * nsplit), toks[gidx[g[0]]]["grad"]))
        out_specs.append(pl.BlockSpec((TM, w), lambda i, j: (i, j)))
    for k in par_grads:
        shp = (pars[k]["arr"].shape[0], pars[k]["w"])
        out_shapes.append(jax.ShapeDtypeStruct(shp, F32))
        out_specs.append(pl.BlockSpec(shp, lambda i, j: (0, 0)))
    n_ct = len(ct_args)

    def body(*refs):
        first = (pl.program_id(0) == 0) & (pl.program_id(1) == 0)
        ext_refs = refs[len(refs) - n_ext:]
        vals, n = _row_load(toks, pars, refs, ext_refs)
        ct_refs = refs[n:n + n_ct]
        o_refs = refs[n + n_ct:len(refs) - n_ext]
        res, vjp = jax.vjp(functools.partial(fn, pl.program_id(1)) if with_j else fn, *vals)
        ct_vals, c = [], 0
        for r, ct in zip(res, cts, strict=True):
            if ct is None:
                ct_vals.append(jnp.zeros_like(r))
            else:
                ct_vals.append(ct_refs[c][...].astype(F32))
                c += 1
        grads = vjp(tuple(ct_vals))
        for g, o_ref in zip(groups, o_refs[:len(groups)], strict=True):
            pieces = [grads[gidx[k]] for k in g]
            val = pieces[0] if len(pieces) == 1 else jnp.concatenate(pieces, axis=1)
            o_ref[...] = val.astype(o_ref.dtype)
        for k, o_ref in zip(par_grads, o_refs[len(groups):], strict=True):
            gk = grads[len(toks) + k]

            @pl.when(first)
            def _(o_ref=o_ref, gk=gk):
                o_ref[...] = gk

            @pl.when(jnp.logical_not(first))
            def _(o_ref=o_ref, gk=gk):
                o_ref[...] += gk

    res = pl.pallas_call(
        body, name=name, grid=(Tp // TM, nsplit), in_specs=specs + ct_specs, out_specs=out_specs, out_shape=out_shapes,
        scratch_shapes=[pltpu.VMEM((TM + HALO, t["w"]), F32) for t in toks if t["conv"] is not None],
        compiler_params=_cp(("arbitrary", "arbitrary")),
    )(*args, *ct_args)
    return res[:len(groups)], res[len(groups):]


def conv_bwd(dy, x, w, name):
    Tp, W = dy.shape
    K = w.shape[0]
    wb = _pick(W, (1024, 1408, 512))
    nt = Tp // TM

    def body(dy_ref, dyn_ref, x_ref, xh_ref, w_ref, dx_ref, dw_ref, ext_dy, ext_x):
        i = pl.program_id(1)
        dyv = dy_ref[...].astype(F32)
        ext_dy[0:TM, :] = dyv
        ext_dy[TM:, :] = jnp.where(i == nt - 1, 0.0, dyn_ref[...].astype(F32))
        ext_x[0:HALO, :] = jnp.where(i == 0, 0.0, xh_ref[...].astype(F32))
        ext_x[HALO:, :] = x_ref[...].astype(F32)
        wv = w_ref[...]
        dx = wv[K - 1:K, :] * dyv
        for k in range(K - 1):
            dx = dx + wv[k:k + 1, :] * ext_dy[pl.ds(K - 1 - k, TM), :]
        dx_ref[...] = dx.astype(dx_ref.dtype)
        rows = [jnp.sum(dyv * ext_x[pl.ds(HALO - (K - 1) + k, TM), :], axis=0, keepdims=True) for k in range(K)]
        dwv = jnp.concatenate(rows, axis=0)

        @pl.when(i == 0)
        def _():
            dw_ref[...] = dwv

        @pl.when(i > 0)
        def _():
            dw_ref[...] += dwv

    return pl.pallas_call(
        body, name=name, grid=(W // wb, nt),
        in_specs=[
            pl.BlockSpec((TM, wb), lambda j, i: (i, j)),
            pl.BlockSpec((HALO, wb), lambda j, i: (jnp.minimum((i + 1) * (TM // HALO), Tp // HALO - 1), j)),
            pl.BlockSpec((TM, wb), lambda j, i: (i, j)),
            pl.BlockSpec((HALO, wb), lambda j, i: (jnp.maximum(i * (TM // HALO) - 1, 0), j)),
            pl.BlockSpec((K, wb), lambda j, i: (0, j)),
        ],
        out_specs=[pl.BlockSpec((TM, wb), lambda j, i: (i, j)), pl.BlockSpec((K, wb), lambda j, i: (0, j))],
        out_shape=[jax.ShapeDtypeStruct((Tp, W), BF16), jax.ShapeDtypeStruct((K, W), F32)],
        scratch_shapes=[pltpu.VMEM((TM + HALO, wb), F32), pltpu.VMEM((TM + HALO, wb), F32)],
        compiler_params=_cp(("parallel", "arbitrary")),
    )(dy, dy, x, x, w)


def _rms(x, g):
    return x * lax.rsqrt(jnp.mean(x * x, axis=-1, keepdims=True) + EPS) * g


def fn_norm(h, g):
    return (_rms(h, g),)


def fn_id_norm(h, g):
    return h, _rms(h, g)


def fn_resid_norm(h, m, g_post, g_next):
    h2 = h + _rms(m, g_post)
    return h2, _rms(h2, g_next)


def fn_resid(h, m, g_post):
    return (h + _rms(m, g_post),)


def fn_geglu(y):
    w = y.shape[1] // 2
    return (jax.nn.gelu(y[:, :w], approximate=True) * y[:, w:],)


def _split3(c):
    hi = c.astype(BF16).astype(F32)
    r = c - hi
    mid = r.astype(BF16).astype(F32)
    lo = (r - mid).astype(BF16).astype(F32)
    return hi, mid, lo


def _aug_cols(rows, entries):
    lane = lax.broadcasted_iota(jnp.int32, (rows, ATT_DH), 1)
    out = jnp.zeros((rows, ATT_DH), F32)
    for ln, val in entries:
        out = jnp.where(lane == ln, val, out)
    return out


def _head_col(c, idx):
    lane = lax.broadcasted_iota(jnp.int32, c.shape, 1)
    return jnp.sum(jnp.where(lane == idx, c, 0.0), axis=1, keepdims=True)


def fn_attn_prep_q(j, xp, c, gain):
    rows = xp.shape[0]
    out = []
    for hh in range(2):
        hi, mid, lo = _split3(lax.stop_gradient(_head_col(c, 2 * j + hh)))
        out += [_rms(xp[:, ATT_DH * hh:ATT_DH * (hh + 1)], gain) * (ATT_DH ** -0.5),
                _aug_cols(rows, [(0, hi), (1, mid), (2, lo), (3, 1.0), (4, 1.0), (5, 1.0)])]
    return (jnp.concatenate(out, axis=1),)


def fn_attn_prep_k(j, xp, c, gain):
    rows = xp.shape[0]
    out = []
    for hh in range(2):
        hi, mid, lo = _split3(lax.stop_gradient(_head_col(c, 2 * j + hh)))
        out += [_rms(xp[:, ATT_DH * hh:ATT_DH * (hh + 1)], gain),
                _aug_cols(rows, [(0, 1.0), (1, 1.0), (2, 1.0), (3, -hi), (4, -mid), (5, -lo), (6, 1.0), (7, 1.0), (8, 1.0)])]
    return (jnp.concatenate(out, axis=1),)


def fn_attn_prep_v(xp):
    rows = xp.shape[0]
    out = []
    for hh in range(2):
        out += [xp[:, ATT_DH * hh:ATT_DH * (hh + 1)], _aug_cols(rows, [(0, 1.0), (1, 1.0), (2, 1.0)])]
    return (jnp.concatenate(out, axis=1),)


def fn_attn_gate(o_aug, og):
    outs = []
    for h in range(og.shape[1] // ATT_DH):
        o = o_aug[:, AUG * h:AUG * h + ATT_DH]
        outs.append(o * jax.nn.sigmoid(og[:, ATT_DH * h:ATT_DH * (h + 1)]))
    return (jnp.concatenate(outs, axis=1),)


def fn_attn_bwd_prep(dgo, o_aug, og, q_aug):
    rows = dgo.shape[0]
    lane = lax.broadcasted_iota(jnp.int32, (rows, ATT_DH), 1)
    dos, dogs, qs = [], [], []
    for h in range(og.shape[1] // ATT_DH):
        sl = slice(ATT_DH * h, ATT_DH * (h + 1))
        o = o_aug[:, AUG * h:AUG * h + ATT_DH]
        lse = o_aug[:, AUG * h + ATT_DH:AUG * h + ATT_DH + 1]
        sig = jax.nn.sigmoid(og[:, sl])
        do = dgo[:, sl] * sig
        dogs.append(dgo[:, sl] * o * sig * (1.0 - sig))
        dhi, dmid, dlo = _split3(-jnp.sum(do * o, axis=-1, keepdims=True))
        dos += [do, _aug_cols(rows, [(0, dhi), (1, dmid), (2, dlo)])]
        lhi, lmid, llo = _split3(-lse)
        qa = q_aug[:, AUG * h + ATT_DH:AUG * (h + 1)]
        qa = jnp.where(lane == 6, lhi, jnp.where(lane == 7, lmid, jnp.where(lane == 8, llo, qa)))
        qs += [q_aug[:, AUG * h:AUG * h + ATT_DH], qa]
    return jnp.concatenate(dos, axis=1), jnp.concatenate(dogs, axis=1), jnp.concatenate(qs, axis=1)


def fn_attn_dc(dq_aug, dk_aug):
    lane = lax.broadcasted_iota(jnp.int32, (dk_aug.shape[0], AUG), 1)
    dc = jnp.zeros((dk_aug.shape[0], AUG), F32)
    for h in range(ATT_HEADS):
        col = AUG * h + ATT_DH
        dc = jnp.where(lane == h, dq_aug[:, col:col + 1] - dk_aug[:, col + 3:col + 4], dc)
    return (dc,)


def fn_dn_prep(j, y):
    s = jax.nn.silu(y)
    scale = jnp.where(j < DN_HD // 256, DN_DH ** -0.5, 1.0)
    out = []
    for hh in range(2):
        sh = s[:, DN_DH * hh:DN_DH * (hh + 1)]
        n = sh * lax.rsqrt(jnp.sum(sh * sh, axis=-1, keepdims=True) + EPS) * scale
        out.append(jnp.where(j < 2 * (DN_HD // 256), n, sh))
    return (jnp.concatenate(out, axis=1),)


def fn_dn_gates(lg, alog, dtb):
    lane = lax.broadcasted_iota(jnp.int32, lg.shape, 1)
    beta = jax.nn.sigmoid(lg)
    g = -jnp.exp(alog) * jax.nn.softplus(lg + dtb)
    return (jnp.where(lane < DN_HEADS, beta, jnp.where(lane < 2 * DN_HEADS, g, 0.0)),)


def fn_dn_post(o, og, gain):
    outs = []
    for h in range(DN_HEADS):
        sl = slice(DN_DH * h, DN_DH * (h + 1))
        outs.append(_rms(o[:, sl], gain) * jax.nn.silu(og[:, sl]))
    return (jnp.concatenate(outs, axis=1),)


def _tri(n, lower):
    r = lax.broadcasted_iota(jnp.int32, (n, n), 0)
    c = lax.broadcasted_iota(jnp.int32, (n, n), 1)
    return jnp.where((r >= c) if lower else (r <= c), 1.0, 0.0).astype(F32)


def forget_cumsum(fl, bias, name):
    Tp, W = fl.shape

    def body(fl_ref, b_ref, c_ref, carry):
        i = pl.program_id(0)

        @pl.when(i == 0)
        def _():
            carry[...] = jnp.zeros_like(carry)

        logf = jax.nn.log_sigmoid(fl_ref[...] + b_ref[...])
        c = jnp.dot(_tri(TM, True), logf, precision=lax.Precision.HIGHEST, preferred_element_type=F32) + carry[...]
        c_ref[...] = c
        carry[...] = c[TM - 1:TM, :]

    return pl.pallas_call(
        body, name=name, grid=(Tp // TM,),
        in_specs=[pl.BlockSpec((TM, W), lambda i: (i, 0)), pl.BlockSpec((1, W), lambda i: (0, 0))],
        out_specs=pl.BlockSpec((TM, W), lambda i: (i, 0)), out_shape=jax.ShapeDtypeStruct((Tp, W), F32),
        scratch_shapes=[pltpu.VMEM((1, W), F32)], compiler_params=_cp(("arbitrary",)),
    )(fl, bias)


def forget_cumsum_bwd(dc, fl, bias, name):
    Tp, W = fl.shape
    nt = Tp // TM

    def body(dc_ref, fl_ref, b_ref, dfl_ref, db_ref, carry):
        i = pl.program_id(0)

        @pl.when(i == 0)
        def _():
            carry[...] = jnp.zeros_like(carry)

        dlogf = jnp.dot(_tri(TM, False), dc_ref[...], precision=lax.Precision.HIGHEST, preferred_element_type=F32) + carry[...]
        carry[...] = dlogf[0:1, :]
        dfl = dlogf * jax.nn.sigmoid(-(fl_ref[...] + b_ref[...]))
        dfl_ref[...] = dfl.astype(dfl_ref.dtype)
        s = jnp.sum(dfl, axis=0, keepdims=True)

        @pl.when(i == 0)
        def _():
            db_ref[...] = s

        @pl.when(i > 0)
        def _():
            db_ref[...] += s

    rev = lambda i: (nt - 1 - i, 0)
    return pl.pallas_call(
        body, name=name, grid=(nt,),
        in_specs=[pl.BlockSpec((TM, W), rev), pl.BlockSpec((TM, W), rev), pl.BlockSpec((1, W), lambda i: (0, 0))],
        out_specs=[pl.BlockSpec((TM, W), rev), pl.BlockSpec((1, W), lambda i: (0, 0))],
        out_shape=[jax.ShapeDtypeStruct((Tp, W), BF16), jax.ShapeDtypeStruct((1, W), F32)],
        scratch_shapes=[pltpu.VMEM((1, W), F32)], compiler_params=_cp(("arbitrary",)),
    )(dc, fl, bias)


_NT = (((1,), (1,)), ((), ()))
_TN = (((0,), (0,)), ((), ()))


def flash_fwd(q_aug, k_aug, v_aug, name):
    Tp = q_aug.shape[0]
    nb = Tp // TM

    def body(q_ref, k_ref, v_ref, o_ref, m_ref, acc_ref):
        i, j = pl.program_id(1), pl.program_id(2)

        @pl.when(j == 0)
        def _():
            m_ref[...] = jnp.full_like(m_ref, NEG)
            acc_ref[...] = jnp.zeros_like(acc_ref)

        def step(masked):
            s = lax.dot_general(q_ref[...], k_ref[...], _NT, preferred_element_type=F32)
            if masked:
                r = lax.broadcasted_iota(jnp.int32, (TM, TM), 0)
                c = lax.broadcasted_iota(jnp.int32, (TM, TM), 1)
                s = jnp.where(c <= r, s, NEG)
            m_old = m_ref[...]
            m_new = jnp.maximum(m_old, jnp.max(s, axis=-1, keepdims=True))
            p = jnp.exp(s - m_new)
            acc_ref[...] = jnp.exp(m_old - m_new) * acc_ref[...] + jnp.dot(p.astype(BF16), v_ref[...], preferred_element_type=F32)
            m_ref[...] = m_new

        @pl.when(j < i)
        def _():
            step(False)

        @pl.when(j == i)
        def _():
            step(True)
            acc = acc_ref[...]
            l = acc[:, ATT_DH:ATT_DH + 1]
            lse = m_ref[...] + jnp.log(l)
            lane = lax.broadcasted_iota(jnp.int32, (TM, AUG), 1)
            o_ref[...] = jnp.where(lane < ATT_DH, acc / l, lse)

    return pl.pallas_call(
        body, name=name, grid=(ATT_HEADS, nb, nb),
        in_specs=[pl.BlockSpec((TM, AUG), lambda h, i, j: (i, h)),
                  pl.BlockSpec((TM, AUG), lambda h, i, j: (jnp.minimum(j, i), h)),
                  pl.BlockSpec((TM, AUG), lambda h, i, j: (jnp.minimum(j, i), h))],
        out_specs=pl.BlockSpec((TM, AUG), lambda h, i, j: (i, h)),
        out_shape=jax.ShapeDtypeStruct((Tp, ATT_AUG), F32),
        scratch_shapes=[pltpu.VMEM((TM, 1), F32), pltpu.VMEM((TM, AUG), F32)],
        compiler_params=_cp(("parallel", "parallel", "arbitrary")),
    )(q_aug, k_aug, v_aug)


def flash_bwd(q_aug2, k_aug, v_aug, do_aug, name):
    Tp = q_aug2.shape[0]
    nb = Tp // TM

    def body(q_ref, do_ref, k_ref, v_ref, dq_ref, dk_ref, dv_ref, dk_acc, dv_acc):
        j, i = pl.program_id(1), pl.program_id(2)

        @pl.when(i == j)
        def _():
            dk_acc[...] = jnp.zeros_like(dk_acc)
            dv_acc[...] = jnp.zeros_like(dv_acc)

        def step(masked):
            q, do, k, v = q_ref[...], do_ref[...], k_ref[...], v_ref[...]
            st = lax.dot_general(k, q, _NT, preferred_element_type=F32)
            if masked:
                r = lax.broadcasted_iota(jnp.int32, (TM, TM), 0)
                c = lax.broadcasted_iota(jnp.int32, (TM, TM), 1)
                st = jnp.where(r <= c, st, NEG)
            pt = jnp.exp(st)
            dpt = lax.dot_general(v, do, _NT, preferred_element_type=F32)
            dst = (pt * dpt).astype(BF16)
            dv_acc[...] += jnp.dot(pt.astype(BF16), do, preferred_element_type=F32)
            dk_acc[...] += jnp.dot(dst, q, preferred_element_type=F32)
            dq = lax.dot_general(dst, k, _TN, preferred_element_type=F32)
            rows = pl.ds(pl.multiple_of(i * TM, TM), TM)

            @pl.when(j == 0)
            def _():
                dq_ref[rows, :] = dq

            @pl.when(j > 0)
            def _():
                dq_ref[rows, :] += dq

        @pl.when(i > j)
        def _():
            step(False)

        @pl.when(i == j)
        def _():
            step(True)

        @pl.when(i == nb - 1)
        def _():
            dk_ref[...] = dk_acc[...]
            dv_ref[...] = dv_acc[...]

    qmap = lambda h, j, i: (jnp.maximum(i, j), h)
    kmap = lambda h, j, i: (j, h)
    return pl.pallas_call(
        body, name=name, grid=(ATT_HEADS, nb, nb),
        in_specs=[pl.BlockSpec((TM, AUG), qmap), pl.BlockSpec((TM, AUG), qmap),
                  pl.BlockSpec((TM, AUG), kmap), pl.BlockSpec((TM, AUG), kmap)],
        out_specs=[pl.BlockSpec((Tp, AUG), lambda h, j, i: (0, h)), pl.BlockSpec((TM, AUG), kmap), pl.BlockSpec((TM, AUG), kmap)],
        out_shape=[jax.ShapeDtypeStruct((Tp, ATT_AUG), F32)] * 3,
        scratch_shapes=[pltpu.VMEM((TM, AUG), F32), pltpu.VMEM((TM, AUG), F32)],
        compiler_params=_cp(("parallel", "arbitrary", "arbitrary")),
    )(q_aug2, do_aug, k_aug, v_aug)


_BATCH = ((0,), (0,))


def _dg(x, y, cx, cy):
    return lax.dot_general(x.astype(BF16), y.astype(BF16), (((cx + 1,), (cy + 1,)), _BATCH), preferred_element_type=F32)


def _split2(x):
    hi = x.astype(BF16)
    return hi, (x - hi.astype(F32)).astype(BF16)


def _dg3(x, y, cx, cy):
    (xh, xl), (yh, yl) = _split2(x), _split2(y)
    d = lambda a, b: lax.dot_general(a, b, (((cx + 1,), (cy + 1,)), _BATCH), preferred_element_type=F32)
    return d(xh, yh) + (d(xl, yh) + d(xh, yl))


def _make_bdot(ca, cb, dg):
    @jax.custom_vjp
    def f(a, b):
        return dg(a, b, ca, cb)

    def fwd(a, b):
        return dg(a, b, ca, cb), (a, b)

    def bwd(res, ct):
        a, b = res
        da = dg(ct, b, 1, 1 if cb == 0 else 0) if ca == 1 else dg(b, ct, 1 if cb == 0 else 0, 1)
        db = dg(a, ct, 0 if ca == 1 else 1, 0) if cb == 0 else dg(ct, a, 0, 0 if ca == 1 else 1)
        return da, db

    f.defvjp(fwd, bwd)
    return f


_bd_nn, _bd_nt, _bd_tn = _make_bdot(1, 0, _dg), _make_bdot(1, 1, _dg), _make_bdot(0, 0, _dg)
_bd3_nn = _make_bdot(1, 0, _dg3)


@jax.custom_vjp
def _chunk_cumsum(x):
    return jnp.dot(_tri(x.shape[0], True), x, precision=lax.Precision.HIGHEST, preferred_element_type=F32)


def _chunk_cumsum_fwd(x):
    return _chunk_cumsum(x), None


def _chunk_cumsum_bwd(_, ct):
    return (jnp.dot(_tri(ct.shape[0], False), ct, precision=lax.Precision.HIGHEST, preferred_element_type=F32),)


_chunk_cumsum.defvjp(_chunk_cumsum_fwd, _chunk_cumsum_bwd)


def dn_chunk(S, q, k, v, bg):
    C = q.shape[0]
    H = DN_HEADS
    heads = lambda t: jnp.stack([t[:, DN_DH * h:DN_DH * (h + 1)] for h in range(H)])
    cols = lambda t, o: jnp.stack([t[:, o + h:o + h + 1] for h in range(H)])
    qh, kh, vh = heads(q), heads(k), heads(v)
    beta, gc = cols(bg, 0), cols(_chunk_cumsum(bg), H)
    r = lax.broadcasted_iota(jnp.int32, (H, C, C), 1)
    c = lax.broadcasted_iota(jnp.int32, (H, C, C), 2)
    gcb = jnp.broadcast_to(gc, (H, C, C))
    dec = jnp.exp(jnp.where(r >= c, gcb - jnp.swapaxes(gcb, 1, 2), NEG))
    e_gc = jnp.exp(gc)
    kq = _bd_nt(jnp.concatenate([kh, qh], axis=1), kh)
    A = jnp.where(r > c, kq[:, :C] * dec * beta, 0.0)
    qk = kq[:, C:] * dec
    X = -A
    P = _bd3_nn(X, X)
    N = X
    for i in range(5):
        NP = _bd3_nn(jnp.concatenate([N, P], axis=1) if i < 4 else N, P)
        N = N + P + NP[:, :C]
        if i < 4:
            P = NP[:, C:]
    R = jnp.concatenate([kh * (beta * e_gc), vh * beta], axis=2)
    WU = R + _bd3_nn(N, R)
    W, U0 = WU[:, :, :DN_DH], WU[:, :, DN_DH:]
    gl = gc[:, C - 1:C, :]
    WqS = _bd_nt(jnp.concatenate([W, qh * e_gc], axis=1), S)
    U = U0 - WqS[:, :C]
    O = WqS[:, C:] + _bd_nn(qk, U)
    S_new = jnp.exp(gl) * S + _bd_tn(U, kh * jnp.exp(gl - gc))
    return jnp.concatenate([O[h] for h in range(H)], axis=1), S_new


def dn_scan_fwd(qkv, bg, name):
    Tp = qkv.shape[0]
    n = Tp // DN_CHUNK
    tokspec = lambda w, c=0: pl.BlockSpec((DN_CHUNK, w), lambda i: (i, c))

    def body(q_ref, k_ref, v_ref, bg_ref, o_ref, s_ref, state):
        @pl.when(pl.program_id(0) == 0)
        def _():
            state[...] = jnp.zeros_like(state)

        S = state[...]
        s_ref[0] = S
        O, S_new = dn_chunk(S, q_ref[...].astype(F32), k_ref[...].astype(F32), v_ref[...].astype(F32), bg_ref[...])
        o_ref[...] = O.astype(o_ref.dtype)
        state[...] = S_new

    return pl.pallas_call(
        body, name=name, grid=(n,),
        in_specs=[tokspec(DN_HD, 0), tokspec(DN_HD, 1), tokspec(DN_HD, 2), tokspec(AUG)],
        out_specs=[tokspec(DN_HD), pl.BlockSpec((1, DN_HEADS, DN_DH, DN_DH), lambda i: (i, 0, 0, 0))],
        out_shape=[jax.ShapeDtypeStruct((Tp, DN_HD), BF16), jax.ShapeDtypeStruct((n, DN_HEADS, DN_DH, DN_DH), F32)],
        scratch_shapes=[pltpu.VMEM((DN_HEADS, DN_DH, DN_DH), F32)],
        compiler_params=_cp(("arbitrary",)),
    )(qkv, qkv, qkv, bg)


def dn_scan_bwd(qkv, bg, states, dO, name):
    Tp = qkv.shape[0]
    n = Tp // DN_CHUNK
    tokspec = lambda w, c=0: pl.BlockSpec((DN_CHUNK, w), lambda i: (n - 1 - i, c))

    def body(q_ref, k_ref, v_ref, bg_ref, s_ref, do_ref, dqkv_ref, dbg_ref, dstate):
        @pl.when(pl.program_id(0) == 0)
        def _():
            dstate[...] = jnp.zeros_like(dstate)

        _, vjp = jax.vjp(dn_chunk, s_ref[0], q_ref[...].astype(F32), k_ref[...].astype(F32), v_ref[...].astype(F32), bg_ref[...])
        dS, dq, dk, dv, dbg = vjp((do_ref[...].astype(F32), dstate[...]))
        dqkv_ref[...] = jnp.concatenate([dq, dk, dv], axis=1).astype(dqkv_ref.dtype)
        dbg_ref[...] = dbg
        dstate[...] = dS

    return pl.pallas_call(
        body, name=name, grid=(n,),
        in_specs=[tokspec(DN_HD, 0), tokspec(DN_HD, 1), tokspec(DN_HD, 2), tokspec(AUG),
                  pl.BlockSpec((1, DN_HEADS, DN_DH, DN_DH), lambda i: (n - 1 - i, 0, 0, 0)), tokspec(DN_HD)],
        out_specs=[tokspec(3 * DN_HD), tokspec(AUG)],
        out_shape=[jax.ShapeDtypeStruct((Tp, 3 * DN_HD), BF16), jax.ShapeDtypeStruct((Tp, AUG), F32)],
        scratch_shapes=[pltpu.VMEM((DN_HEADS, DN_DH, DN_DH), F32)],
        compiler_params=_cp(("arbitrary",)),
    )(qkv, qkv, qkv, bg, states, dO)


def loss_head(h, tgt, n_tok, name):
    Tp, Dm = h.shape

    def body(h_ref, t_ref, l_ref, dh_ref):
        i = pl.program_id(0)
        row = i * TM + lax.broadcasted_iota(jnp.int32, (TM, Dm), 0)
        e = jnp.where((row >= N_META) & (row < N_META + n_tok), h_ref[...] - t_ref[...], 0.0)
        dh_ref[...] = e * (1.0 / Dm)
        s = jnp.sum(e * e, axis=0, keepdims=True) * (0.5 / Dm)

        @pl.when(i == 0)
        def _():
            l_ref[...] = s

        @pl.when(i > 0)
        def _():
            l_ref[...] += s

    tile = pl.BlockSpec((TM, Dm), lambda i: (i, 0))
    return pl.pallas_call(
        body, name=name, grid=(Tp // TM,), in_specs=[tile, tile],
        out_specs=[pl.BlockSpec((1, Dm), lambda i: (0, 0)), tile],
        out_shape=[jax.ShapeDtypeStruct((1, Dm), F32), jax.ShapeDtypeStruct((Tp, Dm), F32)],
        compiler_params=_cp(("arbitrary",)),
    )(h, tgt)


def _row_block(R):
    return _pick(R, (256, 176, 128, 64, 32, 16, 8))


def sum_slots(parts, name):
    P, R, C = parts.shape
    rb = _row_block(R)

    def body(p_ref, o_ref):
        acc = p_ref[0].astype(F32)
        for s in range(1, P):
            acc = acc + p_ref[s].astype(F32)
        o_ref[...] = acc

    return pl.pallas_call(
        body, name=name, grid=(R // rb,), in_specs=[pl.BlockSpec((P, rb, C), lambda i: (0, i, 0))],
        out_specs=pl.BlockSpec((rb, C), lambda i: (i, 0)), out_shape=jax.ShapeDtypeStruct((R, C), F32),
        compiler_params=_cp(("parallel",)),
    )(parts)


def _adamw_math(w, g, m, v):
    m2 = ADAM_B1 * m + (1.0 - ADAM_B1) * g
    v2 = ADAM_B2 * v + (1.0 - ADAM_B2) * (g * g)
    m_hat = m2 / (1.0 - ADAM_B1 ** ADAM_STEP)
    v_hat = v2 / (1.0 - ADAM_B2 ** ADAM_STEP)
    delta = -ADAM_LR * (m_hat / (jnp.sqrt(v_hat) + ADAM_EPS) + ADAM_WD * w)
    return delta, m2, v2


def adamw_big(w, m, v, g_a, g_b, name):
    R, C = w.shape
    rb = _row_block(R)

    def body(w_ref, m_ref, v_ref, ga_ref, gb_ref, g_ref, d_ref, m2_ref, v2_ref):
        g = ga_ref[...] + gb_ref[...]
        delta, m2, v2 = _adamw_math(w_ref[...], g, m_ref[...], v_ref[...])
        g_ref[...], d_ref[...], m2_ref[...], v2_ref[...] = g, delta, m2, v2

    spec = pl.BlockSpec((rb, C), lambda i: (i, 0))
    return pl.pallas_call(
        body, name=name, grid=(R // rb,), in_specs=[spec] * 5, out_specs=[spec] * 4,
        out_shape=[jax.ShapeDtypeStruct((R, C), F32)] * 4, compiler_params=_cp(("parallel",)),
    )(w, m, v, g_a, g_b)


def adamw_small(ws, ms, vs, gs, name):
    n = len(ws)

    def body(*refs):
        w_r, m_r, v_r, g_r = refs[:n], refs[n:2 * n], refs[2 * n:3 * n], refs[3 * n:4 * n]
        d_o, m_o, v_o = refs[4 * n:5 * n], refs[5 * n:6 * n], refs[6 * n:7 * n]
        for k in range(n):
            delta, m2, v2 = _adamw_math(w_r[k][...], g_r[k][...], m_r[k][...], v_r[k][...])
            d_o[k][...], m_o[k][...], v_o[k][...] = delta, m2, v2

    shapes = [jax.ShapeDtypeStruct(w.shape, F32) for w in ws]
    res = pl.pallas_call(body, name=name, out_shape=shapes * 3, compiler_params=pltpu.CompilerParams(vmem_limit_bytes=VMEM_LIMIT))(
        *ws, *ms, *vs, *gs)
    return res[:n], res[n:2 * n], res[2 * n:]


def sum8(own4, sib4, name):
    _, R, C = own4.shape

    def body(a_ref, b_ref, o_ref):
        a = ((a_ref[0] + a_ref[1]) + a_ref[2]) + a_ref[3]
        b = ((b_ref[0] + b_ref[1]) + b_ref[2]) + b_ref[3]
        o_ref[...] = a + b

    return pl.pallas_call(body, name=name, out_shape=jax.ShapeDtypeStruct((R, C), F32),
                          compiler_params=pltpu.CompilerParams(vmem_limit_bytes=VMEM_LIMIT))(own4, sib4)


_HBM = pl.BlockSpec(memory_space=pltpu.HBM)


def _place():
    x, y, c = lax.axis_index("x"), lax.axis_index("y"), lax.axis_index("c")
    return x, y, c, [(1 - x, y), (x, 1 - y), (1 - x, 1 - y)]


def gather_chips(arrs, name):
    n = len(arrs)

    def body(*refs):
        ins, outs = refs[:n], refs[n:2 * n]
        send_sems, recv_sems, local_sems = refs[2 * n:]
        x, y, c, chips = _place()
        s = 2 * x + y
        copies = []
        for a in range(n):
            loc = pltpu.make_async_copy(ins[a], outs[a].at[s], local_sems.at[a])
            loc.start()
            copies.append(loc)
            for p, (px, py) in enumerate(chips):
                cp = pltpu.make_async_remote_copy(
                    src_ref=ins[a], dst_ref=outs[a].at[s], send_sem=send_sems.at[a, p], recv_sem=recv_sems.at[a, p],
                    device_id=(px, py, c), device_id_type=MESH_ID)
                cp.start()
                copies.append(cp)
        for cp in copies:
            cp.wait()

    return pl.pallas_call(
        body, name=name, in_specs=[_HBM] * n, out_specs=[_HBM] * n,
        out_shape=[jax.ShapeDtypeStruct((4,) + a.shape, a.dtype) for a in arrs],
        scratch_shapes=[pltpu.SemaphoreType.DMA((n, 3)), pltpu.SemaphoreType.DMA((n, 3)), pltpu.SemaphoreType.DMA((n,))],
    )(*arrs)


def scatter_chips(arrs, name):
    n = len(arrs)

    def body(*refs):
        ins, outs = refs[:n], refs[n:2 * n]
        send_sems, recv_sems, local_sems = refs[2 * n:]
        x, y, c, chips = _place()
        s = 2 * x + y
        copies = []
        for a in range(n):
            loc = pltpu.make_async_copy(ins[a].at[s], outs[a].at[s], local_sems.at[a])
            loc.start()
            copies.append(loc)
            for p, (px, py) in enumerate(chips):
                cp = pltpu.make_async_remote_copy(
                    src_ref=ins[a].at[2 * px + py], dst_ref=outs[a].at[s], send_sem=send_sems.at[a, p],
                    recv_sem=recv_sems.at[a, p], device_id=(px, py, c), device_id_type=MESH_ID)
                cp.start()
                copies.append(cp)
        for cp in copies:
            cp.wait()

    return pl.pallas_call(
        body, name=name, in_specs=[_HBM] * n, out_specs=[_HBM] * n,
        out_shape=[jax.ShapeDtypeStruct(a.shape, a.dtype) for a in arrs],
        scratch_shapes=[pltpu.SemaphoreType.DMA((n, 3)), pltpu.SemaphoreType.DMA((n, 3)), pltpu.SemaphoreType.DMA((n,))],
    )(*arrs)


def swap_cores(arrs, name):
    n = len(arrs)

    def body(*refs):
        ins, outs = refs[:n], refs[n:2 * n]
        send_sems, recv_sems = refs[2 * n:]
        x, y, c, _ = _place()
        copies = []
        for a in range(n):
            cp = pltpu.make_async_remote_copy(
                src_ref=ins[a], dst_ref=outs[a], send_sem=send_sems.at[a], recv_sem=recv_sems.at[a],
                device_id=(x, y, 1 - c), device_id_type=MESH_ID)
            cp.start()
            copies.append(cp)
        for cp in copies:
            cp.wait()

    return pl.pallas_call(
        body, name=name, in_specs=[_HBM] * n, out_specs=[_HBM] * n,
        out_shape=[jax.ShapeDtypeStruct(a.shape, a.dtype) for a in arrs],
        scratch_shapes=[pltpu.SemaphoreType.DMA((n,)), pltpu.SemaphoreType.DMA((n,))],
    )(*arrs)


def _il(w):
    lead = w.shape[:-1]
    return w.reshape(lead + (2, FFN_DIM // 256, 256)).swapaxes(-3, -2).reshape(lead + (2 * FFN_DIM,))


def _unil(w):
    lead = w.shape[:-1]
    return w.reshape(lead + (FFN_DIM // 256, 2, 256)).swapaxes(-3, -2).reshape(lead + (2 * FFN_DIM,))


def _pad_cols(w, n=AUG):
    return jnp.pad(w, ((0, 0), (0, n - w.shape[1])))


ATT_SPLIT = ATT_HEADS // 2


def attn_fwd(a, W, tag):
    proj = matmul(a, W["main"], "nn", BF16, f"{tag}_proj")
    fl = matmul(a, W["small"], "nn", F32, f"{tag}_fl")
    c = forget_cumsum(fl, W["bias"], f"{tag}_cumsum")
    cfix = tok(c, fixed=True)
    q_aug = rowwise(fn_attn_prep_q, [tok(proj, AUG, 0), cfix], [par(W["qg"])], [(2 * AUG, BF16)], f"{tag}_prepq", ATT_SPLIT, True)[0]
    k_aug = rowwise(fn_attn_prep_k, [tok(proj, AUG, ATT_SPLIT), cfix], [par(W["kg"])], [(2 * AUG, BF16)], f"{tag}_prepk", ATT_SPLIT, True)[0]
    v_aug = rowwise(fn_attn_prep_v, [tok(proj, AUG, 2 * ATT_SPLIT)], [], [(2 * AUG, BF16)], f"{tag}_prepv", ATT_SPLIT)[0]
    o_aug = flash_fwd(q_aug, k_aug, v_aug, f"{tag}_flash")
    go = rowwise(fn_attn_gate, [tok(o_aug, 2 * AUG, 0), tok(proj, AUG, 3 * ATT_SPLIT)], [], [(AUG, BF16)], f"{tag}_gate", ATT_SPLIT)[0]
    m = matmul(go, W["out"], "nn", BF16, f"{tag}_out")
    return m, (a, proj, fl, c, q_aug, k_aug, v_aug, o_aug, go)


def attn_bwd(dm, saved, W, tag):
    a, proj, fl, c, q_aug, k_aug, v_aug, o_aug, go = saved
    dgo = matmul(dm, W["out"], "nt", BF16, f"{tag}_dgo")
    d_out = matmul(go, dm, "tn", BF16, f"{tag}_dwout")
    do_aug, dog, q_aug2 = rowwise(
        fn_attn_bwd_prep, [tok(dgo, AUG, 0), tok(o_aug, 2 * AUG, 0), tok(proj, AUG, 3 * ATT_SPLIT), tok(q_aug, 2 * AUG, 0)], [],
        [(2 * AUG, BF16), (AUG, BF16), (2 * AUG, BF16)], f"{tag}_bprep", ATT_SPLIT)
    dq_aug, dk_aug, dv_aug = flash_bwd(q_aug2, k_aug, v_aug, do_aug, f"{tag}_flashb")
    dc = rowwise(fn_attn_dc, [tok(dq_aug), tok(dk_aug)], [], [(AUG, F32)], f"{tag}_dc")[0]
    cfix = tok(c, fixed=True)
    (dq,), (d_qg,) = rowwise_bwd(fn_attn_prep_q, [tok(proj, AUG, 0, grad=BF16), cfix], [par(W["qg"])], [dq_aug],
                                 f"{tag}_prepqb", nsplit=ATT_SPLIT, with_j=True)
    (dk,), (d_kg,) = rowwise_bwd(fn_attn_prep_k, [tok(proj, AUG, ATT_SPLIT, grad=BF16), cfix], [par(W["kg"])], [dk_aug],
                                 f"{tag}_prepkb", nsplit=ATT_SPLIT, with_j=True)
    (dv,), _ = rowwise_bwd(fn_attn_prep_v, [tok(proj, AUG, 2 * ATT_SPLIT, grad=BF16)], [], [dv_aug], f"{tag}_prepvb", nsplit=ATT_SPLIT)
    dfl, d_bias = forget_cumsum_bwd(dc, fl, W["bias"], f"{tag}_cumsumb")
    dproj = jnp.concatenate([dq, dk, dv, dog], axis=1)
    da = matmul(dproj, W["main"], "nt", BF16, f"{tag}_da", add=matmul(dfl, W["small"], "nt", F32, f"{tag}_da0"))
    d_in = jnp.concatenate([matmul(a, dproj, "tn", BF16, f"{tag}_dw"), matmul(a, dfl, "tn", BF16, f"{tag}_dw2")[:, :ATT_HEADS]], axis=1)
    return da, dict(w_in=d_in, w_out=d_out, b_forget=d_bias[0, :ATT_HEADS], q_norm=d_qg[0], k_norm=d_kg[0])


DN_SPLIT = 3 * DN_HD // 256


def dn_fwd(a, W, tag):
    proj = matmul(a, W["main"], "nn", BF16, f"{tag}_proj")
    lg = matmul(a, W["small"], "nn", F32, f"{tag}_lg")
    qkv = rowwise(fn_dn_prep, [tok(proj, 256, 0, conv=(W["conv"], 0))], [], [(256, BF16)], f"{tag}_prep", DN_SPLIT, True)[0]
    bg = rowwise(fn_dn_gates, [tok(lg)], [par(W["alog"]), par(W["dtb"])], [(AUG, F32)], f"{tag}_gates")[0]
    O, states = dn_scan_fwd(qkv, bg, f"{tag}_scan")
    go = rowwise(fn_dn_post, [tok(O), tok(proj, DN_HD, 3)], [par(W["ogain"])], [(DN_HD, BF16)], f"{tag}_post")[0]
    m = matmul(go, W["out"], "nn", BF16, f"{tag}_out")
    return m, (a, proj, lg, qkv, bg, O, states, go)


def dn_bwd(dm, saved, W, tag):
    a, proj, lg, qkv, bg, O, states, go = saved
    dgo = matmul(dm, W["out"], "nt", BF16, f"{tag}_dgo")
    d_out = matmul(go, dm, "tn", BF16, f"{tag}_dwout")
    (dO, dog), (d_ogain,) = rowwise_bwd(fn_dn_post, [tok(O, grad=BF16), tok(proj, DN_HD, 3, grad=BF16)], [par(W["ogain"])],
                                        [dgo], f"{tag}_postb")
    dqkv_n, dbg = dn_scan_bwd(qkv, bg, states, dO, f"{tag}_scanb")
    (dy,), _ = rowwise_bwd(fn_dn_prep, [tok(proj, 256, 0, conv=(W["conv"], 0), grad=BF16)], [], [dqkv_n], f"{tag}_prepb",
                           nsplit=DN_SPLIT, with_j=True)
    dqkv, d_conv = conv_bwd(dy, proj, W["conv"], f"{tag}_convb")
    (dlg,), (d_alog, d_dtb) = rowwise_bwd(fn_dn_gates, [tok(lg, grad=BF16)], [par(W["alog"]), par(W["dtb"])], [dbg], f"{tag}_gatesb")
    dproj = jnp.concatenate([dqkv, dog], axis=1)
    da = matmul(dproj, W["main"], "nt", BF16, f"{tag}_da", add=matmul(dlg, W["small"], "nt", F32, f"{tag}_da0"))
    d_in = jnp.concatenate([matmul(a, dproj, "tn", BF16, f"{tag}_dw"), matmul(a, dlg, "tn", BF16, f"{tag}_dw2")[:, :2 * DN_HEADS]], axis=1)
    return da, dict(w_in=d_in, w_out=d_out, conv=d_conv, a_log=d_alog[0, DN_HEADS:2 * DN_HEADS],
                    dt_bias=d_dtb[0, DN_HEADS:2 * DN_HEADS], o_norm=d_ogain[0])


FFN_SPLIT = FFN_DIM // 256


def ffn_fwd(b, W, tag):
    u0 = matmul(b, W["up"], "nn", BF16, f"{tag}_up")
    act = rowwise(fn_geglu, [tok(u0, 512, 0, conv=(W["conv"], 0))], [], [(256, BF16)], f"{tag}_act", nsplit=FFN_SPLIT)[0]
    f = matmul(act, W["down"], "nn", BF16, f"{tag}_down")
    return f, (b, u0, act)


def ffn_bwd(df, saved, W, tag):
    b, u0, act = saved
    dact = matmul(df, W["down"], "nt", BF16, f"{tag}_dact")
    d_down = matmul(act, df, "tn", BF16, f"{tag}_dwdown")
    (du,), _ = rowwise_bwd(fn_geglu, [tok(u0, 512, 0, conv=(W["conv"], 0), grad=BF16)], [], [dact], f"{tag}_actb", nsplit=FFN_SPLIT)
    du0, d_conv = conv_bwd(du, u0, W["conv"], f"{tag}_convb")
    db = matmul(du0, W["up"], "nt", BF16, f"{tag}_db")
    d_up = matmul(b, du0, "tn", BF16, f"{tag}_dwup")
    return db, dict(up=_unil(d_up), conv=_unil(d_conv), down=d_down)


def _cols_from_slots(g, L):
    K, Cs = g.shape[1] // L, g.shape[2]
    return g.reshape(4, L, K, Cs).transpose(1, 2, 0, 3).reshape(L, K, 4 * Cs)


def _rows_from_slots(g, L):
    Rs, C = g.shape[1] // L, g.shape[2]
    return g.reshape(4, L, Rs, C).transpose(1, 0, 2, 3).reshape(L, 4 * Rs, C)


def _cols_to_slots(w):
    L, K, C = w.shape
    return w.reshape(L, K, 4, C // 4).transpose(2, 0, 1, 3).reshape(4, L * K, C // 4)


def _rows_to_slots(w):
    L, R, C = w.shape
    return w.reshape(L, 4, R // 4, C).transpose(1, 0, 2, 3).reshape(4, L * (R // 4), C)


SMALL_NAMES = ["meta_tokens", "norm_mix_pre", "norm_mix_post", "norm_ffn_pre", "norm_ffn_post", "attn_b_forget", "attn_q_norm",
               "attn_k_norm", "dn_conv", "dn_a_log", "dn_dt_bias", "dn_o_norm", "ffn_conv"]
BIG_NAMES = ["attn_w_in", "attn_w_out", "dn_w_in", "dn_w_out", "ffn_w_up", "ffn_w_down"]
WEIGHT_NAMES = ["meta_tokens", "norm_mix_pre", "norm_mix_post", "norm_ffn_pre", "norm_ffn_post", "attn_w_in", "attn_b_forget",
                "attn_q_norm", "attn_k_norm", "attn_w_out", "dn_w_in", "dn_conv", "dn_a_log", "dn_dt_bias", "dn_o_norm", "dn_w_out",
                "ffn_w_up", "ffn_conv", "ffn_w_down"]
PACK_COLS = 1024


def train_step(x, loss_target, w, m, v):
    L = x.shape[1]
    T = L + N_META
    Tp = -(-T // TM) * TM
    xi, yi = lax.axis_index("x"), lax.axis_index("y")
    slot = 2 * xi + yi
    as2d = lambda t: t.reshape(-1, t.shape[-1])

    gat = gather_chips([as2d(w[n]).astype(BF16) for n in BIG_NAMES] + [w["meta_tokens"], as2d(w["dn_conv"]), as2d(w["ffn_conv"])],
                       "gather_weights")
    attn_in, dn_in = _cols_from_slots(gat[0], 2), _cols_from_slots(gat[2], 2)
    attn_out, dn_out = _rows_from_slots(gat[1], 2), _rows_from_slots(gat[3], 2)
    ffn_up = _il(_cols_from_slots(gat[4], DEPTH))
    ffn_down = _rows_from_slots(gat[5], DEPTH)
    meta = gat[6].transpose(1, 0, 2).reshape(N_META, D_MODEL)
    dn_conv = _cols_from_slots(gat[7], 2)
    ffn_conv = _il(_cols_from_slots(gat[8], DEPTH))
    lane8 = lambda t: jnp.pad(t[None, :], ((0, 0), (DN_HEADS, AUG - 2 * DN_HEADS)))
    WA = [dict(main=attn_in[j, :, :4 * ATT_HD],
               small=_pad_cols(attn_in[j, :, 4 * ATT_HD:]), bias=_pad_cols(w["attn_b_forget"][j][None, :]),
               qg=w["attn_q_norm"][j][None, :], kg=w["attn_k_norm"][j][None, :], out=attn_out[j]) for j in range(2)]
    WD = [dict(main=dn_in[j, :, :4 * DN_HD],
               small=_pad_cols(dn_in[j, :, 4 * DN_HD:]), conv=dn_conv[j], alog=lane8(w["dn_a_log"][j]), dtb=lane8(w["dn_dt_bias"][j]),
               ogain=w["dn_o_norm"][j][None, :], out=dn_out[j]) for j in range(2)]
    WF = [dict(up=ffn_up[i], conv=ffn_conv[i], down=ffn_down[i]) for i in range(DEPTH)]
    row = lambda t, i: t[i][None, :]

    h = jnp.concatenate([meta, x[0], jnp.zeros((Tp - T, D_MODEL), F32)], axis=0)
    tgt = jnp.concatenate([jnp.zeros((N_META, D_MODEL), F32), loss_target[0], jnp.zeros((Tp - T, D_MODEL), F32)], axis=0)
    h0 = h
    a = rowwise(fn_norm, [tok(h)], [par(row(w["norm_mix_pre"], 0))], [(D_MODEL, BF16)], "norm0")[0]
    saved = []
    for i in range(DEPTH):
        j = i // 2
        if i % 2 == 0:
            mix, sv_mix = attn_fwd(a, WA[j], f"attn{j}")
        else:
            mix, sv_mix = dn_fwd(a, WD[j], f"dn{j}")
        h_mid, b = rowwise(fn_resid_norm, [tok(h), tok(mix)], [par(row(w["norm_mix_post"], i)), par(row(w["norm_ffn_pre"], i))],
                           [(D_MODEL, F32), (D_MODEL, BF16)], f"resid_mix{i}")
        f, sv_ffn = ffn_fwd(b, WF[i], f"ffn{i}")
        if i < DEPTH - 1:
            h_out, a = rowwise(fn_resid_norm, [tok(h_mid), tok(f)], [par(row(w["norm_ffn_post"], i)), par(row(w["norm_mix_pre"], i + 1))],
                               [(D_MODEL, F32), (D_MODEL, BF16)], f"resid_ffn{i}")
        else:
            h_out = rowwise(fn_resid, [tok(h_mid), tok(f)], [par(row(w["norm_ffn_post"], i))], [(D_MODEL, F32)], f"resid_ffn{i}")[0]
        saved.append((h, mix, sv_mix, h_mid, f, sv_ffn))
        h = h_out
    lvec, dh = loss_head(h, tgt, L, "loss_head")
    loss = lax.psum(jnp.sum(lvec), ("x", "y", "c"))

    gn = {n: [None] * DEPTH for n in ("norm_mix_pre", "norm_mix_post", "norm_ffn_pre", "norm_ffn_post")}
    g_attn, g_dn, g_ffn = [None, None], [None, None], [None] * DEPTH
    da = None
    for i in reversed(range(DEPTH)):
        j = i // 2
        h_in, mix, sv_mix, h_mid, f, sv_ffn = saved[i]
        if i == DEPTH - 1:
            (dh, df), (gn["norm_ffn_post"][i],) = rowwise_bwd(
                fn_resid, [tok(h_mid, grad=F32), tok(f, grad=BF16)], [par(row(w["norm_ffn_post"], i))], [dh], f"resid_ffn{i}_b")
        else:
            (dh, df), (gn["norm_ffn_post"][i], gn["norm_mix_pre"][i + 1]) = rowwise_bwd(
                fn_resid_norm, [tok(h_mid, grad=F32), tok(f, grad=BF16)],
                [par(row(w["norm_ffn_post"], i)), par(row(w["norm_mix_pre"], i + 1))], [dh, da], f"resid_ffn{i}_b")
        db, g_ffn[i] = ffn_bwd(df, sv_ffn, WF[i], f"ffn{i}")
        (dh, dm), (gn["norm_mix_post"][i], gn["norm_ffn_pre"][i]) = rowwise_bwd(
            fn_resid_norm, [tok(h_in, grad=F32), tok(mix, grad=BF16)],
            [par(row(w["norm_mix_post"], i)), par(row(w["norm_ffn_pre"], i))], [dh, db], f"resid_mix{i}_b")
        if i % 2 == 0:
            da, g_attn[j] = attn_bwd(dm, sv_mix, WA[j], f"attn{j}")
        else:
            da, g_dn[j] = dn_bwd(dm, sv_mix, WD[j], f"dn{j}")
    (dh0,), (gn["norm_mix_pre"][0],) = rowwise_bwd(fn_id_norm, [tok(h0, grad=F32)], [par(row(w["norm_mix_pre"], 0))], [dh, da], "norm0_b")
    grad_x = dh0[N_META:T][None]

    full = dict(
        meta_tokens=dh0[:N_META],
        **{n: jnp.concatenate(gn[n], axis=0) for n in gn},
        attn_b_forget=jnp.stack([g["b_forget"] for g in g_attn]), attn_q_norm=jnp.stack([g["q_norm"] for g in g_attn]),
        attn_k_norm=jnp.stack([g["k_norm"] for g in g_attn]),
        dn_conv=jnp.stack([g["conv"] for g in g_dn]), dn_a_log=jnp.stack([g["a_log"] for g in g_dn]),
        dn_dt_bias=jnp.stack([g["dt_bias"] for g in g_dn]), dn_o_norm=jnp.stack([g["o_norm"] for g in g_dn]),
        ffn_conv=jnp.stack([g["conv"] for g in g_ffn]))
    flat = jnp.concatenate([full[n].reshape(-1) for n in SMALL_NAMES])
    rows = -(-flat.shape[0] // PACK_COLS)
    rows = -(-rows // 8) * 8
    pack = jnp.pad(flat, (0, rows * PACK_COLS - flat.shape[0])).reshape(rows, PACK_COLS)
    own4 = gather_chips([pack], "gather_small_grads")[0]
    sib4 = swap_cores([own4], "swap_small_grads")[0]
    tot = sum8(own4, sib4, "sum_small_grads").reshape(-1)
    grads, off = {}, 0
    for n in SMALL_NAMES:
        size = full[n].size
        g = tot[off:off + size].reshape(full[n].shape)
        off += size
        if g.shape != w[n].shape:
            g = lax.dynamic_slice_in_dim(g, slot * w[n].shape[-1], w[n].shape[-1], axis=g.ndim - 1)
        grads[n] = g
    d_s, m_s, v_s = adamw_small([as2d(w[n]) for n in SMALL_NAMES], [as2d(m[n]) for n in SMALL_NAMES],
                                [as2d(v[n]) for n in SMALL_NAMES], [as2d(grads[n]) for n in SMALL_NAMES], "adamw_small")
    deltas = {n: d.reshape(w[n].shape) for n, d in zip(SMALL_NAMES, d_s, strict=True)}
    new_m = {n: d.reshape(w[n].shape) for n, d in zip(SMALL_NAMES, m_s, strict=True)}
    new_v = {n: d.reshape(w[n].shape) for n, d in zip(SMALL_NAMES, v_s, strict=True)}

    send = [_cols_to_slots(jnp.stack([g["w_in"] for g in g_attn])), _rows_to_slots(jnp.stack([g["w_out"] for g in g_attn])),
            _cols_to_slots(jnp.stack([g["w_in"] for g in g_dn])), _rows_to_slots(jnp.stack([g["w_out"] for g in g_dn])),
            _cols_to_slots(jnp.stack([g["up"] for g in g_ffn])), _rows_to_slots(jnp.stack([g["down"] for g in g_ffn]))]
    recv = scatter_chips(send, "scatter_grads")
    part = [sum_slots(r, f"sum_{n}") for n, r in zip(BIG_NAMES, recv, strict=True)]
    other = swap_cores(part, "swap_grads")
    for n, pa, pb in zip(BIG_NAMES, part, other, strict=True):
        g, d, m2, v2 = adamw_big(as2d(w[n]), as2d(m[n]), as2d(v[n]), pa, pb, f"adamw_{n}")
        grads[n], deltas[n], new_m[n], new_v[n] = (t.reshape(w[n].shape) for t in (g, d, m2, v2))
    return loss, grad_x, grads, deltas, new_m, new_v


def kernel(x, meta_tokens, norm_mix_pre, norm_mix_post, norm_ffn_pre, norm_ffn_post, attn_w_in, attn_b_forget, attn_q_norm, attn_k_norm, attn_w_out, dn_w_in, dn_conv, dn_a_log, dn_dt_bias, dn_o_norm, dn_w_out, ffn_w_up, ffn_conv, ffn_w_down, loss_target, m_meta_tokens, m_norm_mix_pre, m_norm_mix_post, m_norm_ffn_pre, m_norm_ffn_post, m_attn_w_in, m_attn_b_forget, m_attn_q_norm, m_attn_k_norm, m_attn_w_out, m_dn_w_in, m_dn_conv, m_dn_a_log, m_dn_dt_bias, m_dn_o_norm, m_dn_w_out, m_ffn_w_up, m_ffn_conv, m_ffn_w_down, v_meta_tokens, v_norm_mix_pre, v_norm_mix_post, v_norm_ffn_pre, v_norm_ffn_post, v_attn_w_in, v_attn_b_forget, v_attn_q_norm, v_attn_k_norm, v_attn_w_out, v_dn_w_in, v_dn_conv, v_dn_a_log, v_dn_dt_bias, v_dn_o_norm, v_dn_w_out, v_ffn_w_up, v_ffn_conv, v_ffn_w_down):
    w = dict(meta_tokens=meta_tokens, norm_mix_pre=norm_mix_pre, norm_mix_post=norm_mix_post, norm_ffn_pre=norm_ffn_pre, norm_ffn_post=norm_ffn_post, attn_w_in=attn_w_in, attn_b_forget=attn_b_forget, attn_q_norm=attn_q_norm, attn_k_norm=attn_k_norm, attn_w_out=attn_w_out, dn_w_in=dn_w_in, dn_conv=dn_conv, dn_a_log=dn_a_log, dn_dt_bias=dn_dt_bias, dn_o_norm=dn_o_norm, dn_w_out=dn_w_out, ffn_w_up=ffn_w_up, ffn_conv=ffn_conv, ffn_w_down=ffn_w_down)
    m = dict(meta_tokens=m_meta_tokens, norm_mix_pre=m_norm_mix_pre, norm_mix_post=m_norm_mix_post, norm_ffn_pre=m_norm_ffn_pre, norm_ffn_post=m_norm_ffn_post, attn_w_in=m_attn_w_in, attn_b_forget=m_attn_b_forget, attn_q_norm=m_attn_q_norm, attn_k_norm=m_attn_k_norm, attn_w_out=m_attn_w_out, dn_w_in=m_dn_w_in, dn_conv=m_dn_conv, dn_a_log=m_dn_a_log, dn_dt_bias=m_dn_dt_bias, dn_o_norm=m_dn_o_norm, dn_w_out=m_dn_w_out, ffn_w_up=m_ffn_w_up, ffn_conv=m_ffn_conv, ffn_w_down=m_ffn_w_down)
    v = dict(meta_tokens=v_meta_tokens, norm_mix_pre=v_norm_mix_pre, norm_mix_post=v_norm_mix_post, norm_ffn_pre=v_norm_ffn_pre, norm_ffn_post=v_norm_ffn_post, attn_w_in=v_attn_w_in, attn_b_forget=v_attn_b_forget, attn_q_norm=v_attn_q_norm, attn_k_norm=v_attn_k_norm, attn_w_out=v_attn_w_out, dn_w_in=v_dn_w_in, dn_conv=v_dn_conv, dn_a_log=v_dn_a_log, dn_dt_bias=v_dn_dt_bias, dn_o_norm=v_dn_o_norm, dn_w_out=v_dn_w_out, ffn_w_up=v_ffn_w_up, ffn_conv=v_ffn_conv, ffn_w_down=v_ffn_w_down)
    loss, grad_x, grads, deltas, new_m, new_v = train_step(x, loss_target, w, m, v)
    return (loss, grad_x, *[grads[n] for n in WEIGHT_NAMES], *[deltas[n] for n in WEIGHT_NAMES],
            *[new_m[n] for n in WEIGHT_NAMES], *[new_v[n] for n in WEIGHT_NAMES])
```

```python
import functools

import jax
import jax.numpy as jnp
from jax import lax
from jax.experimental import pallas as pl
from jax.experimental.pallas import tpu as pltpu

F32, BF16 = jnp.float32, jnp.bfloat16

D_MODEL = 1024
N_META = 16
ATT_HEADS, ATT_DH = 16, 64
ATT_HD = ATT_HEADS * ATT_DH
AUG = 128
ATT_AUG = ATT_HEADS * AUG
DN_HEADS, DN_DH = 8, 128
DN_HD = DN_HEADS * DN_DH
DN_CHUNK = 64
FFN_DIM = 2816
DEPTH = 4
EPS = 1e-6
NEG = -1e30

ADAM_LR, ADAM_B1, ADAM_B2, ADAM_EPS, ADAM_WD, ADAM_STEP = 0.001, 0.9, 0.999, 1e-08, 0.01, 10

TM = 640
HALO = 8
VMEM_LIMIT = 52 * 1024 * 1024
MESH_ID = pl.DeviceIdType.MESH


def _cp(sem):
    return pltpu.CompilerParams(dimension_semantics=sem, vmem_limit_bytes=VMEM_LIMIT)


def _pick(n, cands):
    for c in cands:
        if n % c == 0:
            return c
    return n


def matmul(a, b, mode, out_dtype, name, add=None):
    if mode == "nn":
        (M, K), (_, N) = a.shape, b.shape
    elif mode == "nt":
        (M, K), (N, _) = a.shape, b.shape
    else:
        (K, M), (_, N) = a.shape, b.shape
    tm = _pick(M, (TM, 1024, 1408, 512, 256, 128))
    tn = _pick(N, (1024, 1408, 512, 256, 128))
    tk = _pick(K, (TM, 1024, 1408, 512, 256, 128))
    nk = K // tk
    if mode == "nn":
        a_spec = pl.BlockSpec((tm, tk), lambda i, j, k: (i, k))
        b_spec = pl.BlockSpec((tk, tn), lambda i, j, k: (k, j))
        dims = (((1,), (0,)), ((), ()))
    elif mode == "nt":
        a_spec = pl.BlockSpec((tm, tk), lambda i, j, k: (i, k))
        b_spec = pl.BlockSpec((tn, tk), lambda i, j, k: (j, k))
        dims = (((1,), (1,)), ((), ()))
    else:
        a_spec = pl.BlockSpec((tk, tm), lambda i, j, k: (k, i))
        b_spec = pl.BlockSpec((tk, tn), lambda i, j, k: (k, j))
        dims = (((0,), (0,)), ((), ()))
    o_spec = pl.BlockSpec((tm, tn), lambda i, j, k: (i, j))
    has_add = add is not None

    def body(*refs):
        if has_add:
            a_ref, b_ref, add_ref, o_ref, acc_ref = refs
        else:
            a_ref, b_ref, o_ref, acc_ref = refs
        k = pl.program_id(2)
        part = lax.dot_general(a_ref[...], b_ref[...], dims, preferred_element_type=F32)

        @pl.when(k == 0)
        def _():
            acc_ref[...] = part

        @pl.when(k > 0)
        def _():
            acc_ref[...] += part

        @pl.when(k == nk - 1)
        def _():
            r = acc_ref[...]
            if has_add:
                r = r + add_ref[...].astype(F32)
            o_ref[...] = r.astype(out_dtype)

    in_specs = [a_spec, b_spec] + ([o_spec] if has_add else [])
    args = (a, b) + ((add,) if has_add else ())
    return pl.pallas_call(
        body, name=name, grid=(M // tm, N // tn, nk), in_specs=in_specs, out_specs=o_spec,
        out_shape=jax.ShapeDtypeStruct((M, N), out_dtype),
        scratch_shapes=[pltpu.VMEM((tm, tn), F32)],
        compiler_params=_cp(("parallel", "parallel", "arbitrary")),
    )(*args)


def tok(arr, width=None, col=0, conv=None, grad=None, fixed=False):
    return dict(arr=arr, w=arr.shape[1] if width is None else width, col=col, conv=conv, grad=grad, step=0 if fixed else 1)


def par(arr, width=None, col=0):
    return dict(arr=arr, w=arr.shape[1] if width is None else width, col=col)


def _conv_apply(x, halo, w, ext_ref):
    K = w.shape[0]
    rows = x.shape[0]
    ext_ref[0:HALO, :] = halo
    ext_ref[HALO:, :] = x
    y = w[K - 1:K, :] * x
    for k in range(K - 1):
        y = y + w[k:k + 1, :] * ext_ref[pl.ds(HALO - (K - 1) + k, rows), :]
    return y


def _row_specs(toks, pars):
    specs, args = [], []
    for t in toks:
        specs.append(pl.BlockSpec((TM, t["w"]), functools.partial(lambda i, j, c, st: (i, c + st * j), c=t["col"], st=t["step"])))
        args.append(t["arr"])
        if t["conv"] is not None:
            specs.append(pl.BlockSpec((HALO, t["w"]), functools.partial(
                lambda i, j, c: (jnp.maximum(i * (TM // HALO) - 1, 0), c + j), c=t["col"])))
            args.append(t["arr"])
            cw, ccol = t["conv"]
            specs.append(pl.BlockSpec((cw.shape[0], t["w"]), functools.partial(lambda i, j, c: (0, c + j), c=ccol)))
            args.append(cw)
    for p in pars:
        specs.append(pl.BlockSpec((p["arr"].shape[0], p["w"]), functools.partial(lambda i, j, c: (0, c), c=p["col"])))
        args.append(p["arr"])
    return specs, args


def _row_load(toks, pars, refs, ext_refs):
    i = pl.program_id(0)
    vals, n, e = [], 0, 0
    for t in toks:
        x = refs[n][...].astype(F32)
        n += 1
        if t["conv"] is not None:
            halo = jnp.where(i == 0, 0.0, refs[n][...].astype(F32))
            w = refs[n + 1][...]
            n += 2
            x = _conv_apply(x, halo, w, ext_refs[e])
            e += 1
        vals.append(x)
    for _ in pars:
        vals.append(refs[n][...])
        n += 1
    return vals, n


def rowwise(fn, toks, pars, outs, name, nsplit=1, with_j=False):
    Tp = toks[0]["arr"].shape[0]
    specs, args = _row_specs(toks, pars)
    n_ext = sum(t["conv"] is not None for t in toks)

    def body(*refs):
        ext_refs = refs[len(refs) - n_ext:]
        vals, n = _row_load(toks, pars, refs, ext_refs)
        res = fn(pl.program_id(1), *vals) if with_j else fn(*vals)
        for r, o_ref in zip(res, refs[n:n + len(outs)], strict=True):
            o_ref[...] = r.astype(o_ref.dtype)

    return pl.pallas_call(
        body, name=name, grid=(Tp // TM, nsplit), in_specs=specs,
        out_specs=[pl.BlockSpec((TM, w), lambda i, j: (i, j)) for w, _ in outs],
        out_shape=[jax.ShapeDtypeStruct((Tp, w * nsplit), dt) for w, dt in outs],
        scratch_shapes=[pltpu.VMEM((TM + HALO, t["w"]), F32) for t in toks if t["conv"] is not None],
        compiler_params=_cp(("parallel", "parallel")),
    )(*args)


def rowwise_bwd(fn, toks, pars, cts, name, groups=None, par_grads=None, nsplit=1, with_j=False):
    Tp = toks[0]["arr"].shape[0]
    specs, args = _row_specs(toks, pars)
    n_ext = sum(t["conv"] is not None for t in toks)
    gidx = [k for k, t in enumerate(toks) if t["grad"] is not None]
    if groups is None:
        groups = [[k] for k in range(len(gidx))]
    par_grads = list(range(len(pars))) if par_grads is None else par_grads
    ct_specs, ct_args = [], []
    for c in cts:
        if c is not None:
            ct_specs.append(pl.BlockSpec((TM, c.shape[1] // nsplit), lambda i, j: (i, j)))
            ct_args.append(c)
    out_shapes, out_specs = [], []
    for g in groups:
        w = sum(toks[gidx[k]]["w"] for k in g)
        out_shapes.append(jax.ShapeDtypeStruct((Tp, w * nsplit), toks[gidx[g[0]]]["grad"]))
        out_specs.append(pl.BlockSpec((TM, w), lambda i, j: (i, j)))
    for k in par_grads:
        shp = (pars[k]["arr"].shape[0], pars[k]["w"])
        out_shapes.append(jax.ShapeDtypeStruct(shp, F32))
        out_specs.append(pl.BlockSpec(shp, lambda i, j: (0, 0)))
    n_ct = len(ct_args)

    def body(*refs):
        first = (pl.program_id(0) == 0) & (pl.program_id(1) == 0)
        ext_refs = refs[len(refs) - n_ext:]
        vals, n = _row_load(toks, pars, refs, ext_refs)
        ct_refs = refs[n:n + n_ct]
        o_refs = refs[n + n_ct:len(refs) - n_ext]
        res, vjp = jax.vjp(functools.partial(fn, pl.program_id(1)) if with_j else fn, *vals)
        ct_vals, c = [], 0
        for r, ct in zip(res, cts, strict=True):
            if ct is None:
                ct_vals.append(jnp.zeros_like(r))
            else:
                ct_vals.append(ct_refs[c][...].astype(F32))
                c += 1
        grads = vjp(tuple(ct_vals))
        for g, o_ref in zip(groups, o_refs[:len(groups)], strict=True):
            pieces = [grads[gidx[k]] for k in g]
            val = pieces[0] if len(pieces) == 1 else jnp.concatenate(pieces, axis=1)
            o_ref[...] = val.astype(o_ref.dtype)
        for k, o_ref in zip(par_grads, o_refs[len(groups):], strict=True):
            gk = grads[len(toks) + k]

            @pl.when(first)
            def _(o_ref=o_ref, gk=gk):
                o_ref[...] = gk

            @pl.when(jnp.logical_not(first))
            def _(o_ref=o_ref, gk=gk):
                o_ref[...] += gk

    res = pl.pallas_call(
        body, name=name, grid=(Tp // TM, nsplit), in_specs=specs + ct_specs, out_specs=out_specs, out_shape=out_shapes,
        scratch_shapes=[pltpu.VMEM((TM + HALO, t["w"]), F32) for t in toks if t["conv"] is not None],
        compiler_params=_cp(("arbitrary", "arbitrary")),
    )(*args, *ct_args)
    return res[:len(groups)], res[len(groups):]


def conv_bwd(dy, x, w, name):
    Tp, W = dy.shape
    K = w.shape[0]
    wb = _pick(W, (1024, 1408, 512))
    nt = Tp // TM

    def body(dy_ref, dyn_ref, x_ref, xh_ref, w_ref, dx_ref, dw_ref, ext_dy, ext_x):
        i = pl.program_id(1)
        dyv = dy_ref[...].astype(F32)
        ext_dy[0:TM, :] = dyv
        ext_dy[TM:, :] = jnp.where(i == nt - 1, 0.0, dyn_ref[...].astype(F32))
        ext_x[0:HALO, :] = jnp.where(i == 0, 0.0, xh_ref[...].astype(F32))
        ext_x[HALO:, :] = x_ref[...].astype(F32)
        wv = w_ref[...]
        dx = wv[K - 1:K, :] * dyv
        for k in range(K - 1):
            dx = dx + wv[k:k + 1, :] * ext_dy[pl.ds(K - 1 - k, TM), :]
        dx_ref[...] = dx.astype(dx_ref.dtype)
        rows = [jnp.sum(dyv * ext_x[pl.ds(HALO - (K - 1) + k, TM), :], axis=0, keepdims=True) for k in range(K)]
        dwv = jnp.concatenate(rows, axis=0)

        @pl.when(i == 0)
        def _():
            dw_ref[...] = dwv

        @pl.when(i > 0)
        def _():
            dw_ref[...] += dwv

    return pl.pallas_call(
        body, name=name, grid=(W // wb, nt),
        in_specs=[
            pl.BlockSpec((TM, wb), lambda j, i: (i, j)),
            pl.BlockSpec((HALO, wb), lambda j, i: (jnp.minimum((i + 1) * (TM // HALO), Tp // HALO - 1), j)),
            pl.BlockSpec((TM, wb), lambda j, i: (i, j)),
            pl.BlockSpec((HALO, wb), lambda j, i: (jnp.maximum(i * (TM // HALO) - 1, 0), j)),
            pl.BlockSpec((K, wb), lambda j, i: (0, j)),
        ],
        out_specs=[pl.BlockSpec((TM, wb), lambda j, i: (i, j)), pl.BlockSpec((K, wb), lambda j, i: (0, j))],
        out_shape=[jax.ShapeDtypeStruct((Tp, W), BF16), jax.ShapeDtypeStruct((K, W), F32)],
        scratch_shapes=[pltpu.VMEM((TM + HALO, wb), F32), pltpu.VMEM((TM + HALO, wb), F32)],
        compiler_params=_cp(("parallel", "arbitrary")),
    )(dy, dy, x, x, w)


def _rms(x, g):
    return x * lax.rsqrt(jnp.mean(x * x, axis=-1, keepdims=True) + EPS) * g


def fn_norm(h, g):
    return (_rms(h, g),)


def fn_id_norm(h, g):
    return h, _rms(h, g)


def fn_resid_norm(h, m, g_post, g_next):
    h2 = h + _rms(m, g_post)
    return h2, _rms(h2, g_next)


def fn_resid(h, m, g_post):
    return (h + _rms(m, g_post),)


def fn_geglu(y):
    w = y.shape[1] // 2
    return (jax.nn.gelu(y[:, :w], approximate=True) * y[:, w:],)


def _split3(c):
    hi = c.astype(BF16).astype(F32)
    r = c - hi
    mid = r.astype(BF16).astype(F32)
    lo = (r - mid).astype(BF16).astype(F32)
    return hi, mid, lo


def _aug_cols(rows, entries):
    lane = lax.broadcasted_iota(jnp.int32, (rows, ATT_DH), 1)
    out = jnp.zeros((rows, ATT_DH), F32)
    for ln, val in entries:
        out = jnp.where(lane == ln, val, out)
    return out


def _head_col(c, idx):
    lane = lax.broadcasted_iota(jnp.int32, c.shape, 1)
    return jnp.sum(jnp.where(lane == idx, c, 0.0), axis=1, keepdims=True)


def fn_attn_prep_q(j, xp, c, gain):
    rows = xp.shape[0]
    out = []
    for hh in range(2):
        hi, mid, lo = _split3(lax.stop_gradient(_head_col(c, 2 * j + hh)))
        out += [_rms(xp[:, ATT_DH * hh:ATT_DH * (hh + 1)], gain) * (ATT_DH ** -0.5),
                _aug_cols(rows, [(0, hi), (1, mid), (2, lo), (3, 1.0), (4, 1.0), (5, 1.0)])]
    return (jnp.concatenate(out, axis=1),)


def fn_attn_prep_k(j, xp, c, gain):
    rows = xp.shape[0]
    out = []
    for hh in range(2):
        hi, mid, lo = _split3(lax.stop_gradient(_head_col(c, 2 * j + hh)))
        out += [_rms(xp[:, ATT_DH * hh:ATT_DH * (hh + 1)], gain),
                _aug_cols(rows, [(0, 1.0), (1, 1.0), (2, 1.0), (3, -hi), (4, -mid), (5, -lo), (6, 1.0), (7, 1.0), (8, 1.0)])]
    return (jnp.concatenate(out, axis=1),)


def fn_attn_prep_v(xp):
    rows = xp.shape[0]
    out = []
    for hh in range(2):
        out += [xp[:, ATT_DH * hh:ATT_DH * (hh + 1)], _aug_cols(rows, [(0, 1.0), (1, 1.0), (2, 1.0)])]
    return (jnp.concatenate(out, axis=1),)


def fn_attn_gate(o_aug, og):
    outs = []
    for h in range(og.shape[1] // ATT_DH):
        o = o_aug[:, AUG * h:AUG * h + ATT_DH]
        outs.append(o * jax.nn.sigmoid(og[:, ATT_DH * h:ATT_DH * (h + 1)]))
    return (jnp.concatenate(outs, axis=1),)


def fn_attn_bwd_prep(dgo, o_aug, og, q_aug):
    rows = dgo.shape[0]
    lane = lax.broadcasted_iota(jnp.int32, (rows, ATT_DH), 1)
    dos, dogs, qs = [], [], []
    for h in range(og.shape[1] // ATT_DH):
        sl = slice(ATT_DH * h, ATT_DH * (h + 1))
        o = o_aug[:, AUG * h:AUG * h + ATT_DH]
        lse = o_aug[:, AUG * h + ATT_DH:AUG * h + ATT_DH + 1]
        sig = jax.nn.sigmoid(og[:, sl])
        do = dgo[:, sl] * sig
        dogs.append(dgo[:, sl] * o * sig * (1.0 - sig))
        dhi, dmid, dlo = _split3(-jnp.sum(do * o, axis=-1, keepdims=True))
        dos += [do, _aug_cols(rows, [(0, dhi), (1, dmid), (2, dlo)])]
        lhi, lmid, llo = _split3(-lse)
        qa = q_aug[:, AUG * h + ATT_DH:AUG * (h + 1)]
        qa = jnp.where(lane == 6, lhi, jnp.where(lane == 7, lmid, jnp.where(lane == 8, llo, qa)))
        qs += [q_aug[:, AUG * h:AUG * h + ATT_DH], qa]
    return jnp.concatenate(dos, axis=1), jnp.concatenate(dogs, axis=1), jnp.concatenate(qs, axis=1)


def fn_attn_dc(dq_aug, dk_aug):
    lane = lax.broadcasted_iota(jnp.int32, (dk_aug.shape[0], AUG), 1)
    dc = jnp.zeros((dk_aug.shape[0], AUG), F32)
    for h in range(ATT_HEADS):
        col = AUG * h + ATT_DH
        dc = jnp.where(lane == h, dq_aug[:, col:col + 1] - dk_aug[:, col + 3:col + 4], dc)
    return (dc,)


def fn_dn_prep(j, y):
    s = jax.nn.silu(y)
    scale = jnp.where(j < DN_HD // 256, DN_DH ** -0.5, 1.0)
    out = []
    for hh in range(2):
        sh = s[:, DN_DH * hh:DN_DH * (hh + 1)]
        n = sh * lax.rsqrt(jnp.sum(sh * sh, axis=-1, keepdims=True) + EPS) * scale
        out.append(jnp.where(j < 2 * (DN_HD // 256), n, sh))
    return (jnp.concatenate(out, axis=1),)


def fn_dn_gates(lg, alog, dtb):
    lane = lax.broadcasted_iota(jnp.int32, lg.shape, 1)
    beta = jax.nn.sigmoid(lg)
    g = -jnp.exp(alog) * jax.nn.softplus(lg + dtb)
    return (jnp.where(lane < DN_HEADS, beta, jnp.where(lane < 2 * DN_HEADS, g, 0.0)),)


def fn_dn_post(o, og, gain):
    outs = []
    for h in range(DN_HEADS):
        sl = slice(DN_DH * h, DN_DH * (h + 1))
        outs.append(_rms(o[:, sl], gain) * jax.nn.silu(og[:, sl]))
    return (jnp.concatenate(outs, axis=1),)


def _tri(n, lower):
    r = lax.broadcasted_iota(jnp.int32, (n, n), 0)
    c = lax.broadcasted_iota(jnp.int32, (n, n), 1)
    return jnp.where((r >= c) if lower else (r <= c), 1.0, 0.0).astype(F32)


def forget_cumsum(fl, bias, name):
    Tp, W = fl.shape

    def body(fl_ref, b_ref, c_ref, carry):
        i = pl.program_id(0)

        @pl.when(i == 0)
        def _():
            carry[...] = jnp.zeros_like(carry)

        logf = jax.nn.log_sigmoid(fl_ref[...] + b_ref[...])
        c = jnp.dot(_tri(TM, True), logf, precision=lax.Precision.HIGHEST, preferred_element_type=F32) + carry[...]
        c_ref[...] = c
        carry[...] = c[TM - 1:TM, :]

    return pl.pallas_call(
        body, name=name, grid=(Tp // TM,),
        in_specs=[pl.BlockSpec((TM, W), lambda i: (i, 0)), pl.BlockSpec((1, W), lambda i: (0, 0))],
        out_specs=pl.BlockSpec((TM, W), lambda i: (i, 0)), out_shape=jax.ShapeDtypeStruct((Tp, W), F32),
        scratch_shapes=[pltpu.VMEM((1, W), F32)], compiler_params=_cp(("arbitrary",)),
    )(fl, bias)


def forget_cumsum_bwd(dc, fl, bias, name):
    Tp, W = fl.shape
    nt = Tp // TM

    def body(dc_ref, fl_ref, b_ref, dfl_ref, db_ref, carry):
        i = pl.program_id(0)

        @pl.when(i == 0)
        def _():
            carry[...] = jnp.zeros_like(carry)

        dlogf = jnp.dot(_tri(TM, False), dc_ref[...], precision=lax.Precision.HIGHEST, preferred_element_type=F32) + carry[...]
        carry[...] = dlogf[0:1, :]
        dfl = dlogf * jax.nn.sigmoid(-(fl_ref[...] + b_ref[...]))
        dfl_ref[...] = dfl.astype(dfl_ref.dtype)
        s = jnp.sum(dfl, axis=0, keepdims=True)

        @pl.when(i == 0)
        def _():
            db_ref[...] = s

        @pl.when(i > 0)
        def _():
            db_ref[...] += s

    rev = lambda i: (nt - 1 - i, 0)
    return pl.pallas_call(
        body, name=name, grid=(nt,),
        in_specs=[pl.BlockSpec((TM, W), rev), pl.BlockSpec((TM, W), rev), pl.BlockSpec((1, W), lambda i: (0, 0))],
        out_specs=[pl.BlockSpec((TM, W), rev), pl.BlockSpec((1, W), lambda i: (0, 0))],
        out_shape=[jax.ShapeDtypeStruct((Tp, W), BF16), jax.ShapeDtypeStruct((1, W), F32)],
        scratch_shapes=[pltpu.VMEM((1, W), F32)], compiler_params=_cp(("arbitrary",)),
    )(dc, fl, bias)


_HBM = pl.BlockSpec(memory_space=pltpu.HBM)


def _place():
    x, y, c = lax.axis_index("x"), lax.axis_index("y"), lax.axis_index("c")
    return x, y, c, [(1 - x, y), (x, 1 - y), (1 - x, 1 - y)]


def _chip_copies(scatter, ins, outs, send_sems, recv_sems, local_sems):
    x, y, c, chips = _place()
    s = 2 * x + y
    copies = []
    for a in range(len(ins)):
        copies.append(pltpu.make_async_copy(ins[a].at[s] if scatter else ins[a], outs[a].at[s], local_sems.at[a]))
        for p, (px, py) in enumerate(chips):
            copies.append(pltpu.make_async_remote_copy(
                src_ref=ins[a].at[2 * px + py] if scatter else ins[a], dst_ref=outs[a].at[s], send_sem=send_sems.at[a, p],
                recv_sem=recv_sems.at[a, p], device_id=(px, py, c), device_id_type=MESH_ID))
    return copies


def _chip_exchange_shapes(scatter, arrs):
    n = len(arrs)
    out_shape = [jax.ShapeDtypeStruct(a.shape if scatter else (4,) + a.shape, a.dtype) for a in arrs]
    sems = [pltpu.SemaphoreType.DMA((n, 3)), pltpu.SemaphoreType.DMA((n, 3)), pltpu.SemaphoreType.DMA((n,))]
    return out_shape, sems


def exchange_chips(scatter, arrs, name):
    n = len(arrs)

    def body(*refs):
        copies = _chip_copies(scatter, refs[:n], refs[n:2 * n], *refs[2 * n:])
        for cp in copies:
            cp.start()
        for cp in copies:
            cp.wait()

    out_shape, sems = _chip_exchange_shapes(scatter, arrs)
    return pl.pallas_call(body, name=name, in_specs=[_HBM] * n, out_specs=[_HBM] * n, out_shape=out_shape, scratch_shapes=sems)(*arrs)


def swap_cores(arrs, name):
    n = len(arrs)

    def body(*refs):
        ins, outs = refs[:n], refs[n:2 * n]
        send_sems, recv_sems = refs[2 * n:]
        x, y, c, _ = _place()
        copies = []
        for a in range(n):
            cp = pltpu.make_async_remote_copy(
                src_ref=ins[a], dst_ref=outs[a], send_sem=send_sems.at[a], recv_sem=recv_sems.at[a],
                device_id=(x, y, 1 - c), device_id_type=MESH_ID)
            cp.start()
            copies.append(cp)
        for cp in copies:
            cp.wait()

    return pl.pallas_call(
        body, name=name, in_specs=[_HBM] * n, out_specs=[_HBM] * n,
        out_shape=[jax.ShapeDtypeStruct(a.shape, a.dtype) for a in arrs],
        scratch_shapes=[pltpu.SemaphoreType.DMA((n,)), pltpu.SemaphoreType.DMA((n,))],
    )(*arrs)


_NT = (((1,), (1,)), ((), ()))
_TN = (((0,), (0,)), ((), ()))


def _cargo_edges(scatter, n, refs, nb):
    first = (pl.program_id(0) == 0) & (pl.program_id(1) == 0)
    last = (pl.program_id(0) == ATT_HEADS - 1) & (pl.program_id(1) == nb - 1)

    @pl.when(first)
    def _():
        for cp in _chip_copies(scatter, refs[:n], refs[n:2 * n], *refs[2 * n:]):
            cp.start()

    @pl.when(last)
    def _():
        for cp in _chip_copies(scatter, refs[:n], refs[n:2 * n], *refs[2 * n:]):
            cp.wait()


def flash_fwd(q_aug, k_aug, v_aug, name, cargo=()):
    Tp = q_aug.shape[0]
    nb = Tp // TM
    nc = len(cargo)

    def body(*refs):
        q_ref, k_ref, v_ref = refs[:3]
        o_ref = refs[3 + nc]
        if nc:
            _cargo_edges(False, nc, refs[3:3 + nc] + refs[4 + nc:], nb)
        i = pl.program_id(1)
        q = q_ref[...]

        def block(j, carry, masked):
            m_old, acc = carry
            rows = pl.ds(pl.multiple_of(j * TM, TM), TM)
            s = lax.dot_general(q, k_ref[rows, :], _NT, preferred_element_type=F32)
            if masked:
                r = lax.broadcasted_iota(jnp.int32, (TM, TM), 0)
                c = lax.broadcasted_iota(jnp.int32, (TM, TM), 1)
                s = jnp.where(c <= r, s, NEG)
            m_new = jnp.maximum(m_old, jnp.max(s, axis=-1, keepdims=True))
            p = jnp.exp(s - m_new)
            acc = jnp.exp(m_old - m_new) * acc + jnp.dot(p.astype(BF16), v_ref[rows, :], preferred_element_type=F32)
            return m_new, acc

        carry = (jnp.full((TM, 1), NEG, F32), jnp.zeros((TM, AUG), F32))
        carry = lax.fori_loop(0, i, lambda j, cr: block(j, cr, False), carry)
        m, acc = block(i, carry, True)
        l = acc[:, ATT_DH:ATT_DH + 1]
        lane = lax.broadcasted_iota(jnp.int32, (TM, AUG), 1)
        o_ref[...] = jnp.where(lane < ATT_DH, acc / l, m + jnp.log(l))

    head = pl.BlockSpec((Tp, AUG), lambda h, i: (0, h))
    cargo_shape, sems = _chip_exchange_shapes(False, cargo) if nc else ([], [])
    res = pl.pallas_call(
        body, name=name, grid=(ATT_HEADS, nb),
        in_specs=[pl.BlockSpec((TM, AUG), lambda h, i: (i, h)), head, head] + [_HBM] * nc,
        out_specs=[pl.BlockSpec((TM, AUG), lambda h, i: (i, h))] + [_HBM] * nc,
        out_shape=[jax.ShapeDtypeStruct((Tp, ATT_AUG), F32)] + cargo_shape, scratch_shapes=sems,
        compiler_params=_cp(("arbitrary", "arbitrary")),
    )(q_aug, k_aug, v_aug, *cargo)
    return res[0], res[1:]


def flash_bwd(q_aug2, k_aug, v_aug, do_aug, name, cargo=()):
    Tp = q_aug2.shape[0]
    nb = Tp // TM
    nc = len(cargo)

    def body(*refs):
        q_ref, do_ref, k_ref, v_ref = refs[:4]
        dq_ref, dk_ref, dv_ref = refs[4 + nc:7 + nc]
        if nc:
            _cargo_edges(True, nc, refs[4:4 + nc] + refs[7 + nc:], nb)
        j = pl.program_id(1)
        k, v = k_ref[...], v_ref[...]

        @pl.when(j == 0)
        def _():
            dq_ref[...] = jnp.zeros_like(dq_ref)

        def block(i, carry, masked):
            dk, dv = carry
            rows = pl.ds(pl.multiple_of(i * TM, TM), TM)
            q, do = q_ref[rows, :], do_ref[rows, :]
            st = lax.dot_general(k, q, _NT, preferred_element_type=F32)
            if masked:
                r = lax.broadcasted_iota(jnp.int32, (TM, TM), 0)
                c = lax.broadcasted_iota(jnp.int32, (TM, TM), 1)
                st = jnp.where(r <= c, st, NEG)
            pt = jnp.exp(st)
            dpt = lax.dot_general(v, do, _NT, preferred_element_type=F32)
            dst = (pt * dpt).astype(BF16)
            dv = dv + jnp.dot(pt.astype(BF16), do, preferred_element_type=F32)
            dk = dk + jnp.dot(dst, q, preferred_element_type=F32)
            dq_ref[rows, :] += lax.dot_general(dst, k, _TN, preferred_element_type=F32)
            return dk, dv

        carry = block(j, (jnp.zeros((TM, AUG), F32), jnp.zeros((TM, AUG), F32)), True)
        dk, dv = lax.fori_loop(j + 1, nb, lambda i, cr: block(i, cr, False), carry)
        dk_ref[...] = dk
        dv_ref[...] = dv

    head = pl.BlockSpec((Tp, AUG), lambda h, j: (0, h))
    blk = pl.BlockSpec((TM, AUG), lambda h, j: (j, h))
    cargo_shape, sems = _chip_exchange_shapes(True, cargo) if nc else ([], [])
    res = pl.pallas_call(
        body, name=name, grid=(ATT_HEADS, nb),
        in_specs=[head, head, blk, blk] + [_HBM] * nc, out_specs=[head, blk, blk] + [_HBM] * nc,
        out_shape=[jax.ShapeDtypeStruct((Tp, ATT_AUG), F32)] * 3 + cargo_shape, scratch_shapes=sems,
        compiler_params=_cp(("arbitrary", "arbitrary")),
    )(q_aug2, do_aug, k_aug, v_aug, *cargo)
    return res[:3], res[3:]


_BATCH = ((0,), (0,))


def _dg(x, y, cx, cy):
    return lax.dot_general(x.astype(BF16), y.astype(BF16), (((cx + 1,), (cy + 1,)), _BATCH), preferred_element_type=F32)


def _split2(x):
    hi = x.astype(BF16)
    return hi, (x - hi.astype(F32)).astype(BF16)


def _dg3(x, y, cx, cy):
    (xh, xl), (yh, yl) = _split2(x), _split2(y)
    d = lambda a, b: lax.dot_general(a, b, (((cx + 1,), (cy + 1,)), _BATCH), preferred_element_type=F32)
    return d(xh, yh) + (d(xl, yh) + d(xh, yl))


def _make_bdot(ca, cb, dg):
    @jax.custom_vjp
    def f(a, b):
        return dg(a, b, ca, cb)

    def fwd(a, b):
        return dg(a, b, ca, cb), (a, b)

    def bwd(res, ct):
        a, b = res
        da = dg(ct, b, 1, 1 if cb == 0 else 0) if ca == 1 else dg(b, ct, 1 if cb == 0 else 0, 1)
        db = dg(a, ct, 0 if ca == 1 else 1, 0) if cb == 0 else dg(ct, a, 0, 0 if ca == 1 else 1)
        return da, db

    f.defvjp(fwd, bwd)
    return f


_bd_nn, _bd_nt, _bd_tn = _make_bdot(1, 0, _dg), _make_bdot(1, 1, _dg), _make_bdot(0, 0, _dg)
_bd3_nn = _make_bdot(1, 0, _dg3)


@jax.custom_vjp
def _chunk_cumsum(x):
    return jnp.dot(_tri(x.shape[0], True), x, precision=lax.Precision.HIGHEST, preferred_element_type=F32)


def _chunk_cumsum_fwd(x):
    return _chunk_cumsum(x), None


def _chunk_cumsum_bwd(_, ct):
    return (jnp.dot(_tri(ct.shape[0], False), ct, precision=lax.Precision.HIGHEST, preferred_element_type=F32),)


_chunk_cumsum.defvjp(_chunk_cumsum_fwd, _chunk_cumsum_bwd)


def dn_chunk(S, q, k, v, bg):
    C = q.shape[0]
    H = DN_HEADS
    heads = lambda t: jnp.stack([t[:, DN_DH * h:DN_DH * (h + 1)] for h in range(H)])
    cols = lambda t, o: jnp.stack([t[:, o + h:o + h + 1] for h in range(H)])
    qh, kh, vh = heads(q), heads(k), heads(v)
    beta, gc = cols(bg, 0), cols(_chunk_cumsum(bg), H)
    r = lax.broadcasted_iota(jnp.int32, (H, C, C), 1)
    c = lax.broadcasted_iota(jnp.int32, (H, C, C), 2)
    gcb = jnp.broadcast_to(gc, (H, C, C))
    dec = jnp.exp(jnp.where(r >= c, gcb - jnp.swapaxes(gcb, 1, 2), NEG))
    e_gc = jnp.exp(gc)
    kq = _bd_nt(jnp.concatenate([kh, qh], axis=1), kh)
    A = jnp.where(r > c, kq[:, :C] * dec * beta, 0.0)
    qk = kq[:, C:] * dec
    X = -A
    P = _bd3_nn(X, X)
    N = X
    for i in range(5):
        NP = _bd3_nn(jnp.concatenate([N, P], axis=1) if i < 4 else N, P)
        N = N + P + NP[:, :C]
        if i < 4:
            P = NP[:, C:]
    R = jnp.concatenate([kh * (beta * e_gc), vh * beta], axis=2)
    WU = R + _bd3_nn(N, R)
    W, U0 = WU[:, :, :DN_DH], WU[:, :, DN_DH:]
    gl = gc[:, C - 1:C, :]
    WqS = _bd_nt(jnp.concatenate([W, qh * e_gc], axis=1), S)
    U = U0 - WqS[:, :C]
    O = WqS[:, C:] + _bd_nn(qk, U)
    S_new = jnp.exp(gl) * S + _bd_tn(U, kh * jnp.exp(gl - gc))
    return jnp.concatenate([O[h] for h in range(H)], axis=1), S_new


def dn_scan_fwd(qkv, bg, name):
    Tp = qkv.shape[0]
    n = Tp // DN_CHUNK
    tokspec = lambda w, c=0: pl.BlockSpec((DN_CHUNK, w), lambda i: (i, c))

    def body(q_ref, k_ref, v_ref, bg_ref, o_ref, s_ref, state):
        @pl.when(pl.program_id(0) == 0)
        def _():
            state[...] = jnp.zeros_like(state)

        S = state[...]
        s_ref[0] = S
        O, S_new = dn_chunk(S, q_ref[...].astype(F32), k_ref[...].astype(F32), v_ref[...].astype(F32), bg_ref[...])
        o_ref[...] = O.astype(o_ref.dtype)
        state[...] = S_new

    return pl.pallas_call(
        body, name=name, grid=(n,),
        in_specs=[tokspec(DN_HD, 0), tokspec(DN_HD, 1), tokspec(DN_HD, 2), tokspec(AUG)],
        out_specs=[tokspec(DN_HD), pl.BlockSpec((1, DN_HEADS, DN_DH, DN_DH), lambda i: (i, 0, 0, 0))],
        out_shape=[jax.ShapeDtypeStruct((Tp, DN_HD), BF16), jax.ShapeDtypeStruct((n, DN_HEADS, DN_DH, DN_DH), F32)],
        scratch_shapes=[pltpu.VMEM((DN_HEADS, DN_DH, DN_DH), F32)],
        compiler_params=_cp(("arbitrary",)),
    )(qkv, qkv, qkv, bg)


def dn_scan_bwd(qkv, bg, states, dO, name):
    Tp = qkv.shape[0]
    n = Tp // DN_CHUNK
    tokspec = lambda w, c=0: pl.BlockSpec((DN_CHUNK, w), lambda i: (n - 1 - i, c))

    def body(q_ref, k_ref, v_ref, bg_ref, s_ref, do_ref, dqkv_ref, dbg_ref, dstate):
        @pl.when(pl.program_id(0) == 0)
        def _():
            dstate[...] = jnp.zeros_like(dstate)

        _, vjp = jax.vjp(dn_chunk, s_ref[0], q_ref[...].astype(F32), k_ref[...].astype(F32), v_ref[...].astype(F32), bg_ref[...])
        dS, dq, dk, dv, dbg = vjp((do_ref[...].astype(F32), dstate[...]))
        dqkv_ref[...] = jnp.concatenate([dq, dk, dv], axis=1).astype(dqkv_ref.dtype)
        dbg_ref[...] = dbg
        dstate[...] = dS

    return pl.pallas_call(
        body, name=name, grid=(n,),
        in_specs=[tokspec(DN_HD, 0), tokspec(DN_HD, 1), tokspec(DN_HD, 2), tokspec(AUG),
                  pl.BlockSpec((1, DN_HEADS, DN_DH, DN_DH), lambda i: (n - 1 - i, 0, 0, 0)), tokspec(DN_HD)],
        out_specs=[tokspec(3 * DN_HD), tokspec(AUG)],
        out_shape=[jax.ShapeDtypeStruct((Tp, 3 * DN_HD), BF16), jax.ShapeDtypeStruct((Tp, AUG), F32)],
        scratch_shapes=[pltpu.VMEM((DN_HEADS, DN_DH, DN_DH), F32)],
        compiler_params=_cp(("arbitrary",)),
    )(qkv, qkv, qkv, bg, states, dO)


def loss_head(h, tgt, n_tok, name):
    Tp, Dm = h.shape

    def body(h_ref, t_ref, l_ref, dh_ref):
        i = pl.program_id(0)
        row = i * TM + lax.broadcasted_iota(jnp.int32, (TM, Dm), 0)
        e = jnp.where((row >= N_META) & (row < N_META + n_tok), h_ref[...] - t_ref[...], 0.0)
        dh_ref[...] = e * (1.0 / Dm)
        s = jnp.sum(e * e, axis=0, keepdims=True) * (0.5 / Dm)

        @pl.when(i == 0)
        def _():
            l_ref[...] = s

        @pl.when(i > 0)
        def _():
            l_ref[...] += s

    tile = pl.BlockSpec((TM, Dm), lambda i: (i, 0))
    return pl.pallas_call(
        body, name=name, grid=(Tp // TM,), in_specs=[tile, tile],
        out_specs=[pl.BlockSpec((1, Dm), lambda i: (0, 0)), tile],
        out_shape=[jax.ShapeDtypeStruct((1, Dm), F32), jax.ShapeDtypeStruct((Tp, Dm), F32)],
        compiler_params=_cp(("arbitrary",)),
    )(h, tgt)


def _row_block(R):
    return _pick(R, (256, 176, 128, 64, 32, 16, 8))


def sum_slots(parts, name):
    P, R, C = parts.shape
    rb = _row_block(R)

    def body(p_ref, o_ref):
        acc = p_ref[0].astype(F32)
        for s in range(1, P):
            acc = acc + p_ref[s].astype(F32)
        o_ref[...] = acc

    return pl.pallas_call(
        body, name=name, grid=(R // rb,), in_specs=[pl.BlockSpec((P, rb, C), lambda i: (0, i, 0))],
        out_specs=pl.BlockSpec((rb, C), lambda i: (i, 0)), out_shape=jax.ShapeDtypeStruct((R, C), F32),
        compiler_params=_cp(("parallel",)),
    )(parts)


def _adamw_math(w, g, m, v):
    m2 = ADAM_B1 * m + (1.0 - ADAM_B1) * g
    v2 = ADAM_B2 * v + (1.0 - ADAM_B2) * (g * g)
    m_hat = m2 / (1.0 - ADAM_B1 ** ADAM_STEP)
    v_hat = v2 / (1.0 - ADAM_B2 ** ADAM_STEP)
    delta = -ADAM_LR * (m_hat / (jnp.sqrt(v_hat) + ADAM_EPS) + ADAM_WD * w)
    return delta, m2, v2


def adamw_big(w, m, v, g_a, g_b, name):
    R, C = w.shape
    rb = _row_block(R)

    def body(w_ref, m_ref, v_ref, ga_ref, gb_ref, g_ref, d_ref, m2_ref, v2_ref):
        g = ga_ref[...] + gb_ref[...]
        delta, m2, v2 = _adamw_math(w_ref[...], g, m_ref[...], v_ref[...])
        g_ref[...], d_ref[...], m2_ref[...], v2_ref[...] = g, delta, m2, v2

    spec = pl.BlockSpec((rb, C), lambda i: (i, 0))
    return pl.pallas_call(
        body, name=name, grid=(R // rb,), in_specs=[spec] * 5, out_specs=[spec] * 4,
        out_shape=[jax.ShapeDtypeStruct((R, C), F32)] * 4, compiler_params=_cp(("parallel",)),
    )(w, m, v, g_a, g_b)


def adamw_small(ws, ms, vs, gs, name):
    n = len(ws)

    def body(*refs):
        w_r, m_r, v_r, g_r = refs[:n], refs[n:2 * n], refs[2 * n:3 * n], refs[3 * n:4 * n]
        d_o, m_o, v_o = refs[4 * n:5 * n], refs[5 * n:6 * n], refs[6 * n:7 * n]
        for k in range(n):
            delta, m2, v2 = _adamw_math(w_r[k][...], g_r[k][...], m_r[k][...], v_r[k][...])
            d_o[k][...], m_o[k][...], v_o[k][...] = delta, m2, v2

    shapes = [jax.ShapeDtypeStruct(w.shape, F32) for w in ws]
    res = pl.pallas_call(body, name=name, out_shape=shapes * 3, compiler_params=pltpu.CompilerParams(vmem_limit_bytes=VMEM_LIMIT))(
        *ws, *ms, *vs, *gs)
    return res[:n], res[n:2 * n], res[2 * n:]


def sum8(own4, sib4, name):
    _, R, C = own4.shape

    def body(a_ref, b_ref, o_ref):
        a = ((a_ref[0] + a_ref[1]) + a_ref[2]) + a_ref[3]
        b = ((b_ref[0] + b_ref[1]) + b_ref[2]) + b_ref[3]
        o_ref[...] = a + b

    return pl.pallas_call(body, name=name, out_shape=jax.ShapeDtypeStruct((R, C), F32),
                          compiler_params=pltpu.CompilerParams(vmem_limit_bytes=VMEM_LIMIT))(own4, sib4)


def _il(w):
    lead = w.shape[:-1]
    return w.reshape(lead + (2, FFN_DIM // 256, 256)).swapaxes(-3, -2).reshape(lead + (2 * FFN_DIM,))


def _unil(w):
    lead = w.shape[:-1]
    return w.reshape(lead + (FFN_DIM // 256, 2, 256)).swapaxes(-3, -2).reshape(lead + (2 * FFN_DIM,))


def _pad_cols(w, n=AUG):
    return jnp.pad(w, ((0, 0), (0, n - w.shape[1])))


ATT_SPLIT = ATT_HEADS // 2


def attn_fwd(a, W, tag, cargo=()):
    proj = matmul(a, W["main"], "nn", BF16, f"{tag}_proj")
    fl = matmul(a, W["small"], "nn", F32, f"{tag}_fl")
    c = forget_cumsum(fl, W["bias"], f"{tag}_cumsum")
    cfix = tok(c, fixed=True)
    q_aug = rowwise(fn_attn_prep_q, [tok(proj, AUG, 0), cfix], [par(W["qg"])], [(2 * AUG, BF16)], f"{tag}_prepq", ATT_SPLIT, True)[0]
    k_aug = rowwise(fn_attn_prep_k, [tok(proj, AUG, ATT_SPLIT), cfix], [par(W["kg"])], [(2 * AUG, BF16)], f"{tag}_prepk", ATT_SPLIT, True)[0]
    v_aug = rowwise(fn_attn_prep_v, [tok(proj, AUG, 2 * ATT_SPLIT)], [], [(2 * AUG, BF16)], f"{tag}_prepv", ATT_SPLIT)[0]
    o_aug, landed = flash_fwd(q_aug, k_aug, v_aug, f"{tag}_flash", cargo)
    go = rowwise(fn_attn_gate, [tok(o_aug, 2 * AUG, 0), tok(proj, AUG, 3 * ATT_SPLIT)], [], [(AUG, BF16)], f"{tag}_gate", ATT_SPLIT)[0]
    return (a, proj, fl, c, q_aug, k_aug, v_aug, o_aug, go), landed


def attn_out(saved, W, tag):
    return matmul(saved[-1], W["out"], "nn", BF16, f"{tag}_out")


def attn_bwd(dm, saved, W, tag, cargo=None):
    a, proj, fl, c, q_aug, k_aug, v_aug, o_aug, go = saved
    dgo = matmul(dm, W["out"], "nt", BF16, f"{tag}_dgo")
    d_out = matmul(go, dm, "tn", BF16, f"{tag}_dwout")
    do_aug, dog, q_aug2 = rowwise(
        fn_attn_bwd_prep, [tok(dgo, AUG, 0), tok(o_aug, 2 * AUG, 0), tok(proj, AUG, 3 * ATT_SPLIT), tok(q_aug, 2 * AUG, 0)], [],
        [(2 * AUG, BF16), (AUG, BF16), (2 * AUG, BF16)], f"{tag}_bprep", ATT_SPLIT)
    (dq_aug, dk_aug, dv_aug), landed = flash_bwd(q_aug2, k_aug, v_aug, do_aug, f"{tag}_flashb", cargo(d_out) if cargo else ())
    dc = rowwise(fn_attn_dc, [tok(dq_aug), tok(dk_aug)], [], [(AUG, F32)], f"{tag}_dc")[0]
    cfix = tok(c, fixed=True)
    (dq,), (d_qg,) = rowwise_bwd(fn_attn_prep_q, [tok(proj, AUG, 0, grad=BF16), cfix], [par(W["qg"])], [dq_aug],
                                 f"{tag}_prepqb", nsplit=ATT_SPLIT, with_j=True)
    (dk,), (d_kg,) = rowwise_bwd(fn_attn_prep_k, [tok(proj, AUG, ATT_SPLIT, grad=BF16), cfix], [par(W["kg"])], [dk_aug],
                                 f"{tag}_prepkb", nsplit=ATT_SPLIT, with_j=True)
    (dv,), _ = rowwise_bwd(fn_attn_prep_v, [tok(proj, AUG, 2 * ATT_SPLIT, grad=BF16)], [], [dv_aug], f"{tag}_prepvb", nsplit=ATT_SPLIT)
    dfl, d_bias = forget_cumsum_bwd(dc, fl, W["bias"], f"{tag}_cumsumb")
    dproj = jnp.concatenate([dq, dk, dv, dog], axis=1)
    da = matmul(dproj, W["main"], "nt", BF16, f"{tag}_da", add=matmul(dfl, W["small"], "nt", F32, f"{tag}_da0"))
    d_in = jnp.concatenate([matmul(a, dproj, "tn", BF16, f"{tag}_dw"), matmul(a, dfl, "tn", BF16, f"{tag}_dw2")[:, :ATT_HEADS]], axis=1)
    return da, dict(w_in=d_in, w_out=d_out, b_forget=d_bias[0, :ATT_HEADS], q_norm=d_qg[0], k_norm=d_kg[0]), landed


DN_SPLIT = 3 * DN_HD // 256


def dn_fwd(a, W, tag):
    proj = matmul(a, W["main"], "nn", BF16, f"{tag}_proj")
    lg = matmul(a, W["small"], "nn", F32, f"{tag}_lg")
    qkv = rowwise(fn_dn_prep, [tok(proj, 256, 0, conv=(W["conv"], 0))], [], [(256, BF16)], f"{tag}_prep", DN_SPLIT, True)[0]
    bg = rowwise(fn_dn_gates, [tok(lg)], [par(W["alog"]), par(W["dtb"])], [(AUG, F32)], f"{tag}_gates")[0]
    O, states = dn_scan_fwd(qkv, bg, f"{tag}_scan")
    go = rowwise(fn_dn_post, [tok(O), tok(proj, DN_HD, 3)], [par(W["ogain"])], [(DN_HD, BF16)], f"{tag}_post")[0]
    m = matmul(go, W["out"], "nn", BF16, f"{tag}_out")
    return m, (a, proj, lg, qkv, bg, O, states, go)


def dn_bwd(dm, saved, W, tag):
    a, proj, lg, qkv, bg, O, states, go = saved
    dgo = matmul(dm, W["out"], "nt", BF16, f"{tag}_dgo")
    d_out = matmul(go, dm, "tn", BF16, f"{tag}_dwout")
    (dO, dog), (d_ogain,) = rowwise_bwd(fn_dn_post, [tok(O, grad=BF16), tok(proj, DN_HD, 3, grad=BF16)], [par(W["ogain"])],
                                        [dgo], f"{tag}_postb")
    dqkv_n, dbg = dn_scan_bwd(qkv, bg, states, dO, f"{tag}_scanb")
    (dy,), _ = rowwise_bwd(fn_dn_prep, [tok(proj, 256, 0, conv=(W["conv"], 0), grad=BF16)], [], [dqkv_n], f"{tag}_prepb",
                           nsplit=DN_SPLIT, with_j=True)
    dqkv, d_conv = conv_bwd(dy, proj, W["conv"], f"{tag}_convb")
    (dlg,), (d_alog, d_dtb) = rowwise_bwd(fn_dn_gates, [tok(lg, grad=BF16)], [par(W["alog"]), par(W["dtb"])], [dbg], f"{tag}_gatesb")
    dproj = jnp.concatenate([dqkv, dog], axis=1)
    da = matmul(dproj, W["main"], "nt", BF16, f"{tag}_da", add=matmul(dlg, W["small"], "nt", F32, f"{tag}_da0"))
    d_in = jnp.concatenate([matmul(a, dproj, "tn", BF16, f"{tag}_dw"), matmul(a, dlg, "tn", BF16, f"{tag}_dw2")[:, :2 * DN_HEADS]], axis=1)
    return da, dict(w_in=d_in, w_out=d_out, conv=d_conv, a_log=d_alog[0, DN_HEADS:2 * DN_HEADS],
                    dt_bias=d_dtb[0, DN_HEADS:2 * DN_HEADS], o_norm=d_ogain[0])


FFN_SPLIT = FFN_DIM // 256


def ffn_fwd(b, W, tag):
    u0 = matmul(b, W["up"], "nn", BF16, f"{tag}_up")
    act = rowwise(fn_geglu, [tok(u0, 512, 0, conv=(W["conv"], 0))], [], [(256, BF16)], f"{tag}_act", nsplit=FFN_SPLIT)[0]
    f = matmul(act, W["down"], "nn", BF16, f"{tag}_down")
    return f, (b, u0, act)


def ffn_bwd(df, saved, W, tag):
    b, u0, act = saved
    dact = matmul(df, W["down"], "nt", BF16, f"{tag}_dact")
    d_down = matmul(act, df, "tn", BF16, f"{tag}_dwdown")
    (du,), _ = rowwise_bwd(fn_geglu, [tok(u0, 512, 0, conv=(W["conv"], 0), grad=BF16)], [], [dact], f"{tag}_actb", nsplit=FFN_SPLIT)
    du0, d_conv = conv_bwd(du, u0, W["conv"], f"{tag}_convb")
    db = matmul(du0, W["up"], "nt", BF16, f"{tag}_db")
    d_up = matmul(b, du0, "tn", BF16, f"{tag}_dwup")
    return db, dict(up=_unil(d_up), conv=_unil(d_conv), down=d_down)


def _cols_from_slots(g, L):
    K, Cs = g.shape[1] // L, g.shape[2]
    return g.reshape(4, L, K, Cs).transpose(1, 2, 0, 3).reshape(L, K, 4 * Cs)


def _rows_from_slots(g, L):
    Rs, C = g.shape[1] // L, g.shape[2]
    return g.reshape(4, L, Rs, C).transpose(1, 0, 2, 3).reshape(L, 4 * Rs, C)


def _cols_to_slots(w):
    L, K, C = w.shape
    return w.reshape(L, K, 4, C // 4).transpose(2, 0, 1, 3).reshape(4, L * K, C // 4)


def _rows_to_slots(w):
    L, R, C = w.shape
    return w.reshape(L, 4, R // 4, C).transpose(1, 0, 2, 3).reshape(4, L * (R // 4), C)


SMALL_NAMES = ["meta_tokens", "norm_mix_pre", "norm_mix_post", "norm_ffn_pre", "norm_ffn_post", "attn_b_forget", "attn_q_norm",
               "attn_k_norm", "dn_conv", "dn_a_log", "dn_dt_bias", "dn_o_norm", "ffn_conv"]
BIG_NAMES = ["attn_w_in", "attn_w_out", "dn_w_in", "dn_w_out", "ffn_w_up", "ffn_w_down"]
WEIGHT_NAMES = ["meta_tokens", "norm_mix_pre", "norm_mix_post", "norm_ffn_pre", "norm_ffn_post", "attn_w_in", "attn_b_forget",
                "attn_q_norm", "attn_k_norm", "attn_w_out", "dn_w_in", "dn_conv", "dn_a_log", "dn_dt_bias", "dn_o_norm", "dn_w_out",
                "ffn_w_up", "ffn_conv", "ffn_w_down"]
PACK_COLS = 1024


def train_step(x, loss_target, w, m, v):
    L = x.shape[1]
    T = L + N_META
    Tp = -(-T // TM) * TM
    xi, yi = lax.axis_index("x"), lax.axis_index("y")
    slot = 2 * xi + yi
    as2d = lambda t: t.reshape(-1, t.shape[-1])

    bf2d = lambda n: as2d(w[n]).astype(BF16)
    attn_in_shard = w["attn_w_in"].astype(BF16)
    first = exchange_chips(False, [attn_in_shard[0], w["meta_tokens"], as2d(w["dn_conv"]), as2d(w["ffn_conv"])], "gather_first")
    later = [attn_in_shard[1], bf2d("attn_w_out"), bf2d("dn_w_in"), bf2d("dn_w_out"), bf2d("ffn_w_up"), bf2d("ffn_w_down")]
    meta = first[1].transpose(1, 0, 2).reshape(N_META, D_MODEL)
    dn_conv = _cols_from_slots(first[2], 2)
    ffn_conv = _il(_cols_from_slots(first[3], DEPTH))
    lane8 = lambda t: jnp.pad(t[None, :], ((0, 0), (DN_HEADS, AUG - 2 * DN_HEADS)))

    def attn_weights(j, w_in, w_out):
        return dict(main=w_in[:, :4 * ATT_HD], small=_pad_cols(w_in[:, 4 * ATT_HD:]), bias=_pad_cols(w["attn_b_forget"][j][None, :]),
                    qg=w["attn_q_norm"][j][None, :], kg=w["attn_k_norm"][j][None, :], out=w_out)

    WA = [attn_weights(0, _cols_from_slots(first[0], 1)[0], None), None]
    WD, WF = None, None
    row = lambda t, i: t[i][None, :]

    h = jnp.concatenate([meta, x[0], jnp.zeros((Tp - T, D_MODEL), F32)], axis=0)
    tgt = jnp.concatenate([jnp.zeros((N_META, D_MODEL), F32), loss_target[0], jnp.zeros((Tp - T, D_MODEL), F32)], axis=0)
    h0 = h
    a = rowwise(fn_norm, [tok(h)], [par(row(w["norm_mix_pre"], 0))], [(D_MODEL, BF16)], "norm0")[0]
    saved = []
    for i in range(DEPTH):
        j = i // 2
        if i == 0:
            sv_mix, landed = attn_fwd(a, WA[0], "attn0", later)
            attn_out_w, dn_in, dn_out = _rows_from_slots(landed[1], 2), _cols_from_slots(landed[2], 2), _rows_from_slots(landed[3], 2)
            ffn_up, ffn_down = _il(_cols_from_slots(landed[4], DEPTH)), _rows_from_slots(landed[5], DEPTH)
            WA[0]["out"] = attn_out_w[0]
            WA[1] = attn_weights(1, _cols_from_slots(landed[0], 1)[0], attn_out_w[1])
            WD = [dict(main=dn_in[k, :, :4 * DN_HD], small=_pad_cols(dn_in[k, :, 4 * DN_HD:]), conv=dn_conv[k], alog=lane8(w["dn_a_log"][k]),
                       dtb=lane8(w["dn_dt_bias"][k]), ogain=w["dn_o_norm"][k][None, :], out=dn_out[k]) for k in range(2)]
            WF = [dict(up=ffn_up[k], conv=ffn_conv[k], down=ffn_down[k]) for k in range(DEPTH)]
            mix = attn_out(sv_mix, WA[0], "attn0")
        elif i % 2 == 0:
            sv_mix, _ = attn_fwd(a, WA[j], f"attn{j}")
            mix = attn_out(sv_mix, WA[j], f"attn{j}")
        else:
            mix, sv_mix = dn_fwd(a, WD[j], f"dn{j}")
        h_mid, b = rowwise(fn_resid_norm, [tok(h), tok(mix)], [par(row(w["norm_mix_post"], i)), par(row(w["norm_ffn_pre"], i))],
                           [(D_MODEL, F32), (D_MODEL, BF16)], f"resid_mix{i}")
        f, sv_ffn = ffn_fwd(b, WF[i], f"ffn{i}")
        if i < DEPTH - 1:
            h_out, a = rowwise(fn_resid_norm, [tok(h_mid), tok(f)], [par(row(w["norm_ffn_post"], i)), par(row(w["norm_mix_pre"], i + 1))],
                               [(D_MODEL, F32), (D_MODEL, BF16)], f"resid_ffn{i}")
        else:
            h_out = rowwise(fn_resid, [tok(h_mid), tok(f)], [par(row(w["norm_ffn_post"], i))], [(D_MODEL, F32)], f"resid_ffn{i}")[0]
        saved.append((h, mix, sv_mix, h_mid, f, sv_ffn))
        h = h_out
    lvec, dh = loss_head(h, tgt, L, "loss_head")
    loss = lax.psum(jnp.sum(lvec), ("x", "y", "c"))

    gn = {n: [None] * DEPTH for n in ("norm_mix_pre", "norm_mix_post", "norm_ffn_pre", "norm_ffn_post")}
    g_attn, g_dn, g_ffn = [None, None], [None, None], [None] * DEPTH
    da = None
    for i in reversed(range(DEPTH)):
        j = i // 2
        h_in, mix, sv_mix, h_mid, f, sv_ffn = saved[i]
        if i == DEPTH - 1:
            (dh, df), (gn["norm_ffn_post"][i],) = rowwise_bwd(
                fn_resid, [tok(h_mid, grad=F32), tok(f, grad=BF16)], [par(row(w["norm_ffn_post"], i))], [dh], f"resid_ffn{i}_b")
        else:
            (dh, df), (gn["norm_ffn_post"][i], gn["norm_mix_pre"][i + 1]) = rowwise_bwd(
                fn_resid_norm, [tok(h_mid, grad=F32), tok(f, grad=BF16)],
                [par(row(w["norm_ffn_post"], i)), par(row(w["norm_mix_pre"], i + 1))], [dh, da], f"resid_ffn{i}_b")
        db, g_ffn[i] = ffn_bwd(df, sv_ffn, WF[i], f"ffn{i}")
        (dh, dm), (gn["norm_mix_post"][i], gn["norm_ffn_pre"][i]) = rowwise_bwd(
            fn_resid_norm, [tok(h_in, grad=F32), tok(mix, grad=BF16)],
            [par(row(w["norm_mix_post"], i)), par(row(w["norm_ffn_pre"], i))], [dh, db], f"resid_mix{i}_b")
        if i == 0:
            def riders(d_out0):
                return [_cols_to_slots(g_attn[1]["w_in"][None]), _rows_to_slots(jnp.stack([d_out0, g_attn[1]["w_out"]])),
                        _cols_to_slots(jnp.stack([g["w_in"] for g in g_dn])), _rows_to_slots(jnp.stack([g["w_out"] for g in g_dn])),
                        _cols_to_slots(jnp.stack([g["up"] for g in g_ffn])), _rows_to_slots(jnp.stack([g["down"] for g in g_ffn]))]
            da, g_attn[0], landed_grads = attn_bwd(dm, sv_mix, WA[0], "attn0", riders)
        elif i % 2 == 0:
            da, g_attn[j], _ = attn_bwd(dm, sv_mix, WA[j], f"attn{j}")
        else:
            da, g_dn[j] = dn_bwd(dm, sv_mix, WD[j], f"dn{j}")
    (dh0,), (gn["norm_mix_pre"][0],) = rowwise_bwd(fn_id_norm, [tok(h0, grad=F32)], [par(row(w["norm_mix_pre"], 0))], [dh, da], "norm0_b")
    grad_x = dh0[N_META:T][None]

    full = dict(
        meta_tokens=dh0[:N_META],
        **{n: jnp.concatenate(gn[n], axis=0) for n in gn},
        attn_b_forget=jnp.stack([g["b_forget"] for g in g_attn]), attn_q_norm=jnp.stack([g["q_norm"] for g in g_attn]),
        attn_k_norm=jnp.stack([g["k_norm"] for g in g_attn]),
        dn_conv=jnp.stack([g["conv"] for g in g_dn]), dn_a_log=jnp.stack([g["a_log"] for g in g_dn]),
        dn_dt_bias=jnp.stack([g["dt_bias"] for g in g_dn]), dn_o_norm=jnp.stack([g["o_norm"] for g in g_dn]),
        ffn_conv=jnp.stack([g["conv"] for g in g_ffn]))
    flat = jnp.concatenate([full[n].reshape(-1) for n in SMALL_NAMES])
    rows = -(-flat.shape[0] // PACK_COLS)
    rows = -(-rows // 8) * 8
    pack = jnp.pad(flat, (0, rows * PACK_COLS - flat.shape[0])).reshape(rows, PACK_COLS)
    own4 = exchange_chips(False, [pack], "gather_small_grads")[0]
    sib4 = swap_cores([own4], "swap_small_grads")[0]
    tot = sum8(own4, sib4, "sum_small_grads").reshape(-1)
    grads, off = {}, 0
    for n in SMALL_NAMES:
        size = full[n].size
        g = tot[off:off + size].reshape(full[n].shape)
        off += size
        if g.shape != w[n].shape:
            g = lax.dynamic_slice_in_dim(g, slot * w[n].shape[-1], w[n].shape[-1], axis=g.ndim - 1)
        grads[n] = g
    d_s, m_s, v_s = adamw_small([as2d(w[n]) for n in SMALL_NAMES], [as2d(m[n]) for n in SMALL_NAMES],
                                [as2d(v[n]) for n in SMALL_NAMES], [as2d(grads[n]) for n in SMALL_NAMES], "adamw_small")
    deltas = {n: d.reshape(w[n].shape) for n, d in zip(SMALL_NAMES, d_s, strict=True)}
    new_m = {n: d.reshape(w[n].shape) for n, d in zip(SMALL_NAMES, m_s, strict=True)}
    new_v = {n: d.reshape(w[n].shape) for n, d in zip(SMALL_NAMES, v_s, strict=True)}

    last = exchange_chips(True, [_cols_to_slots(g_attn[0]["w_in"][None])], "scatter_last")[0]
    recv = [jnp.concatenate([last, landed_grads[0]], axis=1)] + list(landed_grads[1:])
    part = [sum_slots(r, f"sum_{n}") for n, r in zip(BIG_NAMES, recv, strict=True)]
    other = swap_cores(part, "swap_grads")
    for n, pa, pb in zip(BIG_NAMES, part, other, strict=True):
        g, d, m2, v2 = adamw_big(as2d(w[n]), as2d(m[n]), as2d(v[n]), pa, pb, f"adamw_{n}")
        grads[n], deltas[n], new_m[n], new_v[n] = (t.reshape(w[n].shape) for t in (g, d, m2, v2))
    return loss, grad_x, grads, deltas, new_m, new_v


def kernel(x, meta_tokens, norm_mix_pre, norm_mix_post, norm_ffn_pre, norm_ffn_post, attn_w_in, attn_b_forget, attn_q_norm, attn_k_norm, attn_w_out, dn_w_in, dn_conv, dn_a_log, dn_dt_bias, dn_o_norm, dn_w_out, ffn_w_up, ffn_conv, ffn_w_down, loss_target, m_meta_tokens, m_norm_mix_pre, m_norm_mix_post, m_norm_ffn_pre, m_norm_ffn_post, m_attn_w_in, m_attn_b_forget, m_attn_q_norm, m_attn_k_norm, m_attn_w_out, m_dn_w_in, m_dn_conv, m_dn_a_log, m_dn_dt_bias, m_dn_o_norm, m_dn_w_out, m_ffn_w_up, m_ffn_conv, m_ffn_w_down, v_meta_tokens, v_norm_mix_pre, v_norm_mix_post, v_norm_ffn_pre, v_norm_ffn_post, v_attn_w_in, v_attn_b_forget, v_attn_q_norm, v_attn_k_norm, v_attn_w_out, v_dn_w_in, v_dn_conv, v_dn_a_log, v_dn_dt_bias, v_dn_o_norm, v_dn_w_out, v_ffn_w_up, v_ffn_conv, v_ffn_w_down):
    w = dict(meta_tokens=meta_tokens, norm_mix_pre=norm_mix_pre, norm_mix_post=norm_mix_post, norm_ffn_pre=norm_ffn_pre, norm_ffn_post=norm_ffn_post, attn_w_in=attn_w_in, attn_b_forget=attn_b_forget, attn_q_norm=attn_q_norm, attn_k_norm=attn_k_norm, attn_w_out=attn_w_out, dn_w_in=dn_w_in, dn_conv=dn_conv, dn_a_log=dn_a_log, dn_dt_bias=dn_dt_bias, dn_o_norm=dn_o_norm, dn_w_out=dn_w_out, ffn_w_up=ffn_w_up, ffn_conv=ffn_conv, ffn_w_down=ffn_w_down)
    m = dict(meta_tokens=m_meta_tokens, norm_mix_pre=m_norm_mix_pre, norm_mix_post=m_norm_mix_post, norm_ffn_pre=m_norm_ffn_pre, norm_ffn_post=m_norm_ffn_post, attn_w_in=m_attn_w_in, attn_b_forget=m_attn_b_forget, attn_q_norm=m_attn_q_norm, attn_k_norm=m_attn_k_norm, attn_w_out=m_attn_w_out, dn_w_in=m_dn_w_in, dn_conv=m_dn_conv, dn_a_log=m_dn_a_log, dn_dt_bias=m_dn_dt_bias, dn_o_norm=m_dn_o_norm, dn_w_out=m_dn_w_out, ffn_w_up=m_ffn_w_up, ffn_conv=m_ffn_conv, ffn_w_down=m_ffn_w_down)
    v = dict(meta_tokens=v_meta_tokens, norm_mix_pre=v_norm_mix_pre, norm_mix_post=v_norm_mix_post, norm_ffn_pre=v_norm_ffn_pre, norm_ffn_post=v_norm_ffn_post, attn_w_in=v_attn_w_in, attn_b_forget=v_attn_b_forget, attn_q_norm=v_attn_q_norm, attn_k_norm=v_attn_k_norm, attn_w_out=v_attn_w_out, dn_w_in=v_dn_w_in, dn_conv=v_dn_conv, dn_a_log=v_dn_a_log, dn_dt_bias=v_dn_dt_bias, dn_o_norm=v_dn_o_norm, dn_w_out=v_dn_w_out, ffn_w_up=v_ffn_w_up, ffn_conv=v_ffn_conv, ffn_w_down=v_ffn_w_down)
    loss, grad_x, grads, deltas, new_m, new_v = train_step(x, loss_target, w, m, v)
    return (loss, grad_x, *[grads[n] for n in WEIGHT_NAMES], *[deltas[n] for n in WEIGHT_NAMES],
            *[new_m[n] for n in WEIGHT_NAMES], *[new_v[n] for n in WEIGHT_NAMES])
```

```python
import functools

import jax
import jax.numpy as jnp
from jax import lax
from jax.experimental import pallas as pl
from jax.experimental.pallas import tpu as pltpu

F32, BF16 = jnp.float32, jnp.bfloat16

D_MODEL = 1024
N_META = 16
ATT_HEADS, ATT_DH = 16, 64
ATT_HD = ATT_HEADS * ATT_DH
AUG = 128
ATT_AUG = ATT_HEADS * AUG
DN_HEADS, DN_DH = 8, 128
DN_HD = DN_HEADS * DN_DH
DN_CHUNK = 64
FFN_DIM = 2816
DEPTH = 4
EPS = 1e-6
NEG = -1e30

ADAM_LR, ADAM_B1, ADAM_B2, ADAM_EPS, ADAM_WD, ADAM_STEP = 0.001, 0.9, 0.999, 1e-08, 0.01, 10

TM = 640
HALO = 8
VMEM_LIMIT = 52 * 1024 * 1024
MESH_ID = pl.DeviceIdType.MESH


def _cp(sem):
    return pltpu.CompilerParams(dimension_semantics=sem, vmem_limit_bytes=VMEM_LIMIT)


def _pick(n, cands):
    for c in cands:
        if n % c == 0:
            return c
    return n


def matmul(a, b, mode, out_dtype, name, add=None, ncols=None):
    layer = None
    if isinstance(b, tuple):
        b, layer = b
    bshape = b.shape[-2:]
    if mode == "nn":
        (M, K), N = a.shape, ncols or bshape[1]
    elif mode == "nt":
        (M, K), N = a.shape, bshape[0]
    else:
        (K, M), N = a.shape, bshape[1]
    tm = _pick(M, (TM, 1024, 1408, 512, 256, 128))
    tn = _pick(N, (1024, 1408, 512, 256, 128))
    tk = _pick(K, (TM, 1024, 1408, 512, 256, 128))
    nk = K // tk
    if mode == "nn":
        a_spec = pl.BlockSpec((tm, tk), lambda i, j, k: (i, k))
        b_spec = pl.BlockSpec((tk, tn), lambda i, j, k: (k, j))
        dims = (((1,), (0,)), ((), ()))
    elif mode == "nt":
        a_spec = pl.BlockSpec((tm, tk), lambda i, j, k: (i, k))
        b_spec = pl.BlockSpec((tn, tk), lambda i, j, k: (j, k))
        dims = (((1,), (1,)), ((), ()))
    else:
        a_spec = pl.BlockSpec((tk, tm), lambda i, j, k: (k, i))
        b_spec = pl.BlockSpec((tk, tn), lambda i, j, k: (k, j))
        dims = (((0,), (0,)), ((), ()))
    if layer is not None:
        b_spec = pl.BlockSpec((None,) + b_spec.block_shape, functools.partial(lambda i, j, k, f: (layer,) + f(i, j, k), f=b_spec.index_map))
    o_spec = pl.BlockSpec((tm, tn), lambda i, j, k: (i, j))
    has_add = add is not None

    def body(*refs):
        if has_add:
            a_ref, b_ref, add_ref, o_ref, acc_ref = refs
        else:
            a_ref, b_ref, o_ref, acc_ref = refs
        k = pl.program_id(2)
        part = lax.dot_general(a_ref[...], b_ref[...], dims, preferred_element_type=F32)

        @pl.when(k == 0)
        def _():
            acc_ref[...] = part

        @pl.when(k > 0)
        def _():
            acc_ref[...] += part

        @pl.when(k == nk - 1)
        def _():
            r = acc_ref[...]
            if has_add:
                r = r + add_ref[...].astype(F32)
            o_ref[...] = r.astype(out_dtype)

    in_specs = [a_spec, b_spec] + ([o_spec] if has_add else [])
    args = (a, b) + ((add,) if has_add else ())
    return pl.pallas_call(
        body, name=name, grid=(M // tm, N // tn, nk), in_specs=in_specs, out_specs=o_spec,
        out_shape=jax.ShapeDtypeStruct((M, N), out_dtype),
        scratch_shapes=[pltpu.VMEM((tm, tn), F32)],
        compiler_params=_cp(("parallel", "parallel", "arbitrary")),
    )(*args)


def tok(arr, width=None, col=0, conv=None, grad=None, fixed=False):
    return dict(arr=arr, w=arr.shape[1] if width is None else width, col=col, conv=conv, grad=grad, step=0 if fixed else 1)


def par(arr, width=None, col=0):
    return dict(arr=arr, w=arr.shape[1] if width is None else width, col=col)


def _conv_apply(x, halo, w, ext_ref):
    K = w.shape[0]
    rows = x.shape[0]
    ext_ref[0:HALO, :] = halo
    ext_ref[HALO:, :] = x
    y = w[K - 1:K, :] * x
    for k in range(K - 1):
        y = y + w[k:k + 1, :] * ext_ref[pl.ds(HALO - (K - 1) + k, rows), :]
    return y


def _row_specs(toks, pars):
    specs, args = [], []
    for t in toks:
        specs.append(pl.BlockSpec((TM, t["w"]), functools.partial(lambda i, j, c, st: (i, c + st * j), c=t["col"], st=t["step"])))
        args.append(t["arr"])
        if t["conv"] is not None:
            specs.append(pl.BlockSpec((HALO, t["w"]), functools.partial(
                lambda i, j, c: (jnp.maximum(i * (TM // HALO) - 1, 0), c + j), c=t["col"])))
            args.append(t["arr"])
            cw, ccol = t["conv"]
            specs.append(pl.BlockSpec((cw.shape[0], t["w"]), functools.partial(lambda i, j, c: (0, c + j), c=ccol)))
            args.append(cw)
    for p in pars:
        specs.append(pl.BlockSpec((p["arr"].shape[0], p["w"]), functools.partial(lambda i, j, c: (0, c), c=p["col"])))
        args.append(p["arr"])
    return specs, args


def _row_load(toks, pars, refs, ext_refs):
    i = pl.program_id(0)
    vals, n, e = [], 0, 0
    for t in toks:
        x = refs[n][...].astype(F32)
        n += 1
        if t["conv"] is not None:
            halo = jnp.where(i == 0, 0.0, refs[n][...].astype(F32))
            w = refs[n + 1][...]
            n += 2
            x = _conv_apply(x, halo, w, ext_refs[e])
            e += 1
        vals.append(x)
    for _ in pars:
        vals.append(refs[n][...])
        n += 1
    return vals, n


def rowwise(fn, toks, pars, outs, name, nsplit=1, with_j=False):
    Tp = toks[0]["arr"].shape[0]
    specs, args = _row_specs(toks, pars)
    n_ext = sum(t["conv"] is not None for t in toks)

    def body(*refs):
        ext_refs = refs[len(refs) - n_ext:]
        vals, n = _row_load(toks, pars, refs, ext_refs)
        res = fn(pl.program_id(1), *vals) if with_j else fn(*vals)
        for r, o_ref in zip(res, refs[n:n + len(outs)], strict=True):
            o_ref[...] = r.astype(o_ref.dtype)

    return pl.pallas_call(
        body, name=name, grid=(Tp // TM, nsplit), in_specs=specs,
        out_specs=[pl.BlockSpec((TM, w), lambda i, j: (i, j)) for w, _ in outs],
        out_shape=[jax.ShapeDtypeStruct((Tp, w * nsplit), dt) for w, dt in outs],
        scratch_shapes=[pltpu.VMEM((TM + HALO, t["w"]), F32) for t in toks if t["conv"] is not None],
        compiler_params=_cp(("parallel", "parallel")),
    )(*args)


def rowwise_bwd(fn, toks, pars, cts, name, groups=None, par_grads=None, nsplit=1, with_j=False):
    Tp = toks[0]["arr"].shape[0]
    specs, args = _row_specs(toks, pars)
    n_ext = sum(t["conv"] is not None for t in toks)
    gidx = [k for k, t in enumerate(toks) if t["grad"] is not None]
    if groups is None:
        groups = [[k] for k in range(len(gidx))]
    par_grads = list(range(len(pars))) if par_grads is None else par_grads
    ct_specs, ct_args = [], []
    for c in cts:
        if c is not None:
            ct_specs.append(pl.BlockSpec((TM, c.shape[1] // nsplit), lambda i, j: (i, j)))
            ct_args.append(c)
    out_shapes, out_specs = [], []
    for g in groups:
        w = sum(toks[gidx[k]]["w"] for k in g)
        out_shapes.append(jax.ShapeDtypeStruct((Tp, w * nsplit), toks[gidx[g[0]]]["grad"]))
        out_specs.append(pl.BlockSpec((TM, w), lambda i, j: (i, j)))
    for k in par_grads:
        shp = (pars[k]["arr"].shape[0], pars[k]["w"])
        out_shapes.append(jax.ShapeDtypeStruct(shp, F32))
        out_specs.append(pl.BlockSpec(shp, lambda i, j: (0, 0)))
    n_ct = len(ct_args)

    def body(*refs):
        first = (pl.program_id(0) == 0) & (pl.program_id(1) == 0)
        ext_refs = refs[len(refs) - n_ext:]
        vals, n = _row_load(toks, pars, refs, ext_refs)
        ct_refs = refs[n:n + n_ct]
        o_refs = refs[n + n_ct:len(refs) - n_ext]
        res, vjp = jax.vjp(functools.partial(fn, pl.program_id(1)) if with_j else fn, *vals)
        ct_vals, c = [], 0
        for r, ct in zip(res, cts, strict=True):
            if ct is None:
                ct_vals.append(jnp.zeros_like(r))
            else:
                ct_vals.append(ct_refs[c][...].astype(F32))
                c += 1
        grads = vjp(tuple(ct_vals))
        for g, o_ref in zip(groups, o_refs[:len(groups)], strict=True):
            pieces = [grads[gidx[k]] for k in g]
            val = pieces[0] if len(pieces) == 1 else jnp.concatenate(pieces, axis=1)
            o_ref[...] = val.astype(o_ref.dtype)
        for k, o_ref in zip(par_grads, o_refs[len(groups):], strict=True):
            gk = grads[len(toks) + k]

            @pl.when(first)
            def _(o_ref=o_ref, gk=gk):
                o_ref[...] = gk

            @pl.when(jnp.logical_not(first))
            def _(o_ref=o_ref, gk=gk):
                o_ref[...] += gk

    res = pl.pallas_call(
        body, name=name, grid=(Tp // TM, nsplit), in_specs=specs + ct_specs, out_specs=out_specs, out_shape=out_shapes,
        scratch_shapes=[pltpu.VMEM((TM + HALO, t["w"]), F32) for t in toks if t["conv"] is not None],
        compiler_params=_cp(("arbitrary", "arbitrary")),
    )(*args, *ct_args)
    return res[:len(groups)], res[len(groups):]


ROWS = 16


def _shift_rows(win, s, lo, rows):
    return pltpu.roll(win, (-s) % win.shape[0], 0)[lo:lo + rows, :] if s else win[lo:lo + rows, :]


def conv_bwd(dy, x, w, name):
    Tp, W = dy.shape
    K = w.shape[0]
    wb = _pick(W, (512, 256, 128))
    nt = Tp // TM

    def body(dy_ref, dyn_ref, x_ref, w_ref, dx_ref, dw_ref, ext_dy):
        i = pl.program_id(1)
        ext_dy[0:TM, :] = dy_ref[...].astype(F32)
        ext_dy[TM:, :] = jnp.where(i == nt - 1, 0.0, dyn_ref[...].astype(F32))
        wv = w_ref[...]

        def chunk(t, sums):
            r0 = pl.multiple_of(t * ROWS, ROWS)
            dyw = ext_dy[pl.ds(r0, ROWS + HALO), :]
            xv = x_ref[pl.ds(r0, ROWS), :].astype(F32)
            dx, new = None, []
            for k in range(K):
                dys = _shift_rows(dyw, K - 1 - k, 0, ROWS)
                term = wv[k:k + 1, :] * dys
                dx = term if dx is None else dx + term
                prod = dys * xv
                new.append(sums[k] + (prod[0:HALO, :] + prod[HALO:ROWS, :]))
            dx_ref[pl.ds(r0, ROWS), :] = dx.astype(dx_ref.dtype)
            return tuple(new)

        sums = lax.fori_loop(0, TM // ROWS, chunk, tuple(jnp.zeros((HALO, wb), F32) for _ in range(K)))
        dwv = jnp.concatenate([jnp.sum(sm, axis=0, keepdims=True) for sm in sums], axis=0)

        @pl.when(i == 0)
        def _():
            dw_ref[...] = dwv

        @pl.when(i > 0)
        def _():
            dw_ref[...] += dwv

    return pl.pallas_call(
        body, name=name, grid=(W // wb, nt),
        in_specs=[
            pl.BlockSpec((TM, wb), lambda j, i: (i, j)),
            pl.BlockSpec((HALO, wb), lambda j, i: (jnp.minimum((i + 1) * (TM // HALO), Tp // HALO - 1), j)),
            pl.BlockSpec((TM, wb), lambda j, i: (i, j)),
            pl.BlockSpec((K, wb), lambda j, i: (0, j)),
        ],
        out_specs=[pl.BlockSpec((TM, wb), lambda j, i: (i, j)), pl.BlockSpec((K, wb), lambda j, i: (0, j))],
        out_shape=[jax.ShapeDtypeStruct((Tp, W), BF16), jax.ShapeDtypeStruct((K, W), F32)],
        scratch_shapes=[pltpu.VMEM((TM + HALO, wb), F32)],
        compiler_params=_cp(("parallel", "arbitrary")),
    )(dy, dy, x, w)


def _rms(x, g):
    return x * lax.rsqrt(jnp.mean(x * x, axis=-1, keepdims=True) + EPS) * g


def fn_norm(h, g):
    return (_rms(h, g),)


def fn_id_norm(h, g):
    return h, _rms(h, g)


def fn_resid_norm(h, m, g_post, g_next):
    h2 = h + _rms(m, g_post)
    return h2, _rms(h2, g_next)


def fn_resid(h, m, g_post):
    return (h + _rms(m, g_post),)


def fn_geglu(y):
    w = y.shape[1] // 2
    return (jax.nn.gelu(y[:, :w], approximate=True) * y[:, w:],)


def _split3(c):
    hi = c.astype(BF16).astype(F32)
    r = c - hi
    mid = r.astype(BF16).astype(F32)
    lo = (r - mid).astype(BF16).astype(F32)
    return hi, mid, lo


def _aug_cols(rows, entries):
    lane = lax.broadcasted_iota(jnp.int32, (rows, ATT_DH), 1)
    out = jnp.zeros((rows, ATT_DH), F32)
    for ln, val in entries:
        out = jnp.where(lane == ln, val, out)
    return out


def _head_col(c, idx):
    lane = lax.broadcasted_iota(jnp.int32, c.shape, 1)
    return jnp.sum(jnp.where(lane == idx, c, 0.0), axis=1, keepdims=True)


def fn_attn_prep_q(j, xp, c, gain):
    rows = xp.shape[0]
    out = []
    for hh in range(2):
        hi, mid, lo = _split3(lax.stop_gradient(_head_col(c, 2 * j + hh)))
        out += [_rms(xp[:, ATT_DH * hh:ATT_DH * (hh + 1)], gain) * (ATT_DH ** -0.5),
                _aug_cols(rows, [(0, hi), (1, mid), (2, lo), (3, 1.0), (4, 1.0), (5, 1.0)])]
    return (jnp.concatenate(out, axis=1),)


def fn_attn_prep_k(j, xp, c, gain):
    rows = xp.shape[0]
    out = []
    for hh in range(2):
        hi, mid, lo = _split3(lax.stop_gradient(_head_col(c, 2 * j + hh)))
        out += [_rms(xp[:, ATT_DH * hh:ATT_DH * (hh + 1)], gain),
                _aug_cols(rows, [(0, 1.0), (1, 1.0), (2, 1.0), (3, -hi), (4, -mid), (5, -lo), (6, 1.0), (7, 1.0), (8, 1.0)])]
    return (jnp.concatenate(out, axis=1),)


def fn_attn_prep_v(xp):
    rows = xp.shape[0]
    out = []
    for hh in range(2):
        out += [xp[:, ATT_DH * hh:ATT_DH * (hh + 1)], _aug_cols(rows, [(0, 1.0), (1, 1.0), (2, 1.0)])]
    return (jnp.concatenate(out, axis=1),)


def fn_attn_gate(o_aug, og):
    outs = []
    for h in range(og.shape[1] // ATT_DH):
        o = o_aug[:, AUG * h:AUG * h + ATT_DH]
        outs.append(o * jax.nn.sigmoid(og[:, ATT_DH * h:ATT_DH * (h + 1)]))
    return (jnp.concatenate(outs, axis=1),)


def fn_attn_bwd_prep(dgo, o_aug, og, q_aug):
    rows = dgo.shape[0]
    lane = lax.broadcasted_iota(jnp.int32, (rows, ATT_DH), 1)
    dos, dogs, qs = [], [], []
    for h in range(og.shape[1] // ATT_DH):
        sl = slice(ATT_DH * h, ATT_DH * (h + 1))
        o = o_aug[:, AUG * h:AUG * h + ATT_DH]
        lse = o_aug[:, AUG * h + ATT_DH:AUG * h + ATT_DH + 1]
        sig = jax.nn.sigmoid(og[:, sl])
        do = dgo[:, sl] * sig
        dogs.append(dgo[:, sl] * o * sig * (1.0 - sig))
        dhi, dmid, dlo = _split3(-jnp.sum(do * o, axis=-1, keepdims=True))
        dos += [do, _aug_cols(rows, [(0, dhi), (1, dmid), (2, dlo)])]
        lhi, lmid, llo = _split3(-lse)
        qa = q_aug[:, AUG * h + ATT_DH:AUG * (h + 1)]
        qa = jnp.where(lane == 6, lhi, jnp.where(lane == 7, lmid, jnp.where(lane == 8, llo, qa)))
        qs += [q_aug[:, AUG * h:AUG * h + ATT_DH], qa]
    return jnp.concatenate(dos, axis=1), jnp.concatenate(dogs, axis=1), jnp.concatenate(qs, axis=1)


def fn_attn_dc(dq_aug, dk_aug):
    lane = lax.broadcasted_iota(jnp.int32, (dk_aug.shape[0], AUG), 1)
    dc = jnp.zeros((dk_aug.shape[0], AUG), F32)
    for h in range(ATT_HEADS):
        col = AUG * h + ATT_DH
        dc = jnp.where(lane == h, dq_aug[:, col:col + 1] - dk_aug[:, col + 3:col + 4], dc)
    return (dc,)


def fn_dn_prep(j, y):
    s = jax.nn.silu(y)
    scale = jnp.where(j < DN_HD // 256, DN_DH ** -0.5, 1.0)
    out = []
    for hh in range(2):
        sh = s[:, DN_DH * hh:DN_DH * (hh + 1)]
        n = sh * lax.rsqrt(jnp.sum(sh * sh, axis=-1, keepdims=True) + EPS) * scale
        out.append(jnp.where(j < 2 * (DN_HD // 256), n, sh))
    return (jnp.concatenate(out, axis=1),)


def fn_dn_gates(lg, alog, dtb):
    lane = lax.broadcasted_iota(jnp.int32, lg.shape, 1)
    beta = jax.nn.sigmoid(lg)
    g = -jnp.exp(alog) * jax.nn.softplus(lg + dtb)
    return (jnp.where(lane < DN_HEADS, beta, jnp.where(lane < 2 * DN_HEADS, g, 0.0)),)


def fn_dn_post(o, og, gain):
    outs = []
    for h in range(DN_HEADS):
        sl = slice(DN_DH * h, DN_DH * (h + 1))
        outs.append(_rms(o[:, sl], gain) * jax.nn.silu(og[:, sl]))
    return (jnp.concatenate(outs, axis=1),)


def _tri(n, lower):
    r = lax.broadcasted_iota(jnp.int32, (n, n), 0)
    c = lax.broadcasted_iota(jnp.int32, (n, n), 1)
    return jnp.where((r >= c) if lower else (r <= c), 1.0, 0.0).astype(F32)


def forget_cumsum(fl, bias, name):
    Tp, W = fl.shape

    def body(fl_ref, b_ref, c_ref, carry):
        i = pl.program_id(0)

        @pl.when(i == 0)
        def _():
            carry[...] = jnp.zeros_like(carry)

        logf = jax.nn.log_sigmoid(fl_ref[...] + b_ref[...])
        c = jnp.dot(_tri(TM, True), logf, precision=lax.Precision.HIGHEST, preferred_element_type=F32) + carry[...]
        c_ref[...] = c
        carry[...] = c[TM - 1:TM, :]

    return pl.pallas_call(
        body, name=name, grid=(Tp // TM,),
        in_specs=[pl.BlockSpec((TM, W), lambda i: (i, 0)), pl.BlockSpec((1, W), lambda i: (0, 0))],
        out_specs=pl.BlockSpec((TM, W), lambda i: (i, 0)), out_shape=jax.ShapeDtypeStruct((Tp, W), F32),
        scratch_shapes=[pltpu.VMEM((1, W), F32)], compiler_params=_cp(("arbitrary",)),
    )(fl, bias)


def forget_cumsum_bwd(dc, fl, bias, name):
    Tp, W = fl.shape
    nt = Tp // TM

    def body(dc_ref, fl_ref, b_ref, dfl_ref, db_ref, carry):
        i = pl.program_id(0)

        @pl.when(i == 0)
        def _():
            carry[...] = jnp.zeros_like(carry)

        dlogf = jnp.dot(_tri(TM, False), dc_ref[...], precision=lax.Precision.HIGHEST, preferred_element_type=F32) + carry[...]
        carry[...] = dlogf[0:1, :]
        dfl = dlogf * jax.nn.sigmoid(-(fl_ref[...] + b_ref[...]))
        dfl_ref[...] = dfl.astype(dfl_ref.dtype)
        s = jnp.sum(dfl, axis=0, keepdims=True)

        @pl.when(i == 0)
        def _():
            db_ref[...] = s

        @pl.when(i > 0)
        def _():
            db_ref[...] += s

    rev = lambda i: (nt - 1 - i, 0)
    return pl.pallas_call(
        body, name=name, grid=(nt,),
        in_specs=[pl.BlockSpec((TM, W), rev), pl.BlockSpec((TM, W), rev), pl.BlockSpec((1, W), lambda i: (0, 0))],
        out_specs=[pl.BlockSpec((TM, W), rev), pl.BlockSpec((1, W), lambda i: (0, 0))],
        out_shape=[jax.ShapeDtypeStruct((Tp, W), BF16), jax.ShapeDtypeStruct((1, W), F32)],
        scratch_shapes=[pltpu.VMEM((1, W), F32)], compiler_params=_cp(("arbitrary",)),
    )(dc, fl, bias)


_HBM = pl.BlockSpec(memory_space=pltpu.HBM)


def _place():
    x, y, c = lax.axis_index("x"), lax.axis_index("y"), lax.axis_index("c")
    return x, y, c, [(1 - x, y), (x, 1 - y), (1 - x, 1 - y)]


def _chip_copies(scatter, ins, outs, send_sems, recv_sems, local_sems):
    x, y, c, chips = _place()
    s = 2 * x + y
    copies = []
    for a in range(len(ins)):
        copies.append(pltpu.make_async_copy(ins[a].at[s] if scatter else ins[a], outs[a].at[s], local_sems.at[a]))
        for p, (px, py) in enumerate(chips):
            copies.append(pltpu.make_async_remote_copy(
                src_ref=ins[a].at[2 * px + py] if scatter else ins[a], dst_ref=outs[a].at[s], send_sem=send_sems.at[a, p],
                recv_sem=recv_sems.at[a, p], device_id=(px, py, c), device_id_type=MESH_ID))
    return copies


def _chip_exchange_shapes(scatter, arrs):
    n = len(arrs)
    out_shape = [jax.ShapeDtypeStruct(a.shape if scatter else (4,) + a.shape, a.dtype) for a in arrs]
    sems = [pltpu.SemaphoreType.DMA((n, 3)), pltpu.SemaphoreType.DMA((n, 3)), pltpu.SemaphoreType.DMA((n,))]
    return out_shape, sems


def exchange_chips(scatter, arrs, name):
    n = len(arrs)

    def body(*refs):
        copies = _chip_copies(scatter, refs[:n], refs[n:2 * n], *refs[2 * n:])
        for cp in copies:
            cp.start()
        for cp in copies:
            cp.wait()

    out_shape, sems = _chip_exchange_shapes(scatter, arrs)
    return pl.pallas_call(body, name=name, in_specs=[_HBM] * n, out_specs=[_HBM] * n, out_shape=out_shape, scratch_shapes=sems)(*arrs)


def swap_cores(arrs, name):
    n = len(arrs)

    def body(*refs):
        ins, outs = refs[:n], refs[n:2 * n]
        send_sems, recv_sems = refs[2 * n:]
        x, y, c, _ = _place()
        copies = []
        for a in range(n):
            cp = pltpu.make_async_remote_copy(
                src_ref=ins[a], dst_ref=outs[a], send_sem=send_sems.at[a], recv_sem=recv_sems.at[a],
                device_id=(x, y, 1 - c), device_id_type=MESH_ID)
            cp.start()
            copies.append(cp)
        for cp in copies:
            cp.wait()

    return pl.pallas_call(
        body, name=name, in_specs=[_HBM] * n, out_specs=[_HBM] * n,
        out_shape=[jax.ShapeDtypeStruct(a.shape, a.dtype) for a in arrs],
        scratch_shapes=[pltpu.SemaphoreType.DMA((n,)), pltpu.SemaphoreType.DMA((n,))],
    )(*arrs)


_NT = (((1,), (1,)), ((), ()))
_TN = (((0,), (0,)), ((), ()))


def _cargo_edges(scatter, n, refs, nb):
    first = (pl.program_id(0) == 0) & (pl.program_id(1) == 0)
    last = (pl.program_id(0) == ATT_HEADS - 1) & (pl.program_id(1) == nb - 1)

    @pl.when(first)
    def _():
        for cp in _chip_copies(scatter, refs[:n], refs[n:2 * n], *refs[2 * n:]):
            cp.start()

    @pl.when(last)
    def _():
        for cp in _chip_copies(scatter, refs[:n], refs[n:2 * n], *refs[2 * n:]):
            cp.wait()


def flash_fwd(q_aug, k_aug, v_aug, name, cargo=()):
    Tp = q_aug.shape[0]
    nb = Tp // TM
    nc = len(cargo)

    def body(*refs):
        q_ref, k_ref, v_ref = refs[:3]
        o_ref = refs[3 + nc]
        if nc:
            _cargo_edges(False, nc, refs[3:3 + nc] + refs[4 + nc:], nb)
        i = pl.program_id(1)
        q = q_ref[...]

        def rows(j):
            return pl.ds(pl.multiple_of(j * TM, TM), TM)

        def scores(j):
            return lax.dot_general(q, k_ref[rows(j), :], _NT, preferred_element_type=F32)

        def absorb(j, s, m_old, acc):
            m_new = jnp.maximum(m_old, jnp.max(s, axis=-1, keepdims=True))
            p = jnp.exp(s - m_new)
            acc = jnp.exp(m_old - m_new) * acc + jnp.dot(p.astype(BF16), v_ref[rows(j), :], preferred_element_type=F32)
            return m_new, acc

        def step(j, carry):
            m_old, acc, s = carry
            s_next = scores(j + 1)
            return absorb(j, s, m_old, acc) + (s_next,)

        m, acc, s = lax.fori_loop(0, i, step, (jnp.full((TM, 1), NEG, F32), jnp.zeros((TM, AUG), F32), scores(0)))
        r = lax.broadcasted_iota(jnp.int32, (TM, TM), 0)
        c = lax.broadcasted_iota(jnp.int32, (TM, TM), 1)
        m, acc = absorb(i, jnp.where(c <= r, s, NEG), m, acc)
        l = acc[:, ATT_DH:ATT_DH + 1]
        lane = lax.broadcasted_iota(jnp.int32, (TM, AUG), 1)
        o_ref[...] = jnp.where(lane < ATT_DH, acc / l, m + jnp.log(l))

    head = pl.BlockSpec((Tp, AUG), lambda h, i: (0, h))
    cargo_shape, sems = _chip_exchange_shapes(False, cargo) if nc else ([], [])
    res = pl.pallas_call(
        body, name=name, grid=(ATT_HEADS, nb),
        in_specs=[pl.BlockSpec((TM, AUG), lambda h, i: (i, h)), head, head] + [_HBM] * nc,
        out_specs=[pl.BlockSpec((TM, AUG), lambda h, i: (i, h))] + [_HBM] * nc,
        out_shape=[jax.ShapeDtypeStruct((Tp, ATT_AUG), F32)] + cargo_shape, scratch_shapes=sems,
        compiler_params=_cp(("arbitrary", "arbitrary")),
    )(q_aug, k_aug, v_aug, *cargo)
    return res[0], res[1:]


def flash_bwd(q_aug2, k_aug, v_aug, do_aug, name, cargo=()):
    Tp = q_aug2.shape[0]
    nb = Tp // TM
    nc = len(cargo)

    def body(*refs):
        q_ref, do_ref, k_ref, v_ref = refs[:4]
        dq_ref, dk_ref, dv_ref = refs[4 + nc:7 + nc]
        if nc:
            _cargo_edges(True, nc, refs[4:4 + nc] + refs[7 + nc:], nb)
        j = pl.program_id(1)
        k, v = k_ref[...], v_ref[...]

        @pl.when(j == 0)
        def _():
            dq_ref[...] = jnp.zeros_like(dq_ref)

        def block(i, carry, masked):
            dk, dv = carry
            rows = pl.ds(pl.multiple_of(i * TM, TM), TM)
            q, do = q_ref[rows, :], do_ref[rows, :]
            st = lax.dot_general(k, q, _NT, preferred_element_type=F32)
            if masked:
                r = lax.broadcasted_iota(jnp.int32, (TM, TM), 0)
                c = lax.broadcasted_iota(jnp.int32, (TM, TM), 1)
                st = jnp.where(r <= c, st, NEG)
            pt = jnp.exp(st)
            dpt = lax.dot_general(v, do, _NT, preferred_element_type=F32)
            dst = (pt * dpt).astype(BF16)
            dv = dv + jnp.dot(pt.astype(BF16), do, preferred_element_type=F32)
            dk = dk + jnp.dot(dst, q, preferred_element_type=F32)
            dq_ref[rows, :] += lax.dot_general(dst, k, _TN, preferred_element_type=F32)
            return dk, dv

        carry = block(j, (jnp.zeros((TM, AUG), F32), jnp.zeros((TM, AUG), F32)), True)
        dk, dv = lax.fori_loop(j + 1, nb, lambda i, cr: block(i, cr, False), carry)
        dk_ref[...] = dk
        dv_ref[...] = dv

    head = pl.BlockSpec((Tp, AUG), lambda h, j: (0, h))
    blk = pl.BlockSpec((TM, AUG), lambda h, j: (j, h))
    cargo_shape, sems = _chip_exchange_shapes(True, cargo) if nc else ([], [])
    res = pl.pallas_call(
        body, name=name, grid=(ATT_HEADS, nb),
        in_specs=[head, head, blk, blk] + [_HBM] * nc, out_specs=[head, blk, blk] + [_HBM] * nc,
        out_shape=[jax.ShapeDtypeStruct((Tp, ATT_AUG), F32)] * 3 + cargo_shape, scratch_shapes=sems,
        compiler_params=_cp(("arbitrary", "arbitrary")),
    )(q_aug2, do_aug, k_aug, v_aug, *cargo)
    return res[:3], res[3:]


_BATCH = ((0,), (0,))


def _dg(x, y, cx, cy):
    return lax.dot_general(x.astype(BF16), y.astype(BF16), (((cx + 1,), (cy + 1,)), _BATCH), preferred_element_type=F32)


def _split2(x):
    hi = x.astype(BF16)
    return hi, (x - hi.astype(F32)).astype(BF16)


def _dg3(x, y, cx, cy):
    (xh, xl), (yh, yl) = _split2(x), _split2(y)
    d = lambda a, b: lax.dot_general(a, b, (((cx + 1,), (cy + 1,)), _BATCH), preferred_element_type=F32)
    return d(xh, yh) + (d(xl, yh) + d(xh, yl))


def _make_bdot(ca, cb, dg):
    @jax.custom_vjp
    def f(a, b):
        return dg(a, b, ca, cb)

    def fwd(a, b):
        return dg(a, b, ca, cb), (a, b)

    def bwd(res, ct):
        a, b = res
        da = dg(ct, b, 1, 1 if cb == 0 else 0) if ca == 1 else dg(b, ct, 1 if cb == 0 else 0, 1)
        db = dg(a, ct, 0 if ca == 1 else 1, 0) if cb == 0 else dg(ct, a, 0, 0 if ca == 1 else 1)
        return da, db

    f.defvjp(fwd, bwd)
    return f


_bd_nn, _bd_nt, _bd_tn = _make_bdot(1, 0, _dg), _make_bdot(1, 1, _dg), _make_bdot(0, 0, _dg)
_bd3_nn = _make_bdot(1, 0, _dg3)


@jax.custom_vjp
def _chunk_cumsum(x):
    return jnp.dot(_tri(x.shape[0], True), x, precision=lax.Precision.HIGHEST, preferred_element_type=F32)


def _chunk_cumsum_fwd(x):
    return _chunk_cumsum(x), None


def _chunk_cumsum_bwd(_, ct):
    return (jnp.dot(_tri(ct.shape[0], False), ct, precision=lax.Precision.HIGHEST, preferred_element_type=F32),)


_chunk_cumsum.defvjp(_chunk_cumsum_fwd, _chunk_cumsum_bwd)


def dn_chunk(S, q, k, v, bg):
    C = q.shape[0]
    H = DN_HEADS
    heads = lambda t: jnp.stack([t[:, DN_DH * h:DN_DH * (h + 1)] for h in range(H)])
    cols = lambda t, o: jnp.stack([t[:, o + h:o + h + 1] for h in range(H)])
    qh, kh, vh = heads(q), heads(k), heads(v)
    beta, gc = cols(bg, 0), cols(_chunk_cumsum(bg), H)
    r = lax.broadcasted_iota(jnp.int32, (H, C, C), 1)
    c = lax.broadcasted_iota(jnp.int32, (H, C, C), 2)
    gcb = jnp.broadcast_to(gc, (H, C, C))
    dec = jnp.exp(jnp.where(r >= c, gcb - jnp.swapaxes(gcb, 1, 2), NEG))
    e_gc = jnp.exp(gc)
    kq = _bd_nt(jnp.concatenate([kh, qh], axis=1), kh)
    A = jnp.where(r > c, kq[:, :C] * dec * beta, 0.0)
    qk = kq[:, C:] * dec
    X = -A
    P = _bd3_nn(X, X)
    N = X
    for i in range(5):
        NP = _bd3_nn(jnp.concatenate([N, P], axis=1) if i < 4 else N, P)
        N = N + P + NP[:, :C]
        if i < 4:
            P = NP[:, C:]
    R = jnp.concatenate([kh * (beta * e_gc), vh * beta], axis=2)
    WU = R + _bd3_nn(N, R)
    W, U0 = WU[:, :, :DN_DH], WU[:, :, DN_DH:]
    gl = gc[:, C - 1:C, :]
    WqS = _bd_nt(jnp.concatenate([W, qh * e_gc], axis=1), S)
    U = U0 - WqS[:, :C]
    O = WqS[:, C:] + _bd_nn(qk, U)
    S_new = jnp.exp(gl) * S + _bd_tn(U, kh * jnp.exp(gl - gc))
    return jnp.concatenate([O[h] for h in range(H)], axis=1), S_new


def dn_scan_fwd(qkv, bg, name):
    Tp = qkv.shape[0]
    n = Tp // DN_CHUNK
    tokspec = lambda w, c=0: pl.BlockSpec((DN_CHUNK, w), lambda i: (i, c))

    def body(q_ref, k_ref, v_ref, bg_ref, o_ref, s_ref, state):
        @pl.when(pl.program_id(0) == 0)
        def _():
            state[...] = jnp.zeros_like(state)

        S = state[...]
        s_ref[0] = S
        O, S_new = dn_chunk(S, q_ref[...].astype(F32), k_ref[...].astype(F32), v_ref[...].astype(F32), bg_ref[...])
        o_ref[...] = O.astype(o_ref.dtype)
        state[...] = S_new

    return pl.pallas_call(
        body, name=name, grid=(n,),
        in_specs=[tokspec(DN_HD, 0), tokspec(DN_HD, 1), tokspec(DN_HD, 2), tokspec(AUG)],
        out_specs=[tokspec(DN_HD), pl.BlockSpec((1, DN_HEADS, DN_DH, DN_DH), lambda i: (i, 0, 0, 0))],
        out_shape=[jax.ShapeDtypeStruct((Tp, DN_HD), BF16), jax.ShapeDtypeStruct((n, DN_HEADS, DN_DH, DN_DH), F32)],
        scratch_shapes=[pltpu.VMEM((DN_HEADS, DN_DH, DN_DH), F32)],
        compiler_params=_cp(("arbitrary",)),
    )(qkv, qkv, qkv, bg)


def dn_scan_bwd(qkv, bg, states, dO, name):
    Tp = qkv.shape[0]
    n = Tp // DN_CHUNK
    tokspec = lambda w, c=0: pl.BlockSpec((DN_CHUNK, w), lambda i: (n - 1 - i, c))

    def body(q_ref, k_ref, v_ref, bg_ref, s_ref, do_ref, dqkv_ref, dbg_ref, dstate):
        @pl.when(pl.program_id(0) == 0)
        def _():
            dstate[...] = jnp.zeros_like(dstate)

        _, vjp = jax.vjp(dn_chunk, s_ref[0], q_ref[...].astype(F32), k_ref[...].astype(F32), v_ref[...].astype(F32), bg_ref[...])
        dS, dq, dk, dv, dbg = vjp((do_ref[...].astype(F32), dstate[...]))
        dqkv_ref[...] = jnp.concatenate([dq, dk, dv], axis=1).astype(dqkv_ref.dtype)
        dbg_ref[...] = dbg
        dstate[...] = dS

    return pl.pallas_call(
        body, name=name, grid=(n,),
        in_specs=[tokspec(DN_HD, 0), tokspec(DN_HD, 1), tokspec(DN_HD, 2), tokspec(AUG),
                  pl.BlockSpec((1, DN_HEADS, DN_DH, DN_DH), lambda i: (n - 1 - i, 0, 0, 0)), tokspec(DN_HD)],
        out_specs=[tokspec(3 * DN_HD), tokspec(AUG)],
        out_shape=[jax.ShapeDtypeStruct((Tp, 3 * DN_HD), BF16), jax.ShapeDtypeStruct((Tp, AUG), F32)],
        scratch_shapes=[pltpu.VMEM((DN_HEADS, DN_DH, DN_DH), F32)],
        compiler_params=_cp(("arbitrary",)),
    )(qkv, qkv, qkv, bg, states, dO)


def loss_head(h, tgt, n_tok, name):
    Tp, Dm = h.shape

    def body(h_ref, t_ref, l_ref, dh_ref):
        i = pl.program_id(0)
        row = i * TM + lax.broadcasted_iota(jnp.int32, (TM, Dm), 0)
        e = jnp.where((row >= N_META) & (row < N_META + n_tok), h_ref[...] - t_ref[...], 0.0)
        dh_ref[...] = e * (1.0 / Dm)
        s = jnp.sum(e * e, axis=0, keepdims=True) * (0.5 / Dm)

        @pl.when(i == 0)
        def _():
            l_ref[...] = s

        @pl.when(i > 0)
        def _():
            l_ref[...] += s

    tile = pl.BlockSpec((TM, Dm), lambda i: (i, 0))
    return pl.pallas_call(
        body, name=name, grid=(Tp // TM,), in_specs=[tile, tile],
        out_specs=[pl.BlockSpec((1, Dm), lambda i: (0, 0)), tile],
        out_shape=[jax.ShapeDtypeStruct((1, Dm), F32), jax.ShapeDtypeStruct((Tp, Dm), F32)],
        compiler_params=_cp(("arbitrary",)),
    )(h, tgt)


def _row_block(R):
    return _pick(R, (256, 176, 128, 64, 32, 16, 8))


def sum_slots(parts, name):
    P, R, C = parts.shape
    rb = _row_block(R)

    def body(p_ref, o_ref):
        acc = p_ref[0].astype(F32)
        for s in range(1, P):
            acc = acc + p_ref[s].astype(F32)
        o_ref[...] = acc

    return pl.pallas_call(
        body, name=name, grid=(R // rb,), in_specs=[pl.BlockSpec((P, rb, C), lambda i: (0, i, 0))],
        out_specs=pl.BlockSpec((rb, C), lambda i: (i, 0)), out_shape=jax.ShapeDtypeStruct((R, C), F32),
        compiler_params=_cp(("parallel",)),
    )(parts)


def _adamw_math(w, g, m, v):
    m2 = ADAM_B1 * m + (1.0 - ADAM_B1) * g
    v2 = ADAM_B2 * v + (1.0 - ADAM_B2) * (g * g)
    m_hat = m2 / (1.0 - ADAM_B1 ** ADAM_STEP)
    v_hat = v2 / (1.0 - ADAM_B2 ** ADAM_STEP)
    delta = -ADAM_LR * (m_hat / (jnp.sqrt(v_hat) + ADAM_EPS) + ADAM_WD * w)
    return delta, m2, v2


def adamw_big(w, m, v, g_a, g_b, name):
    R, C = w.shape
    rb = _row_block(R)

    def body(w_ref, m_ref, v_ref, ga_ref, gb_ref, g_ref, d_ref, m2_ref, v2_ref):
        g = ga_ref[...] + gb_ref[...]
        delta, m2, v2 = _adamw_math(w_ref[...], g, m_ref[...], v_ref[...])
        g_ref[...], d_ref[...], m2_ref[...], v2_ref[...] = g, delta, m2, v2

    spec = pl.BlockSpec((rb, C), lambda i: (i, 0))
    return pl.pallas_call(
        body, name=name, grid=(R // rb,), in_specs=[spec] * 5, out_specs=[spec] * 4,
        out_shape=[jax.ShapeDtypeStruct((R, C), F32)] * 4, compiler_params=_cp(("parallel",)),
    )(w, m, v, g_a, g_b)


def adamw_small(ws, ms, vs, gs, name):
    n = len(ws)

    def body(*refs):
        w_r, m_r, v_r, g_r = refs[:n], refs[n:2 * n], refs[2 * n:3 * n], refs[3 * n:4 * n]
        d_o, m_o, v_o = refs[4 * n:5 * n], refs[5 * n:6 * n], refs[6 * n:7 * n]
        for k in range(n):
            delta, m2, v2 = _adamw_math(w_r[k][...], g_r[k][...], m_r[k][...], v_r[k][...])
            d_o[k][...], m_o[k][...], v_o[k][...] = delta, m2, v2

    shapes = [jax.ShapeDtypeStruct(w.shape, F32) for w in ws]
    res = pl.pallas_call(body, name=name, out_shape=shapes * 3, compiler_params=pltpu.CompilerParams(vmem_limit_bytes=VMEM_LIMIT))(
        *ws, *ms, *vs, *gs)
    return res[:n], res[n:2 * n], res[2 * n:]


def sum8(own4, sib4, name):
    _, R, C = own4.shape

    def body(a_ref, b_ref, o_ref):
        a = ((a_ref[0] + a_ref[1]) + a_ref[2]) + a_ref[3]
        b = ((b_ref[0] + b_ref[1]) + b_ref[2]) + b_ref[3]
        o_ref[...] = a + b

    return pl.pallas_call(body, name=name, out_shape=jax.ShapeDtypeStruct((R, C), F32),
                          compiler_params=pltpu.CompilerParams(vmem_limit_bytes=VMEM_LIMIT))(own4, sib4)


def _il(w):
    lead = w.shape[:-1]
    return w.reshape(lead + (2, FFN_DIM // 256, 256)).swapaxes(-3, -2).reshape(lead + (2 * FFN_DIM,))


def _unil(w):
    lead = w.shape[:-1]
    return w.reshape(lead + (FFN_DIM // 256, 2, 256)).swapaxes(-3, -2).reshape(lead + (2 * FFN_DIM,))


def _pad_cols(w, n=AUG):
    return jnp.pad(w, ((0, 0), (0, n - w.shape[1])))


ATT_SPLIT = ATT_HEADS // 2


def attn_fwd(a, W, tag, cargo=()):
    proj = matmul(a, W["main"], "nn", BF16, f"{tag}_proj", ncols=4 * D_MODEL)
    fl = matmul(a, W["small"], "nn", F32, f"{tag}_fl")
    c = forget_cumsum(fl, W["bias"], f"{tag}_cumsum")
    cfix = tok(c, fixed=True)
    q_aug = rowwise(fn_attn_prep_q, [tok(proj, AUG, 0), cfix], [par(W["qg"])], [(2 * AUG, BF16)], f"{tag}_prepq", ATT_SPLIT, True)[0]
    k_aug = rowwise(fn_attn_prep_k, [tok(proj, AUG, ATT_SPLIT), cfix], [par(W["kg"])], [(2 * AUG, BF16)], f"{tag}_prepk", ATT_SPLIT, True)[0]
    v_aug = rowwise(fn_attn_prep_v, [tok(proj, AUG, 2 * ATT_SPLIT)], [], [(2 * AUG, BF16)], f"{tag}_prepv", ATT_SPLIT)[0]
    o_aug, landed = flash_fwd(q_aug, k_aug, v_aug, f"{tag}_flash", cargo)
    go = rowwise(fn_attn_gate, [tok(o_aug, 2 * AUG, 0), tok(proj, AUG, 3 * ATT_SPLIT)], [], [(AUG, BF16)], f"{tag}_gate", ATT_SPLIT)[0]
    return (a, proj, fl, c, q_aug, k_aug, v_aug, o_aug, go), landed


def attn_out(saved, W, tag):
    return matmul(saved[-1], W["out"], "nn", BF16, f"{tag}_out")


def attn_bwd(dm, saved, W, tag, cargo=None):
    a, proj, fl, c, q_aug, k_aug, v_aug, o_aug, go = saved
    dgo = matmul(dm, W["out"], "nt", BF16, f"{tag}_dgo")
    d_out = matmul(go, dm, "tn", BF16, f"{tag}_dwout")
    do_aug, dog, q_aug2 = rowwise(
        fn_attn_bwd_prep, [tok(dgo, AUG, 0), tok(o_aug, 2 * AUG, 0), tok(proj, AUG, 3 * ATT_SPLIT), tok(q_aug, 2 * AUG, 0)], [],
        [(2 * AUG, BF16), (AUG, BF16), (2 * AUG, BF16)], f"{tag}_bprep", ATT_SPLIT)
    (dq_aug, dk_aug, dv_aug), landed = flash_bwd(q_aug2, k_aug, v_aug, do_aug, f"{tag}_flashb", cargo(d_out) if cargo else ())
    dc = rowwise(fn_attn_dc, [tok(dq_aug), tok(dk_aug)], [], [(AUG, F32)], f"{tag}_dc")[0]
    cfix = tok(c, fixed=True)
    (dq,), (d_qg,) = rowwise_bwd(fn_attn_prep_q, [tok(proj, AUG, 0, grad=BF16), cfix], [par(W["qg"])], [dq_aug],
                                 f"{tag}_prepqb", nsplit=ATT_SPLIT, with_j=True)
    (dk,), (d_kg,) = rowwise_bwd(fn_attn_prep_k, [tok(proj, AUG, ATT_SPLIT, grad=BF16), cfix], [par(W["kg"])], [dk_aug],
                                 f"{tag}_prepkb", nsplit=ATT_SPLIT, with_j=True)
    (dv,), _ = rowwise_bwd(fn_attn_prep_v, [tok(proj, AUG, 2 * ATT_SPLIT, grad=BF16)], [], [dv_aug], f"{tag}_prepvb", nsplit=ATT_SPLIT)
    dfl, d_bias = forget_cumsum_bwd(dc, fl, W["bias"], f"{tag}_cumsumb")
    dproj = jnp.concatenate([dq, dk, dv, dog], axis=1)
    da = matmul(dproj, W["main"], "nt", BF16, f"{tag}_da", add=matmul(dfl, W["small"], "nt", F32, f"{tag}_da0"))
    d_in = jnp.concatenate([matmul(a, dproj, "tn", BF16, f"{tag}_dw"), matmul(a, dfl, "tn", BF16, f"{tag}_dw2")[:, :ATT_HEADS]], axis=1)
    return da, dict(w_in=d_in, w_out=d_out, b_forget=d_bias[0, :ATT_HEADS], q_norm=d_qg[0], k_norm=d_kg[0]), landed


DN_SPLIT = 3 * DN_HD // 256


def dn_fwd(a, W, tag):
    proj = matmul(a, W["main"], "nn", BF16, f"{tag}_proj", ncols=4 * D_MODEL)
    lg = matmul(a, W["small"], "nn", F32, f"{tag}_lg")
    qkv = rowwise(fn_dn_prep, [tok(proj, 256, 0, conv=(W["conv"], 0))], [], [(256, BF16)], f"{tag}_prep", DN_SPLIT, True)[0]
    bg = rowwise(fn_dn_gates, [tok(lg)], [par(W["alog"]), par(W["dtb"])], [(AUG, F32)], f"{tag}_gates")[0]
    O, states = dn_scan_fwd(qkv, bg, f"{tag}_scan")
    go = rowwise(fn_dn_post, [tok(O), tok(proj, DN_HD, 3)], [par(W["ogain"])], [(DN_HD, BF16)], f"{tag}_post")[0]
    m = matmul(go, W["out"], "nn", BF16, f"{tag}_out")
    return m, (a, proj, lg, qkv, bg, O, states, go)


def dn_bwd(dm, saved, W, tag):
    a, proj, lg, qkv, bg, O, states, go = saved
    dgo = matmul(dm, W["out"], "nt", BF16, f"{tag}_dgo")
    d_out = matmul(go, dm, "tn", BF16, f"{tag}_dwout")
    (dO, dog), (d_ogain,) = rowwise_bwd(fn_dn_post, [tok(O, grad=BF16), tok(proj, DN_HD, 3, grad=BF16)], [par(W["ogain"])],
                                        [dgo], f"{tag}_postb")
    dqkv_n, dbg = dn_scan_bwd(qkv, bg, states, dO, f"{tag}_scanb")
    (dy,), _ = rowwise_bwd(fn_dn_prep, [tok(proj, 256, 0, conv=(W["conv"], 0), grad=BF16)], [], [dqkv_n], f"{tag}_prepb",
                           nsplit=DN_SPLIT, with_j=True)
    dqkv, d_conv = conv_bwd(dy, proj, W["conv"], f"{tag}_convb")
    (dlg,), (d_alog, d_dtb) = rowwise_bwd(fn_dn_gates, [tok(lg, grad=BF16)], [par(W["alog"]), par(W["dtb"])], [dbg], f"{tag}_gatesb")
    dproj = jnp.concatenate([dqkv, dog], axis=1)
    da = matmul(dproj, W["main"], "nt", BF16, f"{tag}_da", add=matmul(dlg, W["small"], "nt", F32, f"{tag}_da0"))
    d_in = jnp.concatenate([matmul(a, dproj, "tn", BF16, f"{tag}_dw"), matmul(a, dlg, "tn", BF16, f"{tag}_dw2")[:, :2 * DN_HEADS]], axis=1)
    return da, dict(w_in=d_in, w_out=d_out, conv=d_conv, a_log=d_alog[0, DN_HEADS:2 * DN_HEADS],
                    dt_bias=d_dtb[0, DN_HEADS:2 * DN_HEADS], o_norm=d_ogain[0])


FFN_SPLIT = FFN_DIM // 256


def ffn_fwd(b, W, tag):
    u0 = matmul(b, W["up"], "nn", BF16, f"{tag}_up")
    act = rowwise(fn_geglu, [tok(u0, 512, 0, conv=(W["conv"], 0))], [], [(256, BF16)], f"{tag}_act", nsplit=FFN_SPLIT)[0]
    f = matmul(act, W["down"], "nn", BF16, f"{tag}_down")
    return f, (b, u0, act)


def ffn_bwd(df, saved, W, tag):
    b, u0, act = saved
    dact = matmul(df, W["down"], "nt", BF16, f"{tag}_dact")
    d_down = matmul(act, df, "tn", BF16, f"{tag}_dwdown")
    (du,), _ = rowwise_bwd(fn_geglu, [tok(u0, 512, 0, conv=(W["conv"], 0), grad=BF16)], [], [dact], f"{tag}_actb", nsplit=FFN_SPLIT)
    du0, d_conv = conv_bwd(du, u0, W["conv"], f"{tag}_convb")
    db = matmul(du0, W["up"], "nt", BF16, f"{tag}_db")
    d_up = matmul(b, du0, "tn", BF16, f"{tag}_dwup")
    return db, dict(up=_unil(d_up), conv=_unil(d_conv), down=d_down)


def _cols_from_slots(g, L):
    K, Cs = g.shape[1] // L, g.shape[2]
    return g.reshape(4, L, K, Cs).transpose(1, 2, 0, 3).reshape(L, K, 4 * Cs)


def _rows_from_slots(g, L):
    Rs, C = g.shape[1] // L, g.shape[2]
    return g.reshape(4, L, Rs, C).transpose(1, 0, 2, 3).reshape(L, 4 * Rs, C)


def _cols_to_slots(w):
    L, K, C = w.shape
    return w.reshape(L, K, 4, C // 4).transpose(2, 0, 1, 3).reshape(4, L * K, C // 4)


def _rows_to_slots(w):
    L, R, C = w.shape
    return w.reshape(L, 4, R // 4, C).transpose(1, 0, 2, 3).reshape(4, L * (R // 4), C)


SMALL_NAMES = ["meta_tokens", "norm_mix_pre", "norm_mix_post", "norm_ffn_pre", "norm_ffn_post", "attn_b_forget", "attn_q_norm",
               "attn_k_norm", "dn_conv", "dn_a_log", "dn_dt_bias", "dn_o_norm", "ffn_conv"]
BIG_NAMES = ["attn_w_in", "attn_w_out", "dn_w_in", "dn_w_out", "ffn_w_up", "ffn_w_down"]
WEIGHT_NAMES = ["meta_tokens", "norm_mix_pre", "norm_mix_post", "norm_ffn_pre", "norm_ffn_post", "attn_w_in", "attn_b_forget",
                "attn_q_norm", "attn_k_norm", "attn_w_out", "dn_w_in", "dn_conv", "dn_a_log", "dn_dt_bias", "dn_o_norm", "dn_w_out",
                "ffn_w_up", "ffn_conv", "ffn_w_down"]
PACK_COLS = 1024


def train_step(x, loss_target, w, m, v):
    L = x.shape[1]
    T = L + N_META
    Tp = -(-T // TM) * TM
    xi, yi = lax.axis_index("x"), lax.axis_index("y")
    slot = 2 * xi + yi
    as2d = lambda t: t.reshape(-1, t.shape[-1])

    bf2d = lambda n: as2d(w[n]).astype(BF16)
    attn_in_shard = w["attn_w_in"].astype(BF16)
    first = exchange_chips(False, [attn_in_shard[0], w["meta_tokens"], as2d(w["dn_conv"]), as2d(w["ffn_conv"])], "gather_first")
    later = [attn_in_shard[1], bf2d("attn_w_out"), bf2d("dn_w_in"), bf2d("dn_w_out"), bf2d("ffn_w_up"), bf2d("ffn_w_down")]
    meta = first[1].transpose(1, 0, 2).reshape(N_META, D_MODEL)
    dn_conv = _cols_from_slots(first[2], 2)
    ffn_conv = _il(_cols_from_slots(first[3], DEPTH))
    lane8 = lambda t: jnp.pad(t[None, :], ((0, 0), (DN_HEADS, AUG - 2 * DN_HEADS)))

    def attn_weights(j, w_in, w_out):
        return dict(main=(w_in, 0), small=_pad_cols(w_in[0, :, 4 * ATT_HD:]), bias=_pad_cols(w["attn_b_forget"][j][None, :]),
                    qg=w["attn_q_norm"][j][None, :], kg=w["attn_k_norm"][j][None, :], out=w_out)

    WA = [attn_weights(0, _cols_from_slots(first[0], 1), None), None]
    WD, WF = None, None
    row = lambda t, i: t[i][None, :]

    h = jnp.concatenate([meta, x[0], jnp.zeros((Tp - T, D_MODEL), F32)], axis=0)
    tgt = jnp.concatenate([jnp.zeros((N_META, D_MODEL), F32), loss_target[0], jnp.zeros((Tp - T, D_MODEL), F32)], axis=0)
    h0 = h
    a = rowwise(fn_norm, [tok(h)], [par(row(w["norm_mix_pre"], 0))], [(D_MODEL, BF16)], "norm0")[0]
    saved = []
    for i in range(DEPTH):
        j = i // 2
        if i == 0:
            sv_mix, landed = attn_fwd(a, WA[0], "attn0", later)
            attn_out_w, dn_in, dn_out = _rows_from_slots(landed[1], 2), _cols_from_slots(landed[2], 2), _rows_from_slots(landed[3], 2)
            ffn_up, ffn_down = _il(_cols_from_slots(landed[4], DEPTH)), _rows_from_slots(landed[5], DEPTH)
            WA[0]["out"] = (attn_out_w, 0)
            WA[1] = attn_weights(1, _cols_from_slots(landed[0], 1), (attn_out_w, 1))
            WD = [dict(main=(dn_in, k), small=_pad_cols(dn_in[k, :, 4 * DN_HD:]), conv=dn_conv[k], alog=lane8(w["dn_a_log"][k]),
                       dtb=lane8(w["dn_dt_bias"][k]), ogain=w["dn_o_norm"][k][None, :], out=(dn_out, k)) for k in range(2)]
            WF = [dict(up=(ffn_up, k), conv=ffn_conv[k], down=(ffn_down, k)) for k in range(DEPTH)]
            mix = attn_out(sv_mix, WA[0], "attn0")
        elif i % 2 == 0:
            sv_mix, _ = attn_fwd(a, WA[j], f"attn{j}")
            mix = attn_out(sv_mix, WA[j], f"attn{j}")
        else:
            mix, sv_mix = dn_fwd(a, WD[j], f"dn{j}")
        h_mid, b = rowwise(fn_resid_norm, [tok(h), tok(mix)], [par(row(w["norm_mix_post"], i)), par(row(w["norm_ffn_pre"], i))],
                           [(D_MODEL, F32), (D_MODEL, BF16)], f"resid_mix{i}")
        f, sv_ffn = ffn_fwd(b, WF[i], f"ffn{i}")
        if i < DEPTH - 1:
            h_out, a = rowwise(fn_resid_norm, [tok(h_mid), tok(f)], [par(row(w["norm_ffn_post"], i)), par(row(w["norm_mix_pre"], i + 1))],
                               [(D_MODEL, F32), (D_MODEL, BF16)], f"resid_ffn{i}")
        else:
            h_out = rowwise(fn_resid, [tok(h_mid), tok(f)], [par(row(w["norm_ffn_post"], i))], [(D_MODEL, F32)], f"resid_ffn{i}")[0]
        saved.append((h, mix, sv_mix, h_mid, f, sv_ffn))
        h = h_out
    lvec, dh = loss_head(h, tgt, L, "loss_head")
    loss = lax.psum(jnp.sum(lvec), ("x", "y", "c"))

    gn = {n: [None] * DEPTH for n in ("norm_mix_pre", "norm_mix_post", "norm_ffn_pre", "norm_ffn_post")}
    g_attn, g_dn, g_ffn = [None, None], [None, None], [None] * DEPTH
    da = None
    for i in reversed(range(DEPTH)):
        j = i // 2
        h_in, mix, sv_mix, h_mid, f, sv_ffn = saved[i]
        if i == DEPTH - 1:
            (dh, df), (gn["norm_ffn_post"][i],) = rowwise_bwd(
                fn_resid, [tok(h_mid, grad=F32), tok(f, grad=BF16)], [par(row(w["norm_ffn_post"], i))], [dh], f"resid_ffn{i}_b")
        else:
            (dh, df), (gn["norm_ffn_post"][i], gn["norm_mix_pre"][i + 1]) = rowwise_bwd(
                fn_resid_norm, [tok(h_mid, grad=F32), tok(f, grad=BF16)],
                [par(row(w["norm_ffn_post"], i)), par(row(w["norm_mix_pre"], i + 1))], [dh, da], f"resid_ffn{i}_b")
        db, g_ffn[i] = ffn_bwd(df, sv_ffn, WF[i], f"ffn{i}")
        (dh, dm), (gn["norm_mix_post"][i], gn["norm_ffn_pre"][i]) = rowwise_bwd(
            fn_resid_norm, [tok(h_in, grad=F32), tok(mix, grad=BF16)],
            [par(row(w["norm_mix_post"], i)), par(row(w["norm_ffn_pre"], i))], [dh, db], f"resid_mix{i}_b")
        if i == 0:
            def riders(d_out0):
                return [_cols_to_slots(g_attn[1]["w_in"][None]), _rows_to_slots(jnp.stack([d_out0, g_attn[1]["w_out"]])),
                        _cols_to_slots(jnp.stack([g["w_in"] for g in g_dn])), _rows_to_slots(jnp.stack([g["w_out"] for g in g_dn])),
                        _cols_to_slots(jnp.stack([g["up"] for g in g_ffn])), _rows_to_slots(jnp.stack([g["down"] for g in g_ffn]))]
            da, g_attn[0], landed_grads = attn_bwd(dm, sv_mix, WA[0], "attn0", riders)
        elif i % 2 == 0:
            da, g_attn[j], _ = attn_bwd(dm, sv_mix, WA[j], f"attn{j}")
        else:
            da, g_dn[j] = dn_bwd(dm, sv_mix, WD[j], f"dn{j}")
    (dh0,), (gn["norm_mix_pre"][0],) = rowwise_bwd(fn_id_norm, [tok(h0, grad=F32)], [par(row(w["norm_mix_pre"], 0))], [dh, da], "norm0_b")
    grad_x = dh0[N_META:T][None]

    full = dict(
        meta_tokens=dh0[:N_META],
        **{n: jnp.concatenate(gn[n], axis=0) for n in gn},
        attn_b_forget=jnp.stack([g["b_forget"] for g in g_attn]), attn_q_norm=jnp.stack([g["q_norm"] for g in g_attn]),
        attn_k_norm=jnp.stack([g["k_norm"] for g in g_attn]),
        dn_conv=jnp.stack([g["conv"] for g in g_dn]), dn_a_log=jnp.stack([g["a_log"] for g in g_dn]),
        dn_dt_bias=jnp.stack([g["dt_bias"] for g in g_dn]), dn_o_norm=jnp.stack([g["o_norm"] for g in g_dn]),
        ffn_conv=jnp.stack([g["conv"] for g in g_ffn]))
    flat = jnp.concatenate([full[n].reshape(-1) for n in SMALL_NAMES])
    rows = -(-flat.shape[0] // PACK_COLS)
    rows = -(-rows // 8) * 8
    pack = jnp.pad(flat, (0, rows * PACK_COLS - flat.shape[0])).reshape(rows, PACK_COLS)
    own4 = exchange_chips(False, [pack], "gather_small_grads")[0]
    sib4 = swap_cores([own4], "swap_small_grads")[0]
    tot = sum8(own4, sib4, "sum_small_grads").reshape(-1)
    grads, off = {}, 0
    for n in SMALL_NAMES:
        size = full[n].size
        g = tot[off:off + size].reshape(full[n].shape)
        off += size
        if g.shape != w[n].shape:
            g = lax.dynamic_slice_in_dim(g, slot * w[n].shape[-1], w[n].shape[-1], axis=g.ndim - 1)
        grads[n] = g
    d_s, m_s, v_s = adamw_small([as2d(w[n]) for n in SMALL_NAMES], [as2d(m[n]) for n in SMALL_NAMES],
                                [as2d(v[n]) for n in SMALL_NAMES], [as2d(grads[n]) for n in SMALL_NAMES], "adamw_small")
    deltas = {n: d.reshape(w[n].shape) for n, d in zip(SMALL_NAMES, d_s, strict=True)}
    new_m = {n: d.reshape(w[n].shape) for n, d in zip(SMALL_NAMES, m_s, strict=True)}
    new_v = {n: d.reshape(w[n].shape) for n, d in zip(SMALL_NAMES, v_s, strict=True)}

    last = exchange_chips(True, [_cols_to_slots(g_attn[0]["w_in"][None])], "scatter_last")[0]
    recv = [jnp.concatenate([last, landed_grads[0]], axis=1)] + list(landed_grads[1:])
    part = [sum_slots(r, f"sum_{n}") for n, r in zip(BIG_NAMES, recv, strict=True)]
    other = swap_cores(part, "swap_grads")
    for n, pa, pb in zip(BIG_NAMES, part, other, strict=True):
        g, d, m2, v2 = adamw_big(as2d(w[n]), as2d(m[n]), as2d(v[n]), pa, pb, f"adamw_{n}")
        grads[n], deltas[n], new_m[n], new_v[n] = (t.reshape(w[n].shape) for t in (g, d, m2, v2))
    return loss, grad_x, grads, deltas, new_m, new_v


def kernel(x, meta_tokens, norm_mix_pre, norm_mix_post, norm_ffn_pre, norm_ffn_post, attn_w_in, attn_b_forget, attn_q_norm, attn_k_norm, attn_w_out, dn_w_in, dn_conv, dn_a_log, dn_dt_bias, dn_o_norm, dn_w_out, ffn_w_up, ffn_conv, ffn_w_down, loss_target, m_meta_tokens, m_norm_mix_pre, m_norm_mix_post, m_norm_ffn_pre, m_norm_ffn_post, m_attn_w_in, m_attn_b_forget, m_attn_q_norm, m_attn_k_norm, m_attn_w_out, m_dn_w_in, m_dn_conv, m_dn_a_log, m_dn_dt_bias, m_dn_o_norm, m_dn_w_out, m_ffn_w_up, m_ffn_conv, m_ffn_w_down, v_meta_tokens, v_norm_mix_pre, v_norm_mix_post, v_norm_ffn_pre, v_norm_ffn_post, v_attn_w_in, v_attn_b_forget, v_attn_q_norm, v_attn_k_norm, v_attn_w_out, v_dn_w_in, v_dn_conv, v_dn_a_log, v_dn_dt_bias, v_dn_o_norm, v_dn_w_out, v_ffn_w_up, v_ffn_conv, v_ffn_w_down):
    w = dict(meta_tokens=meta_tokens, norm_mix_pre=norm_mix_pre, norm_mix_post=norm_mix_post, norm_ffn_pre=norm_ffn_pre, norm_ffn_post=norm_ffn_post, attn_w_in=attn_w_in, attn_b_forget=attn_b_forget, attn_q_norm=attn_q_norm, attn_k_norm=attn_k_norm, attn_w_out=attn_w_out, dn_w_in=dn_w_in, dn_conv=dn_conv, dn_a_log=dn_a_log, dn_dt_bias=dn_dt_bias, dn_o_norm=dn_o_norm, dn_w_out=dn_w_out, ffn_w_up=ffn_w_up, ffn_conv=ffn_conv, ffn_w_down=ffn_w_down)
    m = dict(meta_tokens=m_meta_tokens, norm_mix_pre=m_norm_mix_pre, norm_mix_post=m_norm_mix_post, norm_ffn_pre=m_norm_ffn_pre, norm_ffn_post=m_norm_ffn_post, attn_w_in=m_attn_w_in, attn_b_forget=m_attn_b_forget, attn_q_norm=m_attn_q_norm, attn_k_norm=m_attn_k_norm, attn_w_out=m_attn_w_out, dn_w_in=m_dn_w_in, dn_conv=m_dn_conv, dn_a_log=m_dn_a_log, dn_dt_bias=m_dn_dt_bias, dn_o_norm=m_dn_o_norm, dn_w_out=m_dn_w_out, ffn_w_up=m_ffn_w_up, ffn_conv=m_ffn_conv, ffn_w_down=m_ffn_w_down)
    v = dict(meta_tokens=v_meta_tokens, norm_mix_pre=v_norm_mix_pre, norm_mix_post=v_norm_mix_post, norm_ffn_pre=v_norm_ffn_pre, norm_ffn_post=v_norm_ffn_post, attn_w_in=v_attn_w_in, attn_b_forget=v_attn_b_forget, attn_q_norm=v_attn_q_norm, attn_k_norm=v_attn_k_norm, attn_w_out=v_attn_w_out, dn_w_in=v_dn_w_in, dn_conv=v_dn_conv, dn_a_log=v_dn_a_log, dn_dt_bias=v_dn_dt_bias, dn_o_norm=v_dn_o_norm, dn_w_out=v_dn_w_out, ffn_w_up=v_ffn_w_up, ffn_conv=v_ffn_conv, ffn_w_down=v_ffn_w_down)
    loss, grad_x, grads, deltas, new_m, new_v = train_step(x, loss_target, w, m, v)
    return (loss, grad_x, *[grads[n] for n in WEIGHT_NAMES], *[deltas[n] for n in WEIGHT_NAMES],
            *[new_m[n] for n in WEIGHT_NAMES], *[new_v[n] for n in WEIGHT_NAMES])
```

```python
import functools

import jax
import jax.numpy as jnp
from jax import lax
from jax.experimental import pallas as pl
from jax.experimental.pallas import tpu as pltpu

F32, BF16 = jnp.float32, jnp.bfloat16

D_MODEL = 1024
N_META = 16
ATT_HEADS, ATT_DH = 16, 64
ATT_HD = ATT_HEADS * ATT_DH
AUG = 128
ATT_AUG = ATT_HEADS * AUG
DN_HEADS, DN_DH = 8, 128
DN_HD = DN_HEADS * DN_DH
DN_CHUNK = 64
FFN_DIM = 2816
DEPTH = 4
EPS = 1e-6
NEG = -1e30

ADAM_LR, ADAM_B1, ADAM_B2, ADAM_EPS, ADAM_WD, ADAM_STEP = 0.001, 0.9, 0.999, 1e-08, 0.01, 10

TM = 640
HALO = 8
VMEM_LIMIT = 52 * 1024 * 1024
MESH_ID = pl.DeviceIdType.MESH


def _cp(sem):
    return pltpu.CompilerParams(dimension_semantics=sem, vmem_limit_bytes=VMEM_LIMIT)


def _pick(n, cands):
    for c in cands:
        if n % c == 0:
            return c
    return n


def matmul(a, b, mode, out_dtype, name, add=None, ncols=None):
    layer = None
    if isinstance(b, tuple):
        b, layer = b
    bshape = b.shape[-2:]
    if mode == "nn":
        (M, K), N = a.shape, ncols or bshape[1]
    elif mode == "nt":
        (M, K), N = a.shape, bshape[0]
    else:
        (K, M), N = a.shape, bshape[1]
    tm = _pick(M, (TM, 1024, 1408, 512, 256, 128))
    tn = _pick(N, (1024, 1408, 512, 256, 128))
    tk = _pick(K, (1664, TM, 1024, 1408, 512, 256, 128))
    nk = K // tk
    if mode == "nn":
        a_spec = pl.BlockSpec((tm, tk), lambda i, j, k: (i, k))
        b_spec = pl.BlockSpec((tk, tn), lambda i, j, k: (k, j))
        dims = (((1,), (0,)), ((), ()))
    elif mode == "nt":
        a_spec = pl.BlockSpec((tm, tk), lambda i, j, k: (i, k))
        b_spec = pl.BlockSpec((tn, tk), lambda i, j, k: (j, k))
        dims = (((1,), (1,)), ((), ()))
    else:
        a_spec = pl.BlockSpec((tk, tm), lambda i, j, k: (k, i))
        b_spec = pl.BlockSpec((tk, tn), lambda i, j, k: (k, j))
        dims = (((0,), (0,)), ((), ()))
    if layer is not None:
        b_spec = pl.BlockSpec((None,) + b_spec.block_shape, functools.partial(lambda i, j, k, f: (layer,) + f(i, j, k), f=b_spec.index_map))
    o_spec = pl.BlockSpec((tm, tn), lambda i, j, k: (i, j))
    has_add = add is not None

    def body(*refs):
        if has_add:
            a_ref, b_ref, add_ref, o_ref, acc_ref = refs
        else:
            a_ref, b_ref, o_ref, acc_ref = refs
        k = pl.program_id(2)
        part = lax.dot_general(a_ref[...], b_ref[...], dims, preferred_element_type=F32)

        @pl.when(k == 0)
        def _():
            acc_ref[...] = part

        @pl.when(k > 0)
        def _():
            acc_ref[...] += part

        @pl.when(k == nk - 1)
        def _():
            r = acc_ref[...]
            if has_add:
                r = r + add_ref[...].astype(F32)
            o_ref[...] = r.astype(out_dtype)

    in_specs = [a_spec, b_spec] + ([o_spec] if has_add else [])
    args = (a, b) + ((add,) if has_add else ())
    return pl.pallas_call(
        body, name=name, grid=(M // tm, N // tn, nk), in_specs=in_specs, out_specs=o_spec,
        out_shape=jax.ShapeDtypeStruct((M, N), out_dtype),
        scratch_shapes=[pltpu.VMEM((tm, tn), F32)],
        compiler_params=_cp(("parallel", "parallel", "arbitrary")),
    )(*args)


def tok(arr, width=None, col=0, conv=None, grad=None, fixed=False):
    return dict(arr=arr, w=arr.shape[1] if width is None else width, col=col, conv=conv, grad=grad, step=0 if fixed else 1)


def par(arr, width=None, col=0):
    return dict(arr=arr, w=arr.shape[1] if width is None else width, col=col)


def _conv_apply(x, halo, w, ext_ref):
    K = w.shape[0]
    rows = x.shape[0]
    ext_ref[0:HALO, :] = halo
    ext_ref[HALO:, :] = x
    y = w[K - 1:K, :] * x
    for k in range(K - 1):
        y = y + w[k:k + 1, :] * ext_ref[pl.ds(HALO - (K - 1) + k, rows), :]
    return y


def _row_specs(toks, pars):
    specs, args = [], []
    for t in toks:
        specs.append(pl.BlockSpec((TM, t["w"]), functools.partial(lambda i, j, c, st: (i, c + st * j), c=t["col"], st=t["step"])))
        args.append(t["arr"])
        if t["conv"] is not None:
            specs.append(pl.BlockSpec((HALO, t["w"]), functools.partial(
                lambda i, j, c: (jnp.maximum(i * (TM // HALO) - 1, 0), c + j), c=t["col"])))
            args.append(t["arr"])
            cw, ccol = t["conv"]
            specs.append(pl.BlockSpec((cw.shape[0], t["w"]), functools.partial(lambda i, j, c: (0, c + j), c=ccol)))
            args.append(cw)
    for p in pars:
        specs.append(pl.BlockSpec((p["arr"].shape[0], p["w"]), functools.partial(lambda i, j, c: (0, c), c=p["col"])))
        args.append(p["arr"])
    return specs, args


def _row_load(toks, pars, refs, ext_refs):
    i = pl.program_id(0)
    vals, n, e = [], 0, 0
    for t in toks:
        x = refs[n][...].astype(F32)
        n += 1
        if t["conv"] is not None:
            halo = jnp.where(i == 0, 0.0, refs[n][...].astype(F32))
            w = refs[n + 1][...]
            n += 2
            x = _conv_apply(x, halo, w, ext_refs[e])
            e += 1
        vals.append(x)
    for _ in pars:
        vals.append(refs[n][...])
        n += 1
    return vals, n


def rowwise(fn, toks, pars, outs, name, nsplit=1, with_j=False):
    Tp = toks[0]["arr"].shape[0]
    specs, args = _row_specs(toks, pars)
    n_ext = sum(t["conv"] is not None for t in toks)

    def body(*refs):
        ext_refs = refs[len(refs) - n_ext:]
        vals, n = _row_load(toks, pars, refs, ext_refs)
        res = fn(pl.program_id(1), *vals) if with_j else fn(*vals)
        plain = 0
        for o, o_ref in zip(outs, refs[n:n + len(outs)], strict=True):
            if len(o) == 2:
                o_ref[...] = res[plain].astype(o_ref.dtype)
                plain += 1
            else:
                o_ref[...] = res[o[2]].T.astype(o_ref.dtype)

    return pl.pallas_call(
        body, name=name, grid=(Tp // TM, nsplit), in_specs=specs,
        out_specs=[pl.BlockSpec((TM, o[0]), lambda i, j: (i, j)) if len(o) == 2 else pl.BlockSpec((o[0], TM), lambda i, j: (j, i))
                   for o in outs],
        out_shape=[jax.ShapeDtypeStruct((Tp, o[0] * nsplit) if len(o) == 2 else (o[0] * nsplit, Tp), o[1]) for o in outs],
        scratch_shapes=[pltpu.VMEM((TM + HALO, t["w"]), F32) for t in toks if t["conv"] is not None],
        compiler_params=_cp(("parallel", "parallel")),
    )(*args)


def rowwise_bwd(fn, toks, pars, cts, name, groups=None, par_grads=None, nsplit=1, with_j=False):
    Tp = toks[0]["arr"].shape[0]
    specs, args = _row_specs(toks, pars)
    n_ext = sum(t["conv"] is not None for t in toks)
    gidx = [k for k, t in enumerate(toks) if t["grad"] is not None]
    if groups is None:
        groups = [[k] for k in range(len(gidx))]
    par_grads = list(range(len(pars))) if par_grads is None else par_grads
    ct_specs, ct_args = [], []
    for c in cts:
        if c is not None:
            ct_specs.append(pl.BlockSpec((TM, c.shape[1] // nsplit), lambda i, j: (i, j)))
            ct_args.append(c)
    out_shapes, out_specs = [], []
    for g in groups:
        w = sum(toks[gidx[k]]["w"] for k in g)
        out_shapes.append(jax.ShapeDtypeStruct((Tp, w * nsplit), toks[gidx[g[0]]]["grad"]))
        out_specs.append(pl.BlockSpec((TM, w), lambda i, j: (i, j)))
    for k in par_grads:
        shp = (pars[k]["arr"].shape[0], pars[k]["w"])
        out_shapes.append(jax.ShapeDtypeStruct(shp, F32))
        out_specs.append(pl.BlockSpec(shp, lambda i, j: (0, 0)))
    n_ct = len(ct_args)

    def body(*refs):
        first = (pl.program_id(0) == 0) & (pl.program_id(1) == 0)
        ext_refs = refs[len(refs) - n_ext:]
        vals, n = _row_load(toks, pars, refs, ext_refs)
        ct_refs = refs[n:n + n_ct]
        o_refs = refs[n + n_ct:len(refs) - n_ext]
        res, vjp = jax.vjp(functools.partial(fn, pl.program_id(1)) if with_j else fn, *vals)
        ct_vals, c = [], 0
        for r, ct in zip(res, cts, strict=True):
            if ct is None:
                ct_vals.append(jnp.zeros_like(r))
            else:
                ct_vals.append(ct_refs[c][...].astype(F32))
                c += 1
        grads = vjp(tuple(ct_vals))
        for g, o_ref in zip(groups, o_refs[:len(groups)], strict=True):
            pieces = [grads[gidx[k]] for k in g]
            val = pieces[0] if len(pieces) == 1 else jnp.concatenate(pieces, axis=1)
            o_ref[...] = val.astype(o_ref.dtype)
        for k, o_ref in zip(par_grads, o_refs[len(groups):], strict=True):
            gk = grads[len(toks) + k]

            @pl.when(first)
            def _(o_ref=o_ref, gk=gk):
                o_ref[...] = gk

            @pl.when(jnp.logical_not(first))
            def _(o_ref=o_ref, gk=gk):
                o_ref[...] += gk

    res = pl.pallas_call(
        body, name=name, grid=(Tp // TM, nsplit), in_specs=specs + ct_specs, out_specs=out_specs, out_shape=out_shapes,
        scratch_shapes=[pltpu.VMEM((TM + HALO, t["w"]), F32) for t in toks if t["conv"] is not None],
        compiler_params=_cp(("arbitrary", "arbitrary")),
    )(*args, *ct_args)
    return res[:len(groups)], res[len(groups):]


ROWS = 16


def _shift_rows(win, s, lo, rows):
    return pltpu.roll(win, (-s) % win.shape[0], 0)[lo:lo + rows, :] if s else win[lo:lo + rows, :]


def conv_bwd(dy, x, w, name):
    Tp, W = dy.shape
    K = w.shape[0]
    wb = _pick(W, (512, 256, 128))
    nt = Tp // TM

    def body(dy_ref, dyn_ref, x_ref, w_ref, dx_ref, dw_ref, ext_dy):
        i = pl.program_id(1)
        ext_dy[0:TM, :] = dy_ref[...].astype(F32)
        ext_dy[TM:, :] = jnp.where(i == nt - 1, 0.0, dyn_ref[...].astype(F32))
        wv = w_ref[...]

        def chunk(t, sums):
            r0 = pl.multiple_of(t * ROWS, ROWS)
            dyw = ext_dy[pl.ds(r0, ROWS + HALO), :]
            xv = x_ref[pl.ds(r0, ROWS), :].astype(F32)
            dx, new = None, []
            for k in range(K):
                dys = _shift_rows(dyw, K - 1 - k, 0, ROWS)
                term = wv[k:k + 1, :] * dys
                dx = term if dx is None else dx + term
                prod = dys * xv
                new.append(sums[k] + (prod[0:HALO, :] + prod[HALO:ROWS, :]))
            dx_ref[pl.ds(r0, ROWS), :] = dx.astype(dx_ref.dtype)
            return tuple(new)

        sums = lax.fori_loop(0, TM // ROWS, chunk, tuple(jnp.zeros((HALO, wb), F32) for _ in range(K)))
        dwv = jnp.concatenate([jnp.sum(sm, axis=0, keepdims=True) for sm in sums], axis=0)

        @pl.when(i == 0)
        def _():
            dw_ref[...] = dwv

        @pl.when(i > 0)
        def _():
            dw_ref[...] += dwv

    return pl.pallas_call(
        body, name=name, grid=(W // wb, nt),
        in_specs=[
            pl.BlockSpec((TM, wb), lambda j, i: (i, j)),
            pl.BlockSpec((HALO, wb), lambda j, i: (jnp.minimum((i + 1) * (TM // HALO), Tp // HALO - 1), j)),
            pl.BlockSpec((TM, wb), lambda j, i: (i, j)),
            pl.BlockSpec((K, wb), lambda j, i: (0, j)),
        ],
        out_specs=[pl.BlockSpec((TM, wb), lambda j, i: (i, j)), pl.BlockSpec((K, wb), lambda j, i: (0, j))],
        out_shape=[jax.ShapeDtypeStruct((Tp, W), BF16), jax.ShapeDtypeStruct((K, W), F32)],
        scratch_shapes=[pltpu.VMEM((TM + HALO, wb), F32)],
        compiler_params=_cp(("parallel", "arbitrary")),
    )(dy, dy, x, w)


def _rms(x, g):
    return x * lax.rsqrt(jnp.mean(x * x, axis=-1, keepdims=True) + EPS) * g


def fn_norm(h, g):
    return (_rms(h, g),)


def fn_id_norm(h, g):
    return h, _rms(h, g)


def fn_resid_norm(h, m, g_post, g_next):
    h2 = h + _rms(m, g_post)
    return h2, _rms(h2, g_next)


def fn_resid(h, m, g_post):
    return (h + _rms(m, g_post),)


def fn_geglu(y):
    w = y.shape[1] // 2
    return (jax.nn.gelu(y[:, :w], approximate=True) * y[:, w:],)


def _split3(c):
    hi = c.astype(BF16).astype(F32)
    r = c - hi
    mid = r.astype(BF16).astype(F32)
    lo = (r - mid).astype(BF16).astype(F32)
    return hi, mid, lo


def _aug_cols(rows, entries):
    lane = lax.broadcasted_iota(jnp.int32, (rows, ATT_DH), 1)
    out = jnp.zeros((rows, ATT_DH), F32)
    for ln, val in entries:
        out = jnp.where(lane == ln, val, out)
    return out


def _head_col(c, idx):
    lane = lax.broadcasted_iota(jnp.int32, c.shape, 1)
    return jnp.sum(jnp.where(lane == idx, c, 0.0), axis=1, keepdims=True)


def fn_attn_prep_q(j, xp, c, gain):
    rows = xp.shape[0]
    out = []
    for hh in range(2):
        hi, mid, lo = _split3(lax.stop_gradient(_head_col(c, 2 * j + hh)))
        out += [_rms(xp[:, ATT_DH * hh:ATT_DH * (hh + 1)], gain) * (ATT_DH ** -0.5),
                _aug_cols(rows, [(0, hi), (1, mid), (2, lo), (3, 1.0), (4, 1.0), (5, 1.0)])]
    return (jnp.concatenate(out, axis=1),)


def fn_attn_prep_k(j, xp, c, gain):
    rows = xp.shape[0]
    out = []
    for hh in range(2):
        hi, mid, lo = _split3(lax.stop_gradient(_head_col(c, 2 * j + hh)))
        out += [_rms(xp[:, ATT_DH * hh:ATT_DH * (hh + 1)], gain),
                _aug_cols(rows, [(0, 1.0), (1, 1.0), (2, 1.0), (3, -hi), (4, -mid), (5, -lo), (6, 1.0), (7, 1.0), (8, 1.0)])]
    return (jnp.concatenate(out, axis=1),)


def fn_attn_prep_v(xp):
    rows = xp.shape[0]
    out = []
    for hh in range(2):
        out += [xp[:, ATT_DH * hh:ATT_DH * (hh + 1)], _aug_cols(rows, [(0, 1.0), (1, 1.0), (2, 1.0)])]
    return (jnp.concatenate(out, axis=1),)


def fn_attn_gate(o_aug, og):
    outs = []
    for h in range(og.shape[1] // ATT_DH):
        o = o_aug[:, AUG * h:AUG * h + ATT_DH]
        outs.append(o * jax.nn.sigmoid(og[:, ATT_DH * h:ATT_DH * (h + 1)]))
    return (jnp.concatenate(outs, axis=1),)


def fn_attn_bwd_prep(dgo, o_aug, og, q_aug):
    rows = dgo.shape[0]
    lane = lax.broadcasted_iota(jnp.int32, (rows, ATT_DH), 1)
    dos, dogs, qs = [], [], []
    for h in range(og.shape[1] // ATT_DH):
        sl = slice(ATT_DH * h, ATT_DH * (h + 1))
        o = o_aug[:, AUG * h:AUG * h + ATT_DH]
        lse = o_aug[:, AUG * h + ATT_DH:AUG * h + ATT_DH + 1]
        sig = jax.nn.sigmoid(og[:, sl])
        do = dgo[:, sl] * sig
        dogs.append(dgo[:, sl] * o * sig * (1.0 - sig))
        dhi, dmid, dlo = _split3(-jnp.sum(do * o, axis=-1, keepdims=True))
        dos += [do, _aug_cols(rows, [(0, dhi), (1, dmid), (2, dlo)])]
        lhi, lmid, llo = _split3(-lse)
        qa = q_aug[:, AUG * h + ATT_DH:AUG * (h + 1)]
        qa = jnp.where(lane == 6, lhi, jnp.where(lane == 7, lmid, jnp.where(lane == 8, llo, qa)))
        qs += [q_aug[:, AUG * h:AUG * h + ATT_DH], qa]
    return jnp.concatenate(dos, axis=1), jnp.concatenate(dogs, axis=1), jnp.concatenate(qs, axis=1)


def fn_attn_dc(dq_aug, dk_aug):
    lane = lax.broadcasted_iota(jnp.int32, (dk_aug.shape[0], AUG), 1)
    dc = jnp.zeros((dk_aug.shape[0], AUG), F32)
    for h in range(ATT_HEADS):
        col = AUG * h + ATT_DH
        dc = jnp.where(lane == h, dq_aug[:, col:col + 1] - dk_aug[:, col + 3:col + 4], dc)
    return (dc,)


def fn_dn_prep(j, y):
    s = jax.nn.silu(y)
    scale = jnp.where(j < DN_HD // 256, DN_DH ** -0.5, 1.0)
    out = []
    for hh in range(2):
        sh = s[:, DN_DH * hh:DN_DH * (hh + 1)]
        n = sh * lax.rsqrt(jnp.sum(sh * sh, axis=-1, keepdims=True) + EPS) * scale
        out.append(jnp.where(j < 2 * (DN_HD // 256), n, sh))
    return (jnp.concatenate(out, axis=1),)


def fn_dn_gates(lg, alog, dtb):
    lane = lax.broadcasted_iota(jnp.int32, lg.shape, 1)
    beta = jax.nn.sigmoid(lg)
    g = -jnp.exp(alog) * jax.nn.softplus(lg + dtb)
    return (jnp.where(lane < DN_HEADS, beta, jnp.where(lane < 2 * DN_HEADS, g, 0.0)),)


def fn_dn_post(o, og, gain):
    outs = []
    for h in range(DN_HEADS):
        sl = slice(DN_DH * h, DN_DH * (h + 1))
        outs.append(_rms(o[:, sl], gain) * jax.nn.silu(og[:, sl]))
    return (jnp.concatenate(outs, axis=1),)


def _tri(n, lower):
    r = lax.broadcasted_iota(jnp.int32, (n, n), 0)
    c = lax.broadcasted_iota(jnp.int32, (n, n), 1)
    return jnp.where((r >= c) if lower else (r <= c), 1.0, 0.0).astype(F32)


def forget_cumsum(fl, bias, name):
    Tp, W = fl.shape

    def body(fl_ref, b_ref, c_ref, carry):
        i = pl.program_id(0)

        @pl.when(i == 0)
        def _():
            carry[...] = jnp.zeros_like(carry)

        logf = jax.nn.log_sigmoid(fl_ref[...] + b_ref[...])
        c = jnp.dot(_tri(TM, True), logf, precision=lax.Precision.HIGHEST, preferred_element_type=F32) + carry[...]
        c_ref[...] = c
        carry[...] = c[TM - 1:TM, :]

    return pl.pallas_call(
        body, name=name, grid=(Tp // TM,),
        in_specs=[pl.BlockSpec((TM, W), lambda i: (i, 0)), pl.BlockSpec((1, W), lambda i: (0, 0))],
        out_specs=pl.BlockSpec((TM, W), lambda i: (i, 0)), out_shape=jax.ShapeDtypeStruct((Tp, W), F32),
        scratch_shapes=[pltpu.VMEM((1, W), F32)], compiler_params=_cp(("arbitrary",)),
    )(fl, bias)


def forget_cumsum_bwd(dc, fl, bias, name):
    Tp, W = fl.shape
    nt = Tp // TM

    def body(dc_ref, fl_ref, b_ref, dfl_ref, db_ref, carry):
        i = pl.program_id(0)

        @pl.when(i == 0)
        def _():
            carry[...] = jnp.zeros_like(carry)

        dlogf = jnp.dot(_tri(TM, False), dc_ref[...], precision=lax.Precision.HIGHEST, preferred_element_type=F32) + carry[...]
        carry[...] = dlogf[0:1, :]
        dfl = dlogf * jax.nn.sigmoid(-(fl_ref[...] + b_ref[...]))
        dfl_ref[...] = dfl.astype(dfl_ref.dtype)
        s = jnp.sum(dfl, axis=0, keepdims=True)

        @pl.when(i == 0)
        def _():
            db_ref[...] = s

        @pl.when(i > 0)
        def _():
            db_ref[...] += s

    rev = lambda i: (nt - 1 - i, 0)
    return pl.pallas_call(
        body, name=name, grid=(nt,),
        in_specs=[pl.BlockSpec((TM, W), rev), pl.BlockSpec((TM, W), rev), pl.BlockSpec((1, W), lambda i: (0, 0))],
        out_specs=[pl.BlockSpec((TM, W), rev), pl.BlockSpec((1, W), lambda i: (0, 0))],
        out_shape=[jax.ShapeDtypeStruct((Tp, W), BF16), jax.ShapeDtypeStruct((1, W), F32)],
        scratch_shapes=[pltpu.VMEM((1, W), F32)], compiler_params=_cp(("arbitrary",)),
    )(dc, fl, bias)


_HBM = pl.BlockSpec(memory_space=pltpu.HBM)


def _place():
    x, y, c = lax.axis_index("x"), lax.axis_index("y"), lax.axis_index("c")
    return x, y, c, [(1 - x, y), (x, 1 - y), (1 - x, 1 - y)]


def _chip_copies(scatter, ins, outs, send_sems, recv_sems, local_sems):
    x, y, c, chips = _place()
    s = 2 * x + y
    copies = []
    for a in range(len(ins)):
        copies.append(pltpu.make_async_copy(ins[a].at[s] if scatter else ins[a], outs[a].at[s], local_sems.at[a]))
        for p, (px, py) in enumerate(chips):
            copies.append(pltpu.make_async_remote_copy(
                src_ref=ins[a].at[2 * px + py] if scatter else ins[a], dst_ref=outs[a].at[s], send_sem=send_sems.at[a, p],
                recv_sem=recv_sems.at[a, p], device_id=(px, py, c), device_id_type=MESH_ID))
    return copies


def _chip_exchange_shapes(scatter, arrs):
    n = len(arrs)
    out_shape = [jax.ShapeDtypeStruct(a.shape if scatter else (4,) + a.shape, a.dtype) for a in arrs]
    sems = [pltpu.SemaphoreType.DMA((n, 3)), pltpu.SemaphoreType.DMA((n, 3)), pltpu.SemaphoreType.DMA((n,))]
    return out_shape, sems


def exchange_chips(scatter, arrs, name):
    n = len(arrs)

    def body(*refs):
        copies = _chip_copies(scatter, refs[:n], refs[n:2 * n], *refs[2 * n:])
        for cp in copies:
            cp.start()
        for cp in copies:
            cp.wait()

    out_shape, sems = _chip_exchange_shapes(scatter, arrs)
    return pl.pallas_call(body, name=name, in_specs=[_HBM] * n, out_specs=[_HBM] * n, out_shape=out_shape, scratch_shapes=sems)(*arrs)


def swap_cores(arrs, name):
    n = len(arrs)

    def body(*refs):
        ins, outs = refs[:n], refs[n:2 * n]
        send_sems, recv_sems = refs[2 * n:]
        x, y, c, _ = _place()
        copies = []
        for a in range(n):
            cp = pltpu.make_async_remote_copy(
                src_ref=ins[a], dst_ref=outs[a], send_sem=send_sems.at[a], recv_sem=recv_sems.at[a],
                device_id=(x, y, 1 - c), device_id_type=MESH_ID)
            cp.start()
            copies.append(cp)
        for cp in copies:
            cp.wait()

    return pl.pallas_call(
        body, name=name, in_specs=[_HBM] * n, out_specs=[_HBM] * n,
        out_shape=[jax.ShapeDtypeStruct(a.shape, a.dtype) for a in arrs],
        scratch_shapes=[pltpu.SemaphoreType.DMA((n,)), pltpu.SemaphoreType.DMA((n,))],
    )(*arrs)


_NT = (((1,), (1,)), ((), ()))
_TN = (((0,), (0,)), ((), ()))


def _cargo_edges(scatter, n, refs, nb):
    first = (pl.program_id(0) == 0) & (pl.program_id(1) == 0)
    last = (pl.program_id(0) == ATT_HEADS - 1) & (pl.program_id(1) == nb - 1)

    @pl.when(first)
    def _():
        for cp in _chip_copies(scatter, refs[:n], refs[n:2 * n], *refs[2 * n:]):
            cp.start()

    @pl.when(last)
    def _():
        for cp in _chip_copies(scatter, refs[:n], refs[n:2 * n], *refs[2 * n:]):
            cp.wait()


def flash_fwd(q_aug, k_aug, v_aug, name, cargo=()):
    Tp = q_aug.shape[0]
    nb = Tp // TM
    nc = len(cargo)

    def body(*refs):
        q_ref, k_ref, v_ref = refs[:3]
        o_ref = refs[3 + nc]
        if nc:
            _cargo_edges(False, nc, refs[3:3 + nc] + refs[4 + nc:], nb)
        i = pl.program_id(1)
        q = q_ref[...]

        def rows(j):
            return pl.ds(pl.multiple_of(j * TM, TM), TM)

        def scores(j):
            return lax.dot_general(q, k_ref[rows(j), :], _NT, preferred_element_type=F32)

        def absorb(j, s, m_old, acc):
            m_new = jnp.maximum(m_old, jnp.max(s, axis=-1, keepdims=True))
            p = jnp.exp(s - m_new)
            acc = jnp.exp(m_old - m_new) * acc + jnp.dot(p.astype(BF16), v_ref[rows(j), :], preferred_element_type=F32)
            return m_new, acc

        def step(j, carry):
            m_old, acc, s = carry
            s_next = scores(j + 1)
            return absorb(j, s, m_old, acc) + (s_next,)

        m, acc, s = lax.fori_loop(0, i, step, (jnp.full((TM, 1), NEG, F32), jnp.zeros((TM, AUG), F32), scores(0)))
        r = lax.broadcasted_iota(jnp.int32, (TM, TM), 0)
        c = lax.broadcasted_iota(jnp.int32, (TM, TM), 1)
        m, acc = absorb(i, jnp.where(c <= r, s, NEG), m, acc)
        l = acc[:, ATT_DH:ATT_DH + 1]
        lane = lax.broadcasted_iota(jnp.int32, (TM, AUG), 1)
        o_ref[...] = jnp.where(lane < ATT_DH, acc / l, m + jnp.log(l))

    head = pl.BlockSpec((Tp, AUG), lambda h, i: (0, h))
    cargo_shape, sems = _chip_exchange_shapes(False, cargo) if nc else ([], [])
    res = pl.pallas_call(
        body, name=name, grid=(ATT_HEADS, nb),
        in_specs=[pl.BlockSpec((TM, AUG), lambda h, i: (i, h)), head, head] + [_HBM] * nc,
        out_specs=[pl.BlockSpec((TM, AUG), lambda h, i: (i, h))] + [_HBM] * nc,
        out_shape=[jax.ShapeDtypeStruct((Tp, ATT_AUG), F32)] + cargo_shape, scratch_shapes=sems,
        compiler_params=_cp(("arbitrary", "arbitrary")),
    )(q_aug, k_aug, v_aug, *cargo)
    return res[0], res[1:]


def flash_bwd(q_aug2, k_aug, v_aug, do_aug, name, cargo=()):
    Tp = q_aug2.shape[0]
    nb = Tp // TM
    nc = len(cargo)

    def body(*refs):
        q_ref, do_ref, k_ref, v_ref = refs[:4]
        dq_ref, dk_ref, dv_ref = refs[4 + nc:7 + nc]
        if nc:
            _cargo_edges(True, nc, refs[4:4 + nc] + refs[7 + nc:], nb)
        j = pl.program_id(1)
        k, v = k_ref[...], v_ref[...]

        @pl.when(j == 0)
        def _():
            dq_ref[...] = jnp.zeros_like(dq_ref)

        def block(i, carry, masked):
            dk, dv = carry
            rows = pl.ds(pl.multiple_of(i * TM, TM), TM)
            q, do = q_ref[rows, :], do_ref[rows, :]
            st = lax.dot_general(k, q, _NT, preferred_element_type=F32)
            if masked:
                r = lax.broadcasted_iota(jnp.int32, (TM, TM), 0)
                c = lax.broadcasted_iota(jnp.int32, (TM, TM), 1)
                st = jnp.where(r <= c, st, NEG)
            pt = jnp.exp(st)
            dpt = lax.dot_general(v, do, _NT, preferred_element_type=F32)
            dst = (pt * dpt).astype(BF16)
            dv = dv + jnp.dot(pt.astype(BF16), do, preferred_element_type=F32)
            dk = dk + jnp.dot(dst, q, preferred_element_type=F32)
            dq_ref[rows, :] += lax.dot_general(dst, k, _TN, preferred_element_type=F32)
            return dk, dv

        carry = block(j, (jnp.zeros((TM, AUG), F32), jnp.zeros((TM, AUG), F32)), True)
        dk, dv = lax.fori_loop(j + 1, nb, lambda i, cr: block(i, cr, False), carry)
        dk_ref[...] = dk
        dv_ref[...] = dv

    head = pl.BlockSpec((Tp, AUG), lambda h, j: (0, h))
    blk = pl.BlockSpec((TM, AUG), lambda h, j: (j, h))
    cargo_shape, sems = _chip_exchange_shapes(True, cargo) if nc else ([], [])
    res = pl.pallas_call(
        body, name=name, grid=(ATT_HEADS, nb),
        in_specs=[head, head, blk, blk] + [_HBM] * nc, out_specs=[head, blk, blk] + [_HBM] * nc,
        out_shape=[jax.ShapeDtypeStruct((Tp, ATT_AUG), F32)] * 3 + cargo_shape, scratch_shapes=sems,
        compiler_params=_cp(("arbitrary", "arbitrary")),
    )(q_aug2, do_aug, k_aug, v_aug, *cargo)
    return res[:3], res[3:]


_BATCH = ((0,), (0,))


def _dg(x, y, cx, cy):
    return lax.dot_general(x.astype(BF16), y.astype(BF16), (((cx + 1,), (cy + 1,)), _BATCH), preferred_element_type=F32)


def _split2(x):
    hi = x.astype(BF16)
    return hi, (x - hi.astype(F32)).astype(BF16)


def _dg3(x, y, cx, cy):
    (xh, xl), (yh, yl) = _split2(x), _split2(y)
    d = lambda a, b: lax.dot_general(a, b, (((cx + 1,), (cy + 1,)), _BATCH), preferred_element_type=F32)
    return d(xh, yh) + (d(xl, yh) + d(xh, yl))


def _make_bdot(ca, cb, dg):
    @jax.custom_vjp
    def f(a, b):
        return dg(a, b, ca, cb)

    def fwd(a, b):
        return dg(a, b, ca, cb), (a, b)

    def bwd(res, ct):
        a, b = res
        da = dg(ct, b, 1, 1 if cb == 0 else 0) if ca == 1 else dg(b, ct, 1 if cb == 0 else 0, 1)
        db = dg(a, ct, 0 if ca == 1 else 1, 0) if cb == 0 else dg(ct, a, 0, 0 if ca == 1 else 1)
        return da, db

    f.defvjp(fwd, bwd)
    return f


_bd_nn, _bd_nt, _bd_tn = _make_bdot(1, 0, _dg), _make_bdot(1, 1, _dg), _make_bdot(0, 0, _dg)
_bd3_nn = _make_bdot(1, 0, _dg3)


@jax.custom_vjp
def _chunk_cumsum(x):
    return jnp.dot(_tri(x.shape[0], True), x, precision=lax.Precision.HIGHEST, preferred_element_type=F32)


def _chunk_cumsum_fwd(x):
    return _chunk_cumsum(x), None


def _chunk_cumsum_bwd(_, ct):
    return (jnp.dot(_tri(ct.shape[0], False), ct, precision=lax.Precision.HIGHEST, preferred_element_type=F32),)


_chunk_cumsum.defvjp(_chunk_cumsum_fwd, _chunk_cumsum_bwd)


@jax.custom_vjp
def _inverse_given(A, N):
    return N


def _inverse_given_fwd(A, N):
    return N, N


def _inverse_given_bwd(N, ct):
    M = ct + _dg3(N, ct, 0, 0)
    return -(M + _dg3(M, N, 1, 1)), jnp.zeros_like(N)


_inverse_given.defvjp(_inverse_given_fwd, _inverse_given_bwd)


def dn_chunk(S, q, k, v, bg, N_given=None):
    C = q.shape[0]
    H = DN_HEADS
    heads = lambda t: jnp.stack([t[:, DN_DH * h:DN_DH * (h + 1)] for h in range(H)])
    cols = lambda t, o: jnp.stack([t[:, o + h:o + h + 1] for h in range(H)])
    qh, kh, vh = heads(q), heads(k), heads(v)
    beta, gc = cols(bg, 0), cols(_chunk_cumsum(bg), H)
    r = lax.broadcasted_iota(jnp.int32, (H, C, C), 1)
    c = lax.broadcasted_iota(jnp.int32, (H, C, C), 2)
    gcb = jnp.broadcast_to(gc, (H, C, C))
    dec = jnp.exp(jnp.where(r >= c, gcb - jnp.swapaxes(gcb, 1, 2), NEG))
    e_gc = jnp.exp(gc)
    kq = _bd_nt(jnp.concatenate([kh, qh], axis=1), kh)
    A = jnp.where(r > c, kq[:, :C] * dec * beta, 0.0)
    qk = kq[:, C:] * dec
    if N_given is None:
        X = -A
        P = _bd3_nn(X, X)
        N = X
        for i in range(5):
            NP = _bd3_nn(jnp.concatenate([N, P], axis=1) if i < 4 else N, P)
            N = N + P + NP[:, :C]
            if i < 4:
                P = NP[:, C:]
    else:
        N = _inverse_given(A, N_given)
    R = jnp.concatenate([kh * (beta * e_gc), vh * beta], axis=2)
    WU = R + _bd3_nn(N, R)
    W, U0 = WU[:, :, :DN_DH], WU[:, :, DN_DH:]
    gl = gc[:, C - 1:C, :]
    WqS = _bd_nt(jnp.concatenate([W, qh * e_gc], axis=1), S)
    U = U0 - WqS[:, :C]
    O = WqS[:, C:] + _bd_nn(qk, U)
    S_new = jnp.exp(gl) * S + _bd_tn(U, kh * jnp.exp(gl - gc))
    return jnp.concatenate([O[h] for h in range(H)], axis=1), S_new, N


def dn_scan_fwd(qkv, bg, name):
    Tp = qkv.shape[0]
    n = Tp // DN_CHUNK
    tokspec = lambda w, c=0: pl.BlockSpec((DN_CHUNK, w), lambda i: (i, c))

    def body(q_ref, k_ref, v_ref, bg_ref, o_ref, s_ref, n_ref, state):
        @pl.when(pl.program_id(0) == 0)
        def _():
            state[...] = jnp.zeros_like(state)

        S = state[...]
        s_ref[0] = S
        O, S_new, N = dn_chunk(S, q_ref[...].astype(F32), k_ref[...].astype(F32), v_ref[...].astype(F32), bg_ref[...])
        o_ref[...] = O.astype(o_ref.dtype)
        n_ref[0] = N
        state[...] = S_new

    return pl.pallas_call(
        body, name=name, grid=(n,),
        in_specs=[tokspec(DN_HD, 0), tokspec(DN_HD, 1), tokspec(DN_HD, 2), tokspec(AUG)],
        out_specs=[tokspec(DN_HD), pl.BlockSpec((1, DN_HEADS, DN_DH, DN_DH), lambda i: (i, 0, 0, 0)),
                   pl.BlockSpec((1, DN_HEADS, DN_CHUNK, DN_CHUNK), lambda i: (i, 0, 0, 0))],
        out_shape=[jax.ShapeDtypeStruct((Tp, DN_HD), BF16), jax.ShapeDtypeStruct((n, DN_HEADS, DN_DH, DN_DH), F32),
                   jax.ShapeDtypeStruct((n, DN_HEADS, DN_CHUNK, DN_CHUNK), F32)],
        scratch_shapes=[pltpu.VMEM((DN_HEADS, DN_DH, DN_DH), F32)],
        compiler_params=_cp(("arbitrary",)),
    )(qkv, qkv, qkv, bg)


def dn_scan_bwd(qkv, bg, states, inverses, dO, name):
    Tp = qkv.shape[0]
    n = Tp // DN_CHUNK
    tokspec = lambda w, c=0: pl.BlockSpec((DN_CHUNK, w), lambda i: (n - 1 - i, c))

    def body(q_ref, k_ref, v_ref, bg_ref, s_ref, n_ref, do_ref, dqkv_ref, dbg_ref, dstate):
        @pl.when(pl.program_id(0) == 0)
        def _():
            dstate[...] = jnp.zeros_like(dstate)

        N = n_ref[0]
        _, vjp = jax.vjp(lambda *xs: dn_chunk(*xs, N_given=N)[:2],
                         s_ref[0], q_ref[...].astype(F32), k_ref[...].astype(F32), v_ref[...].astype(F32), bg_ref[...])
        dS, dq, dk, dv, dbg = vjp((do_ref[...].astype(F32), dstate[...]))
        dqkv_ref[...] = jnp.concatenate([dq, dk, dv], axis=1).astype(dqkv_ref.dtype)
        dbg_ref[...] = dbg
        dstate[...] = dS

    return pl.pallas_call(
        body, name=name, grid=(n,),
        in_specs=[tokspec(DN_HD, 0), tokspec(DN_HD, 1), tokspec(DN_HD, 2), tokspec(AUG),
                  pl.BlockSpec((1, DN_HEADS, DN_DH, DN_DH), lambda i: (n - 1 - i, 0, 0, 0)),
                  pl.BlockSpec((1, DN_HEADS, DN_CHUNK, DN_CHUNK), lambda i: (n - 1 - i, 0, 0, 0)), tokspec(DN_HD)],
        out_specs=[tokspec(3 * DN_HD), tokspec(AUG)],
        out_shape=[jax.ShapeDtypeStruct((Tp, 3 * DN_HD), BF16), jax.ShapeDtypeStruct((Tp, AUG), F32)],
        scratch_shapes=[pltpu.VMEM((DN_HEADS, DN_DH, DN_DH), F32)],
        compiler_params=_cp(("arbitrary",)),
    )(qkv, qkv, qkv, bg, states, inverses, dO)


def loss_head(h, tgt, n_tok, name):
    Tp, Dm = h.shape

    def body(h_ref, t_ref, l_ref, dh_ref):
        i = pl.program_id(0)
        row = i * TM + lax.broadcasted_iota(jnp.int32, (TM, Dm), 0)
        e = jnp.where((row >= N_META) & (row < N_META + n_tok), h_ref[...] - t_ref[...], 0.0)
        dh_ref[...] = e * (1.0 / Dm)
        s = jnp.sum(e * e, axis=0, keepdims=True) * (0.5 / Dm)

        @pl.when(i == 0)
        def _():
            l_ref[...] = s

        @pl.when(i > 0)
        def _():
            l_ref[...] += s

    tile = pl.BlockSpec((TM, Dm), lambda i: (i, 0))
    return pl.pallas_call(
        body, name=name, grid=(Tp // TM,), in_specs=[tile, tile],
        out_specs=[pl.BlockSpec((1, Dm), lambda i: (0, 0)), tile],
        out_shape=[jax.ShapeDtypeStruct((1, Dm), F32), jax.ShapeDtypeStruct((Tp, Dm), F32)],
        compiler_params=_cp(("arbitrary",)),
    )(h, tgt)


def _row_block(R):
    return _pick(R, (256, 176, 128, 64, 32, 16, 8))


def sum_slots(parts, name):
    P, R, C = parts.shape
    rb = _row_block(R)

    def body(p_ref, o_ref):
        acc = p_ref[0].astype(F32)
        for s in range(1, P):
            acc = acc + p_ref[s].astype(F32)
        o_ref[...] = acc

    return pl.pallas_call(
        body, name=name, grid=(R // rb,), in_specs=[pl.BlockSpec((P, rb, C), lambda i: (0, i, 0))],
        out_specs=pl.BlockSpec((rb, C), lambda i: (i, 0)), out_shape=jax.ShapeDtypeStruct((R, C), F32),
        compiler_params=_cp(("parallel",)),
    )(parts)


def _adamw_math(w, g, m, v):
    m2 = ADAM_B1 * m + (1.0 - ADAM_B1) * g
    v2 = ADAM_B2 * v + (1.0 - ADAM_B2) * (g * g)
    m_hat = m2 / (1.0 - ADAM_B1 ** ADAM_STEP)
    v_hat = v2 / (1.0 - ADAM_B2 ** ADAM_STEP)
    delta = -ADAM_LR * (m_hat / (jnp.sqrt(v_hat) + ADAM_EPS) + ADAM_WD * w)
    return delta, m2, v2


def adamw_big(w, m, v, g_a, g_b, name):
    R, C = w.shape
    rb = _row_block(R)

    def body(w_ref, m_ref, v_ref, ga_ref, gb_ref, g_ref, d_ref, m2_ref, v2_ref):
        g = ga_ref[...] + gb_ref[...]
        delta, m2, v2 = _adamw_math(w_ref[...], g, m_ref[...], v_ref[...])
        g_ref[...], d_ref[...], m2_ref[...], v2_ref[...] = g, delta, m2, v2

    spec = pl.BlockSpec((rb, C), lambda i: (i, 0))
    return pl.pallas_call(
        body, name=name, grid=(R // rb,), in_specs=[spec] * 5, out_specs=[spec] * 4,
        out_shape=[jax.ShapeDtypeStruct((R, C), F32)] * 4, compiler_params=_cp(("parallel",)),
    )(w, m, v, g_a, g_b)


def adamw_small(ws, ms, vs, gs, name):
    n = len(ws)

    def body(*refs):
        w_r, m_r, v_r, g_r = refs[:n], refs[n:2 * n], refs[2 * n:3 * n], refs[3 * n:4 * n]
        d_o, m_o, v_o = refs[4 * n:5 * n], refs[5 * n:6 * n], refs[6 * n:7 * n]
        for k in range(n):
            delta, m2, v2 = _adamw_math(w_r[k][...], g_r[k][...], m_r[k][...], v_r[k][...])
            d_o[k][...], m_o[k][...], v_o[k][...] = delta, m2, v2

    shapes = [jax.ShapeDtypeStruct(w.shape, F32) for w in ws]
    res = pl.pallas_call(body, name=name, out_shape=shapes * 3, compiler_params=pltpu.CompilerParams(vmem_limit_bytes=VMEM_LIMIT))(
        *ws, *ms, *vs, *gs)
    return res[:n], res[n:2 * n], res[2 * n:]


def sum8(own4, sib4, name):
    _, R, C = own4.shape

    def body(a_ref, b_ref, o_ref):
        a = ((a_ref[0] + a_ref[1]) + a_ref[2]) + a_ref[3]
        b = ((b_ref[0] + b_ref[1]) + b_ref[2]) + b_ref[3]
        o_ref[...] = a + b

    return pl.pallas_call(body, name=name, out_shape=jax.ShapeDtypeStruct((R, C), F32),
                          compiler_params=pltpu.CompilerParams(vmem_limit_bytes=VMEM_LIMIT))(own4, sib4)


def _il(w):
    lead = w.shape[:-1]
    return w.reshape(lead + (2, FFN_DIM // 256, 256)).swapaxes(-3, -2).reshape(lead + (2 * FFN_DIM,))


def _unil(w):
    lead = w.shape[:-1]
    return w.reshape(lead + (FFN_DIM // 256, 2, 256)).swapaxes(-3, -2).reshape(lead + (2 * FFN_DIM,))


def _pad_cols(w, n=AUG):
    return jnp.pad(w, ((0, 0), (0, n - w.shape[1])))


ATT_SPLIT = ATT_HEADS // 2


def attn_fwd(a, a_t, W, tag, cargo=()):
    proj = matmul(a, W["main"], "nn", BF16, f"{tag}_proj", ncols=4 * D_MODEL)
    fl = matmul(a, W["small"], "nn", F32, f"{tag}_fl")
    c = forget_cumsum(fl, W["bias"], f"{tag}_cumsum")
    cfix = tok(c, fixed=True)
    q_aug = rowwise(fn_attn_prep_q, [tok(proj, AUG, 0), cfix], [par(W["qg"])], [(2 * AUG, BF16)], f"{tag}_prepq", ATT_SPLIT, True)[0]
    k_aug = rowwise(fn_attn_prep_k, [tok(proj, AUG, ATT_SPLIT), cfix], [par(W["kg"])], [(2 * AUG, BF16)], f"{tag}_prepk", ATT_SPLIT, True)[0]
    v_aug = rowwise(fn_attn_prep_v, [tok(proj, AUG, 2 * ATT_SPLIT)], [], [(2 * AUG, BF16)], f"{tag}_prepv", ATT_SPLIT)[0]
    o_aug, landed = flash_fwd(q_aug, k_aug, v_aug, f"{tag}_flash", cargo)
    go, go_t = rowwise(fn_attn_gate, [tok(o_aug, 2 * AUG, 0), tok(proj, AUG, 3 * ATT_SPLIT)], [], [(AUG, BF16), (AUG, BF16, 0)],
                       f"{tag}_gate", ATT_SPLIT)
    return (a_t, proj, fl, c, q_aug, k_aug, v_aug, o_aug, go_t, go), landed


def attn_out(saved, W, tag):
    return matmul(saved[-1], W["out"], "nn", BF16, f"{tag}_out")


def attn_bwd(dm, saved, W, tag, cargo=None):
    a_t, proj, fl, c, q_aug, k_aug, v_aug, o_aug, go_t, _ = saved
    dgo = matmul(dm, W["out"], "nt", BF16, f"{tag}_dgo")
    d_out = matmul(go_t, dm, "nn", BF16, f"{tag}_dwout")
    do_aug, dog, q_aug2 = rowwise(
        fn_attn_bwd_prep, [tok(dgo, AUG, 0), tok(o_aug, 2 * AUG, 0), tok(proj, AUG, 3 * ATT_SPLIT), tok(q_aug, 2 * AUG, 0)], [],
        [(2 * AUG, BF16), (AUG, BF16), (2 * AUG, BF16)], f"{tag}_bprep", ATT_SPLIT)
    (dq_aug, dk_aug, dv_aug), landed = flash_bwd(q_aug2, k_aug, v_aug, do_aug, f"{tag}_flashb", cargo(d_out) if cargo else ())
    dc = rowwise(fn_attn_dc, [tok(dq_aug), tok(dk_aug)], [], [(AUG, F32)], f"{tag}_dc")[0]
    cfix = tok(c, fixed=True)
    (dq,), (d_qg,) = rowwise_bwd(fn_attn_prep_q, [tok(proj, AUG, 0, grad=BF16), cfix], [par(W["qg"])], [dq_aug],
                                 f"{tag}_prepqb", nsplit=ATT_SPLIT, with_j=True)
    (dk,), (d_kg,) = rowwise_bwd(fn_attn_prep_k, [tok(proj, AUG, ATT_SPLIT, grad=BF16), cfix], [par(W["kg"])], [dk_aug],
                                 f"{tag}_prepkb", nsplit=ATT_SPLIT, with_j=True)
    (dv,), _ = rowwise_bwd(fn_attn_prep_v, [tok(proj, AUG, 2 * ATT_SPLIT, grad=BF16)], [], [dv_aug], f"{tag}_prepvb", nsplit=ATT_SPLIT)
    dfl, d_bias = forget_cumsum_bwd(dc, fl, W["bias"], f"{tag}_cumsumb")
    dproj = jnp.concatenate([dq, dk, dv, dog], axis=1)
    da = matmul(dproj, W["main"], "nt", BF16, f"{tag}_da", add=matmul(dfl, W["small"], "nt", F32, f"{tag}_da0"))
    d_in = jnp.concatenate([matmul(a_t, dproj, "nn", BF16, f"{tag}_dw"), matmul(a_t, dfl, "nn", BF16, f"{tag}_dw2")[:, :ATT_HEADS]], axis=1)
    return da, dict(w_in=d_in, w_out=d_out, b_forget=d_bias[0, :ATT_HEADS], q_norm=d_qg[0], k_norm=d_kg[0]), landed


DN_SPLIT = 3 * DN_HD // 256


def dn_fwd(a, a_t, W, tag):
    proj = matmul(a, W["main"], "nn", BF16, f"{tag}_proj", ncols=4 * D_MODEL)
    lg = matmul(a, W["small"], "nn", F32, f"{tag}_lg")
    qkv = rowwise(fn_dn_prep, [tok(proj, 256, 0, conv=(W["conv"], 0))], [], [(256, BF16)], f"{tag}_prep", DN_SPLIT, True)[0]
    bg = rowwise(fn_dn_gates, [tok(lg)], [par(W["alog"]), par(W["dtb"])], [(AUG, F32)], f"{tag}_gates")[0]
    O, states, inverses = dn_scan_fwd(qkv, bg, f"{tag}_scan")
    go, go_t = rowwise(fn_dn_post, [tok(O), tok(proj, DN_HD, 3)], [par(W["ogain"])], [(DN_HD, BF16), (DN_HD, BF16, 0)], f"{tag}_post")
    m = matmul(go, W["out"], "nn", BF16, f"{tag}_out")
    return m, (a_t, proj, lg, qkv, bg, O, states, inverses, go_t)


def dn_bwd(dm, saved, W, tag):
    a_t, proj, lg, qkv, bg, O, states, inverses, go_t = saved
    dgo = matmul(dm, W["out"], "nt", BF16, f"{tag}_dgo")
    d_out = matmul(go_t, dm, "nn", BF16, f"{tag}_dwout")
    (dO, dog), (d_ogain,) = rowwise_bwd(fn_dn_post, [tok(O, grad=BF16), tok(proj, DN_HD, 3, grad=BF16)], [par(W["ogain"])],
                                        [dgo], f"{tag}_postb")
    dqkv_n, dbg = dn_scan_bwd(qkv, bg, states, inverses, dO, f"{tag}_scanb")
    (dy,), _ = rowwise_bwd(fn_dn_prep, [tok(proj, 256, 0, conv=(W["conv"], 0), grad=BF16)], [], [dqkv_n], f"{tag}_prepb",
                           nsplit=DN_SPLIT, with_j=True)
    dqkv, d_conv = conv_bwd(dy, proj, W["conv"], f"{tag}_convb")
    (dlg,), (d_alog, d_dtb) = rowwise_bwd(fn_dn_gates, [tok(lg, grad=BF16)], [par(W["alog"]), par(W["dtb"])], [dbg], f"{tag}_gatesb")
    dproj = jnp.concatenate([dqkv, dog], axis=1)
    da = matmul(dproj, W["main"], "nt", BF16, f"{tag}_da", add=matmul(dlg, W["small"], "nt", F32, f"{tag}_da0"))
    d_in = jnp.concatenate([matmul(a_t, dproj, "nn", BF16, f"{tag}_dw"), matmul(a_t, dlg, "nn", BF16, f"{tag}_dw2")[:, :2 * DN_HEADS]], axis=1)
    return da, dict(w_in=d_in, w_out=d_out, conv=d_conv, a_log=d_alog[0, DN_HEADS:2 * DN_HEADS],
                    dt_bias=d_dtb[0, DN_HEADS:2 * DN_HEADS], o_norm=d_ogain[0])


FFN_SPLIT = FFN_DIM // 256


def ffn_fwd(b, b_t, W, tag):
    u0 = matmul(b, W["up"], "nn", BF16, f"{tag}_up")
    act, act_t = rowwise(fn_geglu, [tok(u0, 512, 0, conv=(W["conv"], 0))], [], [(256, BF16), (256, BF16, 0)], f"{tag}_act", nsplit=FFN_SPLIT)
    f = matmul(act, W["down"], "nn", BF16, f"{tag}_down")
    return f, (b_t, u0, act_t)


def ffn_bwd(df, saved, W, tag):
    b_t, u0, act_t = saved
    dact = matmul(df, W["down"], "nt", BF16, f"{tag}_dact")
    d_down = matmul(act_t, df, "nn", BF16, f"{tag}_dwdown")
    (du,), _ = rowwise_bwd(fn_geglu, [tok(u0, 512, 0, conv=(W["conv"], 0), grad=BF16)], [], [dact], f"{tag}_actb", nsplit=FFN_SPLIT)
    du0, d_conv = conv_bwd(du, u0, W["conv"], f"{tag}_convb")
    db = matmul(du0, W["up"], "nt", BF16, f"{tag}_db")
    d_up = matmul(b_t, du0, "nn", BF16, f"{tag}_dwup")
    return db, dict(up=_unil(d_up), conv=_unil(d_conv), down=d_down)


def _cols_from_slots(g, L):
    K, Cs = g.shape[1] // L, g.shape[2]
    return g.reshape(4, L, K, Cs).transpose(1, 2, 0, 3).reshape(L, K, 4 * Cs)


def _rows_from_slots(g, L):
    Rs, C = g.shape[1] // L, g.shape[2]
    return g.reshape(4, L, Rs, C).transpose(1, 0, 2, 3).reshape(L, 4 * Rs, C)


def _cols_to_slots(w):
    L, K, C = w.shape
    return w.reshape(L, K, 4, C // 4).transpose(2, 0, 1, 3).reshape(4, L * K, C // 4)


def _rows_to_slots(w):
    L, R, C = w.shape
    return w.reshape(L, 4, R // 4, C).transpose(1, 0, 2, 3).reshape(4, L * (R // 4), C)


SMALL_NAMES = ["meta_tokens", "norm_mix_pre", "norm_mix_post", "norm_ffn_pre", "norm_ffn_post", "attn_b_forget", "attn_q_norm",
               "attn_k_norm", "dn_conv", "dn_a_log", "dn_dt_bias", "dn_o_norm", "ffn_conv"]
BIG_NAMES = ["attn_w_in", "attn_w_out", "dn_w_in", "dn_w_out", "ffn_w_up", "ffn_w_down"]
WEIGHT_NAMES = ["meta_tokens", "norm_mix_pre", "norm_mix_post", "norm_ffn_pre", "norm_ffn_post", "attn_w_in", "attn_b_forget",
                "attn_q_norm", "attn_k_norm", "attn_w_out", "dn_w_in", "dn_conv", "dn_a_log", "dn_dt_bias", "dn_o_norm", "dn_w_out",
                "ffn_w_up", "ffn_conv", "ffn_w_down"]
PACK_COLS = 1024


def train_step(x, loss_target, w, m, v):
    L = x.shape[1]
    T = L + N_META
    Tp = -(-T // TM) * TM
    xi, yi = lax.axis_index("x"), lax.axis_index("y")
    slot = 2 * xi + yi
    as2d = lambda t: t.reshape(-1, t.shape[-1])

    bf2d = lambda n: as2d(w[n]).astype(BF16)
    attn_in_shard = w["attn_w_in"].astype(BF16)
    first = exchange_chips(False, [attn_in_shard[0], w["meta_tokens"], as2d(w["dn_conv"]), as2d(w["ffn_conv"])], "gather_first")
    later = [attn_in_shard[1], bf2d("attn_w_out"), bf2d("dn_w_in"), bf2d("dn_w_out"), bf2d("ffn_w_up"), bf2d("ffn_w_down")]
    meta = first[1].transpose(1, 0, 2).reshape(N_META, D_MODEL)
    dn_conv = _cols_from_slots(first[2], 2)
    ffn_conv = _il(_cols_from_slots(first[3], DEPTH))
    lane8 = lambda t: jnp.pad(t[None, :], ((0, 0), (DN_HEADS, AUG - 2 * DN_HEADS)))

    def attn_weights(j, w_in, w_out):
        return dict(main=(w_in, 0), small=_pad_cols(w_in[0, :, 4 * ATT_HD:]), bias=_pad_cols(w["attn_b_forget"][j][None, :]),
                    qg=w["attn_q_norm"][j][None, :], kg=w["attn_k_norm"][j][None, :], out=w_out)

    WA = [attn_weights(0, _cols_from_slots(first[0], 1), None), None]
    WD, WF = None, None
    row = lambda t, i: t[i][None, :]

    h = jnp.concatenate([meta, x[0], jnp.zeros((Tp - T, D_MODEL), F32)], axis=0)
    tgt = jnp.concatenate([jnp.zeros((N_META, D_MODEL), F32), loss_target[0], jnp.zeros((Tp - T, D_MODEL), F32)], axis=0)
    h0 = h
    a, a_t = rowwise(fn_norm, [tok(h)], [par(row(w["norm_mix_pre"], 0))], [(D_MODEL, BF16), (D_MODEL, BF16, 0)], "norm0")
    saved = []
    for i in range(DEPTH):
        j = i // 2
        if i == 0:
            sv_mix, landed = attn_fwd(a, a_t, WA[0], "attn0", later)
            attn_out_w, dn_in, dn_out = _rows_from_slots(landed[1], 2), _cols_from_slots(landed[2], 2), _rows_from_slots(landed[3], 2)
            ffn_up, ffn_down = _il(_cols_from_slots(landed[4], DEPTH)), _rows_from_slots(landed[5], DEPTH)
            WA[0]["out"] = (attn_out_w, 0)
            WA[1] = attn_weights(1, _cols_from_slots(landed[0], 1), (attn_out_w, 1))
            WD = [dict(main=(dn_in, k), small=_pad_cols(dn_in[k, :, 4 * DN_HD:]), conv=dn_conv[k], alog=lane8(w["dn_a_log"][k]),
                       dtb=lane8(w["dn_dt_bias"][k]), ogain=w["dn_o_norm"][k][None, :], out=(dn_out, k)) for k in range(2)]
            WF = [dict(up=(ffn_up, k), conv=ffn_conv[k], down=(ffn_down, k)) for k in range(DEPTH)]
            mix = attn_out(sv_mix, WA[0], "attn0")
        elif i % 2 == 0:
            sv_mix, _ = attn_fwd(a, a_t, WA[j], f"attn{j}")
            mix = attn_out(sv_mix, WA[j], f"attn{j}")
        else:
            mix, sv_mix = dn_fwd(a, a_t, WD[j], f"dn{j}")
        norm_outs = [(D_MODEL, F32), (D_MODEL, BF16), (D_MODEL, BF16, 1)]
        h_mid, b, b_t = rowwise(fn_resid_norm, [tok(h), tok(mix)], [par(row(w["norm_mix_post"], i)), par(row(w["norm_ffn_pre"], i))],
                                norm_outs, f"resid_mix{i}")
        f, sv_ffn = ffn_fwd(b, b_t, WF[i], f"ffn{i}")
        if i < DEPTH - 1:
            h_out, a, a_t = rowwise(fn_resid_norm, [tok(h_mid), tok(f)], [par(row(w["norm_ffn_post"], i)), par(row(w["norm_mix_pre"], i + 1))],
                                    norm_outs, f"resid_ffn{i}")
        else:
            h_out = rowwise(fn_resid, [tok(h_mid), tok(f)], [par(row(w["norm_ffn_post"], i))], [(D_MODEL, F32)], f"resid_ffn{i}")[0]
        saved.append((h, mix, sv_mix, h_mid, f, sv_ffn))
        h = h_out
    lvec, dh = loss_head(h, tgt, L, "loss_head")
    loss = lax.psum(jnp.sum(lvec), ("x", "y", "c"))

    gn = {n: [None] * DEPTH for n in ("norm_mix_pre", "norm_mix_post", "norm_ffn_pre", "norm_ffn_post")}
    g_attn, g_dn, g_ffn = [None, None], [None, None], [None] * DEPTH
    da = None
    for i in reversed(range(DEPTH)):
        j = i // 2
        h_in, mix, sv_mix, h_mid, f, sv_ffn = saved[i]
        if i == DEPTH - 1:
            (dh, df), (gn["norm_ffn_post"][i],) = rowwise_bwd(
                fn_resid, [tok(h_mid, grad=F32), tok(f, grad=BF16)], [par(row(w["norm_ffn_post"], i))], [dh], f"resid_ffn{i}_b")
        else:
            (dh, df), (gn["norm_ffn_post"][i], gn["norm_mix_pre"][i + 1]) = rowwise_bwd(
                fn_resid_norm, [tok(h_mid, grad=F32), tok(f, grad=BF16)],
                [par(row(w["norm_ffn_post"], i)), par(row(w["norm_mix_pre"], i + 1))], [dh, da], f"resid_ffn{i}_b")
        db, g_ffn[i] = ffn_bwd(df, sv_ffn, WF[i], f"ffn{i}")
        (dh, dm), (gn["norm_mix_post"][i], gn["norm_ffn_pre"][i]) = rowwise_bwd(
            fn_resid_norm, [tok(h_in, grad=F32), tok(mix, grad=BF16)],
            [par(row(w["norm_mix_post"], i)), par(row(w["norm_ffn_pre"], i))], [dh, db], f"resid_mix{i}_b")
        if i == 0:
            def riders(d_out0):
                return [_cols_to_slots(g_attn[1]["w_in"][None]), _rows_to_slots(jnp.stack([d_out0, g_attn[1]["w_out"]])),
                        _cols_to_slots(jnp.stack([g["w_in"] for g in g_dn])), _rows_to_slots(jnp.stack([g["w_out"] for g in g_dn])),
                        _cols_to_slots(jnp.stack([g["up"] for g in g_ffn])), _rows_to_slots(jnp.stack([g["down"] for g in g_ffn]))]
            da, g_attn[0], landed_grads = attn_bwd(dm, sv_mix, WA[0], "attn0", riders)
        elif i % 2 == 0:
            da, g_attn[j], _ = attn_bwd(dm, sv_mix, WA[j], f"attn{j}")
        else:
            da, g_dn[j] = dn_bwd(dm, sv_mix, WD[j], f"dn{j}")
    (dh0,), (gn["norm_mix_pre"][0],) = rowwise_bwd(fn_id_norm, [tok(h0, grad=F32)], [par(row(w["norm_mix_pre"], 0))], [dh, da], "norm0_b")
    grad_x = dh0[N_META:T][None]

    full = dict(
        meta_tokens=dh0[:N_META],
        **{n: jnp.concatenate(gn[n], axis=0) for n in gn},
        attn_b_forget=jnp.stack([g["b_forget"] for g in g_attn]), attn_q_norm=jnp.stack([g["q_norm"] for g in g_attn]),
        attn_k_norm=jnp.stack([g["k_norm"] for g in g_attn]),
        dn_conv=jnp.stack([g["conv"] for g in g_dn]), dn_a_log=jnp.stack([g["a_log"] for g in g_dn]),
        dn_dt_bias=jnp.stack([g["dt_bias"] for g in g_dn]), dn_o_norm=jnp.stack([g["o_norm"] for g in g_dn]),
        ffn_conv=jnp.stack([g["conv"] for g in g_ffn]))
    flat = jnp.concatenate([full[n].reshape(-1) for n in SMALL_NAMES])
    rows = -(-flat.shape[0] // PACK_COLS)
    rows = -(-rows // 8) * 8
    pack = jnp.pad(flat, (0, rows * PACK_COLS - flat.shape[0])).reshape(rows, PACK_COLS)
    own4 = exchange_chips(False, [pack], "gather_small_grads")[0]
    sib4 = swap_cores([own4], "swap_small_grads")[0]
    tot = sum8(own4, sib4, "sum_small_grads").reshape(-1)
    grads, off = {}, 0
    for n in SMALL_NAMES:
        size = full[n].size
        g = tot[off:off + size].reshape(full[n].shape)
        off += size
        if g.shape != w[n].shape:
            g = lax.dynamic_slice_in_dim(g, slot * w[n].shape[-1], w[n].shape[-1], axis=g.ndim - 1)
        grads[n] = g
    d_s, m_s, v_s = adamw_small([as2d(w[n]) for n in SMALL_NAMES], [as2d(m[n]) for n in SMALL_NAMES],
                                [as2d(v[n]) for n in SMALL_NAMES], [as2d(grads[n]) for n in SMALL_NAMES], "adamw_small")
    deltas = {n: d.reshape(w[n].shape) for n, d in zip(SMALL_NAMES, d_s, strict=True)}
    new_m = {n: d.reshape(w[n].shape) for n, d in zip(SMALL_NAMES, m_s, strict=True)}
    new_v = {n: d.reshape(w[n].shape) for n, d in zip(SMALL_NAMES, v_s, strict=True)}

    last = exchange_chips(True, [_cols_to_slots(g_attn[0]["w_in"][None])], "scatter_last")[0]
    recv = [jnp.concatenate([last, landed_grads[0]], axis=1)] + list(landed_grads[1:])
    part = [sum_slots(r, f"sum_{n}") for n, r in zip(BIG_NAMES, recv, strict=True)]
    other = swap_cores(part, "swap_grads")
    for n, pa, pb in zip(BIG_NAMES, part, other, strict=True):
        g, d, m2, v2 = adamw_big(as2d(w[n]), as2d(m[n]), as2d(v[n]), pa, pb, f"adamw_{n}")
        grads[n], deltas[n], new_m[n], new_v[n] = (t.reshape(w[n].shape) for t in (g, d, m2, v2))
    return loss, grad_x, grads, deltas, new_m, new_v


def kernel(x, meta_tokens, norm_mix_pre, norm_mix_post, norm_ffn_pre, norm_ffn_post, attn_w_in, attn_b_forget, attn_q_norm, attn_k_norm, attn_w_out, dn_w_in, dn_conv, dn_a_log, dn_dt_bias, dn_o_norm, dn_w_out, ffn_w_up, ffn_conv, ffn_w_down, loss_target, m_meta_tokens, m_norm_mix_pre, m_norm_mix_post, m_norm_ffn_pre, m_norm_ffn_post, m_attn_w_in, m_attn_b_forget, m_attn_q_norm, m_attn_k_norm, m_attn_w_out, m_dn_w_in, m_dn_conv, m_dn_a_log, m_dn_dt_bias, m_dn_o_norm, m_dn_w_out, m_ffn_w_up, m_ffn_conv, m_ffn_w_down, v_meta_tokens, v_norm_mix_pre, v_norm_mix_post, v_norm_ffn_pre, v_norm_ffn_post, v_attn_w_in, v_attn_b_forget, v_attn_q_norm, v_attn_k_norm, v_attn_w_out, v_dn_w_in, v_dn_conv, v_dn_a_log, v_dn_dt_bias, v_dn_o_norm, v_dn_w_out, v_ffn_w_up, v_ffn_conv, v_ffn_w_down):
    w = dict(meta_tokens=meta_tokens, norm_mix_pre=norm_mix_pre, norm_mix_post=norm_mix_post, norm_ffn_pre=norm_ffn_pre, norm_ffn_post=norm_ffn_post, attn_w_in=attn_w_in, attn_b_forget=attn_b_forget, attn_q_norm=attn_q_norm, attn_k_norm=attn_k_norm, attn_w_out=attn_w_out, dn_w_in=dn_w_in, dn_conv=dn_conv, dn_a_log=dn_a_log, dn_dt_bias=dn_dt_bias, dn_o_norm=dn_o_norm, dn_w_out=dn_w_out, ffn_w_up=ffn_w_up, ffn_conv=ffn_conv, ffn_w_down=ffn_w_down)
    m = dict(meta_tokens=m_meta_tokens, norm_mix_pre=m_norm_mix_pre, norm_mix_post=m_norm_mix_post, norm_ffn_pre=m_norm_ffn_pre, norm_ffn_post=m_norm_ffn_post, attn_w_in=m_attn_w_in, attn_b_forget=m_attn_b_forget, attn_q_norm=m_attn_q_norm, attn_k_norm=m_attn_k_norm, attn_w_out=m_attn_w_out, dn_w_in=m_dn_w_in, dn_conv=m_dn_conv, dn_a_log=m_dn_a_log, dn_dt_bias=m_dn_dt_bias, dn_o_norm=m_dn_o_norm, dn_w_out=m_dn_w_out, ffn_w_up=m_ffn_w_up, ffn_conv=m_ffn_conv, ffn_w_down=m_ffn_w_down)
    v = dict(meta_tokens=v_meta_tokens, norm_mix_pre=v_norm_mix_pre, norm_mix_post=v_norm_mix_post, norm_ffn_pre=v_norm_ffn_pre, norm_ffn_post=v_norm_ffn_post, attn_w_in=v_attn_w_in, attn_b_forget=v_attn_b_forget, attn_q_norm=v_attn_q_norm, attn_k_norm=v_attn_k_norm, attn_w_out=v_attn_w_out, dn_w_in=v_dn_w_in, dn_conv=v_dn_conv, dn_a_log=v_dn_a_log, dn_dt_bias=v_dn_dt_bias, dn_o_norm=v_dn_o_norm, dn_w_out=v_dn_w_out, ffn_w_up=v_ffn_w_up, ffn_conv=v_ffn_conv, ffn_w_down=v_ffn_w_down)
    loss, grad_x, grads, deltas, new_m, new_v = train_step(x, loss_target, w, m, v)
    return (loss, grad_x, *[grads[n] for n in WEIGHT_NAMES], *[deltas[n] for n in WEIGHT_NAMES],
            *[new_m[n] for n in WEIGHT_NAMES], *[new_v[n] for n in WEIGHT_NAMES])
```

```python
import functools

import jax
import jax.numpy as jnp
from jax import lax
from jax.experimental import pallas as pl
from jax.experimental.pallas import tpu as pltpu

F32, BF16 = jnp.float32, jnp.bfloat16

D_MODEL = 1024
N_META = 16
ATT_HEADS, ATT_DH = 16, 64
ATT_HD = ATT_HEADS * ATT_DH
AUG = 128
ATT_AUG = ATT_HEADS * AUG
DN_HEADS, DN_DH = 8, 128
DN_HD = DN_HEADS * DN_DH
DN_CHUNK = 64
FFN_DIM = 2816
DEPTH = 4
EPS = 1e-6
NEG = -1e30

ADAM_LR, ADAM_B1, ADAM_B2, ADAM_EPS, ADAM_WD, ADAM_STEP = 0.001, 0.9, 0.999, 1e-08, 0.01, 10

TM = 640
HALO = 8
VMEM_LIMIT = 52 * 1024 * 1024
MESH_ID = pl.DeviceIdType.MESH


def _cp(sem):
    return pltpu.CompilerParams(dimension_semantics=sem, vmem_limit_bytes=VMEM_LIMIT)


def _pick(n, cands):
    for c in cands:
        if n % c == 0:
            return c
    return n


def matmul(a, b, mode, out_dtype, name, add=None, ncols=None):
    layer = None
    if isinstance(b, tuple):
        b, layer = b
    bshape = b.shape[-2:]
    if mode == "nn":
        (M, K), N = a.shape, ncols or bshape[1]
    elif mode == "nt":
        (M, K), N = a.shape, bshape[0]
    else:
        (K, M), N = a.shape, bshape[1]
    tm = _pick(M, (TM, 1024, 1408, 512, 256, 128))
    tn = _pick(N, (1024, 1408, 512, 256, 128))
    tk = _pick(K, (1664, TM, 1024, 1408, 512, 256, 128))
    nk = K // tk
    if mode == "nn":
        a_spec = pl.BlockSpec((tm, tk), lambda i, j, k: (i, k))
        b_spec = pl.BlockSpec((tk, tn), lambda i, j, k: (k, j))
        dims = (((1,), (0,)), ((), ()))
    elif mode == "nt":
        a_spec = pl.BlockSpec((tm, tk), lambda i, j, k: (i, k))
        b_spec = pl.BlockSpec((tn, tk), lambda i, j, k: (j, k))
        dims = (((1,), (1,)), ((), ()))
    else:
        a_spec = pl.BlockSpec((tk, tm), lambda i, j, k: (k, i))
        b_spec = pl.BlockSpec((tk, tn), lambda i, j, k: (k, j))
        dims = (((0,), (0,)), ((), ()))
    if layer is not None:
        b_spec = pl.BlockSpec((None,) + b_spec.block_shape, functools.partial(lambda i, j, k, f: (layer,) + f(i, j, k), f=b_spec.index_map))
    o_spec = pl.BlockSpec((tm, tn), lambda i, j, k: (i, j))
    has_add = add is not None

    def body(*refs):
        if has_add:
            a_ref, b_ref, add_ref, o_ref, acc_ref = refs
        else:
            a_ref, b_ref, o_ref, acc_ref = refs
        k = pl.program_id(2)
        part = lax.dot_general(a_ref[...], b_ref[...], dims, preferred_element_type=F32)

        @pl.when(k == 0)
        def _():
            acc_ref[...] = part

        @pl.when(k > 0)
        def _():
            acc_ref[...] += part

        @pl.when(k == nk - 1)
        def _():
            r = acc_ref[...]
            if has_add:
                r = r + add_ref[...].astype(F32)
            o_ref[...] = r.astype(out_dtype)

    in_specs = [a_spec, b_spec] + ([o_spec] if has_add else [])
    args = (a, b) + ((add,) if has_add else ())
    return pl.pallas_call(
        body, name=name, grid=(M // tm, N // tn, nk), in_specs=in_specs, out_specs=o_spec,
        out_shape=jax.ShapeDtypeStruct((M, N), out_dtype),
        scratch_shapes=[pltpu.VMEM((tm, tn), F32)],
        compiler_params=_cp(("parallel", "parallel", "arbitrary")),
    )(*args)


def tok(arr, width=None, col=0, conv=None, grad=None, fixed=False):
    return dict(arr=arr, w=arr.shape[1] if width is None else width, col=col, conv=conv, grad=grad, step=0 if fixed else 1)


def par(arr, width=None, col=0):
    return dict(arr=arr, w=arr.shape[1] if width is None else width, col=col)


def _conv_apply(x, halo, w, ext_ref):
    K = w.shape[0]
    rows = x.shape[0]
    ext_ref[0:HALO, :] = halo
    ext_ref[HALO:, :] = x
    y = w[K - 1:K, :] * x
    for k in range(K - 1):
        y = y + w[k:k + 1, :] * ext_ref[pl.ds(HALO - (K - 1) + k, rows), :]
    return y


def _row_specs(toks, pars):
    specs, args = [], []
    for t in toks:
        specs.append(pl.BlockSpec((TM, t["w"]), functools.partial(lambda i, j, c, st: (i, c + st * j), c=t["col"], st=t["step"])))
        args.append(t["arr"])
        if t["conv"] is not None:
            specs.append(pl.BlockSpec((HALO, t["w"]), functools.partial(
                lambda i, j, c: (jnp.maximum(i * (TM // HALO) - 1, 0), c + j), c=t["col"])))
            args.append(t["arr"])
            cw, ccol = t["conv"]
            specs.append(pl.BlockSpec((cw.shape[0], t["w"]), functools.partial(lambda i, j, c: (0, c + j), c=ccol)))
            args.append(cw)
    for p in pars:
        specs.append(pl.BlockSpec((p["arr"].shape[0], p["w"]), functools.partial(lambda i, j, c: (0, c), c=p["col"])))
        args.append(p["arr"])
    return specs, args


def _row_load(toks, pars, refs, ext_refs):
    i = pl.program_id(0)
    vals, n, e = [], 0, 0
    for t in toks:
        x = refs[n][...].astype(F32)
        n += 1
        if t["conv"] is not None:
            halo = jnp.where(i == 0, 0.0, refs[n][...].astype(F32))
            w = refs[n + 1][...]
            n += 2
            x = _conv_apply(x, halo, w, ext_refs[e])
            e += 1
        vals.append(x)
    for _ in pars:
        vals.append(refs[n][...])
        n += 1
    return vals, n


def rowwise(fn, toks, pars, outs, name, nsplit=1, with_j=False):
    Tp = toks[0]["arr"].shape[0]
    specs, args = _row_specs(toks, pars)
    n_ext = sum(t["conv"] is not None for t in toks)

    def body(*refs):
        ext_refs = refs[len(refs) - n_ext:]
        vals, n = _row_load(toks, pars, refs, ext_refs)
        res = fn(pl.program_id(1), *vals) if with_j else fn(*vals)
        plain = 0
        for o, o_ref in zip(outs, refs[n:n + len(outs)], strict=True):
            if len(o) == 2:
                o_ref[...] = res[plain].astype(o_ref.dtype)
                plain += 1
            else:
                o_ref[...] = res[o[2]].T.astype(o_ref.dtype)

    return pl.pallas_call(
        body, name=name, grid=(Tp // TM, nsplit), in_specs=specs,
        out_specs=[pl.BlockSpec((TM, o[0]), lambda i, j: (i, j)) if len(o) == 2 else pl.BlockSpec((o[0], TM), lambda i, j: (j, i))
                   for o in outs],
        out_shape=[jax.ShapeDtypeStruct((Tp, o[0] * nsplit) if len(o) == 2 else (o[0] * nsplit, Tp), o[1]) for o in outs],
        scratch_shapes=[pltpu.VMEM((TM + HALO, t["w"]), F32) for t in toks if t["conv"] is not None],
        compiler_params=_cp(("parallel", "parallel")),
    )(*args)


def rowwise_bwd(fn, toks, pars, cts, name, groups=None, par_grads=None, nsplit=1, with_j=False):
    Tp = toks[0]["arr"].shape[0]
    specs, args = _row_specs(toks, pars)
    n_ext = sum(t["conv"] is not None for t in toks)
    gidx = [k for k, t in enumerate(toks) if t["grad"] is not None]
    if groups is None:
        groups = [[k] for k in range(len(gidx))]
    par_grads = list(range(len(pars))) if par_grads is None else par_grads
    ct_specs, ct_args = [], []
    for c in cts:
        if c is not None:
            ct_specs.append(pl.BlockSpec((TM, c.shape[1] // nsplit), lambda i, j: (i, j)))
            ct_args.append(c)
    out_shapes, out_specs = [], []
    for g in groups:
        w = sum(toks[gidx[k]]["w"] for k in g)
        out_shapes.append(jax.ShapeDtypeStruct((Tp, w * nsplit), toks[gidx[g[0]]]["grad"]))
        out_specs.append(pl.BlockSpec((TM, w), lambda i, j: (i, j)))
    for k in par_grads:
        shp = (pars[k]["arr"].shape[0], pars[k]["w"])
        out_shapes.append(jax.ShapeDtypeStruct(shp, F32))
        out_specs.append(pl.BlockSpec(shp, lambda i, j: (0, 0)))
    n_ct = len(ct_args)

    def body(*refs):
        first = (pl.program_id(0) == 0) & (pl.program_id(1) == 0)
        ext_refs = refs[len(refs) - n_ext:]
        vals, n = _row_load(toks, pars, refs, ext_refs)
        ct_refs = refs[n:n + n_ct]
        o_refs = refs[n + n_ct:len(refs) - n_ext]
        res, vjp = jax.vjp(functools.partial(fn, pl.program_id(1)) if with_j else fn, *vals)
        ct_vals, c = [], 0
        for r, ct in zip(res, cts, strict=True):
            if ct is None:
                ct_vals.append(jnp.zeros_like(r))
            else:
                ct_vals.append(ct_refs[c][...].astype(F32))
                c += 1
        grads = vjp(tuple(ct_vals))
        for g, o_ref in zip(groups, o_refs[:len(groups)], strict=True):
            pieces = [grads[gidx[k]] for k in g]
            val = pieces[0] if len(pieces) == 1 else jnp.concatenate(pieces, axis=1)
            o_ref[...] = val.astype(o_ref.dtype)
        for k, o_ref in zip(par_grads, o_refs[len(groups):], strict=True):
            gk = grads[len(toks) + k]

            @pl.when(first)
            def _(o_ref=o_ref, gk=gk):
                o_ref[...] = gk

            @pl.when(jnp.logical_not(first))
            def _(o_ref=o_ref, gk=gk):
                o_ref[...] += gk

    res = pl.pallas_call(
        body, name=name, grid=(Tp // TM, nsplit), in_specs=specs + ct_specs, out_specs=out_specs, out_shape=out_shapes,
        scratch_shapes=[pltpu.VMEM((TM + HALO, t["w"]), F32) for t in toks if t["conv"] is not None],
        compiler_params=_cp(("arbitrary", "arbitrary")),
    )(*args, *ct_args)
    return res[:len(groups)], res[len(groups):]


ROWS = 16


def _shift_rows(win, s, lo, rows):
    return pltpu.roll(win, (-s) % win.shape[0], 0)[lo:lo + rows, :] if s else win[lo:lo + rows, :]


def conv_bwd(dy, x, w, name):
    Tp, W = dy.shape
    K = w.shape[0]
    wb = _pick(W, (512, 256, 128))
    nt = Tp // TM

    def body(dy_ref, dyn_ref, x_ref, w_ref, dx_ref, dw_ref, ext_dy):
        i = pl.program_id(1)
        ext_dy[0:TM, :] = dy_ref[...].astype(F32)
        ext_dy[TM:, :] = jnp.where(i == nt - 1, 0.0, dyn_ref[...].astype(F32))
        wv = w_ref[...]

        def chunk(t, sums):
            r0 = pl.multiple_of(t * ROWS, ROWS)
            dyw = ext_dy[pl.ds(r0, ROWS + HALO), :]
            xv = x_ref[pl.ds(r0, ROWS), :].astype(F32)
            dx, new = None, []
            for k in range(K):
                dys = _shift_rows(dyw, K - 1 - k, 0, ROWS)
                term = wv[k:k + 1, :] * dys
                dx = term if dx is None else dx + term
                prod = dys * xv
                new.append(sums[k] + (prod[0:HALO, :] + prod[HALO:ROWS, :]))
            dx_ref[pl.ds(r0, ROWS), :] = dx.astype(dx_ref.dtype)
            return tuple(new)

        sums = lax.fori_loop(0, TM // ROWS, chunk, tuple(jnp.zeros((HALO, wb), F32) for _ in range(K)))
        dwv = jnp.concatenate([jnp.sum(sm, axis=0, keepdims=True) for sm in sums], axis=0)

        @pl.when(i == 0)
        def _():
            dw_ref[...] = dwv

        @pl.when(i > 0)
        def _():
            dw_ref[...] += dwv

    return pl.pallas_call(
        body, name=name, grid=(W // wb, nt),
        in_specs=[
            pl.BlockSpec((TM, wb), lambda j, i: (i, j)),
            pl.BlockSpec((HALO, wb), lambda j, i: (jnp.minimum((i + 1) * (TM // HALO), Tp // HALO - 1), j)),
            pl.BlockSpec((TM, wb), lambda j, i: (i, j)),
            pl.BlockSpec((K, wb), lambda j, i: (0, j)),
        ],
        out_specs=[pl.BlockSpec((TM, wb), lambda j, i: (i, j)), pl.BlockSpec((K, wb), lambda j, i: (0, j))],
        out_shape=[jax.ShapeDtypeStruct((Tp, W), BF16), jax.ShapeDtypeStruct((K, W), F32)],
        scratch_shapes=[pltpu.VMEM((TM + HALO, wb), F32)],
        compiler_params=_cp(("parallel", "arbitrary")),
    )(dy, dy, x, w)


def _rms(x, g):
    return x * lax.rsqrt(jnp.mean(x * x, axis=-1, keepdims=True) + EPS) * g


def fn_norm(h, g):
    return (_rms(h, g),)


def fn_id_norm(h, g):
    return h, _rms(h, g)


def fn_resid_norm(h, m, g_post, g_next):
    h2 = h + _rms(m, g_post)
    return h2, _rms(h2, g_next)


def fn_resid(h, m, g_post):
    return (h + _rms(m, g_post),)


def fn_geglu(y):
    w = y.shape[1] // 2
    return (jax.nn.gelu(y[:, :w], approximate=True) * y[:, w:],)


def fn_geglu_keep(y):
    return fn_geglu(y) + (y,)


def _split3(c):
    hi = c.astype(BF16).astype(F32)
    r = c - hi
    mid = r.astype(BF16).astype(F32)
    lo = (r - mid).astype(BF16).astype(F32)
    return hi, mid, lo


def _aug_cols(rows, entries):
    lane = lax.broadcasted_iota(jnp.int32, (rows, ATT_DH), 1)
    out = jnp.zeros((rows, ATT_DH), F32)
    for ln, val in entries:
        out = jnp.where(lane == ln, val, out)
    return out


def _head_col(c, idx):
    lane = lax.broadcasted_iota(jnp.int32, c.shape, 1)
    return jnp.sum(jnp.where(lane == idx, c, 0.0), axis=1, keepdims=True)


def fn_attn_prep_q(j, xp, c, gain):
    rows = xp.shape[0]
    out = []
    for hh in range(2):
        hi, mid, lo = _split3(lax.stop_gradient(_head_col(c, 2 * j + hh)))
        out += [_rms(xp[:, ATT_DH * hh:ATT_DH * (hh + 1)], gain) * (ATT_DH ** -0.5),
                _aug_cols(rows, [(0, hi), (1, mid), (2, lo), (3, 1.0), (4, 1.0), (5, 1.0)])]
    return (jnp.concatenate(out, axis=1),)


def fn_attn_prep_k(j, xp, c, gain):
    rows = xp.shape[0]
    out = []
    for hh in range(2):
        hi, mid, lo = _split3(lax.stop_gradient(_head_col(c, 2 * j + hh)))
        out += [_rms(xp[:, ATT_DH * hh:ATT_DH * (hh + 1)], gain),
                _aug_cols(rows, [(0, 1.0), (1, 1.0), (2, 1.0), (3, -hi), (4, -mid), (5, -lo), (6, 1.0), (7, 1.0), (8, 1.0)])]
    return (jnp.concatenate(out, axis=1),)


def fn_attn_prep_v(xp):
    rows = xp.shape[0]
    out = []
    for hh in range(2):
        out += [xp[:, ATT_DH * hh:ATT_DH * (hh + 1)], _aug_cols(rows, [(0, 1.0), (1, 1.0), (2, 1.0)])]
    return (jnp.concatenate(out, axis=1),)


def fn_attn_gate(o_aug, og):
    outs = []
    for h in range(og.shape[1] // ATT_DH):
        o = o_aug[:, AUG * h:AUG * h + ATT_DH]
        outs.append(o * jax.nn.sigmoid(og[:, ATT_DH * h:ATT_DH * (h + 1)]))
    return (jnp.concatenate(outs, axis=1),)


def fn_attn_bwd_prep(dgo, o_aug, og, q_aug):
    rows = dgo.shape[0]
    lane = lax.broadcasted_iota(jnp.int32, (rows, ATT_DH), 1)
    dos, dogs, qs = [], [], []
    for h in range(og.shape[1] // ATT_DH):
        sl = slice(ATT_DH * h, ATT_DH * (h + 1))
        o = o_aug[:, AUG * h:AUG * h + ATT_DH]
        lse = o_aug[:, AUG * h + ATT_DH:AUG * h + ATT_DH + 1]
        sig = jax.nn.sigmoid(og[:, sl])
        do = dgo[:, sl] * sig
        dogs.append(dgo[:, sl] * o * sig * (1.0 - sig))
        dhi, dmid, dlo = _split3(-jnp.sum(do * o, axis=-1, keepdims=True))
        dos += [do, _aug_cols(rows, [(0, dhi), (1, dmid), (2, dlo)])]
        lhi, lmid, llo = _split3(-lse)
        qa = q_aug[:, AUG * h + ATT_DH:AUG * (h + 1)]
        qa = jnp.where(lane == 6, lhi, jnp.where(lane == 7, lmid, jnp.where(lane == 8, llo, qa)))
        qs += [q_aug[:, AUG * h:AUG * h + ATT_DH], qa]
    return jnp.concatenate(dos, axis=1), jnp.concatenate(dogs, axis=1), jnp.concatenate(qs, axis=1)


def fn_attn_dc(dq_aug, dk_aug):
    lane = lax.broadcasted_iota(jnp.int32, (dk_aug.shape[0], AUG), 1)
    dc = jnp.zeros((dk_aug.shape[0], AUG), F32)
    for h in range(ATT_HEADS):
        col = AUG * h + ATT_DH
        dc = jnp.where(lane == h, dq_aug[:, col:col + 1] - dk_aug[:, col + 3:col + 4], dc)
    return (dc,)


def fn_dn_prep_keep(j, y):
    return fn_dn_prep(j, y) + (y,)


def fn_dn_prep(j, y):
    s = jax.nn.silu(y)
    scale = jnp.where(j < DN_HD // 256, DN_DH ** -0.5, 1.0)
    out = []
    for hh in range(2):
        sh = s[:, DN_DH * hh:DN_DH * (hh + 1)]
        n = sh * lax.rsqrt(jnp.sum(sh * sh, axis=-1, keepdims=True) + EPS) * scale
        out.append(jnp.where(j < 2 * (DN_HD // 256), n, sh))
    return (jnp.concatenate(out, axis=1),)


def fn_dn_gates(lg, alog, dtb):
    lane = lax.broadcasted_iota(jnp.int32, lg.shape, 1)
    beta = jax.nn.sigmoid(lg)
    g = -jnp.exp(alog) * jax.nn.softplus(lg + dtb)
    return (jnp.where(lane < DN_HEADS, beta, jnp.where(lane < 2 * DN_HEADS, g, 0.0)),)


def fn_dn_post(o, og, gain):
    outs = []
    for h in range(DN_HEADS):
        sl = slice(DN_DH * h, DN_DH * (h + 1))
        outs.append(_rms(o[:, sl], gain) * jax.nn.silu(og[:, sl]))
    return (jnp.concatenate(outs, axis=1),)


def _tri(n, lower):
    r = lax.broadcasted_iota(jnp.int32, (n, n), 0)
    c = lax.broadcasted_iota(jnp.int32, (n, n), 1)
    return jnp.where((r >= c) if lower else (r <= c), 1.0, 0.0).astype(F32)


def forget_cumsum(fl, bias, name):
    Tp, W = fl.shape

    def body(fl_ref, b_ref, c_ref, carry):
        i = pl.program_id(0)

        @pl.when(i == 0)
        def _():
            carry[...] = jnp.zeros_like(carry)

        logf = jax.nn.log_sigmoid(fl_ref[...] + b_ref[...])
        c = jnp.dot(_tri(TM, True), logf, precision=lax.Precision.HIGHEST, preferred_element_type=F32) + carry[...]
        c_ref[...] = c
        carry[...] = c[TM - 1:TM, :]

    return pl.pallas_call(
        body, name=name, grid=(Tp // TM,),
        in_specs=[pl.BlockSpec((TM, W), lambda i: (i, 0)), pl.BlockSpec((1, W), lambda i: (0, 0))],
        out_specs=pl.BlockSpec((TM, W), lambda i: (i, 0)), out_shape=jax.ShapeDtypeStruct((Tp, W), F32),
        scratch_shapes=[pltpu.VMEM((1, W), F32)], compiler_params=_cp(("arbitrary",)),
    )(fl, bias)


def forget_cumsum_bwd(dc, fl, bias, name):
    Tp, W = fl.shape
    nt = Tp // TM

    def body(dc_ref, fl_ref, b_ref, dfl_ref, db_ref, carry):
        i = pl.program_id(0)

        @pl.when(i == 0)
        def _():
            carry[...] = jnp.zeros_like(carry)

        dlogf = jnp.dot(_tri(TM, False), dc_ref[...], precision=lax.Precision.HIGHEST, preferred_element_type=F32) + carry[...]
        carry[...] = dlogf[0:1, :]
        dfl = dlogf * jax.nn.sigmoid(-(fl_ref[...] + b_ref[...]))
        dfl_ref[...] = dfl.astype(dfl_ref.dtype)
        s = jnp.sum(dfl, axis=0, keepdims=True)

        @pl.when(i == 0)
        def _():
            db_ref[...] = s

        @pl.when(i > 0)
        def _():
            db_ref[...] += s

    rev = lambda i: (nt - 1 - i, 0)
    return pl.pallas_call(
        body, name=name, grid=(nt,),
        in_specs=[pl.BlockSpec((TM, W), rev), pl.BlockSpec((TM, W), rev), pl.BlockSpec((1, W), lambda i: (0, 0))],
        out_specs=[pl.BlockSpec((TM, W), rev), pl.BlockSpec((1, W), lambda i: (0, 0))],
        out_shape=[jax.ShapeDtypeStruct((Tp, W), BF16), jax.ShapeDtypeStruct((1, W), F32)],
        scratch_shapes=[pltpu.VMEM((1, W), F32)], compiler_params=_cp(("arbitrary",)),
    )(dc, fl, bias)


_HBM = pl.BlockSpec(memory_space=pltpu.HBM)


def _place():
    x, y, c = lax.axis_index("x"), lax.axis_index("y"), lax.axis_index("c")
    return x, y, c, [(1 - x, y), (x, 1 - y), (1 - x, 1 - y)]


def _chip_copies(scatter, ins, outs, send_sems, recv_sems, local_sems):
    x, y, c, chips = _place()
    s = 2 * x + y
    copies = []
    for a in range(len(ins)):
        copies.append(pltpu.make_async_copy(ins[a].at[s] if scatter else ins[a], outs[a].at[s], local_sems.at[a]))
        for p, (px, py) in enumerate(chips):
            copies.append(pltpu.make_async_remote_copy(
                src_ref=ins[a].at[2 * px + py] if scatter else ins[a], dst_ref=outs[a].at[s], send_sem=send_sems.at[a, p],
                recv_sem=recv_sems.at[a, p], device_id=(px, py, c), device_id_type=MESH_ID))
    return copies


def _chip_exchange_shapes(scatter, arrs):
    n = len(arrs)
    out_shape = [jax.ShapeDtypeStruct(a.shape if scatter else (4,) + a.shape, a.dtype) for a in arrs]
    sems = [pltpu.SemaphoreType.DMA((n, 3)), pltpu.SemaphoreType.DMA((n, 3)), pltpu.SemaphoreType.DMA((n,))]
    return out_shape, sems


def exchange_chips(scatter, arrs, name):
    n = len(arrs)

    def body(*refs):
        copies = _chip_copies(scatter, refs[:n], refs[n:2 * n], *refs[2 * n:])
        for cp in copies:
            cp.start()
        for cp in copies:
            cp.wait()

    out_shape, sems = _chip_exchange_shapes(scatter, arrs)
    return pl.pallas_call(body, name=name, in_specs=[_HBM] * n, out_specs=[_HBM] * n, out_shape=out_shape, scratch_shapes=sems)(*arrs)


def swap_cores(arrs, name):
    n = len(arrs)

    def body(*refs):
        ins, outs = refs[:n], refs[n:2 * n]
        send_sems, recv_sems = refs[2 * n:]
        x, y, c, _ = _place()
        copies = []
        for a in range(n):
            cp = pltpu.make_async_remote_copy(
                src_ref=ins[a], dst_ref=outs[a], send_sem=send_sems.at[a], recv_sem=recv_sems.at[a],
                device_id=(x, y, 1 - c), device_id_type=MESH_ID)
            cp.start()
            copies.append(cp)
        for cp in copies:
            cp.wait()

    return pl.pallas_call(
        body, name=name, in_specs=[_HBM] * n, out_specs=[_HBM] * n,
        out_shape=[jax.ShapeDtypeStruct(a.shape, a.dtype) for a in arrs],
        scratch_shapes=[pltpu.SemaphoreType.DMA((n,)), pltpu.SemaphoreType.DMA((n,))],
    )(*arrs)


_NT = (((1,), (1,)), ((), ()))
_TN = (((0,), (0,)), ((), ()))


def _cargo_edges(scatter, n, refs, nb):
    first = (pl.program_id(0) == 0) & (pl.program_id(1) == 0)
    last = (pl.program_id(0) == ATT_HEADS - 1) & (pl.program_id(1) == nb - 1)

    @pl.when(first)
    def _():
        for cp in _chip_copies(scatter, refs[:n], refs[n:2 * n], *refs[2 * n:]):
            cp.start()

    @pl.when(last)
    def _():
        for cp in _chip_copies(scatter, refs[:n], refs[n:2 * n], *refs[2 * n:]):
            cp.wait()


def flash_fwd(q_aug, k_aug, v_aug, name, cargo=()):
    Tp = q_aug.shape[0]
    nb = Tp // TM
    nc = len(cargo)

    def body(*refs):
        q_ref, k_ref, v_ref = refs[:3]
        o_ref = refs[3 + nc]
        if nc:
            _cargo_edges(False, nc, refs[3:3 + nc] + refs[4 + nc:], nb)
        i = pl.program_id(1)
        q = q_ref[...]

        def rows(j):
            return pl.ds(pl.multiple_of(j * TM, TM), TM)

        def scores(j):
            return lax.dot_general(q, k_ref[rows(j), :], _NT, preferred_element_type=F32)

        def absorb(j, s, m_old, acc):
            m_new = jnp.maximum(m_old, jnp.max(s, axis=-1, keepdims=True))
            p = jnp.exp(s - m_new)
            acc = jnp.exp(m_old - m_new) * acc + jnp.dot(p.astype(BF16), v_ref[rows(j), :], preferred_element_type=F32)
            return m_new, acc

        def step(j, carry):
            m_old, acc, s = carry
            s_next = scores(j + 1)
            return absorb(j, s, m_old, acc) + (s_next,)

        m, acc, s = lax.fori_loop(0, i, step, (jnp.full((TM, 1), NEG, F32), jnp.zeros((TM, AUG), F32), scores(0)))
        r = lax.broadcasted_iota(jnp.int32, (TM, TM), 0)
        c = lax.broadcasted_iota(jnp.int32, (TM, TM), 1)
        m, acc = absorb(i, jnp.where(c <= r, s, NEG), m, acc)
        l = acc[:, ATT_DH:ATT_DH + 1]
        lane = lax.broadcasted_iota(jnp.int32, (TM, AUG), 1)
        o_ref[...] = jnp.where(lane < ATT_DH, acc / l, m + jnp.log(l))

    head = pl.BlockSpec((Tp, AUG), lambda h, i: (0, h))
    cargo_shape, sems = _chip_exchange_shapes(False, cargo) if nc else ([], [])
    res = pl.pallas_call(
        body, name=name, grid=(ATT_HEADS, nb),
        in_specs=[pl.BlockSpec((TM, AUG), lambda h, i: (i, h)), head, head] + [_HBM] * nc,
        out_specs=[pl.BlockSpec((TM, AUG), lambda h, i: (i, h))] + [_HBM] * nc,
        out_shape=[jax.ShapeDtypeStruct((Tp, ATT_AUG), F32)] + cargo_shape, scratch_shapes=sems,
        compiler_params=_cp(("arbitrary", "arbitrary")),
    )(q_aug, k_aug, v_aug, *cargo)
    return res[0], res[1:]


def flash_bwd(q_aug2, k_aug, v_aug, do_aug, name, cargo=()):
    Tp = q_aug2.shape[0]
    nb = Tp // TM
    nc = len(cargo)

    def body(*refs):
        q_ref, do_ref, k_ref, v_ref = refs[:4]
        dq_ref, dk_ref, dv_ref = refs[4 + nc:7 + nc]
        if nc:
            _cargo_edges(True, nc, refs[4:4 + nc] + refs[7 + nc:], nb)
        j = pl.program_id(1)
        k, v = k_ref[...], v_ref[...]

        @pl.when(j == 0)
        def _():
            dq_ref[...] = jnp.zeros_like(dq_ref)

        def block(i, carry, masked):
            dk, dv = carry
            rows = pl.ds(pl.multiple_of(i * TM, TM), TM)
            q, do = q_ref[rows, :], do_ref[rows, :]
            st = lax.dot_general(k, q, _NT, preferred_element_type=F32)
            if masked:
                r = lax.broadcasted_iota(jnp.int32, (TM, TM), 0)
                c = lax.broadcasted_iota(jnp.int32, (TM, TM), 1)
                st = jnp.where(r <= c, st, NEG)
            pt = jnp.exp(st)
            dpt = lax.dot_general(v, do, _NT, preferred_element_type=F32)
            dst = (pt * dpt).astype(BF16)
            dv = dv + jnp.dot(pt.astype(BF16), do, preferred_element_type=F32)
            dk = dk + jnp.dot(dst, q, preferred_element_type=F32)
            dq_ref[rows, :] += lax.dot_general(dst, k, _TN, preferred_element_type=F32)
            return dk, dv

        carry = block(j, (jnp.zeros((TM, AUG), F32), jnp.zeros((TM, AUG), F32)), True)
        dk, dv = lax.fori_loop(j + 1, nb, lambda i, cr: block(i, cr, False), carry)
        dk_ref[...] = dk
        dv_ref[...] = dv

    head = pl.BlockSpec((Tp, AUG), lambda h, j: (0, h))
    blk = pl.BlockSpec((TM, AUG), lambda h, j: (j, h))
    cargo_shape, sems = _chip_exchange_shapes(True, cargo) if nc else ([], [])
    res = pl.pallas_call(
        body, name=name, grid=(ATT_HEADS, nb),
        in_specs=[head, head, blk, blk] + [_HBM] * nc, out_specs=[head, blk, blk] + [_HBM] * nc,
        out_shape=[jax.ShapeDtypeStruct((Tp, ATT_AUG), F32)] * 3 + cargo_shape, scratch_shapes=sems,
        compiler_params=_cp(("arbitrary", "arbitrary")),
    )(q_aug2, do_aug, k_aug, v_aug, *cargo)
    return res[:3], res[3:]


_BATCH = ((0,), (0,))


def _dg(x, y, cx, cy):
    return lax.dot_general(x.astype(BF16), y.astype(BF16), (((cx + 1,), (cy + 1,)), _BATCH), preferred_element_type=F32)


def _split2(x):
    hi = x.astype(BF16)
    return hi, (x - hi.astype(F32)).astype(BF16)


def _dg3(x, y, cx, cy):
    (xh, xl), (yh, yl) = _split2(x), _split2(y)
    d = lambda a, b: lax.dot_general(a, b, (((cx + 1,), (cy + 1,)), _BATCH), preferred_element_type=F32)
    return d(xh, yh) + (d(xl, yh) + d(xh, yl))


def _make_bdot(ca, cb, dg):
    @jax.custom_vjp
    def f(a, b):
        return dg(a, b, ca, cb)

    def fwd(a, b):
        return dg(a, b, ca, cb), (a, b)

    def bwd(res, ct):
        a, b = res
        da = dg(ct, b, 1, 1 if cb == 0 else 0) if ca == 1 else dg(b, ct, 1 if cb == 0 else 0, 1)
        db = dg(a, ct, 0 if ca == 1 else 1, 0) if cb == 0 else dg(ct, a, 0, 0 if ca == 1 else 1)
        return da, db

    f.defvjp(fwd, bwd)
    return f


_bd_nn, _bd_nt, _bd_tn = _make_bdot(1, 0, _dg), _make_bdot(1, 1, _dg), _make_bdot(0, 0, _dg)
_bd3_nn = _make_bdot(1, 0, _dg3)


@jax.custom_vjp
def _chunk_cumsum(x):
    return jnp.dot(_tri(x.shape[0], True), x, precision=lax.Precision.HIGHEST, preferred_element_type=F32)


def _chunk_cumsum_fwd(x):
    return _chunk_cumsum(x), None


def _chunk_cumsum_bwd(_, ct):
    return (jnp.dot(_tri(ct.shape[0], False), ct, precision=lax.Precision.HIGHEST, preferred_element_type=F32),)


_chunk_cumsum.defvjp(_chunk_cumsum_fwd, _chunk_cumsum_bwd)


@jax.custom_vjp
def _inverse_given(A, N):
    return N


def _inverse_given_fwd(A, N):
    return N, N


def _inverse_given_bwd(N, ct):
    M = ct + _dg3(N, ct, 0, 0)
    return -(M + _dg3(M, N, 1, 1)), jnp.zeros_like(N)


_inverse_given.defvjp(_inverse_given_fwd, _inverse_given_bwd)


def dn_chunk(S, q, k, v, bg, N_given=None):
    C = q.shape[0]
    H = DN_HEADS
    heads = lambda t: jnp.stack([t[:, DN_DH * h:DN_DH * (h + 1)] for h in range(H)])
    cols = lambda t, o: jnp.stack([t[:, o + h:o + h + 1] for h in range(H)])
    qh, kh, vh = heads(q), heads(k), heads(v)
    beta, gc = cols(bg, 0), cols(_chunk_cumsum(bg), H)
    r = lax.broadcasted_iota(jnp.int32, (H, C, C), 1)
    c = lax.broadcasted_iota(jnp.int32, (H, C, C), 2)
    gcb = jnp.broadcast_to(gc, (H, C, C))
    dec = jnp.exp(jnp.where(r >= c, gcb - jnp.swapaxes(gcb, 1, 2), NEG))
    e_gc = jnp.exp(gc)
    kq = _bd_nt(jnp.concatenate([kh, qh], axis=1), kh)
    A = jnp.where(r > c, kq[:, :C] * dec * beta, 0.0)
    qk = kq[:, C:] * dec
    if N_given is None:
        X = -A
        P = _bd3_nn(X, X)
        N = X
        for i in range(5):
            NP = _bd3_nn(jnp.concatenate([N, P], axis=1) if i < 4 else N, P)
            N = N + P + NP[:, :C]
            if i < 4:
                P = NP[:, C:]
    else:
        N = _inverse_given(A, N_given)
    R = jnp.concatenate([kh * (beta * e_gc), vh * beta], axis=2)
    WU = R + _bd3_nn(N, R)
    W, U0 = WU[:, :, :DN_DH], WU[:, :, DN_DH:]
    gl = gc[:, C - 1:C, :]
    WqS = _bd_nt(jnp.concatenate([W, qh * e_gc], axis=1), S)
    U = U0 - WqS[:, :C]
    O = WqS[:, C:] + _bd_nn(qk, U)
    S_new = jnp.exp(gl) * S + _bd_tn(U, kh * jnp.exp(gl - gc))
    return jnp.concatenate([O[h] for h in range(H)], axis=1), S_new, N


def dn_scan_fwd(qkv, bg, name):
    Tp = qkv.shape[0]
    n = Tp // DN_CHUNK
    tokspec = lambda w, c=0: pl.BlockSpec((DN_CHUNK, w), lambda i: (i, c))

    def body(q_ref, k_ref, v_ref, bg_ref, o_ref, s_ref, n_ref, state):
        @pl.when(pl.program_id(0) == 0)
        def _():
            state[...] = jnp.zeros_like(state)

        S = state[...]
        s_ref[0] = S
        O, S_new, N = dn_chunk(S, q_ref[...].astype(F32), k_ref[...].astype(F32), v_ref[...].astype(F32), bg_ref[...])
        o_ref[...] = O.astype(o_ref.dtype)
        n_ref[0] = N
        state[...] = S_new

    return pl.pallas_call(
        body, name=name, grid=(n,),
        in_specs=[tokspec(DN_HD, 0), tokspec(DN_HD, 1), tokspec(DN_HD, 2), tokspec(AUG)],
        out_specs=[tokspec(DN_HD), pl.BlockSpec((1, DN_HEADS, DN_DH, DN_DH), lambda i: (i, 0, 0, 0)),
                   pl.BlockSpec((1, DN_HEADS, DN_CHUNK, DN_CHUNK), lambda i: (i, 0, 0, 0))],
        out_shape=[jax.ShapeDtypeStruct((Tp, DN_HD), BF16), jax.ShapeDtypeStruct((n, DN_HEADS, DN_DH, DN_DH), F32),
                   jax.ShapeDtypeStruct((n, DN_HEADS, DN_CHUNK, DN_CHUNK), F32)],
        scratch_shapes=[pltpu.VMEM((DN_HEADS, DN_DH, DN_DH), F32)],
        compiler_params=_cp(("arbitrary",)),
    )(qkv, qkv, qkv, bg)


def dn_scan_bwd(qkv, bg, states, inverses, dO, name):
    Tp = qkv.shape[0]
    n = Tp // DN_CHUNK
    tokspec = lambda w, c=0: pl.BlockSpec((DN_CHUNK, w), lambda i: (n - 1 - i, c))

    def body(q_ref, k_ref, v_ref, bg_ref, s_ref, n_ref, do_ref, dqkv_ref, dbg_ref, dstate):
        @pl.when(pl.program_id(0) == 0)
        def _():
            dstate[...] = jnp.zeros_like(dstate)

        N = n_ref[0]
        _, vjp = jax.vjp(lambda *xs: dn_chunk(*xs, N_given=N)[:2],
                         s_ref[0], q_ref[...].astype(F32), k_ref[...].astype(F32), v_ref[...].astype(F32), bg_ref[...])
        dS, dq, dk, dv, dbg = vjp((do_ref[...].astype(F32), dstate[...]))
        dqkv_ref[...] = jnp.concatenate([dq, dk, dv], axis=1).astype(dqkv_ref.dtype)
        dbg_ref[...] = dbg
        dstate[...] = dS

    return pl.pallas_call(
        body, name=name, grid=(n,),
        in_specs=[tokspec(DN_HD, 0), tokspec(DN_HD, 1), tokspec(DN_HD, 2), tokspec(AUG),
                  pl.BlockSpec((1, DN_HEADS, DN_DH, DN_DH), lambda i: (n - 1 - i, 0, 0, 0)),
                  pl.BlockSpec((1, DN_HEADS, DN_CHUNK, DN_CHUNK), lambda i: (n - 1 - i, 0, 0, 0)), tokspec(DN_HD)],
        out_specs=[tokspec(3 * DN_HD), tokspec(AUG)],
        out_shape=[jax.ShapeDtypeStruct((Tp, 3 * DN_HD), BF16), jax.ShapeDtypeStruct((Tp, AUG), F32)],
        scratch_shapes=[pltpu.VMEM((DN_HEADS, DN_DH, DN_DH), F32)],
        compiler_params=_cp(("arbitrary",)),
    )(qkv, qkv, qkv, bg, states, inverses, dO)


def loss_head(h, tgt, n_tok, name):
    Tp, Dm = h.shape

    def body(h_ref, t_ref, l_ref, dh_ref):
        i = pl.program_id(0)
        row = i * TM + lax.broadcasted_iota(jnp.int32, (TM, Dm), 0)
        e = jnp.where((row >= N_META) & (row < N_META + n_tok), h_ref[...] - t_ref[...], 0.0)
        dh_ref[...] = e * (1.0 / Dm)
        s = jnp.sum(e * e, axis=0, keepdims=True) * (0.5 / Dm)

        @pl.when(i == 0)
        def _():
            l_ref[...] = s

        @pl.when(i > 0)
        def _():
            l_ref[...] += s

    tile = pl.BlockSpec((TM, Dm), lambda i: (i, 0))
    return pl.pallas_call(
        body, name=name, grid=(Tp // TM,), in_specs=[tile, tile],
        out_specs=[pl.BlockSpec((1, Dm), lambda i: (0, 0)), tile],
        out_shape=[jax.ShapeDtypeStruct((1, Dm), F32), jax.ShapeDtypeStruct((Tp, Dm), F32)],
        compiler_params=_cp(("arbitrary",)),
    )(h, tgt)


def _row_block(R):
    return _pick(R, (256, 176, 128, 64, 32, 16, 8))


def sum_slots(parts, name):
    P, R, C = parts.shape
    rb = _row_block(R)

    def body(p_ref, o_ref):
        acc = p_ref[0].astype(F32)
        for s in range(1, P):
            acc = acc + p_ref[s].astype(F32)
        o_ref[...] = acc

    return pl.pallas_call(
        body, name=name, grid=(R // rb,), in_specs=[pl.BlockSpec((P, rb, C), lambda i: (0, i, 0))],
        out_specs=pl.BlockSpec((rb, C), lambda i: (i, 0)), out_shape=jax.ShapeDtypeStruct((R, C), F32),
        compiler_params=_cp(("parallel",)),
    )(parts)


def _adamw_math(w, g, m, v):
    m2 = ADAM_B1 * m + (1.0 - ADAM_B1) * g
    v2 = ADAM_B2 * v + (1.0 - ADAM_B2) * (g * g)
    m_hat = m2 / (1.0 - ADAM_B1 ** ADAM_STEP)
    v_hat = v2 / (1.0 - ADAM_B2 ** ADAM_STEP)
    delta = -ADAM_LR * (m_hat / (jnp.sqrt(v_hat) + ADAM_EPS) + ADAM_WD * w)
    return delta, m2, v2


def adamw_big(w, m, v, g_a, g_b, name):
    R, C = w.shape
    rb = _row_block(R)

    def body(w_ref, m_ref, v_ref, ga_ref, gb_ref, g_ref, d_ref, m2_ref, v2_ref):
        g = ga_ref[...] + gb_ref[...]
        delta, m2, v2 = _adamw_math(w_ref[...], g, m_ref[...], v_ref[...])
        g_ref[...], d_ref[...], m2_ref[...], v2_ref[...] = g, delta, m2, v2

    spec = pl.BlockSpec((rb, C), lambda i: (i, 0))
    return pl.pallas_call(
        body, name=name, grid=(R // rb,), in_specs=[spec] * 5, out_specs=[spec] * 4,
        out_shape=[jax.ShapeDtypeStruct((R, C), F32)] * 4, compiler_params=_cp(("parallel",)),
    )(w, m, v, g_a, g_b)


def adamw_small(ws, ms, vs, gs, name):
    n = len(ws)

    def body(*refs):
        w_r, m_r, v_r, g_r = refs[:n], refs[n:2 * n], refs[2 * n:3 * n], refs[3 * n:4 * n]
        d_o, m_o, v_o = refs[4 * n:5 * n], refs[5 * n:6 * n], refs[6 * n:7 * n]
        for k in range(n):
            delta, m2, v2 = _adamw_math(w_r[k][...], g_r[k][...], m_r[k][...], v_r[k][...])
            d_o[k][...], m_o[k][...], v_o[k][...] = delta, m2, v2

    shapes = [jax.ShapeDtypeStruct(w.shape, F32) for w in ws]
    res = pl.pallas_call(body, name=name, out_shape=shapes * 3, compiler_params=pltpu.CompilerParams(vmem_limit_bytes=VMEM_LIMIT))(
        *ws, *ms, *vs, *gs)
    return res[:n], res[n:2 * n], res[2 * n:]


def sum8(own4, sib4, name):
    _, R, C = own4.shape

    def body(a_ref, b_ref, o_ref):
        a = ((a_ref[0] + a_ref[1]) + a_ref[2]) + a_ref[3]
        b = ((b_ref[0] + b_ref[1]) + b_ref[2]) + b_ref[3]
        o_ref[...] = a + b

    return pl.pallas_call(body, name=name, out_shape=jax.ShapeDtypeStruct((R, C), F32),
                          compiler_params=pltpu.CompilerParams(vmem_limit_bytes=VMEM_LIMIT))(own4, sib4)


def _il(w):
    lead = w.shape[:-1]
    return w.reshape(lead + (2, FFN_DIM // 256, 256)).swapaxes(-3, -2).reshape(lead + (2 * FFN_DIM,))


def _unil(w):
    lead = w.shape[:-1]
    return w.reshape(lead + (FFN_DIM // 256, 2, 256)).swapaxes(-3, -2).reshape(lead + (2 * FFN_DIM,))


def _pad_cols(w, n=AUG):
    return jnp.pad(w, ((0, 0), (0, n - w.shape[1])))


ATT_SPLIT = ATT_HEADS // 2


def attn_fwd(a, a_t, W, tag, cargo=()):
    proj = matmul(a, W["main"], "nn", BF16, f"{tag}_proj", ncols=4 * D_MODEL)
    fl = matmul(a, W["small"], "nn", F32, f"{tag}_fl")
    c = forget_cumsum(fl, W["bias"], f"{tag}_cumsum")
    cfix = tok(c, fixed=True)
    q_aug = rowwise(fn_attn_prep_q, [tok(proj, AUG, 0), cfix], [par(W["qg"])], [(2 * AUG, BF16)], f"{tag}_prepq", ATT_SPLIT, True)[0]
    k_aug = rowwise(fn_attn_prep_k, [tok(proj, AUG, ATT_SPLIT), cfix], [par(W["kg"])], [(2 * AUG, BF16)], f"{tag}_prepk", ATT_SPLIT, True)[0]
    v_aug = rowwise(fn_attn_prep_v, [tok(proj, AUG, 2 * ATT_SPLIT)], [], [(2 * AUG, BF16)], f"{tag}_prepv", ATT_SPLIT)[0]
    o_aug, landed = flash_fwd(q_aug, k_aug, v_aug, f"{tag}_flash", cargo)
    go, go_t = rowwise(fn_attn_gate, [tok(o_aug, 2 * AUG, 0), tok(proj, AUG, 3 * ATT_SPLIT)], [], [(AUG, BF16), (AUG, BF16, 0)],
                       f"{tag}_gate", ATT_SPLIT)
    return (a_t, proj, fl, c, q_aug, k_aug, v_aug, o_aug, go_t, go), landed


def attn_out(saved, W, tag):
    return matmul(saved[-1], W["out"], "nn", BF16, f"{tag}_out")


def attn_bwd(dm, saved, W, tag, cargo=None):
    a_t, proj, fl, c, q_aug, k_aug, v_aug, o_aug, go_t, _ = saved
    dgo = matmul(dm, W["out"], "nt", BF16, f"{tag}_dgo")
    d_out = matmul(go_t, dm, "nn", BF16, f"{tag}_dwout")
    do_aug, dog, q_aug2 = rowwise(
        fn_attn_bwd_prep, [tok(dgo, AUG, 0), tok(o_aug, 2 * AUG, 0), tok(proj, AUG, 3 * ATT_SPLIT), tok(q_aug, 2 * AUG, 0)], [],
        [(2 * AUG, BF16), (AUG, BF16), (2 * AUG, BF16)], f"{tag}_bprep", ATT_SPLIT)
    (dq_aug, dk_aug, dv_aug), landed = flash_bwd(q_aug2, k_aug, v_aug, do_aug, f"{tag}_flashb", cargo(d_out) if cargo else ())
    dc = rowwise(fn_attn_dc, [tok(dq_aug), tok(dk_aug)], [], [(AUG, F32)], f"{tag}_dc")[0]
    cfix = tok(c, fixed=True)
    (dq,), (d_qg,) = rowwise_bwd(fn_attn_prep_q, [tok(proj, AUG, 0, grad=BF16), cfix], [par(W["qg"])], [dq_aug],
                                 f"{tag}_prepqb", nsplit=ATT_SPLIT, with_j=True)
    (dk,), (d_kg,) = rowwise_bwd(fn_attn_prep_k, [tok(proj, AUG, ATT_SPLIT, grad=BF16), cfix], [par(W["kg"])], [dk_aug],
                                 f"{tag}_prepkb", nsplit=ATT_SPLIT, with_j=True)
    (dv,), _ = rowwise_bwd(fn_attn_prep_v, [tok(proj, AUG, 2 * ATT_SPLIT, grad=BF16)], [], [dv_aug], f"{tag}_prepvb", nsplit=ATT_SPLIT)
    dfl, d_bias = forget_cumsum_bwd(dc, fl, W["bias"], f"{tag}_cumsumb")
    dproj = jnp.concatenate([dq, dk, dv, dog], axis=1)
    da = matmul(dproj, W["main"], "nt", BF16, f"{tag}_da", add=matmul(dfl, W["small"], "nt", F32, f"{tag}_da0"))
    d_in = jnp.concatenate([matmul(a_t, dproj, "nn", BF16, f"{tag}_dw"), matmul(a_t, dfl, "nn", BF16, f"{tag}_dw2")[:, :ATT_HEADS]], axis=1)
    return da, dict(w_in=d_in, w_out=d_out, b_forget=d_bias[0, :ATT_HEADS], q_norm=d_qg[0], k_norm=d_kg[0]), landed


DN_SPLIT = 3 * DN_HD // 256


def dn_fwd(a, a_t, W, tag):
    proj = matmul(a, W["main"], "nn", BF16, f"{tag}_proj", ncols=4 * D_MODEL)
    lg = matmul(a, W["small"], "nn", F32, f"{tag}_lg")
    qkv, y = rowwise(fn_dn_prep_keep, [tok(proj, 256, 0, conv=(W["conv"], 0))], [], [(256, BF16), (256, BF16)], f"{tag}_prep", DN_SPLIT, True)
    bg = rowwise(fn_dn_gates, [tok(lg)], [par(W["alog"]), par(W["dtb"])], [(AUG, F32)], f"{tag}_gates")[0]
    O, states, inverses = dn_scan_fwd(qkv, bg, f"{tag}_scan")
    go, go_t = rowwise(fn_dn_post, [tok(O), tok(proj, DN_HD, 3)], [par(W["ogain"])], [(DN_HD, BF16), (DN_HD, BF16, 0)], f"{tag}_post")
    m = matmul(go, W["out"], "nn", BF16, f"{tag}_out")
    return m, (a_t, proj, lg, qkv, y, bg, O, states, inverses, go_t)


def dn_bwd(dm, saved, W, tag):
    a_t, proj, lg, qkv, y, bg, O, states, inverses, go_t = saved
    dgo = matmul(dm, W["out"], "nt", BF16, f"{tag}_dgo")
    d_out = matmul(go_t, dm, "nn", BF16, f"{tag}_dwout")
    (dO, dog), (d_ogain,) = rowwise_bwd(fn_dn_post, [tok(O, grad=BF16), tok(proj, DN_HD, 3, grad=BF16)], [par(W["ogain"])],
                                        [dgo], f"{tag}_postb")
    dqkv_n, dbg = dn_scan_bwd(qkv, bg, states, inverses, dO, f"{tag}_scanb")
    (dy,), _ = rowwise_bwd(fn_dn_prep, [tok(y, 256, 0, grad=BF16)], [], [dqkv_n], f"{tag}_prepb", nsplit=DN_SPLIT, with_j=True)
    dqkv, d_conv = conv_bwd(dy, proj, W["conv"], f"{tag}_convb")
    (dlg,), (d_alog, d_dtb) = rowwise_bwd(fn_dn_gates, [tok(lg, grad=BF16)], [par(W["alog"]), par(W["dtb"])], [dbg], f"{tag}_gatesb")
    dproj = jnp.concatenate([dqkv, dog], axis=1)
    da = matmul(dproj, W["main"], "nt", BF16, f"{tag}_da", add=matmul(dlg, W["small"], "nt", F32, f"{tag}_da0"))
    d_in = jnp.concatenate([matmul(a_t, dproj, "nn", BF16, f"{tag}_dw"), matmul(a_t, dlg, "nn", BF16, f"{tag}_dw2")[:, :2 * DN_HEADS]], axis=1)
    return da, dict(w_in=d_in, w_out=d_out, conv=d_conv, a_log=d_alog[0, DN_HEADS:2 * DN_HEADS],
                    dt_bias=d_dtb[0, DN_HEADS:2 * DN_HEADS], o_norm=d_ogain[0])


FFN_SPLIT = FFN_DIM // 256


def ffn_fwd(b, b_t, W, tag):
    u0 = matmul(b, W["up"], "nn", BF16, f"{tag}_up")
    act, u, act_t = rowwise(fn_geglu_keep, [tok(u0, 512, 0, conv=(W["conv"], 0))], [], [(256, BF16), (512, BF16), (256, BF16, 0)],
                            f"{tag}_act", nsplit=FFN_SPLIT)
    f = matmul(act, W["down"], "nn", BF16, f"{tag}_down")
    return f, (b_t, u0, u, act_t)


def ffn_bwd(df, saved, W, tag):
    b_t, u0, u, act_t = saved
    dact = matmul(df, W["down"], "nt", BF16, f"{tag}_dact")
    d_down = matmul(act_t, df, "nn", BF16, f"{tag}_dwdown")
    (du,), _ = rowwise_bwd(fn_geglu, [tok(u, 512, 0, grad=BF16)], [], [dact], f"{tag}_actb", nsplit=FFN_SPLIT)
    du0, d_conv = conv_bwd(du, u0, W["conv"], f"{tag}_convb")
    db = matmul(du0, W["up"], "nt", BF16, f"{tag}_db")
    d_up = matmul(b_t, du0, "nn", BF16, f"{tag}_dwup")
    return db, dict(up=_unil(d_up), conv=_unil(d_conv), down=d_down)


def _cols_from_slots(g, L):
    K, Cs = g.shape[1] // L, g.shape[2]
    return g.reshape(4, L, K, Cs).transpose(1, 2, 0, 3).reshape(L, K, 4 * Cs)


def _rows_from_slots(g, L):
    Rs, C = g.shape[1] // L, g.shape[2]
    return g.reshape(4, L, Rs, C).transpose(1, 0, 2, 3).reshape(L, 4 * Rs, C)


def _cols_to_slots(w):
    L, K, C = w.shape
    return w.reshape(L, K, 4, C // 4).transpose(2, 0, 1, 3).reshape(4, L * K, C // 4)


def _rows_to_slots(w):
    L, R, C = w.shape
    return w.reshape(L, 4, R // 4, C).transpose(1, 0, 2, 3).reshape(4, L * (R // 4), C)


SMALL_NAMES = ["meta_tokens", "norm_mix_pre", "norm_mix_post", "norm_ffn_pre", "norm_ffn_post", "attn_b_forget", "attn_q_norm",
               "attn_k_norm", "dn_conv", "dn_a_log", "dn_dt_bias", "dn_o_norm", "ffn_conv"]
BIG_NAMES = ["attn_w_in", "attn_w_out", "dn_w_in", "dn_w_out", "ffn_w_up", "ffn_w_down"]
WEIGHT_NAMES = ["meta_tokens", "norm_mix_pre", "norm_mix_post", "norm_ffn_pre", "norm_ffn_post", "attn_w_in", "attn_b_forget",
                "attn_q_norm", "attn_k_norm", "attn_w_out", "dn_w_in", "dn_conv", "dn_a_log", "dn_dt_bias", "dn_o_norm", "dn_w_out",
                "ffn_w_up", "ffn_conv", "ffn_w_down"]
PACK_COLS = 1024


def train_step(x, loss_target, w, m, v):
    L = x.shape[1]
    T = L + N_META
    Tp = -(-T // TM) * TM
    xi, yi = lax.axis_index("x"), lax.axis_index("y")
    slot = 2 * xi + yi
    as2d = lambda t: t.reshape(-1, t.shape[-1])

    bf2d = lambda n: as2d(w[n]).astype(BF16)
    attn_in_shard = w["attn_w_in"].astype(BF16)
    first = exchange_chips(False, [attn_in_shard[0], w["meta_tokens"], as2d(w["dn_conv"]), as2d(w["ffn_conv"])], "gather_first")
    later = [attn_in_shard[1], bf2d("attn_w_out"), bf2d("dn_w_in"), bf2d("dn_w_out"), bf2d("ffn_w_up"), bf2d("ffn_w_down")]
    meta = first[1].transpose(1, 0, 2).reshape(N_META, D_MODEL)
    dn_conv = _cols_from_slots(first[2], 2)
    ffn_conv = _il(_cols_from_slots(first[3], DEPTH))
    lane8 = lambda t: jnp.pad(t[None, :], ((0, 0), (DN_HEADS, AUG - 2 * DN_HEADS)))

    def attn_weights(j, w_in, w_out):
        return dict(main=(w_in, 0), small=_pad_cols(w_in[0, :, 4 * ATT_HD:]), bias=_pad_cols(w["attn_b_forget"][j][None, :]),
                    qg=w["attn_q_norm"][j][None, :], kg=w["attn_k_norm"][j][None, :], out=w_out)

    WA = [attn_weights(0, _cols_from_slots(first[0], 1), None), None]
    WD, WF = None, None
    row = lambda t, i: t[i][None, :]

    h = jnp.concatenate([meta, x[0], jnp.zeros((Tp - T, D_MODEL), F32)], axis=0)
    tgt = jnp.concatenate([jnp.zeros((N_META, D_MODEL), F32), loss_target[0], jnp.zeros((Tp - T, D_MODEL), F32)], axis=0)
    h0 = h
    a, a_t = rowwise(fn_norm, [tok(h)], [par(row(w["norm_mix_pre"], 0))], [(D_MODEL, BF16), (D_MODEL, BF16, 0)], "norm0")
    saved = []
    for i in range(DEPTH):
        j = i // 2
        if i == 0:
            sv_mix, landed = attn_fwd(a, a_t, WA[0], "attn0", later)
            attn_out_w, dn_in, dn_out = _rows_from_slots(landed[1], 2), _cols_from_slots(landed[2], 2), _rows_from_slots(landed[3], 2)
            ffn_up, ffn_down = _il(_cols_from_slots(landed[4], DEPTH)), _rows_from_slots(landed[5], DEPTH)
            WA[0]["out"] = (attn_out_w, 0)
            WA[1] = attn_weights(1, _cols_from_slots(landed[0], 1), (attn_out_w, 1))
            WD = [dict(main=(dn_in, k), small=_pad_cols(dn_in[k, :, 4 * DN_HD:]), conv=dn_conv[k], alog=lane8(w["dn_a_log"][k]),
                       dtb=lane8(w["dn_dt_bias"][k]), ogain=w["dn_o_norm"][k][None, :], out=(dn_out, k)) for k in range(2)]
            WF = [dict(up=(ffn_up, k), conv=ffn_conv[k], down=(ffn_down, k)) for k in range(DEPTH)]
            mix = attn_out(sv_mix, WA[0], "attn0")
        elif i % 2 == 0:
            sv_mix, _ = attn_fwd(a, a_t, WA[j], f"attn{j}")
            mix = attn_out(sv_mix, WA[j], f"attn{j}")
        else:
            mix, sv_mix = dn_fwd(a, a_t, WD[j], f"dn{j}")
        norm_outs = [(D_MODEL, F32), (D_MODEL, BF16), (D_MODEL, BF16, 1)]
        h_mid, b, b_t = rowwise(fn_resid_norm, [tok(h), tok(mix)], [par(row(w["norm_mix_post"], i)), par(row(w["norm_ffn_pre"], i))],
                                norm_outs, f"resid_mix{i}")
        f, sv_ffn = ffn_fwd(b, b_t, WF[i], f"ffn{i}")
        if i < DEPTH - 1:
            h_out, a, a_t = rowwise(fn_resid_norm, [tok(h_mid), tok(f)], [par(row(w["norm_ffn_post"], i)), par(row(w["norm_mix_pre"], i + 1))],
                                    norm_outs, f"resid_ffn{i}")
        else:
            h_out = rowwise(fn_resid, [tok(h_mid), tok(f)], [par(row(w["norm_ffn_post"], i))], [(D_MODEL, F32)], f"resid_ffn{i}")[0]
        saved.append((h, mix, sv_mix, h_mid, f, sv_ffn))
        h = h_out
    lvec, dh = loss_head(h, tgt, L, "loss_head")
    loss = lax.psum(jnp.sum(lvec), ("x", "y", "c"))

    gn = {n: [None] * DEPTH for n in ("norm_mix_pre", "norm_mix_post", "norm_ffn_pre", "norm_ffn_post")}
    g_attn, g_dn, g_ffn = [None, None], [None, None], [None] * DEPTH
    da = None
    for i in reversed(range(DEPTH)):
        j = i // 2
        h_in, mix, sv_mix, h_mid, f, sv_ffn = saved[i]
        if i == DEPTH - 1:
            (dh, df), (gn["norm_ffn_post"][i],) = rowwise_bwd(
                fn_resid, [tok(h_mid, grad=F32), tok(f, grad=BF16)], [par(row(w["norm_ffn_post"], i))], [dh], f"resid_ffn{i}_b")
        else:
            (dh, df), (gn["norm_ffn_post"][i], gn["norm_mix_pre"][i + 1]) = rowwise_bwd(
                fn_resid_norm, [tok(h_mid, grad=F32), tok(f, grad=BF16)],
                [par(row(w["norm_ffn_post"], i)), par(row(w["norm_mix_pre"], i + 1))], [dh, da], f"resid_ffn{i}_b")
        db, g_ffn[i] = ffn_bwd(df, sv_ffn, WF[i], f"ffn{i}")
        (dh, dm), (gn["norm_mix_post"][i], gn["norm_ffn_pre"][i]) = rowwise_bwd(
            fn_resid_norm, [tok(h_in, grad=F32), tok(mix, grad=BF16)],
            [par(row(w["norm_mix_post"], i)), par(row(w["norm_ffn_pre"], i))], [dh, db], f"resid_mix{i}_b")
        if i == 0:
            def riders(d_out0):
                return [_cols_to_slots(g_attn[1]["w_in"][None]), _rows_to_slots(jnp.stack([d_out0, g_attn[1]["w_out"]])),
                        _cols_to_slots(jnp.stack([g["w_in"] for g in g_dn])), _rows_to_slots(jnp.stack([g["w_out"] for g in g_dn])),
                        _cols_to_slots(jnp.stack([g["up"] for g in g_ffn])), _rows_to_slots(jnp.stack([g["down"] for g in g_ffn]))]
            da, g_attn[0], landed_grads = attn_bwd(dm, sv_mix, WA[0], "attn0", riders)
        elif i % 2 == 0:
            da, g_attn[j], _ = attn_bwd(dm, sv_mix, WA[j], f"attn{j}")
        else:
            da, g_dn[j] = dn_bwd(dm, sv_mix, WD[j], f"dn{j}")
    (dh0,), (gn["norm_mix_pre"][0],) = rowwise_bwd(fn_id_norm, [tok(h0, grad=F32)], [par(row(w["norm_mix_pre"], 0))], [dh, da], "norm0_b")
    grad_x = dh0[N_META:T][None]

    full = dict(
        meta_tokens=dh0[:N_META],
        **{n: jnp.concatenate(gn[n], axis=0) for n in gn},
        attn_b_forget=jnp.stack([g["b_forget"] for g in g_attn]), attn_q_norm=jnp.stack([g["q_norm"] for g in g_attn]),
        attn_k_norm=jnp.stack([g["k_norm"] for g in g_attn]),
        dn_conv=jnp.stack([g["conv"] for g in g_dn]), dn_a_log=jnp.stack([g["a_log"] for g in g_dn]),
        dn_dt_bias=jnp.stack([g["dt_bias"] for g in g_dn]), dn_o_norm=jnp.stack([g["o_norm"] for g in g_dn]),
        ffn_conv=jnp.stack([g["conv"] for g in g_ffn]))
    flat = jnp.concatenate([full[n].reshape(-1) for n in SMALL_NAMES])
    rows = -(-flat.shape[0] // PACK_COLS)
    rows = -(-rows // 8) * 8
    pack = jnp.pad(flat, (0, rows * PACK_COLS - flat.shape[0])).reshape(rows, PACK_COLS)
    own4 = exchange_chips(False, [pack], "gather_small_grads")[0]
    sib4 = swap_cores([own4], "swap_small_grads")[0]
    tot = sum8(own4, sib4, "sum_small_grads").reshape(-1)
    grads, off = {}, 0
    for n in SMALL_NAMES:
        size = full[n].size
        g = tot[off:off + size].reshape(full[n].shape)
        off += size
        if g.shape != w[n].shape:
            g = lax.dynamic_slice_in_dim(g, slot * w[n].shape[-1], w[n].shape[-1], axis=g.ndim - 1)
        grads[n] = g
    d_s, m_s, v_s = adamw_small([as2d(w[n]) for n in SMALL_NAMES], [as2d(m[n]) for n in SMALL_NAMES],
                                [as2d(v[n]) for n in SMALL_NAMES], [as2d(grads[n]) for n in SMALL_NAMES], "adamw_small")
    deltas = {n: d.reshape(w[n].shape) for n, d in zip(SMALL_NAMES, d_s, strict=True)}
    new_m = {n: d.reshape(w[n].shape) for n, d in zip(SMALL_NAMES, m_s, strict=True)}
    new_v = {n: d.reshape(w[n].shape) for n, d in zip(SMALL_NAMES, v_s, strict=True)}

    last = exchange_chips(True, [_cols_to_slots(g_attn[0]["w_in"][None])], "scatter_last")[0]
    recv = [jnp.concatenate([last, landed_grads[0]], axis=1)] + list(landed_grads[1:])
    part = [sum_slots(r, f"sum_{n}") for n, r in zip(BIG_NAMES, recv, strict=True)]
    other = swap_cores(part, "swap_grads")
    for n, pa, pb in zip(BIG_NAMES, part, other, strict=True):
        g, d, m2, v2 = adamw_big(as2d(w[n]), as2d(m[n]), as2d(v[n]), pa, pb, f"adamw_{n}")
        grads[n], deltas[n], new_m[n], new_v[n] = (t.reshape(w[n].shape) for t in (g, d, m2, v2))
    return loss, grad_x, grads, deltas, new_m, new_v


def kernel(x, meta_tokens, norm_mix_pre, norm_mix_post, norm_ffn_pre, norm_ffn_post, attn_w_in, attn_b_forget, attn_q_norm, attn_k_norm, attn_w_out, dn_w_in, dn_conv, dn_a_log, dn_dt_bias, dn_o_norm, dn_w_out, ffn_w_up, ffn_conv, ffn_w_down, loss_target, m_meta_tokens, m_norm_mix_pre, m_norm_mix_post, m_norm_ffn_pre, m_norm_ffn_post, m_attn_w_in, m_attn_b_forget, m_attn_q_norm, m_attn_k_norm, m_attn_w_out, m_dn_w_in, m_dn_conv, m_dn_a_log, m_dn_dt_bias, m_dn_o_norm, m_dn_w_out, m_ffn_w_up, m_ffn_conv, m_ffn_w_down, v_meta_tokens, v_norm_mix_pre, v_norm_mix_post, v_norm_ffn_pre, v_norm_ffn_post, v_attn_w_in, v_attn_b_forget, v_attn_q_norm, v_attn_k_norm, v_attn_w_out, v_dn_w_in, v_dn_conv, v_dn_a_log, v_dn_dt_bias, v_dn_o_norm, v_dn_w_out, v_ffn_w_up, v_ffn_conv, v_ffn_w_down):
    w = dict(meta_tokens=meta_tokens, norm_mix_pre=norm_mix_pre, norm_mix_post=norm_mix_post, norm_ffn_pre=norm_ffn_pre, norm_ffn_post=norm_ffn_post, attn_w_in=attn_w_in, attn_b_forget=attn_b_forget, attn_q_norm=attn_q_norm, attn_k_norm=attn_k_norm, attn_w_out=attn_w_out, dn_w_in=dn_w_in, dn_conv=dn_conv, dn_a_log=dn_a_log, dn_dt_bias=dn_dt_bias, dn_o_norm=dn_o_norm, dn_w_out=dn_w_out, ffn_w_up=ffn_w_up, ffn_conv=ffn_conv, ffn_w_down=ffn_w_down)
    m = dict(meta_tokens=m_meta_tokens, norm_mix_pre=m_norm_mix_pre, norm_mix_post=m_norm_mix_post, norm_ffn_pre=m_norm_ffn_pre, norm_ffn_post=m_norm_ffn_post, attn_w_in=m_attn_w_in, attn_b_forget=m_attn_b_forget, attn_q_norm=m_attn_q_norm, attn_k_norm=m_attn_k_norm, attn_w_out=m_attn_w_out, dn_w_in=m_dn_w_in, dn_conv=m_dn_conv, dn_a_log=m_dn_a_log, dn_dt_bias=m_dn_dt_bias, dn_o_norm=m_dn_o_norm, dn_w_out=m_dn_w_out, ffn_w_up=m_ffn_w_up, ffn_conv=m_ffn_conv, ffn_w_down=m_ffn_w_down)
    v = dict(meta_tokens=v_meta_tokens, norm_mix_pre=v_norm_mix_pre, norm_mix_post=v_norm_mix_post, norm_ffn_pre=v_norm_ffn_pre, norm_ffn_post=v_norm_ffn_post, attn_w_in=v_attn_w_in, attn_b_forget=v_attn_b_forget, attn_q_norm=v_attn_q_norm, attn_k_norm=v_attn_k_norm, attn_w_out=v_attn_w_out, dn_w_in=v_dn_w_in, dn_conv=v_dn_conv, dn_a_log=v_dn_a_log, dn_dt_bias=v_dn_dt_bias, dn_o_norm=v_dn_o_norm, dn_w_out=v_dn_w_out, ffn_w_up=v_ffn_w_up, ffn_conv=v_ffn_conv, ffn_w_down=v_ffn_w_down)
    loss, grad_x, grads, deltas, new_m, new_v = train_step(x, loss_target, w, m, v)
    return (loss, grad_x, *[grads[n] for n in WEIGHT_NAMES], *[deltas[n] for n in WEIGHT_NAMES],
            *[new_m[n] for n in WEIGHT_NAMES], *[new_v[n] for n in WEIGHT_NAMES])
```

```python
import functools

import jax
import jax.numpy as jnp
from jax import lax
from jax.experimental import pallas as pl
from jax.experimental.pallas import tpu as pltpu

F32, BF16 = jnp.float32, jnp.bfloat16

D_MODEL = 1024
N_META = 16
ATT_HEADS, ATT_DH = 16, 64
ATT_HD = ATT_HEADS * ATT_DH
AUG = 128
ATT_AUG = ATT_HEADS * AUG
DN_HEADS, DN_DH = 8, 128
DN_HD = DN_HEADS * DN_DH
DN_CHUNK = 64
FFN_DIM = 2816
DEPTH = 4
EPS = 1e-6
NEG = -1e30

ADAM_LR, ADAM_B1, ADAM_B2, ADAM_EPS, ADAM_WD, ADAM_STEP = 0.001, 0.9, 0.999, 1e-08, 0.01, 10

TM = 640
HALO = 8
VMEM_LIMIT = 52 * 1024 * 1024
MESH_ID = pl.DeviceIdType.MESH


def _cp(sem):
    return pltpu.CompilerParams(dimension_semantics=sem, vmem_limit_bytes=VMEM_LIMIT)


def _pick(n, cands):
    for c in cands:
        if n % c == 0:
            return c
    return n


def matmul(a, b, mode, out_dtype, name, add=None, ncols=None):
    layer = None
    if isinstance(b, tuple):
        b, layer = b
    bshape = b.shape[-2:]
    if mode == "nn":
        (M, K), N = a.shape, ncols or bshape[1]
    elif mode == "nt":
        (M, K), N = a.shape, bshape[0]
    else:
        (K, M), N = a.shape, bshape[1]
    tm = _pick(M, (1664, TM, 1024, 1408, 512, 256, 128))
    tn = _pick(N, (1024, 1408, 512, 256, 128))
    tk = _pick(K, (1664, TM, 1024, 1408, 512, 256, 128))
    nk = K // tk
    if mode == "nn":
        a_spec = pl.BlockSpec((tm, tk), lambda i, j, k: (i, k))
        b_spec = pl.BlockSpec((tk, tn), lambda i, j, k: (k, j))
        dims = (((1,), (0,)), ((), ()))
    elif mode == "nt":
        a_spec = pl.BlockSpec((tm, tk), lambda i, j, k: (i, k))
        b_spec = pl.BlockSpec((tn, tk), lambda i, j, k: (j, k))
        dims = (((1,), (1,)), ((), ()))
    else:
        a_spec = pl.BlockSpec((tk, tm), lambda i, j, k: (k, i))
        b_spec = pl.BlockSpec((tk, tn), lambda i, j, k: (k, j))
        dims = (((0,), (0,)), ((), ()))
    if layer is not None:
        b_spec = pl.BlockSpec((None,) + b_spec.block_shape, functools.partial(lambda i, j, k, f: (layer,) + f(i, j, k), f=b_spec.index_map))
    o_spec = pl.BlockSpec((tm, tn), lambda i, j, k: (i, j))
    has_add = add is not None

    def body(*refs):
        if has_add:
            a_ref, b_ref, add_ref, o_ref, acc_ref = refs
        else:
            a_ref, b_ref, o_ref, acc_ref = refs
        k = pl.program_id(2)
        part = lax.dot_general(a_ref[...], b_ref[...], dims, preferred_element_type=F32)

        @pl.when(k == 0)
        def _():
            acc_ref[...] = part

        @pl.when(k > 0)
        def _():
            acc_ref[...] += part

        @pl.when(k == nk - 1)
        def _():
            r = acc_ref[...]
            if has_add:
                r = r + add_ref[...].astype(F32)
            o_ref[...] = r.astype(out_dtype)

    in_specs = [a_spec, b_spec] + ([o_spec] if has_add else [])
    args = (a, b) + ((add,) if has_add else ())
    return pl.pallas_call(
        body, name=name, grid=(M // tm, N // tn, nk), in_specs=in_specs, out_specs=o_spec,
        out_shape=jax.ShapeDtypeStruct((M, N), out_dtype),
        scratch_shapes=[pltpu.VMEM((tm, tn), F32)],
        compiler_params=_cp(("parallel", "parallel", "arbitrary")),
    )(*args)


def tok(arr, width=None, col=0, conv=None, grad=None, fixed=False):
    return dict(arr=arr, w=arr.shape[1] if width is None else width, col=col, conv=conv, grad=grad, step=0 if fixed else 1)


def par(arr, width=None, col=0):
    return dict(arr=arr, w=arr.shape[1] if width is None else width, col=col)


def _conv_apply(x, halo, w, ext_ref):
    K = w.shape[0]
    rows = x.shape[0]
    ext_ref[0:HALO, :] = halo
    ext_ref[HALO:, :] = x
    y = w[K - 1:K, :] * x
    for k in range(K - 1):
        y = y + w[k:k + 1, :] * ext_ref[pl.ds(HALO - (K - 1) + k, rows), :]
    return y


def _row_specs(toks, pars):
    specs, args = [], []
    for t in toks:
        specs.append(pl.BlockSpec((TM, t["w"]), functools.partial(lambda i, j, c, st: (i, c + st * j), c=t["col"], st=t["step"])))
        args.append(t["arr"])
        if t["conv"] is not None:
            specs.append(pl.BlockSpec((HALO, t["w"]), functools.partial(
                lambda i, j, c: (jnp.maximum(i * (TM // HALO) - 1, 0), c + j), c=t["col"])))
            args.append(t["arr"])
            cw, ccol = t["conv"]
            specs.append(pl.BlockSpec((cw.shape[0], t["w"]), functools.partial(lambda i, j, c: (0, c + j), c=ccol)))
            args.append(cw)
    for p in pars:
        specs.append(pl.BlockSpec((p["arr"].shape[0], p["w"]), functools.partial(lambda i, j, c: (0, c), c=p["col"])))
        args.append(p["arr"])
    return specs, args


def _row_load(toks, pars, refs, ext_refs):
    i = pl.program_id(0)
    vals, n, e = [], 0, 0
    for t in toks:
        x = refs[n][...].astype(F32)
        n += 1
        if t["conv"] is not None:
            halo = jnp.where(i == 0, 0.0, refs[n][...].astype(F32))
            w = refs[n + 1][...]
            n += 2
            x = _conv_apply(x, halo, w, ext_refs[e])
            e += 1
        vals.append(x)
    for _ in pars:
        vals.append(refs[n][...])
        n += 1
    return vals, n


def rowwise(fn, toks, pars, outs, name, nsplit=1, with_j=False):
    Tp = toks[0]["arr"].shape[0]
    specs, args = _row_specs(toks, pars)
    n_ext = sum(t["conv"] is not None for t in toks)

    def body(*refs):
        ext_refs = refs[len(refs) - n_ext:]
        vals, n = _row_load(toks, pars, refs, ext_refs)
        res = fn(pl.program_id(1), *vals) if with_j else fn(*vals)
        plain = 0
        for o, o_ref in zip(outs, refs[n:n + len(outs)], strict=True):
            if len(o) == 2:
                o_ref[...] = res[plain].astype(o_ref.dtype)
                plain += 1
            else:
                o_ref[...] = res[o[2]].T.astype(o_ref.dtype)

    return pl.pallas_call(
        body, name=name, grid=(Tp // TM, nsplit), in_specs=specs,
        out_specs=[pl.BlockSpec((TM, o[0]), lambda i, j: (i, j)) if len(o) == 2 else pl.BlockSpec((o[0], TM), lambda i, j: (j, i))
                   for o in outs],
        out_shape=[jax.ShapeDtypeStruct((Tp, o[0] * nsplit) if len(o) == 2 else (o[0] * nsplit, Tp), o[1]) for o in outs],
        scratch_shapes=[pltpu.VMEM((TM + HALO, t["w"]), F32) for t in toks if t["conv"] is not None],
        compiler_params=_cp(("parallel", "parallel")),
    )(*args)


def rowwise_bwd(fn, toks, pars, cts, name, groups=None, par_grads=None, nsplit=1, with_j=False):
    Tp = toks[0]["arr"].shape[0]
    specs, args = _row_specs(toks, pars)
    n_ext = sum(t["conv"] is not None for t in toks)
    gidx = [k for k, t in enumerate(toks) if t["grad"] is not None]
    if groups is None:
        groups = [[k] for k in range(len(gidx))]
    par_grads = list(range(len(pars))) if par_grads is None else par_grads
    ct_specs, ct_args = [], []
    for c in cts:
        if c is not None:
            ct_specs.append(pl.BlockSpec((TM, c.shape[1] // nsplit), lambda i, j: (i, j)))
            ct_args.append(c)
    out_shapes, out_specs = [], []
    for g in groups:
        w = sum(toks[gidx[k]]["w"] for k in g)
        out_shapes.append(jax.ShapeDtypeStruct((Tp, w * nsplit), toks[gidx[g[0]]]["grad"]))
        out_specs.append(pl.BlockSpec((TM, w), lambda i, j: (i, j)))
    for k in par_grads:
        shp = (pars[k]["arr"].shape[0], pars[k]["w"])
        out_shapes.append(jax.ShapeDtypeStruct(shp, F32))
        out_specs.append(pl.BlockSpec(shp, lambda i, j: (0, 0)))
    n_ct = len(ct_args)

    def body(*refs):
        first = (pl.program_id(0) == 0) & (pl.program_id(1) == 0)
        ext_refs = refs[len(refs) - n_ext:]
        vals, n = _row_load(toks, pars, refs, ext_refs)
        ct_refs = refs[n:n + n_ct]
        o_refs = refs[n + n_ct:len(refs) - n_ext]
        res, vjp = jax.vjp(functools.partial(fn, pl.program_id(1)) if with_j else fn, *vals)
        ct_vals, c = [], 0
        for r, ct in zip(res, cts, strict=True):
            if ct is None:
                ct_vals.append(jnp.zeros_like(r))
            else:
                ct_vals.append(ct_refs[c][...].astype(F32))
                c += 1
        grads = vjp(tuple(ct_vals))
        for g, o_ref in zip(groups, o_refs[:len(groups)], strict=True):
            pieces = [grads[gidx[k]] for k in g]
            val = pieces[0] if len(pieces) == 1 else jnp.concatenate(pieces, axis=1)
            o_ref[...] = val.astype(o_ref.dtype)
        for k, o_ref in zip(par_grads, o_refs[len(groups):], strict=True):
            gk = grads[len(toks) + k]

            @pl.when(first)
            def _(o_ref=o_ref, gk=gk):
                o_ref[...] = gk

            @pl.when(jnp.logical_not(first))
            def _(o_ref=o_ref, gk=gk):
                o_ref[...] += gk

    res = pl.pallas_call(
        body, name=name, grid=(Tp // TM, nsplit), in_specs=specs + ct_specs, out_specs=out_specs, out_shape=out_shapes,
        scratch_shapes=[pltpu.VMEM((TM + HALO, t["w"]), F32) for t in toks if t["conv"] is not None],
        compiler_params=_cp(("arbitrary", "arbitrary")),
    )(*args, *ct_args)
    return res[:len(groups)], res[len(groups):]


ROWS = 16


def _shift_rows(win, s, lo, rows):
    return pltpu.roll(win, (-s) % win.shape[0], 0)[lo:lo + rows, :] if s else win[lo:lo + rows, :]


def conv_bwd(dy, x, w, name):
    Tp, W = dy.shape
    K = w.shape[0]
    wb = _pick(W, (512, 256, 128))
    nt = Tp // TM

    def body(dy_ref, dyn_ref, x_ref, w_ref, dx_ref, dw_ref, ext_dy):
        i = pl.program_id(1)
        ext_dy[0:TM, :] = dy_ref[...].astype(F32)
        ext_dy[TM:, :] = jnp.where(i == nt - 1, 0.0, dyn_ref[...].astype(F32))
        wv = w_ref[...]

        def chunk(t, sums):
            r0 = pl.multiple_of(t * ROWS, ROWS)
            dyw = ext_dy[pl.ds(r0, ROWS + HALO), :]
            xv = x_ref[pl.ds(r0, ROWS), :].astype(F32)
            dx, new = None, []
            for k in range(K):
                dys = _shift_rows(dyw, K - 1 - k, 0, ROWS)
                term = wv[k:k + 1, :] * dys
                dx = term if dx is None else dx + term
                prod = dys * xv
                new.append(sums[k] + (prod[0:HALO, :] + prod[HALO:ROWS, :]))
            dx_ref[pl.ds(r0, ROWS), :] = dx.astype(dx_ref.dtype)
            return tuple(new)

        sums = lax.fori_loop(0, TM // ROWS, chunk, tuple(jnp.zeros((HALO, wb), F32) for _ in range(K)))
        dwv = jnp.concatenate([jnp.sum(sm, axis=0, keepdims=True) for sm in sums], axis=0)

        @pl.when(i == 0)
        def _():
            dw_ref[...] = dwv

        @pl.when(i > 0)
        def _():
            dw_ref[...] += dwv

    return pl.pallas_call(
        body, name=name, grid=(W // wb, nt),
        in_specs=[
            pl.BlockSpec((TM, wb), lambda j, i: (i, j)),
            pl.BlockSpec((HALO, wb), lambda j, i: (jnp.minimum((i + 1) * (TM // HALO), Tp // HALO - 1), j)),
            pl.BlockSpec((TM, wb), lambda j, i: (i, j)),
            pl.BlockSpec((K, wb), lambda j, i: (0, j)),
        ],
        out_specs=[pl.BlockSpec((TM, wb), lambda j, i: (i, j)), pl.BlockSpec((K, wb), lambda j, i: (0, j))],
        out_shape=[jax.ShapeDtypeStruct((Tp, W), BF16), jax.ShapeDtypeStruct((K, W), F32)],
        scratch_shapes=[pltpu.VMEM((TM + HALO, wb), F32)],
        compiler_params=_cp(("parallel", "arbitrary")),
    )(dy, dy, x, w)


def _rms(x, g):
    return x * lax.rsqrt(jnp.mean(x * x, axis=-1, keepdims=True) + EPS) * g


def fn_norm(h, g):
    return (_rms(h, g),)


def fn_id_norm(h, g):
    return h, _rms(h, g)


def fn_resid_norm(h, m, g_post, g_next):
    h2 = h + _rms(m, g_post)
    return h2, _rms(h2, g_next)


def fn_resid(h, m, g_post):
    return (h + _rms(m, g_post),)


def fn_geglu(y):
    w = y.shape[1] // 2
    return (jax.nn.gelu(y[:, :w], approximate=True) * y[:, w:],)


def fn_geglu_keep(y):
    return fn_geglu(y) + (y,)


def _split3(c):
    hi = c.astype(BF16).astype(F32)
    r = c - hi
    mid = r.astype(BF16).astype(F32)
    lo = (r - mid).astype(BF16).astype(F32)
    return hi, mid, lo


def _aug_cols(rows, entries):
    lane = lax.broadcasted_iota(jnp.int32, (rows, ATT_DH), 1)
    out = jnp.zeros((rows, ATT_DH), F32)
    for ln, val in entries:
        out = jnp.where(lane == ln, val, out)
    return out


def _head_col(c, idx):
    lane = lax.broadcasted_iota(jnp.int32, c.shape, 1)
    return jnp.sum(jnp.where(lane == idx, c, 0.0), axis=1, keepdims=True)


def fn_attn_prep_q(j, xp, c, gain):
    rows = xp.shape[0]
    out = []
    for hh in range(2):
        hi, mid, lo = _split3(lax.stop_gradient(_head_col(c, 2 * j + hh)))
        out += [_rms(xp[:, ATT_DH * hh:ATT_DH * (hh + 1)], gain) * (ATT_DH ** -0.5),
                _aug_cols(rows, [(0, hi), (1, mid), (2, lo), (3, 1.0), (4, 1.0), (5, 1.0)])]
    return (jnp.concatenate(out, axis=1),)


def fn_attn_prep_k(j, xp, c, gain):
    rows = xp.shape[0]
    out = []
    for hh in range(2):
        hi, mid, lo = _split3(lax.stop_gradient(_head_col(c, 2 * j + hh)))
        out += [_rms(xp[:, ATT_DH * hh:ATT_DH * (hh + 1)], gain),
                _aug_cols(rows, [(0, 1.0), (1, 1.0), (2, 1.0), (3, -hi), (4, -mid), (5, -lo), (6, 1.0), (7, 1.0), (8, 1.0)])]
    return (jnp.concatenate(out, axis=1),)


def fn_attn_prep_v(xp):
    rows = xp.shape[0]
    out = []
    for hh in range(2):
        out += [xp[:, ATT_DH * hh:ATT_DH * (hh + 1)], _aug_cols(rows, [(0, 1.0), (1, 1.0), (2, 1.0)])]
    return (jnp.concatenate(out, axis=1),)


def fn_attn_gate(o_aug, og):
    outs = []
    for h in range(og.shape[1] // ATT_DH):
        o = o_aug[:, AUG * h:AUG * h + ATT_DH]
        outs.append(o * jax.nn.sigmoid(og[:, ATT_DH * h:ATT_DH * (h + 1)]))
    return (jnp.concatenate(outs, axis=1),)


def fn_attn_bwd_prep(dgo, o_aug, og, q_aug):
    rows = dgo.shape[0]
    lane = lax.broadcasted_iota(jnp.int32, (rows, ATT_DH), 1)
    dos, dogs, qs = [], [], []
    for h in range(og.shape[1] // ATT_DH):
        sl = slice(ATT_DH * h, ATT_DH * (h + 1))
        o = o_aug[:, AUG * h:AUG * h + ATT_DH]
        lse = o_aug[:, AUG * h + ATT_DH:AUG * h + ATT_DH + 1]
        sig = jax.nn.sigmoid(og[:, sl])
        do = dgo[:, sl] * sig
        dogs.append(dgo[:, sl] * o * sig * (1.0 - sig))
        dhi, dmid, dlo = _split3(-jnp.sum(do * o, axis=-1, keepdims=True))
        dos += [do, _aug_cols(rows, [(0, dhi), (1, dmid), (2, dlo)])]
        lhi, lmid, llo = _split3(-lse)
        qa = q_aug[:, AUG * h + ATT_DH:AUG * (h + 1)]
        qa = jnp.where(lane == 6, lhi, jnp.where(lane == 7, lmid, jnp.where(lane == 8, llo, qa)))
        qs += [q_aug[:, AUG * h:AUG * h + ATT_DH], qa]
    return jnp.concatenate(dos, axis=1), jnp.concatenate(dogs, axis=1), jnp.concatenate(qs, axis=1)


def fn_attn_dc(dq_aug, dk_aug):
    lane = lax.broadcasted_iota(jnp.int32, (dk_aug.shape[0], AUG), 1)
    dc = jnp.zeros((dk_aug.shape[0], AUG), F32)
    for h in range(ATT_HEADS):
        col = AUG * h + ATT_DH
        dc = jnp.where(lane == h, dq_aug[:, col:col + 1] - dk_aug[:, col + 3:col + 4], dc)
    return (dc,)


def fn_dn_prep_keep(j, y):
    return fn_dn_prep(j, y) + (y,)


def fn_dn_prep(j, y):
    s = jax.nn.silu(y)
    scale = jnp.where(j < DN_HD // 256, DN_DH ** -0.5, 1.0)
    out = []
    for hh in range(2):
        sh = s[:, DN_DH * hh:DN_DH * (hh + 1)]
        n = sh * lax.rsqrt(jnp.sum(sh * sh, axis=-1, keepdims=True) + EPS) * scale
        out.append(jnp.where(j < 2 * (DN_HD // 256), n, sh))
    return (jnp.concatenate(out, axis=1),)


def fn_dn_gates(lg, alog, dtb):
    lane = lax.broadcasted_iota(jnp.int32, lg.shape, 1)
    beta = jax.nn.sigmoid(lg)
    g = -jnp.exp(alog) * jax.nn.softplus(lg + dtb)
    return (jnp.where(lane < DN_HEADS, beta, jnp.where(lane < 2 * DN_HEADS, g, 0.0)),)


def fn_dn_post(o, og, gain):
    outs = []
    for h in range(DN_HEADS):
        sl = slice(DN_DH * h, DN_DH * (h + 1))
        outs.append(_rms(o[:, sl], gain) * jax.nn.silu(og[:, sl]))
    return (jnp.concatenate(outs, axis=1),)


def _tri(n, lower):
    r = lax.broadcasted_iota(jnp.int32, (n, n), 0)
    c = lax.broadcasted_iota(jnp.int32, (n, n), 1)
    return jnp.where((r >= c) if lower else (r <= c), 1.0, 0.0).astype(F32)


def forget_cumsum(fl, bias, name):
    Tp, W = fl.shape

    def body(fl_ref, b_ref, c_ref, carry):
        i = pl.program_id(0)

        @pl.when(i == 0)
        def _():
            carry[...] = jnp.zeros_like(carry)

        logf = jax.nn.log_sigmoid(fl_ref[...] + b_ref[...])
        c = jnp.dot(_tri(TM, True), logf, precision=lax.Precision.HIGHEST, preferred_element_type=F32) + carry[...]
        c_ref[...] = c
        carry[...] = c[TM - 1:TM, :]

    return pl.pallas_call(
        body, name=name, grid=(Tp // TM,),
        in_specs=[pl.BlockSpec((TM, W), lambda i: (i, 0)), pl.BlockSpec((1, W), lambda i: (0, 0))],
        out_specs=pl.BlockSpec((TM, W), lambda i: (i, 0)), out_shape=jax.ShapeDtypeStruct((Tp, W), F32),
        scratch_shapes=[pltpu.VMEM((1, W), F32)], compiler_params=_cp(("arbitrary",)),
    )(fl, bias)


def forget_cumsum_bwd(dc, fl, bias, name):
    Tp, W = fl.shape
    nt = Tp // TM

    def body(dc_ref, fl_ref, b_ref, dfl_ref, db_ref, carry):
        i = pl.program_id(0)

        @pl.when(i == 0)
        def _():
            carry[...] = jnp.zeros_like(carry)

        dlogf = jnp.dot(_tri(TM, False), dc_ref[...], precision=lax.Precision.HIGHEST, preferred_element_type=F32) + carry[...]
        carry[...] = dlogf[0:1, :]
        dfl = dlogf * jax.nn.sigmoid(-(fl_ref[...] + b_ref[...]))
        dfl_ref[...] = dfl.astype(dfl_ref.dtype)
        s = jnp.sum(dfl, axis=0, keepdims=True)

        @pl.when(i == 0)
        def _():
            db_ref[...] = s

        @pl.when(i > 0)
        def _():
            db_ref[...] += s

    rev = lambda i: (nt - 1 - i, 0)
    return pl.pallas_call(
        body, name=name, grid=(nt,),
        in_specs=[pl.BlockSpec((TM, W), rev), pl.BlockSpec((TM, W), rev), pl.BlockSpec((1, W), lambda i: (0, 0))],
        out_specs=[pl.BlockSpec((TM, W), rev), pl.BlockSpec((1, W), lambda i: (0, 0))],
        out_shape=[jax.ShapeDtypeStruct((Tp, W), BF16), jax.ShapeDtypeStruct((1, W), F32)],
        scratch_shapes=[pltpu.VMEM((1, W), F32)], compiler_params=_cp(("arbitrary",)),
    )(dc, fl, bias)


_HBM = pl.BlockSpec(memory_space=pltpu.HBM)


def _place():
    x, y, c = lax.axis_index("x"), lax.axis_index("y"), lax.axis_index("c")
    return x, y, c, [(1 - x, y), (x, 1 - y), (1 - x, 1 - y)]


def _chip_copies(scatter, ins, outs, send_sems, recv_sems, local_sems):
    x, y, c, chips = _place()
    s = 2 * x + y
    copies = []
    for a in range(len(ins)):
        copies.append(pltpu.make_async_copy(ins[a].at[s] if scatter else ins[a], outs[a].at[s], local_sems.at[a]))
        for p, (px, py) in enumerate(chips):
            copies.append(pltpu.make_async_remote_copy(
                src_ref=ins[a].at[2 * px + py] if scatter else ins[a], dst_ref=outs[a].at[s], send_sem=send_sems.at[a, p],
                recv_sem=recv_sems.at[a, p], device_id=(px, py, c), device_id_type=MESH_ID))
    return copies


def _chip_exchange_shapes(scatter, arrs):
    n = len(arrs)
    out_shape = [jax.ShapeDtypeStruct(a.shape if scatter else (4,) + a.shape, a.dtype) for a in arrs]
    sems = [pltpu.SemaphoreType.DMA((n, 3)), pltpu.SemaphoreType.DMA((n, 3)), pltpu.SemaphoreType.DMA((n,))]
    return out_shape, sems


def exchange_chips(scatter, arrs, name):
    n = len(arrs)

    def body(*refs):
        copies = _chip_copies(scatter, refs[:n], refs[n:2 * n], *refs[2 * n:])
        for cp in copies:
            cp.start()
        for cp in copies:
            cp.wait()

    out_shape, sems = _chip_exchange_shapes(scatter, arrs)
    return pl.pallas_call(body, name=name, in_specs=[_HBM] * n, out_specs=[_HBM] * n, out_shape=out_shape, scratch_shapes=sems)(*arrs)


def swap_cores(arrs, name):
    n = len(arrs)

    def body(*refs):
        ins, outs = refs[:n], refs[n:2 * n]
        send_sems, recv_sems = refs[2 * n:]
        x, y, c, _ = _place()
        copies = []
        for a in range(n):
            cp = pltpu.make_async_remote_copy(
                src_ref=ins[a], dst_ref=outs[a], send_sem=send_sems.at[a], recv_sem=recv_sems.at[a],
                device_id=(x, y, 1 - c), device_id_type=MESH_ID)
            cp.start()
            copies.append(cp)
        for cp in copies:
            cp.wait()

    return pl.pallas_call(
        body, name=name, in_specs=[_HBM] * n, out_specs=[_HBM] * n,
        out_shape=[jax.ShapeDtypeStruct(a.shape, a.dtype) for a in arrs],
        scratch_shapes=[pltpu.SemaphoreType.DMA((n,)), pltpu.SemaphoreType.DMA((n,))],
    )(*arrs)


_NT = (((1,), (1,)), ((), ()))
_TN = (((0,), (0,)), ((), ()))


def _cargo_edges(scatter, n, refs, nb):
    first = (pl.program_id(0) == 0) & (pl.program_id(1) == 0)
    last = (pl.program_id(0) == ATT_HEADS - 1) & (pl.program_id(1) == nb - 1)

    @pl.when(first)
    def _():
        for cp in _chip_copies(scatter, refs[:n], refs[n:2 * n], *refs[2 * n:]):
            cp.start()

    @pl.when(last)
    def _():
        for cp in _chip_copies(scatter, refs[:n], refs[n:2 * n], *refs[2 * n:]):
            cp.wait()


def flash_fwd(q_aug, k_aug, v_aug, name, cargo=()):
    Tp = q_aug.shape[0]
    nb = Tp // TM
    nc = len(cargo)

    def body(*refs):
        q_ref, k_ref, v_ref = refs[:3]
        o_ref = refs[3 + nc]
        if nc:
            _cargo_edges(False, nc, refs[3:3 + nc] + refs[4 + nc:], nb)
        i = pl.program_id(1)
        q = q_ref[...]

        def rows(j):
            return pl.ds(pl.multiple_of(j * TM, TM), TM)

        def scores(j):
            return lax.dot_general(q, k_ref[rows(j), :], _NT, preferred_element_type=F32)

        def absorb(j, s, m_old, acc):
            m_new = jnp.maximum(m_old, jnp.max(s, axis=-1, keepdims=True))
            p = jnp.exp(s - m_new)
            acc = jnp.exp(m_old - m_new) * acc + jnp.dot(p.astype(BF16), v_ref[rows(j), :], preferred_element_type=F32)
            return m_new, acc

        def step(j, carry):
            m_old, acc, s = carry
            s_next = scores(j + 1)
            return absorb(j, s, m_old, acc) + (s_next,)

        m, acc, s = lax.fori_loop(0, i, step, (jnp.full((TM, 1), NEG, F32), jnp.zeros((TM, AUG), F32), scores(0)))
        r = lax.broadcasted_iota(jnp.int32, (TM, TM), 0)
        c = lax.broadcasted_iota(jnp.int32, (TM, TM), 1)
        m, acc = absorb(i, jnp.where(c <= r, s, NEG), m, acc)
        l = acc[:, ATT_DH:ATT_DH + 1]
        lane = lax.broadcasted_iota(jnp.int32, (TM, AUG), 1)
        o_ref[...] = jnp.where(lane < ATT_DH, acc / l, m + jnp.log(l))

    head = pl.BlockSpec((Tp, AUG), lambda h, i: (0, h))
    cargo_shape, sems = _chip_exchange_shapes(False, cargo) if nc else ([], [])
    res = pl.pallas_call(
        body, name=name, grid=(ATT_HEADS, nb),
        in_specs=[pl.BlockSpec((TM, AUG), lambda h, i: (i, h)), head, head] + [_HBM] * nc,
        out_specs=[pl.BlockSpec((TM, AUG), lambda h, i: (i, h))] + [_HBM] * nc,
        out_shape=[jax.ShapeDtypeStruct((Tp, ATT_AUG), F32)] + cargo_shape, scratch_shapes=sems,
        compiler_params=_cp(("arbitrary", "arbitrary")),
    )(q_aug, k_aug, v_aug, *cargo)
    return res[0], res[1:]


def flash_bwd(q_aug2, k_aug, v_aug, do_aug, name, cargo=()):
    Tp = q_aug2.shape[0]
    nb = Tp // TM
    nc = len(cargo)

    def body(*refs):
        q_ref, do_ref, k_ref, v_ref = refs[:4]
        dq_ref, dk_ref, dv_ref = refs[4 + nc:7 + nc]
        if nc:
            _cargo_edges(True, nc, refs[4:4 + nc] + refs[7 + nc:], nb)
        j = pl.program_id(1)
        k, v = k_ref[...], v_ref[...]

        @pl.when(j == 0)
        def _():
            dq_ref[...] = jnp.zeros_like(dq_ref)

        def block(i, carry, masked):
            dk, dv = carry
            rows = pl.ds(pl.multiple_of(i * TM, TM), TM)
            q, do = q_ref[rows, :], do_ref[rows, :]
            st = lax.dot_general(k, q, _NT, preferred_element_type=F32)
            if masked:
                r = lax.broadcasted_iota(jnp.int32, (TM, TM), 0)
                c = lax.broadcasted_iota(jnp.int32, (TM, TM), 1)
                st = jnp.where(r <= c, st, NEG)
            pt = jnp.exp(st)
            dpt = lax.dot_general(v, do, _NT, preferred_element_type=F32)
            dst = (pt * dpt).astype(BF16)
            dv = dv + jnp.dot(pt.astype(BF16), do, preferred_element_type=F32)
            dk = dk + jnp.dot(dst, q, preferred_element_type=F32)
            dq_ref[rows, :] += lax.dot_general(dst, k, _TN, preferred_element_type=F32)
            return dk, dv

        carry = block(j, (jnp.zeros((TM, AUG), F32), jnp.zeros((TM, AUG), F32)), True)
        dk, dv = lax.fori_loop(j + 1, nb, lambda i, cr: block(i, cr, False), carry)
        dk_ref[...] = dk
        dv_ref[...] = dv

    head = pl.BlockSpec((Tp, AUG), lambda h, j: (0, h))
    blk = pl.BlockSpec((TM, AUG), lambda h, j: (j, h))
    cargo_shape, sems = _chip_exchange_shapes(True, cargo) if nc else ([], [])
    res = pl.pallas_call(
        body, name=name, grid=(ATT_HEADS, nb),
        in_specs=[head, head, blk, blk] + [_HBM] * nc, out_specs=[head, blk, blk] + [_HBM] * nc,
        out_shape=[jax.ShapeDtypeStruct((Tp, ATT_AUG), F32)] * 3 + cargo_shape, scratch_shapes=sems,
        compiler_params=_cp(("arbitrary", "arbitrary")),
    )(q_aug2, do_aug, k_aug, v_aug, *cargo)
    return res[:3], res[3:]


_BATCH = ((0,), (0,))


def _dg(x, y, cx, cy):
    return lax.dot_general(x.astype(BF16), y.astype(BF16), (((cx + 1,), (cy + 1,)), _BATCH), preferred_element_type=F32)


def _split2(x):
    hi = x.astype(BF16)
    return hi, (x - hi.astype(F32)).astype(BF16)


def _dg3(x, y, cx, cy):
    (xh, xl), (yh, yl) = _split2(x), _split2(y)
    d = lambda a, b: lax.dot_general(a, b, (((cx + 1,), (cy + 1,)), _BATCH), preferred_element_type=F32)
    return d(xh, yh) + (d(xl, yh) + d(xh, yl))


def _make_bdot(ca, cb, dg):
    @jax.custom_vjp
    def f(a, b):
        return dg(a, b, ca, cb)

    def fwd(a, b):
        return dg(a, b, ca, cb), (a, b)

    def bwd(res, ct):
        a, b = res
        da = dg(ct, b, 1, 1 if cb == 0 else 0) if ca == 1 else dg(b, ct, 1 if cb == 0 else 0, 1)
        db = dg(a, ct, 0 if ca == 1 else 1, 0) if cb == 0 else dg(ct, a, 0, 0 if ca == 1 else 1)
        return da, db

    f.defvjp(fwd, bwd)
    return f


_bd_nn, _bd_nt, _bd_tn = _make_bdot(1, 0, _dg), _make_bdot(1, 1, _dg), _make_bdot(0, 0, _dg)
_bd3_nn = _make_bdot(1, 0, _dg3)


@jax.custom_vjp
def _chunk_cumsum(x):
    return jnp.dot(_tri(x.shape[0], True), x, precision=lax.Precision.HIGHEST, preferred_element_type=F32)


def _chunk_cumsum_fwd(x):
    return _chunk_cumsum(x), None


def _chunk_cumsum_bwd(_, ct):
    return (jnp.dot(_tri(ct.shape[0], False), ct, precision=lax.Precision.HIGHEST, preferred_element_type=F32),)


_chunk_cumsum.defvjp(_chunk_cumsum_fwd, _chunk_cumsum_bwd)


@jax.custom_vjp
def _inverse_given(A, N):
    return N


def _inverse_given_fwd(A, N):
    return N, N


def _inverse_given_bwd(N, ct):
    M = ct + _dg3(N, ct, 0, 0)
    return -(M + _dg3(M, N, 1, 1)), jnp.zeros_like(N)


_inverse_given.defvjp(_inverse_given_fwd, _inverse_given_bwd)


def dn_chunk(S, q, k, v, bg, N_given=None):
    C = q.shape[0]
    H = DN_HEADS
    heads = lambda t: jnp.stack([t[:, DN_DH * h:DN_DH * (h + 1)] for h in range(H)])
    cols = lambda t, o: jnp.stack([t[:, o + h:o + h + 1] for h in range(H)])
    qh, kh, vh = heads(q), heads(k), heads(v)
    beta, gc = cols(bg, 0), cols(_chunk_cumsum(bg), H)
    r = lax.broadcasted_iota(jnp.int32, (H, C, C), 1)
    c = lax.broadcasted_iota(jnp.int32, (H, C, C), 2)
    gcb = jnp.broadcast_to(gc, (H, C, C))
    dec = jnp.exp(jnp.where(r >= c, gcb - jnp.swapaxes(gcb, 1, 2), NEG))
    e_gc = jnp.exp(gc)
    kq = _bd_nt(jnp.concatenate([kh, qh], axis=1), kh)
    A = jnp.where(r > c, kq[:, :C] * dec * beta, 0.0)
    qk = kq[:, C:] * dec
    if N_given is None:
        X = -A
        P = _bd3_nn(X, X)
        N = X
        for i in range(5):
            NP = _bd3_nn(jnp.concatenate([N, P], axis=1) if i < 4 else N, P)
            N = N + P + NP[:, :C]
            if i < 4:
                P = NP[:, C:]
    else:
        N = _inverse_given(A, N_given)
    R = jnp.concatenate([kh * (beta * e_gc), vh * beta], axis=2)
    WU = R + _bd3_nn(N, R)
    W, U0 = WU[:, :, :DN_DH], WU[:, :, DN_DH:]
    gl = gc[:, C - 1:C, :]
    WqS = _bd_nt(jnp.concatenate([W, qh * e_gc], axis=1), S)
    U = U0 - WqS[:, :C]
    O = WqS[:, C:] + _bd_nn(qk, U)
    S_new = jnp.exp(gl) * S + _bd_tn(U, kh * jnp.exp(gl - gc))
    return jnp.concatenate([O[h] for h in range(H)], axis=1), S_new, N


def dn_scan_fwd(qkv, bg, name):
    Tp = qkv.shape[0]
    n = Tp // DN_CHUNK
    tokspec = lambda w, c=0: pl.BlockSpec((DN_CHUNK, w), lambda i: (i, c))

    def body(q_ref, k_ref, v_ref, bg_ref, o_ref, s_ref, n_ref, state):
        @pl.when(pl.program_id(0) == 0)
        def _():
            state[...] = jnp.zeros_like(state)

        S = state[...]
        s_ref[0] = S
        O, S_new, N = dn_chunk(S, q_ref[...].astype(F32), k_ref[...].astype(F32), v_ref[...].astype(F32), bg_ref[...])
        o_ref[...] = O.astype(o_ref.dtype)
        n_ref[0] = N
        state[...] = S_new

    return pl.pallas_call(
        body, name=name, grid=(n,),
        in_specs=[tokspec(DN_HD, 0), tokspec(DN_HD, 1), tokspec(DN_HD, 2), tokspec(AUG)],
        out_specs=[tokspec(DN_HD), pl.BlockSpec((1, DN_HEADS, DN_DH, DN_DH), lambda i: (i, 0, 0, 0)),
                   pl.BlockSpec((1, DN_HEADS, DN_CHUNK, DN_CHUNK), lambda i: (i, 0, 0, 0))],
        out_shape=[jax.ShapeDtypeStruct((Tp, DN_HD), BF16), jax.ShapeDtypeStruct((n, DN_HEADS, DN_DH, DN_DH), F32),
                   jax.ShapeDtypeStruct((n, DN_HEADS, DN_CHUNK, DN_CHUNK), F32)],
        scratch_shapes=[pltpu.VMEM((DN_HEADS, DN_DH, DN_DH), F32)],
        compiler_params=_cp(("arbitrary",)),
    )(qkv, qkv, qkv, bg)


def dn_scan_bwd(qkv, bg, states, inverses, dO, name):
    Tp = qkv.shape[0]
    n = Tp // DN_CHUNK
    tokspec = lambda w, c=0: pl.BlockSpec((DN_CHUNK, w), lambda i: (n - 1 - i, c))

    def body(q_ref, k_ref, v_ref, bg_ref, s_ref, n_ref, do_ref, dqkv_ref, dbg_ref, dstate):
        @pl.when(pl.program_id(0) == 0)
        def _():
            dstate[...] = jnp.zeros_like(dstate)

        N = n_ref[0]
        _, vjp = jax.vjp(lambda *xs: dn_chunk(*xs, N_given=N)[:2],
                         s_ref[0], q_ref[...].astype(F32), k_ref[...].astype(F32), v_ref[...].astype(F32), bg_ref[...])
        dS, dq, dk, dv, dbg = vjp((do_ref[...].astype(F32), dstate[...]))
        dqkv_ref[...] = jnp.concatenate([dq, dk, dv], axis=1).astype(dqkv_ref.dtype)
        dbg_ref[...] = dbg
        dstate[...] = dS

    return pl.pallas_call(
        body, name=name, grid=(n,),
        in_specs=[tokspec(DN_HD, 0), tokspec(DN_HD, 1), tokspec(DN_HD, 2), tokspec(AUG),
                  pl.BlockSpec((1, DN_HEADS, DN_DH, DN_DH), lambda i: (n - 1 - i, 0, 0, 0)),
                  pl.BlockSpec((1, DN_HEADS, DN_CHUNK, DN_CHUNK), lambda i: (n - 1 - i, 0, 0, 0)), tokspec(DN_HD)],
        out_specs=[tokspec(3 * DN_HD), tokspec(AUG)],
        out_shape=[jax.ShapeDtypeStruct((Tp, 3 * DN_HD), BF16), jax.ShapeDtypeStruct((Tp, AUG), F32)],
        scratch_shapes=[pltpu.VMEM((DN_HEADS, DN_DH, DN_DH), F32)],
        compiler_params=_cp(("arbitrary",)),
    )(qkv, qkv, qkv, bg, states, inverses, dO)


def loss_head(h, tgt, n_tok, name):
    Tp, Dm = h.shape

    def body(h_ref, t_ref, l_ref, dh_ref):
        i = pl.program_id(0)
        row = i * TM + lax.broadcasted_iota(jnp.int32, (TM, Dm), 0)
        e = jnp.where((row >= N_META) & (row < N_META + n_tok), h_ref[...] - t_ref[...], 0.0)
        dh_ref[...] = e * (1.0 / Dm)
        s = jnp.sum(e * e, axis=0, keepdims=True) * (0.5 / Dm)

        @pl.when(i == 0)
        def _():
            l_ref[...] = s

        @pl.when(i > 0)
        def _():
            l_ref[...] += s

    tile = pl.BlockSpec((TM, Dm), lambda i: (i, 0))
    return pl.pallas_call(
        body, name=name, grid=(Tp // TM,), in_specs=[tile, tile],
        out_specs=[pl.BlockSpec((1, Dm), lambda i: (0, 0)), tile],
        out_shape=[jax.ShapeDtypeStruct((1, Dm), F32), jax.ShapeDtypeStruct((Tp, Dm), F32)],
        compiler_params=_cp(("arbitrary",)),
    )(h, tgt)


def _row_block(R):
    return _pick(R, (256, 176, 128, 64, 32, 16, 8))


def sum_slots(parts, name):
    P, R, C = parts.shape
    rb = _row_block(R)

    def body(p_ref, o_ref):
        acc = p_ref[0].astype(F32)
        for s in range(1, P):
            acc = acc + p_ref[s].astype(F32)
        o_ref[...] = acc

    return pl.pallas_call(
        body, name=name, grid=(R // rb,), in_specs=[pl.BlockSpec((P, rb, C), lambda i: (0, i, 0))],
        out_specs=pl.BlockSpec((rb, C), lambda i: (i, 0)), out_shape=jax.ShapeDtypeStruct((R, C), F32),
        compiler_params=_cp(("parallel",)),
    )(parts)


def _adamw_math(w, g, m, v):
    m2 = ADAM_B1 * m + (1.0 - ADAM_B1) * g
    v2 = ADAM_B2 * v + (1.0 - ADAM_B2) * (g * g)
    m_hat = m2 / (1.0 - ADAM_B1 ** ADAM_STEP)
    v_hat = v2 / (1.0 - ADAM_B2 ** ADAM_STEP)
    delta = -ADAM_LR * (m_hat / (jnp.sqrt(v_hat) + ADAM_EPS) + ADAM_WD * w)
    return delta, m2, v2


def adamw_big(w, m, v, g_a, g_b, name):
    R, C = w.shape
    rb = _row_block(R)

    def body(w_ref, m_ref, v_ref, ga_ref, gb_ref, g_ref, d_ref, m2_ref, v2_ref):
        g = ga_ref[...] + gb_ref[...]
        delta, m2, v2 = _adamw_math(w_ref[...], g, m_ref[...], v_ref[...])
        g_ref[...], d_ref[...], m2_ref[...], v2_ref[...] = g, delta, m2, v2

    spec = pl.BlockSpec((rb, C), lambda i: (i, 0))
    return pl.pallas_call(
        body, name=name, grid=(R // rb,), in_specs=[spec] * 5, out_specs=[spec] * 4,
        out_shape=[jax.ShapeDtypeStruct((R, C), F32)] * 4, compiler_params=_cp(("parallel",)),
    )(w, m, v, g_a, g_b)


def adamw_small(ws, ms, vs, gs, name):
    n = len(ws)

    def body(*refs):
        w_r, m_r, v_r, g_r = refs[:n], refs[n:2 * n], refs[2 * n:3 * n], refs[3 * n:4 * n]
        d_o, m_o, v_o = refs[4 * n:5 * n], refs[5 * n:6 * n], refs[6 * n:7 * n]
        for k in range(n):
            delta, m2, v2 = _adamw_math(w_r[k][...], g_r[k][...], m_r[k][...], v_r[k][...])
            d_o[k][...], m_o[k][...], v_o[k][...] = delta, m2, v2

    shapes = [jax.ShapeDtypeStruct(w.shape, F32) for w in ws]
    res = pl.pallas_call(body, name=name, out_shape=shapes * 3, compiler_params=pltpu.CompilerParams(vmem_limit_bytes=VMEM_LIMIT))(
        *ws, *ms, *vs, *gs)
    return res[:n], res[n:2 * n], res[2 * n:]


def sum8(own4, sib4, name):
    _, R, C = own4.shape

    def body(a_ref, b_ref, o_ref):
        a = ((a_ref[0] + a_ref[1]) + a_ref[2]) + a_ref[3]
        b = ((b_ref[0] + b_ref[1]) + b_ref[2]) + b_ref[3]
        o_ref[...] = a + b

    return pl.pallas_call(body, name=name, out_shape=jax.ShapeDtypeStruct((R, C), F32),
                          compiler_params=pltpu.CompilerParams(vmem_limit_bytes=VMEM_LIMIT))(own4, sib4)


def _il(w):
    lead = w.shape[:-1]
    return w.reshape(lead + (2, FFN_DIM // 256, 256)).swapaxes(-3, -2).reshape(lead + (2 * FFN_DIM,))


def _unil(w):
    lead = w.shape[:-1]
    return w.reshape(lead + (FFN_DIM // 256, 2, 256)).swapaxes(-3, -2).reshape(lead + (2 * FFN_DIM,))


def _pad_cols(w, n=AUG):
    return jnp.pad(w, ((0, 0), (0, n - w.shape[1])))


ATT_SPLIT = ATT_HEADS // 2


def attn_fwd(a, a_t, W, tag, cargo=()):
    proj = matmul(a, W["main"], "nn", BF16, f"{tag}_proj", ncols=4 * D_MODEL)
    fl = matmul(a, W["small"], "nn", F32, f"{tag}_fl")
    c = forget_cumsum(fl, W["bias"], f"{tag}_cumsum")
    cfix = tok(c, fixed=True)
    q_aug = rowwise(fn_attn_prep_q, [tok(proj, AUG, 0), cfix], [par(W["qg"])], [(2 * AUG, BF16)], f"{tag}_prepq", ATT_SPLIT, True)[0]
    k_aug = rowwise(fn_attn_prep_k, [tok(proj, AUG, ATT_SPLIT), cfix], [par(W["kg"])], [(2 * AUG, BF16)], f"{tag}_prepk", ATT_SPLIT, True)[0]
    v_aug = rowwise(fn_attn_prep_v, [tok(proj, AUG, 2 * ATT_SPLIT)], [], [(2 * AUG, BF16)], f"{tag}_prepv", ATT_SPLIT)[0]
    o_aug, landed = flash_fwd(q_aug, k_aug, v_aug, f"{tag}_flash", cargo)
    go, go_t = rowwise(fn_attn_gate, [tok(o_aug, 2 * AUG, 0), tok(proj, AUG, 3 * ATT_SPLIT)], [], [(AUG, BF16), (AUG, BF16, 0)],
                       f"{tag}_gate", ATT_SPLIT)
    return (a_t, proj, fl, c, q_aug, k_aug, v_aug, o_aug, go_t, go), landed


def attn_out(saved, W, tag):
    return matmul(saved[-1], W["out"], "nn", BF16, f"{tag}_out")


def attn_bwd(dm, saved, W, tag, cargo=None):
    a_t, proj, fl, c, q_aug, k_aug, v_aug, o_aug, go_t, _ = saved
    dgo = matmul(dm, W["out"], "nt", BF16, f"{tag}_dgo")
    d_out = matmul(go_t, dm, "nn", BF16, f"{tag}_dwout")
    do_aug, dog, q_aug2 = rowwise(
        fn_attn_bwd_prep, [tok(dgo, AUG, 0), tok(o_aug, 2 * AUG, 0), tok(proj, AUG, 3 * ATT_SPLIT), tok(q_aug, 2 * AUG, 0)], [],
        [(2 * AUG, BF16), (AUG, BF16), (2 * AUG, BF16)], f"{tag}_bprep", ATT_SPLIT)
    (dq_aug, dk_aug, dv_aug), landed = flash_bwd(q_aug2, k_aug, v_aug, do_aug, f"{tag}_flashb", cargo(d_out) if cargo else ())
    dc = rowwise(fn_attn_dc, [tok(dq_aug), tok(dk_aug)], [], [(AUG, F32)], f"{tag}_dc")[0]
    cfix = tok(c, fixed=True)
    (dq,), (d_qg,) = rowwise_bwd(fn_attn_prep_q, [tok(proj, AUG, 0, grad=BF16), cfix], [par(W["qg"])], [dq_aug],
                                 f"{tag}_prepqb", nsplit=ATT_SPLIT, with_j=True)
    (dk,), (d_kg,) = rowwise_bwd(fn_attn_prep_k, [tok(proj, AUG, ATT_SPLIT, grad=BF16), cfix], [par(W["kg"])], [dk_aug],
                                 f"{tag}_prepkb", nsplit=ATT_SPLIT, with_j=True)
    (dv,), _ = rowwise_bwd(fn_attn_prep_v, [tok(proj, AUG, 2 * ATT_SPLIT, grad=BF16)], [], [dv_aug], f"{tag}_prepvb", nsplit=ATT_SPLIT)
    dfl, d_bias = forget_cumsum_bwd(dc, fl, W["bias"], f"{tag}_cumsumb")
    dproj = jnp.concatenate([dq, dk, dv, dog], axis=1)
    da = matmul(dproj, W["main"], "nt", BF16, f"{tag}_da", add=matmul(dfl, W["small"], "nt", F32, f"{tag}_da0"))
    d_in = jnp.concatenate([matmul(a_t, dproj, "nn", BF16, f"{tag}_dw"), matmul(a_t, dfl, "nn", BF16, f"{tag}_dw2")[:, :ATT_HEADS]], axis=1)
    return da, dict(w_in=d_in, w_out=d_out, b_forget=d_bias[0, :ATT_HEADS], q_norm=d_qg[0], k_norm=d_kg[0]), landed


DN_SPLIT = 3 * DN_HD // 256


def dn_fwd(a, a_t, W, tag):
    proj = matmul(a, W["main"], "nn", BF16, f"{tag}_proj", ncols=4 * D_MODEL)
    lg = matmul(a, W["small"], "nn", F32, f"{tag}_lg")
    qkv, y = rowwise(fn_dn_prep_keep, [tok(proj, 256, 0, conv=(W["conv"], 0))], [], [(256, BF16), (256, BF16)], f"{tag}_prep", DN_SPLIT, True)
    bg = rowwise(fn_dn_gates, [tok(lg)], [par(W["alog"]), par(W["dtb"])], [(AUG, F32)], f"{tag}_gates")[0]
    O, states, inverses = dn_scan_fwd(qkv, bg, f"{tag}_scan")
    go, go_t = rowwise(fn_dn_post, [tok(O), tok(proj, DN_HD, 3)], [par(W["ogain"])], [(DN_HD, BF16), (DN_HD, BF16, 0)], f"{tag}_post")
    m = matmul(go, W["out"], "nn", BF16, f"{tag}_out")
    return m, (a_t, proj, lg, qkv, y, bg, O, states, inverses, go_t)


def dn_bwd(dm, saved, W, tag):
    a_t, proj, lg, qkv, y, bg, O, states, inverses, go_t = saved
    dgo = matmul(dm, W["out"], "nt", BF16, f"{tag}_dgo")
    d_out = matmul(go_t, dm, "nn", BF16, f"{tag}_dwout")
    (dO, dog), (d_ogain,) = rowwise_bwd(fn_dn_post, [tok(O, grad=BF16), tok(proj, DN_HD, 3, grad=BF16)], [par(W["ogain"])],
                                        [dgo], f"{tag}_postb")
    dqkv_n, dbg = dn_scan_bwd(qkv, bg, states, inverses, dO, f"{tag}_scanb")
    (dy,), _ = rowwise_bwd(fn_dn_prep, [tok(y, 256, 0, grad=BF16)], [], [dqkv_n], f"{tag}_prepb", nsplit=DN_SPLIT, with_j=True)
    dqkv, d_conv = conv_bwd(dy, proj, W["conv"], f"{tag}_convb")
    (dlg,), (d_alog, d_dtb) = rowwise_bwd(fn_dn_gates, [tok(lg, grad=BF16)], [par(W["alog"]), par(W["dtb"])], [dbg], f"{tag}_gatesb")
    dproj = jnp.concatenate([dqkv, dog], axis=1)
    da = matmul(dproj, W["main"], "nt", BF16, f"{tag}_da", add=matmul(dlg, W["small"], "nt", F32, f"{tag}_da0"))
    d_in = jnp.concatenate([matmul(a_t, dproj, "nn", BF16, f"{tag}_dw"), matmul(a_t, dlg, "nn", BF16, f"{tag}_dw2")[:, :2 * DN_HEADS]], axis=1)
    return da, dict(w_in=d_in, w_out=d_out, conv=d_conv, a_log=d_alog[0, DN_HEADS:2 * DN_HEADS],
                    dt_bias=d_dtb[0, DN_HEADS:2 * DN_HEADS], o_norm=d_ogain[0])


FFN_SPLIT = FFN_DIM // 256


def ffn_fwd(b, b_t, W, tag):
    u0 = matmul(b, W["up"], "nn", BF16, f"{tag}_up")
    act, u, act_t = rowwise(fn_geglu_keep, [tok(u0, 512, 0, conv=(W["conv"], 0))], [], [(256, BF16), (512, BF16), (256, BF16, 0)],
                            f"{tag}_act", nsplit=FFN_SPLIT)
    f = matmul(act, W["down"], "nn", BF16, f"{tag}_down")
    return f, (b_t, u0, u, act_t)


def ffn_bwd(df, saved, W, tag):
    b_t, u0, u, act_t = saved
    dact = matmul(df, W["down"], "nt", BF16, f"{tag}_dact")
    d_down = matmul(act_t, df, "nn", BF16, f"{tag}_dwdown")
    (du,), _ = rowwise_bwd(fn_geglu, [tok(u, 512, 0, grad=BF16)], [], [dact], f"{tag}_actb", nsplit=FFN_SPLIT)
    du0, d_conv = conv_bwd(du, u0, W["conv"], f"{tag}_convb")
    db = matmul(du0, W["up"], "nt", BF16, f"{tag}_db")
    d_up = matmul(b_t, du0, "nn", BF16, f"{tag}_dwup")
    return db, dict(up=_unil(d_up), conv=_unil(d_conv), down=d_down)


def _cols_from_slots(g, L):
    K, Cs = g.shape[1] // L, g.shape[2]
    return g.reshape(4, L, K, Cs).transpose(1, 2, 0, 3).reshape(L, K, 4 * Cs)


def _rows_from_slots(g, L):
    Rs, C = g.shape[1] // L, g.shape[2]
    return g.reshape(4, L, Rs, C).transpose(1, 0, 2, 3).reshape(L, 4 * Rs, C)


def _cols_to_slots(w):
    L, K, C = w.shape
    return w.reshape(L, K, 4, C // 4).transpose(2, 0, 1, 3).reshape(4, L * K, C // 4)


def _rows_to_slots(w):
    L, R, C = w.shape
    return w.reshape(L, 4, R // 4, C).transpose(1, 0, 2, 3).reshape(4, L * (R // 4), C)


SMALL_NAMES = ["meta_tokens", "norm_mix_pre", "norm_mix_post", "norm_ffn_pre", "norm_ffn_post", "attn_b_forget", "attn_q_norm",
               "attn_k_norm", "dn_conv", "dn_a_log", "dn_dt_bias", "dn_o_norm", "ffn_conv"]
BIG_NAMES = ["attn_w_in", "attn_w_out", "dn_w_in", "dn_w_out", "ffn_w_up", "ffn_w_down"]
WEIGHT_NAMES = ["meta_tokens", "norm_mix_pre", "norm_mix_post", "norm_ffn_pre", "norm_ffn_post", "attn_w_in", "attn_b_forget",
                "attn_q_norm", "attn_k_norm", "attn_w_out", "dn_w_in", "dn_conv", "dn_a_log", "dn_dt_bias", "dn_o_norm", "dn_w_out",
                "ffn_w_up", "ffn_conv", "ffn_w_down"]
PACK_COLS = 1024


def train_step(x, loss_target, w, m, v):
    L = x.shape[1]
    T = L + N_META
    Tp = -(-T // TM) * TM
    xi, yi = lax.axis_index("x"), lax.axis_index("y")
    slot = 2 * xi + yi
    as2d = lambda t: t.reshape(-1, t.shape[-1])

    bf2d = lambda n: as2d(w[n]).astype(BF16)
    attn_in_shard = w["attn_w_in"].astype(BF16)
    first = exchange_chips(False, [attn_in_shard[0], w["meta_tokens"], as2d(w["dn_conv"]), as2d(w["ffn_conv"])], "gather_first")
    later = [attn_in_shard[1], bf2d("attn_w_out"), bf2d("dn_w_in"), bf2d("dn_w_out"), bf2d("ffn_w_up"), bf2d("ffn_w_down")]
    meta = first[1].transpose(1, 0, 2).reshape(N_META, D_MODEL)
    dn_conv = _cols_from_slots(first[2], 2)
    ffn_conv = _il(_cols_from_slots(first[3], DEPTH))
    lane8 = lambda t: jnp.pad(t[None, :], ((0, 0), (DN_HEADS, AUG - 2 * DN_HEADS)))

    def attn_weights(j, w_in, w_out):
        return dict(main=(w_in, 0), small=_pad_cols(w_in[0, :, 4 * ATT_HD:]), bias=_pad_cols(w["attn_b_forget"][j][None, :]),
                    qg=w["attn_q_norm"][j][None, :], kg=w["attn_k_norm"][j][None, :], out=w_out)

    WA = [attn_weights(0, _cols_from_slots(first[0], 1), None), None]
    WD, WF = None, None
    row = lambda t, i: t[i][None, :]

    h = jnp.concatenate([meta, x[0], jnp.zeros((Tp - T, D_MODEL), F32)], axis=0)
    tgt = jnp.concatenate([jnp.zeros((N_META, D_MODEL), F32), loss_target[0], jnp.zeros((Tp - T, D_MODEL), F32)], axis=0)
    h0 = h
    a, a_t = rowwise(fn_norm, [tok(h)], [par(row(w["norm_mix_pre"], 0))], [(D_MODEL, BF16), (D_MODEL, BF16, 0)], "norm0")
    saved = []
    for i in range(DEPTH):
        j = i // 2
        if i == 0:
            sv_mix, landed = attn_fwd(a, a_t, WA[0], "attn0", later)
            attn_out_w, dn_in, dn_out = _rows_from_slots(landed[1], 2), _cols_from_slots(landed[2], 2), _rows_from_slots(landed[3], 2)
            ffn_up, ffn_down = _il(_cols_from_slots(landed[4], DEPTH)), _rows_from_slots(landed[5], DEPTH)
            WA[0]["out"] = (attn_out_w, 0)
            WA[1] = attn_weights(1, _cols_from_slots(landed[0], 1), (attn_out_w, 1))
            WD = [dict(main=(dn_in, k), small=_pad_cols(dn_in[k, :, 4 * DN_HD:]), conv=dn_conv[k], alog=lane8(w["dn_a_log"][k]),
                       dtb=lane8(w["dn_dt_bias"][k]), ogain=w["dn_o_norm"][k][None, :], out=(dn_out, k)) for k in range(2)]
            WF = [dict(up=(ffn_up, k), conv=ffn_conv[k], down=(ffn_down, k)) for k in range(DEPTH)]
            mix = attn_out(sv_mix, WA[0], "attn0")
        elif i % 2 == 0:
            sv_mix, _ = attn_fwd(a, a_t, WA[j], f"attn{j}")
            mix = attn_out(sv_mix, WA[j], f"attn{j}")
        else:
            mix, sv_mix = dn_fwd(a, a_t, WD[j], f"dn{j}")
        norm_outs = [(D_MODEL, F32), (D_MODEL, BF16), (D_MODEL, BF16, 1)]
        h_mid, b, b_t = rowwise(fn_resid_norm, [tok(h), tok(mix)], [par(row(w["norm_mix_post"], i)), par(row(w["norm_ffn_pre"], i))],
                                norm_outs, f"resid_mix{i}")
        f, sv_ffn = ffn_fwd(b, b_t, WF[i], f"ffn{i}")
        if i < DEPTH - 1:
            h_out, a, a_t = rowwise(fn_resid_norm, [tok(h_mid), tok(f)], [par(row(w["norm_ffn_post"], i)), par(row(w["norm_mix_pre"], i + 1))],
                                    norm_outs, f"resid_ffn{i}")
        else:
            h_out = rowwise(fn_resid, [tok(h_mid), tok(f)], [par(row(w["norm_ffn_post"], i))], [(D_MODEL, F32)], f"resid_ffn{i}")[0]
        saved.append((h, mix, sv_mix, h_mid, f, sv_ffn))
        h = h_out
    lvec, dh = loss_head(h, tgt, L, "loss_head")
    loss = lax.psum(jnp.sum(lvec), ("x", "y", "c"))

    gn = {n: [None] * DEPTH for n in ("norm_mix_pre", "norm_mix_post", "norm_ffn_pre", "norm_ffn_post")}
    g_attn, g_dn, g_ffn = [None, None], [None, None], [None] * DEPTH
    da = None
    for i in reversed(range(DEPTH)):
        j = i // 2
        h_in, mix, sv_mix, h_mid, f, sv_ffn = saved[i]
        if i == DEPTH - 1:
            (dh, df), (gn["norm_ffn_post"][i],) = rowwise_bwd(
                fn_resid, [tok(h_mid, grad=F32), tok(f, grad=BF16)], [par(row(w["norm_ffn_post"], i))], [dh], f"resid_ffn{i}_b")
        else:
            (dh, df), (gn["norm_ffn_post"][i], gn["norm_mix_pre"][i + 1]) = rowwise_bwd(
                fn_resid_norm, [tok(h_mid, grad=F32), tok(f, grad=BF16)],
                [par(row(w["norm_ffn_post"], i)), par(row(w["norm_mix_pre"], i + 1))], [dh, da], f"resid_ffn{i}_b")
        db, g_ffn[i] = ffn_bwd(df, sv_ffn, WF[i], f"ffn{i}")
        (dh, dm), (gn["norm_mix_post"][i], gn["norm_ffn_pre"][i]) = rowwise_bwd(
            fn_resid_norm, [tok(h_in, grad=F32), tok(mix, grad=BF16)],
            [par(row(w["norm_mix_post"], i)), par(row(w["norm_ffn_pre"], i))], [dh, db], f"resid_mix{i}_b")
        if i == 0:
            def riders(d_out0):
                return [_cols_to_slots(g_attn[1]["w_in"][None]), _rows_to_slots(jnp.stack([d_out0, g_attn[1]["w_out"]])),
                        _cols_to_slots(jnp.stack([g["w_in"] for g in g_dn])), _rows_to_slots(jnp.stack([g["w_out"] for g in g_dn])),
                        _cols_to_slots(jnp.stack([g["up"] for g in g_ffn])), _rows_to_slots(jnp.stack([g["down"] for g in g_ffn]))]
            da, g_attn[0], landed_grads = attn_bwd(dm, sv_mix, WA[0], "attn0", riders)
        elif i % 2 == 0:
            da, g_attn[j], _ = attn_bwd(dm, sv_mix, WA[j], f"attn{j}")
        else:
            da, g_dn[j] = dn_bwd(dm, sv_mix, WD[j], f"dn{j}")
    (dh0,), (gn["norm_mix_pre"][0],) = rowwise_bwd(fn_id_norm, [tok(h0, grad=F32)], [par(row(w["norm_mix_pre"], 0))], [dh, da], "norm0_b")
    grad_x = dh0[N_META:T][None]

    full = dict(
        meta_tokens=dh0[:N_META],
        **{n: jnp.concatenate(gn[n], axis=0) for n in gn},
        attn_b_forget=jnp.stack([g["b_forget"] for g in g_attn]), attn_q_norm=jnp.stack([g["q_norm"] for g in g_attn]),
        attn_k_norm=jnp.stack([g["k_norm"] for g in g_attn]),
        dn_conv=jnp.stack([g["conv"] for g in g_dn]), dn_a_log=jnp.stack([g["a_log"] for g in g_dn]),
        dn_dt_bias=jnp.stack([g["dt_bias"] for g in g_dn]), dn_o_norm=jnp.stack([g["o_norm"] for g in g_dn]),
        ffn_conv=jnp.stack([g["conv"] for g in g_ffn]))
    flat = jnp.concatenate([full[n].reshape(-1) for n in SMALL_NAMES])
    rows = -(-flat.shape[0] // PACK_COLS)
    rows = -(-rows // 8) * 8
    pack = jnp.pad(flat, (0, rows * PACK_COLS - flat.shape[0])).reshape(rows, PACK_COLS)
    own4 = exchange_chips(False, [pack], "gather_small_grads")[0]
    sib4 = swap_cores([own4], "swap_small_grads")[0]
    tot = sum8(own4, sib4, "sum_small_grads").reshape(-1)
    grads, off = {}, 0
    for n in SMALL_NAMES:
        size = full[n].size
        g = tot[off:off + size].reshape(full[n].shape)
        off += size
        if g.shape != w[n].shape:
            g = lax.dynamic_slice_in_dim(g, slot * w[n].shape[-1], w[n].shape[-1], axis=g.ndim - 1)
        grads[n] = g
    d_s, m_s, v_s = adamw_small([as2d(w[n]) for n in SMALL_NAMES], [as2d(m[n]) for n in SMALL_NAMES],
                                [as2d(v[n]) for n in SMALL_NAMES], [as2d(grads[n]) for n in SMALL_NAMES], "adamw_small")
    deltas = {n: d.reshape(w[n].shape) for n, d in zip(SMALL_NAMES, d_s, strict=True)}
    new_m = {n: d.reshape(w[n].shape) for n, d in zip(SMALL_NAMES, m_s, strict=True)}
    new_v = {n: d.reshape(w[n].shape) for n, d in zip(SMALL_NAMES, v_s, strict=True)}

    last = exchange_chips(True, [_cols_to_slots(g_attn[0]["w_in"][None])], "scatter_last")[0]
    recv = [jnp.concatenate([last, landed_grads[0]], axis=1)] + list(landed_grads[1:])
    part = [sum_slots(r, f"sum_{n}") for n, r in zip(BIG_NAMES, recv, strict=True)]
    other = swap_cores(part, "swap_grads")
    for n, pa, pb in zip(BIG_NAMES, part, other, strict=True):
        g, d, m2, v2 = adamw_big(as2d(w[n]), as2d(m[n]), as2d(v[n]), pa, pb, f"adamw_{n}")
        grads[n], deltas[n], new_m[n], new_v[n] = (t.reshape(w[n].shape) for t in (g, d, m2, v2))
    return loss, grad_x, grads, deltas, new_m, new_v


def kernel(x, meta_tokens, norm_mix_pre, norm_mix_post, norm_ffn_pre, norm_ffn_post, attn_w_in, attn_b_forget, attn_q_norm, attn_k_norm, attn_w_out, dn_w_in, dn_conv, dn_a_log, dn_dt_bias, dn_o_norm, dn_w_out, ffn_w_up, ffn_conv, ffn_w_down, loss_target, m_meta_tokens, m_norm_mix_pre, m_norm_mix_post, m_norm_ffn_pre, m_norm_ffn_post, m_attn_w_in, m_attn_b_forget, m_attn_q_norm, m_attn_k_norm, m_attn_w_out, m_dn_w_in, m_dn_conv, m_dn_a_log, m_dn_dt_bias, m_dn_o_norm, m_dn_w_out, m_ffn_w_up, m_ffn_conv, m_ffn_w_down, v_meta_tokens, v_norm_mix_pre, v_norm_mix_post, v_norm_ffn_pre, v_norm_ffn_post, v_attn_w_in, v_attn_b_forget, v_attn_q_norm, v_attn_k_norm, v_attn_w_out, v_dn_w_in, v_dn_conv, v_dn_a_log, v_dn_dt_bias, v_dn_o_norm, v_dn_w_out, v_ffn_w_up, v_ffn_conv, v_ffn_w_down):
    w = dict(meta_tokens=meta_tokens, norm_mix_pre=norm_mix_pre, norm_mix_post=norm_mix_post, norm_ffn_pre=norm_ffn_pre, norm_ffn_post=norm_ffn_post, attn_w_in=attn_w_in, attn_b_forget=attn_b_forget, attn_q_norm=attn_q_norm, attn_k_norm=attn_k_norm, attn_w_out=attn_w_out, dn_w_in=dn_w_in, dn_conv=dn_conv, dn_a_log=dn_a_log, dn_dt_bias=dn_dt_bias, dn_o_norm=dn_o_norm, dn_w_out=dn_w_out, ffn_w_up=ffn_w_up, ffn_conv=ffn_conv, ffn_w_down=ffn_w_down)
    m = dict(meta_tokens=m_meta_tokens, norm_mix_pre=m_norm_mix_pre, norm_mix_post=m_norm_mix_post, norm_ffn_pre=m_norm_ffn_pre, norm_ffn_post=m_norm_ffn_post, attn_w_in=m_attn_w_in, attn_b_forget=m_attn_b_forget, attn_q_norm=m_attn_q_norm, attn_k_norm=m_attn_k_norm, attn_w_out=m_attn_w_out, dn_w_in=m_dn_w_in, dn_conv=m_dn_conv, dn_a_log=m_dn_a_log, dn_dt_bias=m_dn_dt_bias, dn_o_norm=m_dn_o_norm, dn_w_out=m_dn_w_out, ffn_w_up=m_ffn_w_up, ffn_conv=m_ffn_conv, ffn_w_down=m_ffn_w_down)
    v = dict(meta_tokens=v_meta_tokens, norm_mix_pre=v_norm_mix_pre, norm_mix_post=v_norm_mix_post, norm_ffn_pre=v_norm_ffn_pre, norm_ffn_post=v_norm_ffn_post, attn_w_in=v_attn_w_in, attn_b_forget=v_attn_b_forget, attn_q_norm=v_attn_q_norm, attn_k_norm=v_attn_k_norm, attn_w_out=v_attn_w_out, dn_w_in=v_dn_w_in, dn_conv=v_dn_conv, dn_a_log=v_dn_a_log, dn_dt_bias=v_dn_dt_bias, dn_o_norm=v_dn_o_norm, dn_w_out=v_dn_w_out, ffn_w_up=v_ffn_w_up, ffn_conv=v_ffn_conv, ffn_w_down=v_ffn_w_down)
    loss, grad_x, grads, deltas, new_m, new_v = train_step(x, loss_target, w, m, v)
    return (loss, grad_x, *[grads[n] for n in WEIGHT_NAMES], *[deltas[n] for n in WEIGHT_NAMES],
            *[new_m[n] for n in WEIGHT_NAMES], *[new_v[n] for n in WEIGHT_NAMES])
```

```python
import functools

import jax
import jax.numpy as jnp
from jax import lax
from jax.experimental import pallas as pl
from jax.experimental.pallas import tpu as pltpu

F32, BF16 = jnp.float32, jnp.bfloat16

D_MODEL = 1024
N_META = 16
ATT_HEADS, ATT_DH = 16, 64
ATT_HD = ATT_HEADS * ATT_DH
AUG = 128
ATT_AUG = ATT_HEADS * AUG
DN_HEADS, DN_DH = 8, 128
DN_HD = DN_HEADS * DN_DH
DN_CHUNK = 64
FFN_DIM = 2816
DEPTH = 4
EPS = 1e-6
NEG = -1e30

ADAM_LR, ADAM_B1, ADAM_B2, ADAM_EPS, ADAM_WD, ADAM_STEP = 0.001, 0.9, 0.999, 1e-08, 0.01, 10

TM = 640
HALO = 8
VMEM_LIMIT = 52 * 1024 * 1024
MESH_ID = pl.DeviceIdType.MESH


def _cp(sem):
    return pltpu.CompilerParams(dimension_semantics=sem, vmem_limit_bytes=VMEM_LIMIT)


def _pick(n, cands):
    for c in cands:
        if n % c == 0:
            return c
    return n


def matmul(a, b, mode, out_dtype, name, add=None, ncols=None):
    layer = None
    if isinstance(b, tuple):
        b, layer = b
    bshape = b.shape[-2:]
    if mode == "nn":
        (M, K), N = a.shape, ncols or bshape[1]
    elif mode == "nt":
        (M, K), N = a.shape, bshape[0]
    else:
        (K, M), N = a.shape, bshape[1]
    tm = _pick(M, (1664, TM, 1024, 1408, 512, 256, 128))
    tn = _pick(N, (1024, 1408, 512, 256, 128))
    tk = _pick(K, (1664, TM, 1024, 1408, 512, 256, 128))
    nk = K // tk
    if mode == "nn":
        a_spec = pl.BlockSpec((tm, tk), lambda i, j, k: (i, k))
        b_spec = pl.BlockSpec((tk, tn), lambda i, j, k: (k, j))
        dims = (((1,), (0,)), ((), ()))
    elif mode == "nt":
        a_spec = pl.BlockSpec((tm, tk), lambda i, j, k: (i, k))
        b_spec = pl.BlockSpec((tn, tk), lambda i, j, k: (j, k))
        dims = (((1,), (1,)), ((), ()))
    else:
        a_spec = pl.BlockSpec((tk, tm), lambda i, j, k: (k, i))
        b_spec = pl.BlockSpec((tk, tn), lambda i, j, k: (k, j))
        dims = (((0,), (0,)), ((), ()))
    if layer is not None:
        b_spec = pl.BlockSpec((None,) + b_spec.block_shape, functools.partial(lambda i, j, k, f: (layer,) + f(i, j, k), f=b_spec.index_map))
    o_spec = pl.BlockSpec((tm, tn), lambda i, j, k: (i, j))
    has_add = add is not None

    def body(*refs):
        if has_add:
            a_ref, b_ref, add_ref, o_ref, acc_ref = refs
        else:
            a_ref, b_ref, o_ref, acc_ref = refs
        k = pl.program_id(2)
        part = lax.dot_general(a_ref[...], b_ref[...], dims, preferred_element_type=F32)

        @pl.when(k == 0)
        def _():
            acc_ref[...] = part

        @pl.when(k > 0)
        def _():
            acc_ref[...] += part

        @pl.when(k == nk - 1)
        def _():
            r = acc_ref[...]
            if has_add:
                r = r + add_ref[...].astype(F32)
            o_ref[...] = r.astype(out_dtype)

    in_specs = [a_spec, b_spec] + ([o_spec] if has_add else [])
    args = (a, b) + ((add,) if has_add else ())
    return pl.pallas_call(
        body, name=name, grid=(M // tm, N // tn, nk), in_specs=in_specs, out_specs=o_spec,
        out_shape=jax.ShapeDtypeStruct((M, N), out_dtype),
        scratch_shapes=[pltpu.VMEM((tm, tn), F32)],
        compiler_params=_cp(("parallel", "parallel", "arbitrary")),
    )(*args)


def tok(arr, width=None, col=0, conv=None, grad=None, fixed=False):
    return dict(arr=arr, w=arr.shape[1] if width is None else width, col=col, conv=conv, grad=grad, step=0 if fixed else 1)


def par(arr, width=None, col=0):
    return dict(arr=arr, w=arr.shape[1] if width is None else width, col=col)


def _conv_apply(x, halo, w, ext_ref):
    K = w.shape[0]
    rows = x.shape[0]
    ext_ref[0:HALO, :] = halo
    ext_ref[HALO:, :] = x
    y = w[K - 1:K, :] * x
    for k in range(K - 1):
        y = y + w[k:k + 1, :] * ext_ref[pl.ds(HALO - (K - 1) + k, rows), :]
    return y


def _row_specs(toks, pars):
    specs, args = [], []
    for t in toks:
        specs.append(pl.BlockSpec((TM, t["w"]), functools.partial(lambda i, j, c, st: (i, c + st * j), c=t["col"], st=t["step"])))
        args.append(t["arr"])
        if t["conv"] is not None:
            specs.append(pl.BlockSpec((HALO, t["w"]), functools.partial(
                lambda i, j, c: (jnp.maximum(i * (TM // HALO) - 1, 0), c + j), c=t["col"])))
            args.append(t["arr"])
            cw, ccol = t["conv"]
            specs.append(pl.BlockSpec((cw.shape[0], t["w"]), functools.partial(lambda i, j, c: (0, c + j), c=ccol)))
            args.append(cw)
    for p in pars:
        specs.append(pl.BlockSpec((p["arr"].shape[0], p["w"]), functools.partial(lambda i, j, c: (0, c), c=p["col"])))
        args.append(p["arr"])
    return specs, args


def _row_load(toks, pars, refs, ext_refs):
    i = pl.program_id(0)
    vals, n, e = [], 0, 0
    for t in toks:
        x = refs[n][...].astype(F32)
        n += 1
        if t["conv"] is not None:
            halo = jnp.where(i == 0, 0.0, refs[n][...].astype(F32))
            w = refs[n + 1][...]
            n += 2
            x = _conv_apply(x, halo, w, ext_refs[e])
            e += 1
        vals.append(x)
    for _ in pars:
        vals.append(refs[n][...])
        n += 1
    return vals, n


def rowwise(fn, toks, pars, outs, name, nsplit=1, with_j=False):
    Tp = toks[0]["arr"].shape[0]
    specs, args = _row_specs(toks, pars)
    n_ext = sum(t["conv"] is not None for t in toks)

    def body(*refs):
        ext_refs = refs[len(refs) - n_ext:]
        vals, n = _row_load(toks, pars, refs, ext_refs)
        res = fn(pl.program_id(1), *vals) if with_j else fn(*vals)
        plain = 0
        for o, o_ref in zip(outs, refs[n:n + len(outs)], strict=True):
            if len(o) == 2:
                o_ref[...] = res[plain].astype(o_ref.dtype)
                plain += 1
            else:
                o_ref[...] = res[o[2]].T.astype(o_ref.dtype)

    return pl.pallas_call(
        body, name=name, grid=(Tp // TM, nsplit), in_specs=specs,
        out_specs=[pl.BlockSpec((TM, o[0]), lambda i, j: (i, j)) if len(o) == 2 else pl.BlockSpec((o[0], TM), lambda i, j: (j, i))
                   for o in outs],
        out_shape=[jax.ShapeDtypeStruct((Tp, o[0] * nsplit) if len(o) == 2 else (o[0] * nsplit, Tp), o[1]) for o in outs],
        scratch_shapes=[pltpu.VMEM((TM + HALO, t["w"]), F32) for t in toks if t["conv"] is not None],
        compiler_params=_cp(("parallel", "parallel")),
    )(*args)


def rowwise_bwd(fn, toks, pars, cts, name, groups=None, par_grads=None, nsplit=1, with_j=False):
    Tp = toks[0]["arr"].shape[0]
    specs, args = _row_specs(toks, pars)
    n_ext = sum(t["conv"] is not None for t in toks)
    gidx = [k for k, t in enumerate(toks) if t["grad"] is not None]
    if groups is None:
        groups = [[k] for k in range(len(gidx))]
    par_grads = list(range(len(pars))) if par_grads is None else par_grads
    ct_specs, ct_args = [], []
    for c in cts:
        if c is not None:
            ct_specs.append(pl.BlockSpec((TM, c.shape[1] // nsplit), lambda i, j: (i, j)))
            ct_args.append(c)
    out_shapes, out_specs = [], []
    for g in groups:
        w = sum(toks[gidx[k]]["w"] for k in g)
        out_shapes.append(jax.ShapeDtypeStruct((Tp, w * nsplit), toks[gidx[g[0]]]["grad"]))
        out_specs.append(pl.BlockSpec((TM, w), lambda i, j: (i, j)))
    for k in par_grads:
        shp = (pars[k]["arr"].shape[0], pars[k]["w"])
        out_shapes.append(jax.ShapeDtypeStruct(shp, F32))
        out_specs.append(pl.BlockSpec(shp, lambda i, j: (0, 0)))
    n_ct = len(ct_args)

    def body(*refs):
        first = (pl.program_id(0) == 0) & (pl.program_id(1) == 0)
        ext_refs = refs[len(refs) - n_ext:]
        vals, n = _row_load(toks, pars, refs, ext_refs)
        ct_refs = refs[n:n + n_ct]
        o_refs = refs[n + n_ct:len(refs) - n_ext]
        res, vjp = jax.vjp(functools.partial(fn, pl.program_id(1)) if with_j else fn, *vals)
        ct_vals, c = [], 0
        for r, ct in zip(res, cts, strict=True):
            if ct is None:
                ct_vals.append(jnp.zeros_like(r))
            else:
                ct_vals.append(ct_refs[c][...].astype(F32))
                c += 1
        grads = vjp(tuple(ct_vals))
        for g, o_ref in zip(groups, o_refs[:len(groups)], strict=True):
            pieces = [grads[gidx[k]] for k in g]
            val = pieces[0] if len(pieces) == 1 else jnp.concatenate(pieces, axis=1)
            o_ref[...] = val.astype(o_ref.dtype)
        for k, o_ref in zip(par_grads, o_refs[len(groups):], strict=True):
            gk = grads[len(toks) + k]

            @pl.when(first)
            def _(o_ref=o_ref, gk=gk):
                o_ref[...] = gk

            @pl.when(jnp.logical_not(first))
            def _(o_ref=o_ref, gk=gk):
                o_ref[...] += gk

    res = pl.pallas_call(
        body, name=name, grid=(Tp // TM, nsplit), in_specs=specs + ct_specs, out_specs=out_specs, out_shape=out_shapes,
        scratch_shapes=[pltpu.VMEM((TM + HALO, t["w"]), F32) for t in toks if t["conv"] is not None],
        compiler_params=_cp(("arbitrary", "arbitrary")),
    )(*args, *ct_args)
    return res[:len(groups)], res[len(groups):]


ROWS = 16


def _shift_rows(win, s, lo, rows):
    return pltpu.roll(win, (-s) % win.shape[0], 0)[lo:lo + rows, :] if s else win[lo:lo + rows, :]


def conv_bwd(dy, x, w, name):
    Tp, W = dy.shape
    K = w.shape[0]
    wb = _pick(W, (512, 256, 128))
    nt = Tp // TM

    def body(dy_ref, dyn_ref, x_ref, w_ref, dx_ref, dw_ref, ext_dy):
        i = pl.program_id(1)
        ext_dy[0:TM, :] = dy_ref[...].astype(F32)
        ext_dy[TM:, :] = jnp.where(i == nt - 1, 0.0, dyn_ref[...].astype(F32))
        wv = w_ref[...]

        def chunk(t, sums):
            r0 = pl.multiple_of(t * ROWS, ROWS)
            dyw = ext_dy[pl.ds(r0, ROWS + HALO), :]
            xv = x_ref[pl.ds(r0, ROWS), :].astype(F32)
            dx, new = None, []
            for k in range(K):
                dys = _shift_rows(dyw, K - 1 - k, 0, ROWS)
                term = wv[k:k + 1, :] * dys
                dx = term if dx is None else dx + term
                prod = dys * xv
                new.append(sums[k] + (prod[0:HALO, :] + prod[HALO:ROWS, :]))
            dx_ref[pl.ds(r0, ROWS), :] = dx.astype(dx_ref.dtype)
            return tuple(new)

        sums = lax.fori_loop(0, TM // ROWS, chunk, tuple(jnp.zeros((HALO, wb), F32) for _ in range(K)))
        dwv = jnp.concatenate([jnp.sum(sm, axis=0, keepdims=True) for sm in sums], axis=0)

        @pl.when(i == 0)
        def _():
            dw_ref[...] = dwv

        @pl.when(i > 0)
        def _():
            dw_ref[...] += dwv

    return pl.pallas_call(
        body, name=name, grid=(W // wb, nt),
        in_specs=[
            pl.BlockSpec((TM, wb), lambda j, i: (i, j)),
            pl.BlockSpec((HALO, wb), lambda j, i: (jnp.minimum((i + 1) * (TM // HALO), Tp // HALO - 1), j)),
            pl.BlockSpec((TM, wb), lambda j, i: (i, j)),
            pl.BlockSpec((K, wb), lambda j, i: (0, j)),
        ],
        out_specs=[pl.BlockSpec((TM, wb), lambda j, i: (i, j)), pl.BlockSpec((K, wb), lambda j, i: (0, j))],
        out_shape=[jax.ShapeDtypeStruct((Tp, W), BF16), jax.ShapeDtypeStruct((K, W), F32)],
        scratch_shapes=[pltpu.VMEM((TM + HALO, wb), F32)],
        compiler_params=_cp(("parallel", "arbitrary")),
    )(dy, dy, x, w)


def _rms(x, g):
    return x * lax.rsqrt(jnp.mean(x * x, axis=-1, keepdims=True) + EPS) * g


def fn_norm(h, g):
    return (_rms(h, g),)


def fn_id_norm(h, g):
    return h, _rms(h, g)


def fn_resid_norm(h, m, g_post, g_next):
    h2 = h + _rms(m, g_post)
    return h2, _rms(h2, g_next)


def fn_resid(h, m, g_post):
    return (h + _rms(m, g_post),)


def fn_geglu(y):
    w = y.shape[1] // 2
    return (jax.nn.gelu(y[:, :w], approximate=True) * y[:, w:],)


def fn_geglu_keep(y):
    return fn_geglu(y) + (y,)


def _split3(c):
    hi = c.astype(BF16).astype(F32)
    r = c - hi
    mid = r.astype(BF16).astype(F32)
    lo = (r - mid).astype(BF16).astype(F32)
    return hi, mid, lo


def _aug_cols(rows, entries):
    lane = lax.broadcasted_iota(jnp.int32, (rows, ATT_DH), 1)
    out = jnp.zeros((rows, ATT_DH), F32)
    for ln, val in entries:
        out = jnp.where(lane == ln, val, out)
    return out


def _head_col(c, idx):
    lane = lax.broadcasted_iota(jnp.int32, c.shape, 1)
    return jnp.sum(jnp.where(lane == idx, c, 0.0), axis=1, keepdims=True)


def fn_attn_prep_q(j, xp, c, gain):
    rows = xp.shape[0]
    out = []
    for hh in range(2):
        hi, mid, lo = _split3(lax.stop_gradient(_head_col(c, 2 * j + hh)))
        out += [_rms(xp[:, ATT_DH * hh:ATT_DH * (hh + 1)], gain) * (ATT_DH ** -0.5),
                _aug_cols(rows, [(0, hi), (1, mid), (2, lo), (3, 1.0), (4, 1.0), (5, 1.0)])]
    return (jnp.concatenate(out, axis=1),)


def fn_attn_prep_k(j, xp, c, gain):
    rows = xp.shape[0]
    out = []
    for hh in range(2):
        hi, mid, lo = _split3(lax.stop_gradient(_head_col(c, 2 * j + hh)))
        out += [_rms(xp[:, ATT_DH * hh:ATT_DH * (hh + 1)], gain),
                _aug_cols(rows, [(0, 1.0), (1, 1.0), (2, 1.0), (3, -hi), (4, -mid), (5, -lo), (6, 1.0), (7, 1.0), (8, 1.0)])]
    return (jnp.concatenate(out, axis=1),)


def fn_attn_prep_v(xp):
    rows = xp.shape[0]
    out = []
    for hh in range(2):
        out += [xp[:, ATT_DH * hh:ATT_DH * (hh + 1)], _aug_cols(rows, [(0, 1.0), (1, 1.0), (2, 1.0)])]
    return (jnp.concatenate(out, axis=1),)


def fn_attn_gate(o_aug, og):
    outs = []
    for h in range(og.shape[1] // ATT_DH):
        o = o_aug[:, AUG * h:AUG * h + ATT_DH]
        outs.append(o * jax.nn.sigmoid(og[:, ATT_DH * h:ATT_DH * (h + 1)]))
    return (jnp.concatenate(outs, axis=1),)


def fn_attn_bwd_prep(dgo, o_aug, og, q_aug):
    rows = dgo.shape[0]
    lane = lax.broadcasted_iota(jnp.int32, (rows, ATT_DH), 1)
    dos, dogs, qs = [], [], []
    for h in range(og.shape[1] // ATT_DH):
        sl = slice(ATT_DH * h, ATT_DH * (h + 1))
        o = o_aug[:, AUG * h:AUG * h + ATT_DH]
        lse = o_aug[:, AUG * h + ATT_DH:AUG * h + ATT_DH + 1]
        sig = jax.nn.sigmoid(og[:, sl])
        do = dgo[:, sl] * sig
        dogs.append(dgo[:, sl] * o * sig * (1.0 - sig))
        dhi, dmid, dlo = _split3(-jnp.sum(do * o, axis=-1, keepdims=True))
        dos += [do, _aug_cols(rows, [(0, dhi), (1, dmid), (2, dlo)])]
        lhi, lmid, llo = _split3(-lse)
        qa = q_aug[:, AUG * h + ATT_DH:AUG * (h + 1)]
        qa = jnp.where(lane == 6, lhi, jnp.where(lane == 7, lmid, jnp.where(lane == 8, llo, qa)))
        qs += [q_aug[:, AUG * h:AUG * h + ATT_DH], qa]
    return jnp.concatenate(dos, axis=1), jnp.concatenate(dogs, axis=1), jnp.concatenate(qs, axis=1)


def fn_attn_dc(dq_aug, dk_aug):
    lane = lax.broadcasted_iota(jnp.int32, (dk_aug.shape[0], AUG), 1)
    dc = jnp.zeros((dk_aug.shape[0], AUG), F32)
    for h in range(ATT_HEADS):
        col = AUG * h + ATT_DH
        dc = jnp.where(lane == h, dq_aug[:, col:col + 1] - dk_aug[:, col + 3:col + 4], dc)
    return (dc,)


def fn_dn_prep_keep(j, y):
    return fn_dn_prep(j, y) + (y,)


def fn_dn_prep(j, y):
    s = jax.nn.silu(y)
    scale = jnp.where(j < DN_HD // 256, DN_DH ** -0.5, 1.0)
    out = []
    for hh in range(2):
        sh = s[:, DN_DH * hh:DN_DH * (hh + 1)]
        n = sh * lax.rsqrt(jnp.sum(sh * sh, axis=-1, keepdims=True) + EPS) * scale
        out.append(jnp.where(j < 2 * (DN_HD // 256), n, sh))
    return (jnp.concatenate(out, axis=1),)


def fn_dn_gates(lg, alog, dtb):
    lane = lax.broadcasted_iota(jnp.int32, lg.shape, 1)
    beta = jax.nn.sigmoid(lg)
    g = -jnp.exp(alog) * jax.nn.softplus(lg + dtb)
    return (jnp.where(lane < DN_HEADS, beta, jnp.where(lane < 2 * DN_HEADS, g, 0.0)),)


def fn_dn_post(o, og, gain):
    outs = []
    for h in range(DN_HEADS):
        sl = slice(DN_DH * h, DN_DH * (h + 1))
        outs.append(_rms(o[:, sl], gain) * jax.nn.silu(og[:, sl]))
    return (jnp.concatenate(outs, axis=1),)


def _tri(n, lower):
    r = lax.broadcasted_iota(jnp.int32, (n, n), 0)
    c = lax.broadcasted_iota(jnp.int32, (n, n), 1)
    return jnp.where((r >= c) if lower else (r <= c), 1.0, 0.0).astype(F32)


def forget_cumsum(fl, bias, name):
    Tp, W = fl.shape

    def body(fl_ref, b_ref, c_ref, carry):
        i = pl.program_id(0)

        @pl.when(i == 0)
        def _():
            carry[...] = jnp.zeros_like(carry)

        logf = jax.nn.log_sigmoid(fl_ref[...] + b_ref[...])
        c = jnp.dot(_tri(TM, True), logf, precision=lax.Precision.HIGHEST, preferred_element_type=F32) + carry[...]
        c_ref[...] = c
        carry[...] = c[TM - 1:TM, :]

    return pl.pallas_call(
        body, name=name, grid=(Tp // TM,),
        in_specs=[pl.BlockSpec((TM, W), lambda i: (i, 0)), pl.BlockSpec((1, W), lambda i: (0, 0))],
        out_specs=pl.BlockSpec((TM, W), lambda i: (i, 0)), out_shape=jax.ShapeDtypeStruct((Tp, W), F32),
        scratch_shapes=[pltpu.VMEM((1, W), F32)], compiler_params=_cp(("arbitrary",)),
    )(fl, bias)


def forget_cumsum_bwd(dc, fl, bias, name):
    Tp, W = fl.shape
    nt = Tp // TM

    def body(dc_ref, fl_ref, b_ref, dfl_ref, db_ref, carry):
        i = pl.program_id(0)

        @pl.when(i == 0)
        def _():
            carry[...] = jnp.zeros_like(carry)

        dlogf = jnp.dot(_tri(TM, False), dc_ref[...], precision=lax.Precision.HIGHEST, preferred_element_type=F32) + carry[...]
        carry[...] = dlogf[0:1, :]
        dfl = dlogf * jax.nn.sigmoid(-(fl_ref[...] + b_ref[...]))
        dfl_ref[...] = dfl.astype(dfl_ref.dtype)
        s = jnp.sum(dfl, axis=0, keepdims=True)

        @pl.when(i == 0)
        def _():
            db_ref[...] = s

        @pl.when(i > 0)
        def _():
            db_ref[...] += s

    rev = lambda i: (nt - 1 - i, 0)
    return pl.pallas_call(
        body, name=name, grid=(nt,),
        in_specs=[pl.BlockSpec((TM, W), rev), pl.BlockSpec((TM, W), rev), pl.BlockSpec((1, W), lambda i: (0, 0))],
        out_specs=[pl.BlockSpec((TM, W), rev), pl.BlockSpec((1, W), lambda i: (0, 0))],
        out_shape=[jax.ShapeDtypeStruct((Tp, W), BF16), jax.ShapeDtypeStruct((1, W), F32)],
        scratch_shapes=[pltpu.VMEM((1, W), F32)], compiler_params=_cp(("arbitrary",)),
    )(dc, fl, bias)


_HBM = pl.BlockSpec(memory_space=pltpu.HBM)


def _place():
    x, y, c = lax.axis_index("x"), lax.axis_index("y"), lax.axis_index("c")
    return x, y, c, [(1 - x, y), (x, 1 - y), (1 - x, 1 - y)]


def _chip_copies(scatter, ins, outs, send_sems, recv_sems, local_sems):
    x, y, c, chips = _place()
    s = 2 * x + y
    copies = []
    for a in range(len(ins)):
        copies.append(pltpu.make_async_copy(ins[a].at[s] if scatter else ins[a], outs[a].at[s], local_sems.at[a]))
        for p, (px, py) in enumerate(chips):
            copies.append(pltpu.make_async_remote_copy(
                src_ref=ins[a].at[2 * px + py] if scatter else ins[a], dst_ref=outs[a].at[s], send_sem=send_sems.at[a, p],
                recv_sem=recv_sems.at[a, p], device_id=(px, py, c), device_id_type=MESH_ID))
    return copies


def _chip_exchange_shapes(scatter, arrs):
    n = len(arrs)
    out_shape = [jax.ShapeDtypeStruct(a.shape if scatter else (4,) + a.shape, a.dtype) for a in arrs]
    sems = [pltpu.SemaphoreType.DMA((n, 3)), pltpu.SemaphoreType.DMA((n, 3)), pltpu.SemaphoreType.DMA((n,))]
    return out_shape, sems


def exchange_chips(scatter, arrs, name):
    n = len(arrs)

    def body(*refs):
        copies = _chip_copies(scatter, refs[:n], refs[n:2 * n], *refs[2 * n:])
        for cp in copies:
            cp.start()
        for cp in copies:
            cp.wait()

    out_shape, sems = _chip_exchange_shapes(scatter, arrs)
    return pl.pallas_call(body, name=name, in_specs=[_HBM] * n, out_specs=[_HBM] * n, out_shape=out_shape, scratch_shapes=sems)(*arrs)


def swap_cores(arrs, name):
    n = len(arrs)

    def body(*refs):
        ins, outs = refs[:n], refs[n:2 * n]
        send_sems, recv_sems = refs[2 * n:]
        x, y, c, _ = _place()
        copies = []
        for a in range(n):
            cp = pltpu.make_async_remote_copy(
                src_ref=ins[a], dst_ref=outs[a], send_sem=send_sems.at[a], recv_sem=recv_sems.at[a],
                device_id=(x, y, 1 - c), device_id_type=MESH_ID)
            cp.start()
            copies.append(cp)
        for cp in copies:
            cp.wait()

    return pl.pallas_call(
        body, name=name, in_specs=[_HBM] * n, out_specs=[_HBM] * n,
        out_shape=[jax.ShapeDtypeStruct(a.shape, a.dtype) for a in arrs],
        scratch_shapes=[pltpu.SemaphoreType.DMA((n,)), pltpu.SemaphoreType.DMA((n,))],
    )(*arrs)


_NT = (((1,), (1,)), ((), ()))
_TN = (((0,), (0,)), ((), ()))


def _cargo_edges(scatter, n, refs, nb):
    first = (pl.program_id(0) == 0) & (pl.program_id(1) == 0)
    last = (pl.program_id(0) == ATT_HEADS - 1) & (pl.program_id(1) == nb - 1)

    @pl.when(first)
    def _():
        for cp in _chip_copies(scatter, refs[:n], refs[n:2 * n], *refs[2 * n:]):
            cp.start()

    @pl.when(last)
    def _():
        for cp in _chip_copies(scatter, refs[:n], refs[n:2 * n], *refs[2 * n:]):
            cp.wait()


def flash_fwd(q_aug, k_aug, v_aug, name, cargo=()):
    Tp = q_aug.shape[0]
    nb = Tp // TM
    nc = len(cargo)

    def body(*refs):
        q_ref, k_ref, v_ref = refs[:3]
        o_ref = refs[3 + nc]
        if nc:
            _cargo_edges(False, nc, refs[3:3 + nc] + refs[4 + nc:], nb)
        i = pl.program_id(1)
        q = q_ref[...]

        def rows(j):
            return pl.ds(pl.multiple_of(j * TM, TM), TM)

        def scores(j):
            return lax.dot_general(q, k_ref[rows(j), :], _NT, preferred_element_type=F32)

        def absorb(j, s, m_old, acc):
            m_new = jnp.maximum(m_old, jnp.max(s, axis=-1, keepdims=True))
            p = jnp.exp(s - m_new)
            acc = jnp.exp(m_old - m_new) * acc + jnp.dot(p.astype(BF16), v_ref[rows(j), :], preferred_element_type=F32)
            return m_new, acc

        def step(j, carry):
            m_old, acc, s = carry
            s_next = scores(j + 1)
            return absorb(j, s, m_old, acc) + (s_next,)

        m, acc, s = lax.fori_loop(0, i, step, (jnp.full((TM, 1), NEG, F32), jnp.zeros((TM, AUG), F32), scores(0)))
        r = lax.broadcasted_iota(jnp.int32, (TM, TM), 0)
        c = lax.broadcasted_iota(jnp.int32, (TM, TM), 1)
        m, acc = absorb(i, jnp.where(c <= r, s, NEG), m, acc)
        l = acc[:, ATT_DH:ATT_DH + 1]
        lane = lax.broadcasted_iota(jnp.int32, (TM, AUG), 1)
        o_ref[...] = jnp.where(lane < ATT_DH, acc / l, m + jnp.log(l))

    head = pl.BlockSpec((Tp, AUG), lambda h, i: (0, h))
    cargo_shape, sems = _chip_exchange_shapes(False, cargo) if nc else ([], [])
    res = pl.pallas_call(
        body, name=name, grid=(ATT_HEADS, nb),
        in_specs=[pl.BlockSpec((TM, AUG), lambda h, i: (i, h)), head, head] + [_HBM] * nc,
        out_specs=[pl.BlockSpec((TM, AUG), lambda h, i: (i, h))] + [_HBM] * nc,
        out_shape=[jax.ShapeDtypeStruct((Tp, ATT_AUG), F32)] + cargo_shape, scratch_shapes=sems,
        compiler_params=_cp(("arbitrary", "arbitrary")),
    )(q_aug, k_aug, v_aug, *cargo)
    return res[0], res[1:]


def flash_bwd(q_aug2, k_aug, v_aug, do_aug, name, cargo=()):
    Tp = q_aug2.shape[0]
    nb = Tp // TM
    nc = len(cargo)

    def body(*refs):
        q_ref, do_ref, k_ref, v_ref = refs[:4]
        dq_ref, dk_ref, dv_ref = refs[4 + nc:7 + nc]
        if nc:
            _cargo_edges(True, nc, refs[4:4 + nc] + refs[7 + nc:], nb)
        j = pl.program_id(1)
        k, v = k_ref[...], v_ref[...]

        @pl.when(j == 0)
        def _():
            dq_ref[...] = jnp.zeros_like(dq_ref)

        def block(i, carry, masked):
            dk, dv = carry
            rows = pl.ds(pl.multiple_of(i * TM, TM), TM)
            q, do = q_ref[rows, :], do_ref[rows, :]
            st = lax.dot_general(k, q, _NT, preferred_element_type=F32)
            if masked:
                r = lax.broadcasted_iota(jnp.int32, (TM, TM), 0)
                c = lax.broadcasted_iota(jnp.int32, (TM, TM), 1)
                st = jnp.where(r <= c, st, NEG)
            pt = jnp.exp(st)
            dpt = lax.dot_general(v, do, _NT, preferred_element_type=F32)
            dst = (pt * dpt).astype(BF16)
            dv = dv + jnp.dot(pt.astype(BF16), do, preferred_element_type=F32)
            dk = dk + jnp.dot(dst, q, preferred_element_type=F32)
            dq_ref[rows, :] += lax.dot_general(dst, k, _TN, preferred_element_type=F32)
            return dk, dv

        carry = block(j, (jnp.zeros((TM, AUG), F32), jnp.zeros((TM, AUG), F32)), True)
        dk, dv = lax.fori_loop(j + 1, nb, lambda i, cr: block(i, cr, False), carry)
        dk_ref[...] = dk
        dv_ref[...] = dv

    head = pl.BlockSpec((Tp, AUG), lambda h, j: (0, h))
    blk = pl.BlockSpec((TM, AUG), lambda h, j: (j, h))
    cargo_shape, sems = _chip_exchange_shapes(True, cargo) if nc else ([], [])
    res = pl.pallas_call(
        body, name=name, grid=(ATT_HEADS, nb),
        in_specs=[head, head, blk, blk] + [_HBM] * nc, out_specs=[head, blk, blk] + [_HBM] * nc,
        out_shape=[jax.ShapeDtypeStruct((Tp, ATT_AUG), F32)] * 3 + cargo_shape, scratch_shapes=sems,
        compiler_params=_cp(("arbitrary", "arbitrary")),
    )(q_aug2, do_aug, k_aug, v_aug, *cargo)
    return res[:3], res[3:]


_BATCH = ((0,), (0,))


def _dg(x, y, cx, cy):
    return lax.dot_general(x.astype(BF16), y.astype(BF16), (((cx + 1,), (cy + 1,)), _BATCH), preferred_element_type=F32)


def _split2(x):
    hi = x.astype(BF16)
    return hi, (x - hi.astype(F32)).astype(BF16)


def _dg3(x, y, cx, cy):
    (xh, xl), (yh, yl) = _split2(x), _split2(y)
    d = lambda a, b: lax.dot_general(a, b, (((cx + 1,), (cy + 1,)), _BATCH), preferred_element_type=F32)
    return d(xh, yh) + (d(xl, yh) + d(xh, yl))


def _make_bdot(ca, cb, dg):
    @jax.custom_vjp
    def f(a, b):
        return dg(a, b, ca, cb)

    def fwd(a, b):
        return dg(a, b, ca, cb), (a, b)

    def bwd(res, ct):
        a, b = res
        da = dg(ct, b, 1, 1 if cb == 0 else 0) if ca == 1 else dg(b, ct, 1 if cb == 0 else 0, 1)
        db = dg(a, ct, 0 if ca == 1 else 1, 0) if cb == 0 else dg(ct, a, 0, 0 if ca == 1 else 1)
        return da, db

    f.defvjp(fwd, bwd)
    return f


_bd_nn, _bd_nt, _bd_tn = _make_bdot(1, 0, _dg), _make_bdot(1, 1, _dg), _make_bdot(0, 0, _dg)
_bd3_nn = _make_bdot(1, 0, _dg3)


@jax.custom_vjp
def _chunk_cumsum(x):
    return jnp.dot(_tri(x.shape[0], True), x, precision=lax.Precision.HIGHEST, preferred_element_type=F32)


def _chunk_cumsum_fwd(x):
    return _chunk_cumsum(x), None


def _chunk_cumsum_bwd(_, ct):
    return (jnp.dot(_tri(ct.shape[0], False), ct, precision=lax.Precision.HIGHEST, preferred_element_type=F32),)


_chunk_cumsum.defvjp(_chunk_cumsum_fwd, _chunk_cumsum_bwd)


@jax.custom_vjp
def _inverse_given(A, N):
    return N


def _inverse_given_fwd(A, N):
    return N, N


def _inverse_given_bwd(N, ct):
    M = ct + _dg3(N, ct, 0, 0)
    return -(M + _dg3(M, N, 1, 1)), jnp.zeros_like(N)


_inverse_given.defvjp(_inverse_given_fwd, _inverse_given_bwd)


def dn_chunk(S, q, k, v, bg, N_given=None):
    C = q.shape[0]
    H = DN_HEADS
    heads = lambda t: jnp.stack([t[:, DN_DH * h:DN_DH * (h + 1)] for h in range(H)])
    cols = lambda t, o: jnp.stack([t[:, o + h:o + h + 1] for h in range(H)])
    qh, kh, vh = heads(q), heads(k), heads(v)
    beta, gc = cols(bg, 0), cols(_chunk_cumsum(bg), H)
    r = lax.broadcasted_iota(jnp.int32, (H, C, C), 1)
    c = lax.broadcasted_iota(jnp.int32, (H, C, C), 2)
    gcb = jnp.broadcast_to(gc, (H, C, C))
    dec = jnp.exp(jnp.where(r >= c, gcb - jnp.swapaxes(gcb, 1, 2), NEG))
    e_gc = jnp.exp(gc)
    kq = _bd_nt(jnp.concatenate([kh, qh], axis=1), kh)
    A = jnp.where(r > c, kq[:, :C] * dec * beta, 0.0)
    qk = kq[:, C:] * dec
    if N_given is None:
        X = -A
        P = _bd3_nn(X, X)
        N = X
        for i in range(5):
            NP = _bd3_nn(jnp.concatenate([N, P], axis=1) if i < 4 else N, P)
            N = N + P + NP[:, :C]
            if i < 4:
                P = NP[:, C:]
    else:
        N = _inverse_given(A, N_given)
    R = jnp.concatenate([kh * (beta * e_gc), vh * beta], axis=2)
    WU = R + _bd3_nn(N, R)
    W, U0 = WU[:, :, :DN_DH], WU[:, :, DN_DH:]
    gl = gc[:, C - 1:C, :]
    WqS = _bd_nt(jnp.concatenate([W, qh * e_gc], axis=1), S)
    U = U0 - WqS[:, :C]
    O = WqS[:, C:] + _bd_nn(qk, U)
    S_new = jnp.exp(gl) * S + _bd_tn(U, kh * jnp.exp(gl - gc))
    return jnp.concatenate([O[h] for h in range(H)], axis=1), S_new, N


def dn_scan_fwd(qkv, bg, name):
    Tp = qkv.shape[0]
    n = Tp // DN_CHUNK
    tokspec = lambda w, c=0: pl.BlockSpec((DN_CHUNK, w), lambda i: (i, c))

    def body(q_ref, k_ref, v_ref, bg_ref, o_ref, s_ref, n_ref, state):
        @pl.when(pl.program_id(0) == 0)
        def _():
            state[...] = jnp.zeros_like(state)

        S = state[...]
        s_ref[0] = S
        O, S_new, N = dn_chunk(S, q_ref[...].astype(F32), k_ref[...].astype(F32), v_ref[...].astype(F32), bg_ref[...])
        o_ref[...] = O.astype(o_ref.dtype)
        n_ref[0] = N
        state[...] = S_new

    return pl.pallas_call(
        body, name=name, grid=(n,),
        in_specs=[tokspec(DN_HD, 0), tokspec(DN_HD, 1), tokspec(DN_HD, 2), tokspec(AUG)],
        out_specs=[tokspec(DN_HD), pl.BlockSpec((1, DN_HEADS, DN_DH, DN_DH), lambda i: (i, 0, 0, 0)),
                   pl.BlockSpec((1, DN_HEADS, DN_CHUNK, DN_CHUNK), lambda i: (i, 0, 0, 0))],
        out_shape=[jax.ShapeDtypeStruct((Tp, DN_HD), BF16), jax.ShapeDtypeStruct((n, DN_HEADS, DN_DH, DN_DH), F32),
                   jax.ShapeDtypeStruct((n, DN_HEADS, DN_CHUNK, DN_CHUNK), F32)],
        scratch_shapes=[pltpu.VMEM((DN_HEADS, DN_DH, DN_DH), F32)],
        compiler_params=_cp(("arbitrary",)),
    )(qkv, qkv, qkv, bg)


def dn_scan_bwd(qkv, bg, states, inverses, dO, name):
    Tp = qkv.shape[0]
    n = Tp // DN_CHUNK
    tokspec = lambda w, c=0: pl.BlockSpec((DN_CHUNK, w), lambda i: (n - 1 - i, c))

    def body(q_ref, k_ref, v_ref, bg_ref, s_ref, n_ref, do_ref, dqkv_ref, dbg_ref, dstate):
        @pl.when(pl.program_id(0) == 0)
        def _():
            dstate[...] = jnp.zeros_like(dstate)

        N = n_ref[0]
        _, vjp = jax.vjp(lambda *xs: dn_chunk(*xs, N_given=N)[:2],
                         s_ref[0], q_ref[...].astype(F32), k_ref[...].astype(F32), v_ref[...].astype(F32), bg_ref[...])
        dS, dq, dk, dv, dbg = vjp((do_ref[...].astype(F32), dstate[...]))
        dqkv_ref[...] = jnp.concatenate([dq, dk, dv], axis=1).astype(dqkv_ref.dtype)
        dbg_ref[...] = dbg
        dstate[...] = dS

    return pl.pallas_call(
        body, name=name, grid=(n,),
        in_specs=[tokspec(DN_HD, 0), tokspec(DN_HD, 1), tokspec(DN_HD, 2), tokspec(AUG),
                  pl.BlockSpec((1, DN_HEADS, DN_DH, DN_DH), lambda i: (n - 1 - i, 0, 0, 0)),
                  pl.BlockSpec((1, DN_HEADS, DN_CHUNK, DN_CHUNK), lambda i: (n - 1 - i, 0, 0, 0)), tokspec(DN_HD)],
        out_specs=[tokspec(3 * DN_HD), tokspec(AUG)],
        out_shape=[jax.ShapeDtypeStruct((Tp, 3 * DN_HD), BF16), jax.ShapeDtypeStruct((Tp, AUG), F32)],
        scratch_shapes=[pltpu.VMEM((DN_HEADS, DN_DH, DN_DH), F32)],
        compiler_params=_cp(("arbitrary",)),
    )(qkv, qkv, qkv, bg, states, inverses, dO)


def loss_head(h, tgt, n_tok, name):
    Tp, Dm = h.shape

    def body(h_ref, t_ref, l_ref, dh_ref):
        i = pl.program_id(0)
        row = i * TM + lax.broadcasted_iota(jnp.int32, (TM, Dm), 0)
        e = jnp.where((row >= N_META) & (row < N_META + n_tok), h_ref[...] - t_ref[...], 0.0)
        dh_ref[...] = e * (1.0 / Dm)
        s = jnp.sum(e * e, axis=0, keepdims=True) * (0.5 / Dm)

        @pl.when(i == 0)
        def _():
            l_ref[...] = s

        @pl.when(i > 0)
        def _():
            l_ref[...] += s

    tile = pl.BlockSpec((TM, Dm), lambda i: (i, 0))
    return pl.pallas_call(
        body, name=name, grid=(Tp // TM,), in_specs=[tile, tile],
        out_specs=[pl.BlockSpec((1, Dm), lambda i: (0, 0)), tile],
        out_shape=[jax.ShapeDtypeStruct((1, Dm), F32), jax.ShapeDtypeStruct((Tp, Dm), F32)],
        compiler_params=_cp(("arbitrary",)),
    )(h, tgt)


def _row_block(R):
    return _pick(R, (256, 176, 128, 64, 32, 16, 8))


def sum_slots(parts, name):
    P, R, C = parts.shape
    rb = _row_block(R)

    def body(p_ref, o_ref):
        acc = p_ref[0].astype(F32)
        for s in range(1, P):
            acc = acc + p_ref[s].astype(F32)
        o_ref[...] = acc

    return pl.pallas_call(
        body, name=name, grid=(R // rb,), in_specs=[pl.BlockSpec((P, rb, C), lambda i: (0, i, 0))],
        out_specs=pl.BlockSpec((rb, C), lambda i: (i, 0)), out_shape=jax.ShapeDtypeStruct((R, C), F32),
        compiler_params=_cp(("parallel",)),
    )(parts)


def _adamw_math(w, g, m, v):
    m2 = ADAM_B1 * m + (1.0 - ADAM_B1) * g
    v2 = ADAM_B2 * v + (1.0 - ADAM_B2) * (g * g)
    m_hat = m2 / (1.0 - ADAM_B1 ** ADAM_STEP)
    v_hat = v2 / (1.0 - ADAM_B2 ** ADAM_STEP)
    delta = -ADAM_LR * (m_hat / (jnp.sqrt(v_hat) + ADAM_EPS) + ADAM_WD * w)
    return delta, m2, v2


def adamw_big(w, m, v, g_a, g_b, name):
    R, C = w.shape
    rb = _row_block(R)

    def body(w_ref, m_ref, v_ref, ga_ref, gb_ref, g_ref, d_ref, m2_ref, v2_ref):
        g = ga_ref[...] + gb_ref[...]
        delta, m2, v2 = _adamw_math(w_ref[...], g, m_ref[...], v_ref[...])
        g_ref[...], d_ref[...], m2_ref[...], v2_ref[...] = g, delta, m2, v2

    spec = pl.BlockSpec((rb, C), lambda i: (i, 0))
    return pl.pallas_call(
        body, name=name, grid=(R // rb,), in_specs=[spec] * 5, out_specs=[spec] * 4,
        out_shape=[jax.ShapeDtypeStruct((R, C), F32)] * 4, compiler_params=_cp(("parallel",)),
    )(w, m, v, g_a, g_b)


def adamw_small(ws, ms, vs, gs, name):
    n = len(ws)

    def body(*refs):
        w_r, m_r, v_r, g_r = refs[:n], refs[n:2 * n], refs[2 * n:3 * n], refs[3 * n:4 * n]
        d_o, m_o, v_o = refs[4 * n:5 * n], refs[5 * n:6 * n], refs[6 * n:7 * n]
        for k in range(n):
            delta, m2, v2 = _adamw_math(w_r[k][...], g_r[k][...], m_r[k][...], v_r[k][...])
            d_o[k][...], m_o[k][...], v_o[k][...] = delta, m2, v2

    shapes = [jax.ShapeDtypeStruct(w.shape, F32) for w in ws]
    res = pl.pallas_call(body, name=name, out_shape=shapes * 3, compiler_params=pltpu.CompilerParams(vmem_limit_bytes=VMEM_LIMIT))(
        *ws, *ms, *vs, *gs)
    return res[:n], res[n:2 * n], res[2 * n:]


def sum8(own4, sib4, name):
    _, R, C = own4.shape

    def body(a_ref, b_ref, o_ref):
        a = ((a_ref[0] + a_ref[1]) + a_ref[2]) + a_ref[3]
        b = ((b_ref[0] + b_ref[1]) + b_ref[2]) + b_ref[3]
        o_ref[...] = a + b

    return pl.pallas_call(body, name=name, out_shape=jax.ShapeDtypeStruct((R, C), F32),
                          compiler_params=pltpu.CompilerParams(vmem_limit_bytes=VMEM_LIMIT))(own4, sib4)


def _il(w):
    lead = w.shape[:-1]
    return w.reshape(lead + (2, FFN_DIM // 256, 256)).swapaxes(-3, -2).reshape(lead + (2 * FFN_DIM,))


def _unil(w):
    lead = w.shape[:-1]
    return w.reshape(lead + (FFN_DIM // 256, 2, 256)).swapaxes(-3, -2).reshape(lead + (2 * FFN_DIM,))


FFN_GRAN = 128


def ffn_up_relayout(src, to_slots, name):
    cs = 2 * FFN_DIM // 4
    per_slot, per_half = cs // FFN_GRAN, FFN_DIM // FFN_GRAN
    L = src.shape[0] if to_slots else src.shape[1] // D_MODEL
    n = L * 4 * per_slot

    def body(src_ref, dst_ref, sems):
        slots, il = (dst_ref, src_ref) if to_slots else (src_ref, dst_ref)
        copies = []
        for l in range(L):
            for g in range(4 * per_slot):
                half, hg = g // per_half, g % per_half
                at = (2 * (256 // FFN_GRAN) * (hg // (256 // FFN_GRAN)) + (256 // FFN_GRAN) * half + hg % (256 // FFN_GRAN)) * FFN_GRAN
                a = slots.at[g // per_slot, pl.ds(l * D_MODEL, D_MODEL), pl.ds((g % per_slot) * FFN_GRAN, FFN_GRAN)]
                b = il.at[l, :, pl.ds(at, FFN_GRAN)]
                copies.append(pltpu.make_async_copy(b, a, sems.at[len(copies)]) if to_slots else pltpu.make_async_copy(a, b, sems.at[len(copies)]))
        for cp in copies:
            cp.start()
        for cp in copies:
            cp.wait()

    shape = (4, L * D_MODEL, cs) if to_slots else (L, D_MODEL, 2 * FFN_DIM)
    return pl.pallas_call(body, name=name, in_specs=[_HBM], out_specs=_HBM, out_shape=jax.ShapeDtypeStruct(shape, src.dtype),
                          scratch_shapes=[pltpu.SemaphoreType.DMA((n,))])(src)


def _pad_cols(w, n=AUG):
    return jnp.pad(w, ((0, 0), (0, n - w.shape[1])))


ATT_SPLIT = ATT_HEADS // 2


def attn_fwd(a, a_t, W, tag, cargo=()):
    proj = matmul(a, W["main"], "nn", BF16, f"{tag}_proj", ncols=4 * D_MODEL)
    fl = matmul(a, W["small"], "nn", F32, f"{tag}_fl")
    c = forget_cumsum(fl, W["bias"], f"{tag}_cumsum")
    cfix = tok(c, fixed=True)
    q_aug = rowwise(fn_attn_prep_q, [tok(proj, AUG, 0), cfix], [par(W["qg"])], [(2 * AUG, BF16)], f"{tag}_prepq", ATT_SPLIT, True)[0]
    k_aug = rowwise(fn_attn_prep_k, [tok(proj, AUG, ATT_SPLIT), cfix], [par(W["kg"])], [(2 * AUG, BF16)], f"{tag}_prepk", ATT_SPLIT, True)[0]
    v_aug = rowwise(fn_attn_prep_v, [tok(proj, AUG, 2 * ATT_SPLIT)], [], [(2 * AUG, BF16)], f"{tag}_prepv", ATT_SPLIT)[0]
    o_aug, landed = flash_fwd(q_aug, k_aug, v_aug, f"{tag}_flash", cargo)
    go, go_t = rowwise(fn_attn_gate, [tok(o_aug, 2 * AUG, 0), tok(proj, AUG, 3 * ATT_SPLIT)], [], [(AUG, BF16), (AUG, BF16, 0)],
                       f"{tag}_gate", ATT_SPLIT)
    return (a_t, proj, fl, c, q_aug, k_aug, v_aug, o_aug, go_t, go), landed


def attn_out(saved, W, tag):
    return matmul(saved[-1], W["out"], "nn", BF16, f"{tag}_out")


def attn_bwd(dm, saved, W, tag, cargo=None):
    a_t, proj, fl, c, q_aug, k_aug, v_aug, o_aug, go_t, _ = saved
    dgo = matmul(dm, W["out"], "nt", BF16, f"{tag}_dgo")
    d_out = matmul(go_t, dm, "nn", BF16, f"{tag}_dwout")
    do_aug, dog, q_aug2 = rowwise(
        fn_attn_bwd_prep, [tok(dgo, AUG, 0), tok(o_aug, 2 * AUG, 0), tok(proj, AUG, 3 * ATT_SPLIT), tok(q_aug, 2 * AUG, 0)], [],
        [(2 * AUG, BF16), (AUG, BF16), (2 * AUG, BF16)], f"{tag}_bprep", ATT_SPLIT)
    (dq_aug, dk_aug, dv_aug), landed = flash_bwd(q_aug2, k_aug, v_aug, do_aug, f"{tag}_flashb", cargo(d_out) if cargo else ())
    dc = rowwise(fn_attn_dc, [tok(dq_aug), tok(dk_aug)], [], [(AUG, F32)], f"{tag}_dc")[0]
    cfix = tok(c, fixed=True)
    (dq,), (d_qg,) = rowwise_bwd(fn_attn_prep_q, [tok(proj, AUG, 0, grad=BF16), cfix], [par(W["qg"])], [dq_aug],
                                 f"{tag}_prepqb", nsplit=ATT_SPLIT, with_j=True)
    (dk,), (d_kg,) = rowwise_bwd(fn_attn_prep_k, [tok(proj, AUG, ATT_SPLIT, grad=BF16), cfix], [par(W["kg"])], [dk_aug],
                                 f"{tag}_prepkb", nsplit=ATT_SPLIT, with_j=True)
    (dv,), _ = rowwise_bwd(fn_attn_prep_v, [tok(proj, AUG, 2 * ATT_SPLIT, grad=BF16)], [], [dv_aug], f"{tag}_prepvb", nsplit=ATT_SPLIT)
    dfl, d_bias = forget_cumsum_bwd(dc, fl, W["bias"], f"{tag}_cumsumb")
    dproj = jnp.concatenate([dq, dk, dv, dog], axis=1)
    da = matmul(dproj, W["main"], "nt", BF16, f"{tag}_da", add=matmul(dfl, W["small"], "nt", F32, f"{tag}_da0"))
    d_in = jnp.concatenate([matmul(a_t, dproj, "nn", BF16, f"{tag}_dw"), matmul(a_t, dfl, "nn", BF16, f"{tag}_dw2")[:, :ATT_HEADS]], axis=1)
    return da, dict(w_in=d_in, w_out=d_out, b_forget=d_bias[0, :ATT_HEADS], q_norm=d_qg[0], k_norm=d_kg[0]), landed


DN_SPLIT = 3 * DN_HD // 256


def dn_fwd(a, a_t, W, tag):
    proj = matmul(a, W["main"], "nn", BF16, f"{tag}_proj", ncols=4 * D_MODEL)
    lg = matmul(a, W["small"], "nn", F32, f"{tag}_lg")
    qkv, y = rowwise(fn_dn_prep_keep, [tok(proj, 256, 0, conv=(W["conv"], 0))], [], [(256, BF16), (256, BF16)], f"{tag}_prep", DN_SPLIT, True)
    bg = rowwise(fn_dn_gates, [tok(lg)], [par(W["alog"]), par(W["dtb"])], [(AUG, F32)], f"{tag}_gates")[0]
    O, states, inverses = dn_scan_fwd(qkv, bg, f"{tag}_scan")
    go, go_t = rowwise(fn_dn_post, [tok(O), tok(proj, DN_HD, 3)], [par(W["ogain"])], [(DN_HD, BF16), (DN_HD, BF16, 0)], f"{tag}_post")
    m = matmul(go, W["out"], "nn", BF16, f"{tag}_out")
    return m, (a_t, proj, lg, qkv, y, bg, O, states, inverses, go_t)


def dn_bwd(dm, saved, W, tag):
    a_t, proj, lg, qkv, y, bg, O, states, inverses, go_t = saved
    dgo = matmul(dm, W["out"], "nt", BF16, f"{tag}_dgo")
    d_out = matmul(go_t, dm, "nn", BF16, f"{tag}_dwout")
    (dO, dog), (d_ogain,) = rowwise_bwd(fn_dn_post, [tok(O, grad=BF16), tok(proj, DN_HD, 3, grad=BF16)], [par(W["ogain"])],
                                        [dgo], f"{tag}_postb")
    dqkv_n, dbg = dn_scan_bwd(qkv, bg, states, inverses, dO, f"{tag}_scanb")
    (dy,), _ = rowwise_bwd(fn_dn_prep, [tok(y, 256, 0, grad=BF16)], [], [dqkv_n], f"{tag}_prepb", nsplit=DN_SPLIT, with_j=True)
    dqkv, d_conv = conv_bwd(dy, proj, W["conv"], f"{tag}_convb")
    (dlg,), (d_alog, d_dtb) = rowwise_bwd(fn_dn_gates, [tok(lg, grad=BF16)], [par(W["alog"]), par(W["dtb"])], [dbg], f"{tag}_gatesb")
    dproj = jnp.concatenate([dqkv, dog], axis=1)
    da = matmul(dproj, W["main"], "nt", BF16, f"{tag}_da", add=matmul(dlg, W["small"], "nt", F32, f"{tag}_da0"))
    d_in = jnp.concatenate([matmul(a_t, dproj, "nn", BF16, f"{tag}_dw"), matmul(a_t, dlg, "nn", BF16, f"{tag}_dw2")[:, :2 * DN_HEADS]], axis=1)
    return da, dict(w_in=d_in, w_out=d_out, conv=d_conv, a_log=d_alog[0, DN_HEADS:2 * DN_HEADS],
                    dt_bias=d_dtb[0, DN_HEADS:2 * DN_HEADS], o_norm=d_ogain[0])


FFN_SPLIT = FFN_DIM // 256


def ffn_fwd(b, b_t, W, tag):
    u0 = matmul(b, W["up"], "nn", BF16, f"{tag}_up")
    act, u, act_t = rowwise(fn_geglu_keep, [tok(u0, 512, 0, conv=(W["conv"], 0))], [], [(256, BF16), (512, BF16), (256, BF16, 0)],
                            f"{tag}_act", nsplit=FFN_SPLIT)
    f = matmul(act, W["down"], "nn", BF16, f"{tag}_down")
    return f, (b_t, u0, u, act_t)


def ffn_bwd(df, saved, W, tag):
    b_t, u0, u, act_t = saved
    dact = matmul(df, W["down"], "nt", BF16, f"{tag}_dact")
    d_down = matmul(act_t, df, "nn", BF16, f"{tag}_dwdown")
    (du,), _ = rowwise_bwd(fn_geglu, [tok(u, 512, 0, grad=BF16)], [], [dact], f"{tag}_actb", nsplit=FFN_SPLIT)
    du0, d_conv = conv_bwd(du, u0, W["conv"], f"{tag}_convb")
    db = matmul(du0, W["up"], "nt", BF16, f"{tag}_db")
    d_up = matmul(b_t, du0, "nn", BF16, f"{tag}_dwup")
    return db, dict(up=d_up, conv=_unil(d_conv), down=d_down)


def _cols_from_slots(g, L):
    K, Cs = g.shape[1] // L, g.shape[2]
    return g.reshape(4, L, K, Cs).transpose(1, 2, 0, 3).reshape(L, K, 4 * Cs)


def _rows_from_slots(g, L):
    Rs, C = g.shape[1] // L, g.shape[2]
    return g.reshape(4, L, Rs, C).transpose(1, 0, 2, 3).reshape(L, 4 * Rs, C)


def _cols_to_slots(w):
    L, K, C = w.shape
    return w.reshape(L, K, 4, C // 4).transpose(2, 0, 1, 3).reshape(4, L * K, C // 4)


def _rows_to_slots(w):
    L, R, C = w.shape
    return w.reshape(L, 4, R // 4, C).transpose(1, 0, 2, 3).reshape(4, L * (R // 4), C)


SMALL_NAMES = ["meta_tokens", "norm_mix_pre", "norm_mix_post", "norm_ffn_pre", "norm_ffn_post", "attn_b_forget", "attn_q_norm",
               "attn_k_norm", "dn_conv", "dn_a_log", "dn_dt_bias", "dn_o_norm", "ffn_conv"]
BIG_NAMES = ["attn_w_in", "attn_w_out", "dn_w_in", "dn_w_out", "ffn_w_up", "ffn_w_down"]
WEIGHT_NAMES = ["meta_tokens", "norm_mix_pre", "norm_mix_post", "norm_ffn_pre", "norm_ffn_post", "attn_w_in", "attn_b_forget",
                "attn_q_norm", "attn_k_norm", "attn_w_out", "dn_w_in", "dn_conv", "dn_a_log", "dn_dt_bias", "dn_o_norm", "dn_w_out",
                "ffn_w_up", "ffn_conv", "ffn_w_down"]
PACK_COLS = 1024


def train_step(x, loss_target, w, m, v):
    L = x.shape[1]
    T = L + N_META
    Tp = -(-T // TM) * TM
    xi, yi = lax.axis_index("x"), lax.axis_index("y")
    slot = 2 * xi + yi
    as2d = lambda t: t.reshape(-1, t.shape[-1])

    bf2d = lambda n: as2d(w[n]).astype(BF16)
    attn_in_shard = w["attn_w_in"].astype(BF16)
    first = exchange_chips(False, [attn_in_shard[0], w["meta_tokens"], as2d(w["dn_conv"]), as2d(w["ffn_conv"])], "gather_first")
    later = [attn_in_shard[1], bf2d("attn_w_out"), bf2d("dn_w_in"), bf2d("dn_w_out"), bf2d("ffn_w_up"), bf2d("ffn_w_down")]
    meta = first[1].transpose(1, 0, 2).reshape(N_META, D_MODEL)
    dn_conv = _cols_from_slots(first[2], 2)
    ffn_conv = _il(_cols_from_slots(first[3], DEPTH))
    lane8 = lambda t: jnp.pad(t[None, :], ((0, 0), (DN_HEADS, AUG - 2 * DN_HEADS)))

    def attn_weights(j, w_in, w_out):
        return dict(main=(w_in, 0), small=_pad_cols(w_in[0, :, 4 * ATT_HD:]), bias=_pad_cols(w["attn_b_forget"][j][None, :]),
                    qg=w["attn_q_norm"][j][None, :], kg=w["attn_k_norm"][j][None, :], out=w_out)

    WA = [attn_weights(0, _cols_from_slots(first[0], 1), None), None]
    WD, WF = None, None
    row = lambda t, i: t[i][None, :]

    h = jnp.concatenate([meta, x[0], jnp.zeros((Tp - T, D_MODEL), F32)], axis=0)
    tgt = jnp.concatenate([jnp.zeros((N_META, D_MODEL), F32), loss_target[0], jnp.zeros((Tp - T, D_MODEL), F32)], axis=0)
    h0 = h
    a, a_t = rowwise(fn_norm, [tok(h)], [par(row(w["norm_mix_pre"], 0))], [(D_MODEL, BF16), (D_MODEL, BF16, 0)], "norm0")
    saved = []
    for i in range(DEPTH):
        j = i // 2
        if i == 0:
            sv_mix, landed = attn_fwd(a, a_t, WA[0], "attn0", later)
            attn_out_w, dn_in, dn_out = _rows_from_slots(landed[1], 2), _cols_from_slots(landed[2], 2), _rows_from_slots(landed[3], 2)
            ffn_up, ffn_down = ffn_up_relayout(landed[4], False, "ffn_up_layout"), _rows_from_slots(landed[5], DEPTH)
            WA[0]["out"] = (attn_out_w, 0)
            WA[1] = attn_weights(1, _cols_from_slots(landed[0], 1), (attn_out_w, 1))
            WD = [dict(main=(dn_in, k), small=_pad_cols(dn_in[k, :, 4 * DN_HD:]), conv=dn_conv[k], alog=lane8(w["dn_a_log"][k]),
                       dtb=lane8(w["dn_dt_bias"][k]), ogain=w["dn_o_norm"][k][None, :], out=(dn_out, k)) for k in range(2)]
            WF = [dict(up=(ffn_up, k), conv=ffn_conv[k], down=(ffn_down, k)) for k in range(DEPTH)]
            mix = attn_out(sv_mix, WA[0], "attn0")
        elif i % 2 == 0:
            sv_mix, _ = attn_fwd(a, a_t, WA[j], f"attn{j}")
            mix = attn_out(sv_mix, WA[j], f"attn{j}")
        else:
            mix, sv_mix = dn_fwd(a, a_t, WD[j], f"dn{j}")
        norm_outs = [(D_MODEL, F32), (D_MODEL, BF16), (D_MODEL, BF16, 1)]
        h_mid, b, b_t = rowwise(fn_resid_norm, [tok(h), tok(mix)], [par(row(w["norm_mix_post"], i)), par(row(w["norm_ffn_pre"], i))],
                                norm_outs, f"resid_mix{i}")
        f, sv_ffn = ffn_fwd(b, b_t, WF[i], f"ffn{i}")
        if i < DEPTH - 1:
            h_out, a, a_t = rowwise(fn_resid_norm, [tok(h_mid), tok(f)], [par(row(w["norm_ffn_post"], i)), par(row(w["norm_mix_pre"], i + 1))],
                                    norm_outs, f"resid_ffn{i}")
        else:
            h_out = rowwise(fn_resid, [tok(h_mid), tok(f)], [par(row(w["norm_ffn_post"], i))], [(D_MODEL, F32)], f"resid_ffn{i}")[0]
        saved.append((h, mix, sv_mix, h_mid, f, sv_ffn))
        h = h_out
    lvec, dh = loss_head(h, tgt, L, "loss_head")
    loss = lax.psum(jnp.sum(lvec), ("x", "y", "c"))

    gn = {n: [None] * DEPTH for n in ("norm_mix_pre", "norm_mix_post", "norm_ffn_pre", "norm_ffn_post")}
    g_attn, g_dn, g_ffn = [None, None], [None, None], [None] * DEPTH
    da = None
    for i in reversed(range(DEPTH)):
        j = i // 2
        h_in, mix, sv_mix, h_mid, f, sv_ffn = saved[i]
        if i == DEPTH - 1:
            (dh, df), (gn["norm_ffn_post"][i],) = rowwise_bwd(
                fn_resid, [tok(h_mid, grad=F32), tok(f, grad=BF16)], [par(row(w["norm_ffn_post"], i))], [dh], f"resid_ffn{i}_b")
        else:
            (dh, df), (gn["norm_ffn_post"][i], gn["norm_mix_pre"][i + 1]) = rowwise_bwd(
                fn_resid_norm, [tok(h_mid, grad=F32), tok(f, grad=BF16)],
                [par(row(w["norm_ffn_post"], i)), par(row(w["norm_mix_pre"], i + 1))], [dh, da], f"resid_ffn{i}_b")
        db, g_ffn[i] = ffn_bwd(df, sv_ffn, WF[i], f"ffn{i}")
        (dh, dm), (gn["norm_mix_post"][i], gn["norm_ffn_pre"][i]) = rowwise_bwd(
            fn_resid_norm, [tok(h_in, grad=F32), tok(mix, grad=BF16)],
            [par(row(w["norm_mix_post"], i)), par(row(w["norm_ffn_pre"], i))], [dh, db], f"resid_mix{i}_b")
        if i == 0:
            def riders(d_out0):
                return [_cols_to_slots(g_attn[1]["w_in"][None]), _rows_to_slots(jnp.stack([d_out0, g_attn[1]["w_out"]])),
                        _cols_to_slots(jnp.stack([g["w_in"] for g in g_dn])), _rows_to_slots(jnp.stack([g["w_out"] for g in g_dn])),
                        ffn_up_relayout(jnp.stack([g["up"] for g in g_ffn]), True, "ffn_dup_layout"),
                        _rows_to_slots(jnp.stack([g["down"] for g in g_ffn]))]
            da, g_attn[0], landed_grads = attn_bwd(dm, sv_mix, WA[0], "attn0", riders)
        elif i % 2 == 0:
            da, g_attn[j], _ = attn_bwd(dm, sv_mix, WA[j], f"attn{j}")
        else:
            da, g_dn[j] = dn_bwd(dm, sv_mix, WD[j], f"dn{j}")
    (dh0,), (gn["norm_mix_pre"][0],) = rowwise_bwd(fn_id_norm, [tok(h0, grad=F32)], [par(row(w["norm_mix_pre"], 0))], [dh, da], "norm0_b")
    grad_x = dh0[N_META:T][None]

    full = dict(
        meta_tokens=dh0[:N_META],
        **{n: jnp.concatenate(gn[n], axis=0) for n in gn},
        attn_b_forget=jnp.stack([g["b_forget"] for g in g_attn]), attn_q_norm=jnp.stack([g["q_norm"] for g in g_attn]),
        attn_k_norm=jnp.stack([g["k_norm"] for g in g_attn]),
        dn_conv=jnp.stack([g["conv"] for g in g_dn]), dn_a_log=jnp.stack([g["a_log"] for g in g_dn]),
        dn_dt_bias=jnp.stack([g["dt_bias"] for g in g_dn]), dn_o_norm=jnp.stack([g["o_norm"] for g in g_dn]),
        ffn_conv=jnp.stack([g["conv"] for g in g_ffn]))
    flat = jnp.concatenate([full[n].reshape(-1) for n in SMALL_NAMES])
    rows = -(-flat.shape[0] // PACK_COLS)
    rows = -(-rows // 8) * 8
    pack = jnp.pad(flat, (0, rows * PACK_COLS - flat.shape[0])).reshape(rows, PACK_COLS)
    own4 = exchange_chips(False, [pack], "gather_small_grads")[0]
    sib4 = swap_cores([own4], "swap_small_grads")[0]
    tot = sum8(own4, sib4, "sum_small_grads").reshape(-1)
    grads, off = {}, 0
    for n in SMALL_NAMES:
        size = full[n].size
        g = tot[off:off + size].reshape(full[n].shape)
        off += size
        if g.shape != w[n].shape:
            g = lax.dynamic_slice_in_dim(g, slot * w[n].shape[-1], w[n].shape[-1], axis=g.ndim - 1)
        grads[n] = g
    d_s, m_s, v_s = adamw_small([as2d(w[n]) for n in SMALL_NAMES], [as2d(m[n]) for n in SMALL_NAMES],
                                [as2d(v[n]) for n in SMALL_NAMES], [as2d(grads[n]) for n in SMALL_NAMES], "adamw_small")
    deltas = {n: d.reshape(w[n].shape) for n, d in zip(SMALL_NAMES, d_s, strict=True)}
    new_m = {n: d.reshape(w[n].shape) for n, d in zip(SMALL_NAMES, m_s, strict=True)}
    new_v = {n: d.reshape(w[n].shape) for n, d in zip(SMALL_NAMES, v_s, strict=True)}

    last = exchange_chips(True, [_cols_to_slots(g_attn[0]["w_in"][None])], "scatter_last")[0]
    recv = [jnp.concatenate([last, landed_grads[0]], axis=1)] + list(landed_grads[1:])
    part = [sum_slots(r, f"sum_{n}") for n, r in zip(BIG_NAMES, recv, strict=True)]
    other = swap_cores(part, "swap_grads")
    for n, pa, pb in zip(BIG_NAMES, part, other, strict=True):
        g, d, m2, v2 = adamw_big(as2d(w[n]), as2d(m[n]), as2d(v[n]), pa, pb, f"adamw_{n}")
        grads[n], deltas[n], new_m[n], new_v[n] = (t.reshape(w[n].shape) for t in (g, d, m2, v2))
    return loss, grad_x, grads, deltas, new_m, new_v


def kernel(x, meta_tokens, norm_mix_pre, norm_mix_post, norm_ffn_pre, norm_ffn_post, attn_w_in, attn_b_forget, attn_q_norm, attn_k_norm, attn_w_out, dn_w_in, dn_conv, dn_a_log, dn_dt_bias, dn_o_norm, dn_w_out, ffn_w_up, ffn_conv, ffn_w_down, loss_target, m_meta_tokens, m_norm_mix_pre, m_norm_mix_post, m_norm_ffn_pre, m_norm_ffn_post, m_attn_w_in, m_attn_b_forget, m_attn_q_norm, m_attn_k_norm, m_attn_w_out, m_dn_w_in, m_dn_conv, m_dn_a_log, m_dn_dt_bias, m_dn_o_norm, m_dn_w_out, m_ffn_w_up, m_ffn_conv, m_ffn_w_down, v_meta_tokens, v_norm_mix_pre, v_norm_mix_post, v_norm_ffn_pre, v_norm_ffn_post, v_attn_w_in, v_attn_b_forget, v_attn_q_norm, v_attn_k_norm, v_attn_w_out, v_dn_w_in, v_dn_conv, v_dn_a_log, v_dn_dt_bias, v_dn_o_norm, v_dn_w_out, v_ffn_w_up, v_ffn_conv, v_ffn_w_down):
    w = dict(meta_tokens=meta_tokens, norm_mix_pre=norm_mix_pre, norm_mix_post=norm_mix_post, norm_ffn_pre=norm_ffn_pre, norm_ffn_post=norm_ffn_post, attn_w_in=attn_w_in, attn_b_forget=attn_b_forget, attn_q_norm=attn_q_norm, attn_k_norm=attn_k_norm, attn_w_out=attn_w_out, dn_w_in=dn_w_in, dn_conv=dn_conv, dn_a_log=dn_a_log, dn_dt_bias=dn_dt_bias, dn_o_norm=dn_o_norm, dn_w_out=dn_w_out, ffn_w_up=ffn_w_up, ffn_conv=ffn_conv, ffn_w_down=ffn_w_down)
    m = dict(meta_tokens=m_meta_tokens, norm_mix_pre=m_norm_mix_pre, norm_mix_post=m_norm_mix_post, norm_ffn_pre=m_norm_ffn_pre, norm_ffn_post=m_norm_ffn_post, attn_w_in=m_attn_w_in, attn_b_forget=m_attn_b_forget, attn_q_norm=m_attn_q_norm, attn_k_norm=m_attn_k_norm, attn_w_out=m_attn_w_out, dn_w_in=m_dn_w_in, dn_conv=m_dn_conv, dn_a_log=m_dn_a_log, dn_dt_bias=m_dn_dt_bias, dn_o_norm=m_dn_o_norm, dn_w_out=m_dn_w_out, ffn_w_up=m_ffn_w_up, ffn_conv=m_ffn_conv, ffn_w_down=m_ffn_w_down)
    v = dict(meta_tokens=v_meta_tokens, norm_mix_pre=v_norm_mix_pre, norm_mix_post=v_norm_mix_post, norm_ffn_pre=v_norm_ffn_pre, norm_ffn_post=v_norm_ffn_post, attn_w_in=v_attn_w_in, attn_b_forget=v_attn_b_forget, attn_q_norm=v_attn_q_norm, attn_k_norm=v_attn_k_norm, attn_w_out=v_attn_w_out, dn_w_in=v_dn_w_in, dn_conv=v_dn_conv, dn_a_log=v_dn_a_log, dn_dt_bias=v_dn_dt_bias, dn_o_norm=v_dn_o_norm, dn_w_out=v_dn_w_out, ffn_w_up=v_ffn_w_up, ffn_conv=v_ffn_conv, ffn_w_down=v_ffn_w_down)
    loss, grad_x, grads, deltas, new_m, new_v = train_step(x, loss_target, w, m, v)
    return (loss, grad_x, *[grads[n] for n in WEIGHT_NAMES], *[deltas[n] for n in WEIGHT_NAMES],
            *[new_m[n] for n in WEIGHT_NAMES], *[new_v[n] for n in WEIGHT_NAMES])
```

```python
import functools

import jax
import jax.numpy as jnp
from jax import lax
from jax.experimental import pallas as pl
from jax.experimental.pallas import tpu as pltpu

F32, BF16 = jnp.float32, jnp.bfloat16

D_MODEL = 1024
N_META = 16
ATT_HEADS, ATT_DH = 16, 64
ATT_HD = ATT_HEADS * ATT_DH
AUG = 128
ATT_AUG = ATT_HEADS * AUG
DN_HEADS, DN_DH = 8, 128
DN_HD = DN_HEADS * DN_DH
DN_CHUNK = 64
FFN_DIM = 2816
DEPTH = 4
EPS = 1e-6
NEG = -1e30

ADAM_LR, ADAM_B1, ADAM_B2, ADAM_EPS, ADAM_WD, ADAM_STEP = 0.001, 0.9, 0.999, 1e-08, 0.01, 10

TM = 640
HALO = 8
VMEM_LIMIT = 52 * 1024 * 1024
MESH_ID = pl.DeviceIdType.MESH


def _cp(sem):
    return pltpu.CompilerParams(dimension_semantics=sem, vmem_limit_bytes=VMEM_LIMIT)


def _pick(n, cands):
    for c in cands:
        if n % c == 0:
            return c
    return n


def matmul(a, b, mode, out_dtype, name, add=None, ncols=None):
    layer = None
    if isinstance(b, tuple):
        b, layer = b
    bshape = b.shape[-2:]
    if mode == "nn":
        (M, K), N = a.shape, ncols or bshape[1]
    elif mode == "nt":
        (M, K), N = a.shape, bshape[0]
    else:
        (K, M), N = a.shape, bshape[1]
    tm = _pick(M, (1664, TM, 1024, 1408, 512, 256, 128))
    tn = _pick(N, (1024, 1408, 512, 256, 128))
    tk = _pick(K, (1664, TM, 1024, 1408, 512, 256, 128))
    nk = K // tk
    if mode == "nn":
        a_spec = pl.BlockSpec((tm, tk), lambda i, j, k: (i, k))
        b_spec = pl.BlockSpec((tk, tn), lambda i, j, k: (k, j))
        dims = (((1,), (0,)), ((), ()))
    elif mode == "nt":
        a_spec = pl.BlockSpec((tm, tk), lambda i, j, k: (i, k))
        b_spec = pl.BlockSpec((tn, tk), lambda i, j, k: (j, k))
        dims = (((1,), (1,)), ((), ()))
    else:
        a_spec = pl.BlockSpec((tk, tm), lambda i, j, k: (k, i))
        b_spec = pl.BlockSpec((tk, tn), lambda i, j, k: (k, j))
        dims = (((0,), (0,)), ((), ()))
    if layer is not None:
        b_spec = pl.BlockSpec((None,) + b_spec.block_shape, functools.partial(lambda i, j, k, f: (layer,) + f(i, j, k), f=b_spec.index_map))
    o_spec = pl.BlockSpec((tm, tn), lambda i, j, k: (i, j))
    has_add = add is not None

    def body(*refs):
        if has_add:
            a_ref, b_ref, add_ref, o_ref, acc_ref = refs
        else:
            a_ref, b_ref, o_ref, acc_ref = refs
        k = pl.program_id(2)
        part = lax.dot_general(a_ref[...], b_ref[...], dims, preferred_element_type=F32)

        @pl.when(k == 0)
        def _():
            acc_ref[...] = part

        @pl.when(k > 0)
        def _():
            acc_ref[...] += part

        @pl.when(k == nk - 1)
        def _():
            r = acc_ref[...]
            if has_add:
                r = r + add_ref[...].astype(F32)
            o_ref[...] = r.astype(out_dtype)

    in_specs = [a_spec, b_spec] + ([o_spec] if has_add else [])
    args = (a, b) + ((add,) if has_add else ())
    return pl.pallas_call(
        body, name=name, grid=(M // tm, N // tn, nk), in_specs=in_specs, out_specs=o_spec,
        out_shape=jax.ShapeDtypeStruct((M, N), out_dtype),
        scratch_shapes=[pltpu.VMEM((tm, tn), F32)],
        compiler_params=_cp(("parallel", "parallel", "arbitrary")),
    )(*args)


def tok(arr, width=None, col=0, conv=None, grad=None, fixed=False):
    return dict(arr=arr, w=arr.shape[1] if width is None else width, col=col, conv=conv, grad=grad, step=0 if fixed else 1)


def par(arr, width=None, col=0):
    return dict(arr=arr, w=arr.shape[1] if width is None else width, col=col)


def _conv_apply(x, halo, w, ext_ref):
    K = w.shape[0]
    rows = x.shape[0]
    ext_ref[0:HALO, :] = halo
    ext_ref[HALO:, :] = x
    y = w[K - 1:K, :] * x
    for k in range(K - 1):
        y = y + w[k:k + 1, :] * ext_ref[pl.ds(HALO - (K - 1) + k, rows), :]
    return y


def _row_specs(toks, pars):
    specs, args = [], []
    for t in toks:
        specs.append(pl.BlockSpec((TM, t["w"]), functools.partial(lambda i, j, c, st: (i, c + st * j), c=t["col"], st=t["step"])))
        args.append(t["arr"])
        if t["conv"] is not None:
            specs.append(pl.BlockSpec((HALO, t["w"]), functools.partial(
                lambda i, j, c: (jnp.maximum(i * (TM // HALO) - 1, 0), c + j), c=t["col"])))
            args.append(t["arr"])
            cw, ccol = t["conv"]
            specs.append(pl.BlockSpec((cw.shape[0], t["w"]), functools.partial(lambda i, j, c: (0, c + j), c=ccol)))
            args.append(cw)
    for p in pars:
        specs.append(pl.BlockSpec((p["arr"].shape[0], p["w"]), functools.partial(lambda i, j, c: (0, c), c=p["col"])))
        args.append(p["arr"])
    return specs, args


def _row_load(toks, pars, refs, ext_refs):
    i = pl.program_id(0)
    vals, n, e = [], 0, 0
    for t in toks:
        x = refs[n][...].astype(F32)
        n += 1
        if t["conv"] is not None:
            halo = jnp.where(i == 0, 0.0, refs[n][...].astype(F32))
            w = refs[n + 1][...]
            n += 2
            x = _conv_apply(x, halo, w, ext_refs[e])
            e += 1
        vals.append(x)
    for _ in pars:
        vals.append(refs[n][...])
        n += 1
    return vals, n


def rowwise(fn, toks, pars, outs, name, nsplit=1, with_j=False):
    Tp = toks[0]["arr"].shape[0]
    specs, args = _row_specs(toks, pars)
    n_ext = sum(t["conv"] is not None for t in toks)

    def body(*refs):
        ext_refs = refs[len(refs) - n_ext:]
        vals, n = _row_load(toks, pars, refs, ext_refs)
        res = fn(pl.program_id(1), *vals) if with_j else fn(*vals)
        plain = 0
        for o, o_ref in zip(outs, refs[n:n + len(outs)], strict=True):
            if len(o) == 2:
                o_ref[...] = res[plain].astype(o_ref.dtype)
                plain += 1
            else:
                o_ref[...] = res[o[2]].T.astype(o_ref.dtype)

    return pl.pallas_call(
        body, name=name, grid=(Tp // TM, nsplit), in_specs=specs,
        out_specs=[pl.BlockSpec((TM, o[0]), lambda i, j: (i, j)) if len(o) == 2 else pl.BlockSpec((o[0], TM), lambda i, j: (j, i))
                   for o in outs],
        out_shape=[jax.ShapeDtypeStruct((Tp, o[0] * nsplit) if len(o) == 2 else (o[0] * nsplit, Tp), o[1]) for o in outs],
        scratch_shapes=[pltpu.VMEM((TM + HALO, t["w"]), F32) for t in toks if t["conv"] is not None],
        compiler_params=_cp(("parallel", "parallel")),
    )(*args)


def rowwise_bwd(fn, toks, pars, cts, name, groups=None, par_grads=None, nsplit=1, with_j=False):
    Tp = toks[0]["arr"].shape[0]
    specs, args = _row_specs(toks, pars)
    n_ext = sum(t["conv"] is not None for t in toks)
    gidx = [k for k, t in enumerate(toks) if t["grad"] is not None]
    if groups is None:
        groups = [[k] for k in range(len(gidx))]
    par_grads = list(range(len(pars))) if par_grads is None else par_grads
    ct_specs, ct_args = [], []
    for c in cts:
        if c is not None:
            ct_specs.append(pl.BlockSpec((TM, c.shape[1] // nsplit), lambda i, j: (i, j)))
            ct_args.append(c)
    out_shapes, out_specs = [], []
    for g in groups:
        w = sum(toks[gidx[k]]["w"] for k in g)
        out_shapes.append(jax.ShapeDtypeStruct((Tp, w * nsplit), toks[gidx[g[0]]]["grad"]))
        out_specs.append(pl.BlockSpec((TM, w), lambda i, j: (i, j)))
    for k in par_grads:
        shp = (pars[k]["arr"].shape[0], pars[k]["w"])
        out_shapes.append(jax.ShapeDtypeStruct(shp, F32))
        out_specs.append(pl.BlockSpec(shp, lambda i, j: (0, 0)))
    n_ct = len(ct_args)

    def body(*refs):
        first = (pl.program_id(0) == 0) & (pl.program_id(1) == 0)
        ext_refs = refs[len(refs) - n_ext:]
        vals, n = _row_load(toks, pars, refs, ext_refs)
        ct_refs = refs[n:n + n_ct]
        o_refs = refs[n + n_ct:len(refs) - n_ext]
        res, vjp = jax.vjp(functools.partial(fn, pl.program_id(1)) if with_j else fn, *vals)
        ct_vals, c = [], 0
        for r, ct in zip(res, cts, strict=True):
            if ct is None:
                ct_vals.append(jnp.zeros_like(r))
            else:
                ct_vals.append(ct_refs[c][...].astype(F32))
                c += 1
        grads = vjp(tuple(ct_vals))
        for g, o_ref in zip(groups, o_refs[:len(groups)], strict=True):
            pieces = [grads[gidx[k]] for k in g]
            val = pieces[0] if len(pieces) == 1 else jnp.concatenate(pieces, axis=1)
            o_ref[...] = val.astype(o_ref.dtype)
        for k, o_ref in zip(par_grads, o_refs[len(groups):], strict=True):
            gk = grads[len(toks) + k]

            @pl.when(first)
            def _(o_ref=o_ref, gk=gk):
                o_ref[...] = gk

            @pl.when(jnp.logical_not(first))
            def _(o_ref=o_ref, gk=gk):
                o_ref[...] += gk

    res = pl.pallas_call(
        body, name=name, grid=(Tp // TM, nsplit), in_specs=specs + ct_specs, out_specs=out_specs, out_shape=out_shapes,
        scratch_shapes=[pltpu.VMEM((TM + HALO, t["w"]), F32) for t in toks if t["conv"] is not None],
        compiler_params=_cp(("arbitrary", "arbitrary")),
    )(*args, *ct_args)
    return res[:len(groups)], res[len(groups):]


ROWS = 16


def _shift_rows(win, s, lo, rows):
    return pltpu.roll(win, (-s) % win.shape[0], 0)[lo:lo + rows, :] if s else win[lo:lo + rows, :]


def conv_bwd(dy, x, w, name):
    Tp, W = dy.shape
    K = w.shape[0]
    wb = _pick(W, (512, 256, 128))
    nt = Tp // TM

    def body(dy_ref, dyn_ref, x_ref, w_ref, dx_ref, dw_ref, ext_dy):
        i = pl.program_id(1)
        ext_dy[0:TM, :] = dy_ref[...].astype(F32)
        ext_dy[TM:, :] = jnp.where(i == nt - 1, 0.0, dyn_ref[...].astype(F32))
        wv = w_ref[...]

        def chunk(t, sums):
            r0 = pl.multiple_of(t * ROWS, ROWS)
            dyw = ext_dy[pl.ds(r0, ROWS + HALO), :]
            xv = x_ref[pl.ds(r0, ROWS), :].astype(F32)
            dx, new = None, []
            for k in range(K):
                dys = _shift_rows(dyw, K - 1 - k, 0, ROWS)
                term = wv[k:k + 1, :] * dys
                dx = term if dx is None else dx + term
                prod = dys * xv
                new.append(sums[k] + (prod[0:HALO, :] + prod[HALO:ROWS, :]))
            dx_ref[pl.ds(r0, ROWS), :] = dx.astype(dx_ref.dtype)
            return tuple(new)

        sums = lax.fori_loop(0, TM // ROWS, chunk, tuple(jnp.zeros((HALO, wb), F32) for _ in range(K)))
        dwv = jnp.concatenate([jnp.sum(sm, axis=0, keepdims=True) for sm in sums], axis=0)

        @pl.when(i == 0)
        def _():
            dw_ref[...] = dwv

        @pl.when(i > 0)
        def _():
            dw_ref[...] += dwv

    return pl.pallas_call(
        body, name=name, grid=(W // wb, nt),
        in_specs=[
            pl.BlockSpec((TM, wb), lambda j, i: (i, j)),
            pl.BlockSpec((HALO, wb), lambda j, i: (jnp.minimum((i + 1) * (TM // HALO), Tp // HALO - 1), j)),
            pl.BlockSpec((TM, wb), lambda j, i: (i, j)),
            pl.BlockSpec((K, wb), lambda j, i: (0, j)),
        ],
        out_specs=[pl.BlockSpec((TM, wb), lambda j, i: (i, j)), pl.BlockSpec((K, wb), lambda j, i: (0, j))],
        out_shape=[jax.ShapeDtypeStruct((Tp, W), BF16), jax.ShapeDtypeStruct((K, W), F32)],
        scratch_shapes=[pltpu.VMEM((TM + HALO, wb), F32)],
        compiler_params=_cp(("parallel", "arbitrary")),
    )(dy, dy, x, w)


def _rms(x, g):
    return x * lax.rsqrt(jnp.mean(x * x, axis=-1, keepdims=True) + EPS) * g


def fn_norm(h, g):
    return (_rms(h, g),)


def fn_id_norm(h, g):
    return h, _rms(h, g)


def fn_resid_norm(h, m, g_post, g_next):
    h2 = h + _rms(m, g_post)
    return h2, _rms(h2, g_next)


def fn_resid(h, m, g_post):
    return (h + _rms(m, g_post),)


def fn_geglu(y):
    w = y.shape[1] // 2
    return (jax.nn.gelu(y[:, :w], approximate=True) * y[:, w:],)


def fn_geglu_keep(y):
    return fn_geglu(y) + (y,)


def _split3(c):
    hi = c.astype(BF16).astype(F32)
    r = c - hi
    mid = r.astype(BF16).astype(F32)
    lo = (r - mid).astype(BF16).astype(F32)
    return hi, mid, lo


def _aug_cols(rows, entries):
    lane = lax.broadcasted_iota(jnp.int32, (rows, ATT_DH), 1)
    out = jnp.zeros((rows, ATT_DH), F32)
    for ln, val in entries:
        out = jnp.where(lane == ln, val, out)
    return out


def _head_col(c, idx):
    lane = lax.broadcasted_iota(jnp.int32, c.shape, 1)
    return jnp.sum(jnp.where(lane == idx, c, 0.0), axis=1, keepdims=True)


def fn_attn_prep_q(j, xp, c, gain):
    rows = xp.shape[0]
    out = []
    for hh in range(2):
        hi, mid, lo = _split3(lax.stop_gradient(_head_col(c, 2 * j + hh)))
        out += [_rms(xp[:, ATT_DH * hh:ATT_DH * (hh + 1)], gain) * (ATT_DH ** -0.5),
                _aug_cols(rows, [(0, hi), (1, mid), (2, lo), (3, 1.0), (4, 1.0), (5, 1.0)])]
    return (jnp.concatenate(out, axis=1),)


def fn_attn_prep_k(j, xp, c, gain):
    rows = xp.shape[0]
    out = []
    for hh in range(2):
        hi, mid, lo = _split3(lax.stop_gradient(_head_col(c, 2 * j + hh)))
        out += [_rms(xp[:, ATT_DH * hh:ATT_DH * (hh + 1)], gain),
                _aug_cols(rows, [(0, 1.0), (1, 1.0), (2, 1.0), (3, -hi), (4, -mid), (5, -lo), (6, 1.0), (7, 1.0), (8, 1.0)])]
    return (jnp.concatenate(out, axis=1),)


def fn_attn_prep_v(xp):
    rows = xp.shape[0]
    out = []
    for hh in range(2):
        out += [xp[:, ATT_DH * hh:ATT_DH * (hh + 1)], _aug_cols(rows, [(0, 1.0), (1, 1.0), (2, 1.0)])]
    return (jnp.concatenate(out, axis=1),)


def fn_attn_gate(o_aug, og):
    outs = []
    for h in range(og.shape[1] // ATT_DH):
        o = o_aug[:, AUG * h:AUG * h + ATT_DH]
        outs.append(o * jax.nn.sigmoid(og[:, ATT_DH * h:ATT_DH * (h + 1)]))
    return (jnp.concatenate(outs, axis=1),)


def fn_attn_bwd_prep(dgo, o_aug, og, q_aug):
    rows = dgo.shape[0]
    lane = lax.broadcasted_iota(jnp.int32, (rows, ATT_DH), 1)
    dos, dogs, qs = [], [], []
    for h in range(og.shape[1] // ATT_DH):
        sl = slice(ATT_DH * h, ATT_DH * (h + 1))
        o = o_aug[:, AUG * h:AUG * h + ATT_DH]
        lse = o_aug[:, AUG * h + ATT_DH:AUG * h + ATT_DH + 1]
        sig = jax.nn.sigmoid(og[:, sl])
        do = dgo[:, sl] * sig
        dogs.append(dgo[:, sl] * o * sig * (1.0 - sig))
        dhi, dmid, dlo = _split3(-jnp.sum(do * o, axis=-1, keepdims=True))
        dos += [do, _aug_cols(rows, [(0, dhi), (1, dmid), (2, dlo)])]
        lhi, lmid, llo = _split3(-lse)
        qa = q_aug[:, AUG * h + ATT_DH:AUG * (h + 1)]
        qa = jnp.where(lane == 6, lhi, jnp.where(lane == 7, lmid, jnp.where(lane == 8, llo, qa)))
        qs += [q_aug[:, AUG * h:AUG * h + ATT_DH], qa]
    return jnp.concatenate(dos, axis=1), jnp.concatenate(dogs, axis=1), jnp.concatenate(qs, axis=1)


def fn_attn_dc(dq_aug, dk_aug):
    lane = lax.broadcasted_iota(jnp.int32, (dk_aug.shape[0], AUG), 1)
    dc = jnp.zeros((dk_aug.shape[0], AUG), F32)
    for h in range(ATT_HEADS):
        col = AUG * h + ATT_DH
        dc = jnp.where(lane == h, dq_aug[:, col:col + 1] - dk_aug[:, col + 3:col + 4], dc)
    return (dc,)


def fn_dn_prep_keep(j, y):
    return fn_dn_prep(j, y) + (y,)


def fn_dn_prep(j, y):
    s = jax.nn.silu(y)
    scale = jnp.where(j < DN_HD // 256, DN_DH ** -0.5, 1.0)
    out = []
    for hh in range(2):
        sh = s[:, DN_DH * hh:DN_DH * (hh + 1)]
        n = sh * lax.rsqrt(jnp.sum(sh * sh, axis=-1, keepdims=True) + EPS) * scale
        out.append(jnp.where(j < 2 * (DN_HD // 256), n, sh))
    return (jnp.concatenate(out, axis=1),)


def fn_dn_gates(lg, alog, dtb):
    lane = lax.broadcasted_iota(jnp.int32, lg.shape, 1)
    beta = jax.nn.sigmoid(lg)
    g = -jnp.exp(alog) * jax.nn.softplus(lg + dtb)
    return (jnp.where(lane < DN_HEADS, beta, jnp.where(lane < 2 * DN_HEADS, g, 0.0)),)


def fn_dn_post(o, og, gain):
    outs = []
    for h in range(DN_HEADS):
        sl = slice(DN_DH * h, DN_DH * (h + 1))
        outs.append(_rms(o[:, sl], gain) * jax.nn.silu(og[:, sl]))
    return (jnp.concatenate(outs, axis=1),)


def _tri(n, lower):
    r = lax.broadcasted_iota(jnp.int32, (n, n), 0)
    c = lax.broadcasted_iota(jnp.int32, (n, n), 1)
    return jnp.where((r >= c) if lower else (r <= c), 1.0, 0.0).astype(F32)


def forget_cumsum(fl, bias, name):
    Tp, W = fl.shape

    def body(fl_ref, b_ref, c_ref, carry):
        i = pl.program_id(0)

        @pl.when(i == 0)
        def _():
            carry[...] = jnp.zeros_like(carry)

        logf = jax.nn.log_sigmoid(fl_ref[...] + b_ref[...])
        c = jnp.dot(_tri(TM, True), logf, precision=lax.Precision.HIGHEST, preferred_element_type=F32) + carry[...]
        c_ref[...] = c
        carry[...] = c[TM - 1:TM, :]

    return pl.pallas_call(
        body, name=name, grid=(Tp // TM,),
        in_specs=[pl.BlockSpec((TM, W), lambda i: (i, 0)), pl.BlockSpec((1, W), lambda i: (0, 0))],
        out_specs=pl.BlockSpec((TM, W), lambda i: (i, 0)), out_shape=jax.ShapeDtypeStruct((Tp, W), F32),
        scratch_shapes=[pltpu.VMEM((1, W), F32)], compiler_params=_cp(("arbitrary",)),
    )(fl, bias)


def forget_cumsum_bwd(dc, fl, bias, name):
    Tp, W = fl.shape
    nt = Tp // TM

    def body(dc_ref, fl_ref, b_ref, dfl_ref, db_ref, carry):
        i = pl.program_id(0)

        @pl.when(i == 0)
        def _():
            carry[...] = jnp.zeros_like(carry)

        dlogf = jnp.dot(_tri(TM, False), dc_ref[...], precision=lax.Precision.HIGHEST, preferred_element_type=F32) + carry[...]
        carry[...] = dlogf[0:1, :]
        dfl = dlogf * jax.nn.sigmoid(-(fl_ref[...] + b_ref[...]))
        dfl_ref[...] = dfl.astype(dfl_ref.dtype)
        s = jnp.sum(dfl, axis=0, keepdims=True)

        @pl.when(i == 0)
        def _():
            db_ref[...] = s

        @pl.when(i > 0)
        def _():
            db_ref[...] += s

    rev = lambda i: (nt - 1 - i, 0)
    return pl.pallas_call(
        body, name=name, grid=(nt,),
        in_specs=[pl.BlockSpec((TM, W), rev), pl.BlockSpec((TM, W), rev), pl.BlockSpec((1, W), lambda i: (0, 0))],
        out_specs=[pl.BlockSpec((TM, W), rev), pl.BlockSpec((1, W), lambda i: (0, 0))],
        out_shape=[jax.ShapeDtypeStruct((Tp, W), BF16), jax.ShapeDtypeStruct((1, W), F32)],
        scratch_shapes=[pltpu.VMEM((1, W), F32)], compiler_params=_cp(("arbitrary",)),
    )(dc, fl, bias)


_HBM = pl.BlockSpec(memory_space=pltpu.HBM)


def _place():
    x, y, c = lax.axis_index("x"), lax.axis_index("y"), lax.axis_index("c")
    return x, y, c, [(1 - x, y), (x, 1 - y), (1 - x, 1 - y)]


def _chip_copies(scatter, ins, outs, send_sems, recv_sems, local_sems):
    x, y, c, chips = _place()
    s = 2 * x + y
    copies = []
    for a in range(len(ins)):
        copies.append(pltpu.make_async_copy(ins[a].at[s] if scatter else ins[a], outs[a].at[s], local_sems.at[a]))
        for p, (px, py) in enumerate(chips):
            copies.append(pltpu.make_async_remote_copy(
                src_ref=ins[a].at[2 * px + py] if scatter else ins[a], dst_ref=outs[a].at[s], send_sem=send_sems.at[a, p],
                recv_sem=recv_sems.at[a, p], device_id=(px, py, c), device_id_type=MESH_ID))
    return copies


def _chip_exchange_shapes(scatter, arrs):
    n = len(arrs)
    out_shape = [jax.ShapeDtypeStruct(a.shape if scatter else (4,) + a.shape, a.dtype) for a in arrs]
    sems = [pltpu.SemaphoreType.DMA((n, 3)), pltpu.SemaphoreType.DMA((n, 3)), pltpu.SemaphoreType.DMA((n,))]
    return out_shape, sems


def exchange_chips(scatter, arrs, name):
    n = len(arrs)

    def body(*refs):
        copies = _chip_copies(scatter, refs[:n], refs[n:2 * n], *refs[2 * n:])
        for cp in copies:
            cp.start()
        for cp in copies:
            cp.wait()

    out_shape, sems = _chip_exchange_shapes(scatter, arrs)
    return pl.pallas_call(body, name=name, in_specs=[_HBM] * n, out_specs=[_HBM] * n, out_shape=out_shape, scratch_shapes=sems)(*arrs)


def swap_cores(arrs, name):
    n = len(arrs)

    def body(*refs):
        ins, outs = refs[:n], refs[n:2 * n]
        send_sems, recv_sems = refs[2 * n:]
        x, y, c, _ = _place()
        copies = []
        for a in range(n):
            cp = pltpu.make_async_remote_copy(
                src_ref=ins[a], dst_ref=outs[a], send_sem=send_sems.at[a], recv_sem=recv_sems.at[a],
                device_id=(x, y, 1 - c), device_id_type=MESH_ID)
            cp.start()
            copies.append(cp)
        for cp in copies:
            cp.wait()

    return pl.pallas_call(
        body, name=name, in_specs=[_HBM] * n, out_specs=[_HBM] * n,
        out_shape=[jax.ShapeDtypeStruct(a.shape, a.dtype) for a in arrs],
        scratch_shapes=[pltpu.SemaphoreType.DMA((n,)), pltpu.SemaphoreType.DMA((n,))],
    )(*arrs)


_NT = (((1,), (1,)), ((), ()))
_TN = (((0,), (0,)), ((), ()))


def _cargo_edges(scatter, n, refs, nb):
    first = (pl.program_id(0) == 0) & (pl.program_id(1) == 0)
    last = (pl.program_id(0) == ATT_HEADS - 1) & (pl.program_id(1) == nb - 1)

    @pl.when(first)
    def _():
        for cp in _chip_copies(scatter, refs[:n], refs[n:2 * n], *refs[2 * n:]):
            cp.start()

    @pl.when(last)
    def _():
        for cp in _chip_copies(scatter, refs[:n], refs[n:2 * n], *refs[2 * n:]):
            cp.wait()


def flash_fwd(q_aug, k_aug, v_aug, name, cargo=()):
    Tp = q_aug.shape[0]
    nb = Tp // TM
    nc = len(cargo)

    def body(*refs):
        q_ref, k_ref, v_ref = refs[:3]
        o_ref = refs[3 + nc]
        if nc:
            _cargo_edges(False, nc, refs[3:3 + nc] + refs[4 + nc:], nb)
        i = pl.program_id(1)
        q = q_ref[...]

        def rows(j):
            return pl.ds(pl.multiple_of(j * TM, TM), TM)

        def scores(j):
            return lax.dot_general(q, k_ref[rows(j), :], _NT, preferred_element_type=F32)

        def absorb(j, s, m_old, acc):
            m_new = jnp.maximum(m_old, jnp.max(s, axis=-1, keepdims=True))
            p = jnp.exp(s - m_new)
            acc = jnp.exp(m_old - m_new) * acc + jnp.dot(p.astype(BF16), v_ref[rows(j), :], preferred_element_type=F32)
            return m_new, acc

        def step(j, carry):
            m_old, acc, s = carry
            s_next = scores(j + 1)
            return absorb(j, s, m_old, acc) + (s_next,)

        m, acc, s = lax.fori_loop(0, i, step, (jnp.full((TM, 1), NEG, F32), jnp.zeros((TM, AUG), F32), scores(0)))
        r = lax.broadcasted_iota(jnp.int32, (TM, TM), 0)
        c = lax.broadcasted_iota(jnp.int32, (TM, TM), 1)
        m, acc = absorb(i, jnp.where(c <= r, s, NEG), m, acc)
        l = acc[:, ATT_DH:ATT_DH + 1]
        lane = lax.broadcasted_iota(jnp.int32, (TM, AUG), 1)
        o_ref[...] = jnp.where(lane < ATT_DH, acc / l, m + jnp.log(l))

    head = pl.BlockSpec((Tp, AUG), lambda h, i: (0, h))
    cargo_shape, sems = _chip_exchange_shapes(False, cargo) if nc else ([], [])
    res = pl.pallas_call(
        body, name=name, grid=(ATT_HEADS, nb),
        in_specs=[pl.BlockSpec((TM, AUG), lambda h, i: (i, h)), head, head] + [_HBM] * nc,
        out_specs=[pl.BlockSpec((TM, AUG), lambda h, i: (i, h))] + [_HBM] * nc,
        out_shape=[jax.ShapeDtypeStruct((Tp, ATT_AUG), F32)] + cargo_shape, scratch_shapes=sems,
        compiler_params=_cp(("arbitrary", "arbitrary")),
    )(q_aug, k_aug, v_aug, *cargo)
    return res[0], res[1:]


def flash_bwd(q_aug2, k_aug, v_aug, do_aug, name, cargo=()):
    Tp = q_aug2.shape[0]
    nb = Tp // TM
    nc = len(cargo)

    def body(*refs):
        q_ref, do_ref, k_ref, v_ref = refs[:4]
        dq_ref, dk_ref, dv_ref = refs[4 + nc:7 + nc]
        if nc:
            _cargo_edges(True, nc, refs[4:4 + nc] + refs[7 + nc:], nb)
        j = pl.program_id(1)
        k, v = k_ref[...], v_ref[...]

        @pl.when(j == 0)
        def _():
            dq_ref[...] = jnp.zeros_like(dq_ref)

        def block(i, carry, masked):
            dk, dv = carry
            rows = pl.ds(pl.multiple_of(i * TM, TM), TM)
            q, do = q_ref[rows, :], do_ref[rows, :]
            st = lax.dot_general(k, q, _NT, preferred_element_type=F32)
            if masked:
                r = lax.broadcasted_iota(jnp.int32, (TM, TM), 0)
                c = lax.broadcasted_iota(jnp.int32, (TM, TM), 1)
                st = jnp.where(r <= c, st, NEG)
            pt = jnp.exp(st)
            dpt = lax.dot_general(v, do, _NT, preferred_element_type=F32)
            dst = (pt * dpt).astype(BF16)
            dv = dv + jnp.dot(pt.astype(BF16), do, preferred_element_type=F32)
            dk = dk + jnp.dot(dst, q, preferred_element_type=F32)
            dq_ref[rows, :] += lax.dot_general(dst, k, _TN, preferred_element_type=F32)
            return dk, dv

        carry = block(j, (jnp.zeros((TM, AUG), F32), jnp.zeros((TM, AUG), F32)), True)
        dk, dv = lax.fori_loop(j + 1, nb, lambda i, cr: block(i, cr, False), carry)
        dk_ref[...] = dk
        dv_ref[...] = dv

    head = pl.BlockSpec((Tp, AUG), lambda h, j: (0, h))
    blk = pl.BlockSpec((TM, AUG), lambda h, j: (j, h))
    cargo_shape, sems = _chip_exchange_shapes(True, cargo) if nc else ([], [])
    res = pl.pallas_call(
        body, name=name, grid=(ATT_HEADS, nb),
        in_specs=[head, head, blk, blk] + [_HBM] * nc, out_specs=[head, blk, blk] + [_HBM] * nc,
        out_shape=[jax.ShapeDtypeStruct((Tp, ATT_AUG), F32)] * 3 + cargo_shape, scratch_shapes=sems,
        compiler_params=_cp(("arbitrary", "arbitrary")),
    )(q_aug2, do_aug, k_aug, v_aug, *cargo)
    return res[:3], res[3:]


_BATCH = ((0,), (0,))


def _dg(x, y, cx, cy):
    return lax.dot_general(x.astype(BF16), y.astype(BF16), (((cx + 1,), (cy + 1,)), _BATCH), preferred_element_type=F32)


def _split2(x):
    hi = x.astype(BF16)
    return hi, (x - hi.astype(F32)).astype(BF16)


def _dg3(x, y, cx, cy):
    (xh, xl), (yh, yl) = _split2(x), _split2(y)
    d = lambda a, b: lax.dot_general(a, b, (((cx + 1,), (cy + 1,)), _BATCH), preferred_element_type=F32)
    return d(xh, yh) + (d(xl, yh) + d(xh, yl))


def _make_bdot(ca, cb, dg):
    @jax.custom_vjp
    def f(a, b):
        return dg(a, b, ca, cb)

    def fwd(a, b):
        return dg(a, b, ca, cb), (a, b)

    def bwd(res, ct):
        a, b = res
        da = dg(ct, b, 1, 1 if cb == 0 else 0) if ca == 1 else dg(b, ct, 1 if cb == 0 else 0, 1)
        db = dg(a, ct, 0 if ca == 1 else 1, 0) if cb == 0 else dg(ct, a, 0, 0 if ca == 1 else 1)
        return da, db

    f.defvjp(fwd, bwd)
    return f


_bd_nn, _bd_nt, _bd_tn = _make_bdot(1, 0, _dg), _make_bdot(1, 1, _dg), _make_bdot(0, 0, _dg)
_bd3_nn = _make_bdot(1, 0, _dg3)


@jax.custom_vjp
def _chunk_cumsum(x):
    return jnp.dot(_tri(x.shape[0], True), x, precision=lax.Precision.HIGHEST, preferred_element_type=F32)


def _chunk_cumsum_fwd(x):
    return _chunk_cumsum(x), None


def _chunk_cumsum_bwd(_, ct):
    return (jnp.dot(_tri(ct.shape[0], False), ct, precision=lax.Precision.HIGHEST, preferred_element_type=F32),)


_chunk_cumsum.defvjp(_chunk_cumsum_fwd, _chunk_cumsum_bwd)


@jax.custom_vjp
def _inverse_given(A, N):
    return N


def _inverse_given_fwd(A, N):
    return N, N


def _inverse_given_bwd(N, ct):
    M = ct + _dg3(N, ct, 0, 0)
    return -(M + _dg3(M, N, 1, 1)), jnp.zeros_like(N)


_inverse_given.defvjp(_inverse_given_fwd, _inverse_given_bwd)


def _dn_heads(t):
    return jnp.stack([t[:, DN_DH * h:DN_DH * (h + 1)] for h in range(DN_HEADS)])


def _dn_gates(bg):
    cols = lambda t, o: jnp.stack([t[:, o + h:o + h + 1] for h in range(DN_HEADS)])
    return cols(bg, 0), cols(_chunk_cumsum(bg), DN_HEADS)


def _dn_decay(gc):
    H, C = gc.shape[0], gc.shape[1]
    r = lax.broadcasted_iota(jnp.int32, (H, C, C), 1)
    c = lax.broadcasted_iota(jnp.int32, (H, C, C), 2)
    gcb = jnp.broadcast_to(gc, (H, C, C))
    return jnp.exp(jnp.where(r >= c, gcb - jnp.swapaxes(gcb, 1, 2), NEG)), r > c


def dn_inverse(kh, beta, gc):
    C = kh.shape[1]
    dec, strict = _dn_decay(gc)
    X = -jnp.where(strict, _bd_nt(kh, kh) * dec * beta, 0.0)
    P = _bd3_nn(X, X)
    N = X
    for i in range(5):
        NP = _bd3_nn(jnp.concatenate([N, P], axis=1) if i < 4 else N, P)
        N = N + P + NP[:, :C]
        if i < 4:
            P = NP[:, C:]
    return N


def dn_chunk(S, q, k, v, bg, N_given):
    C = q.shape[0]
    H = DN_HEADS
    qh, kh, vh = _dn_heads(q), _dn_heads(k), _dn_heads(v)
    beta, gc = _dn_gates(bg)
    dec, strict = _dn_decay(gc)
    e_gc = jnp.exp(gc)
    kq = _bd_nt(jnp.concatenate([kh, qh], axis=1), kh)
    A = jnp.where(strict, kq[:, :C] * dec * beta, 0.0)
    qk = kq[:, C:] * dec
    N = _inverse_given(A, N_given)
    R = jnp.concatenate([kh * (beta * e_gc), vh * beta], axis=2)
    WU = R + _bd3_nn(N, R)
    W, U0 = WU[:, :, :DN_DH], WU[:, :, DN_DH:]
    gl = gc[:, C - 1:C, :]
    WqS = _bd_nt(jnp.concatenate([W, qh * e_gc], axis=1), S)
    U = U0 - WqS[:, :C]
    O = WqS[:, C:] + _bd_nn(qk, U)
    S_new = jnp.exp(gl) * S + _bd_tn(U, kh * jnp.exp(gl - gc))
    return jnp.concatenate([O[h] for h in range(H)], axis=1), S_new


DN_STEP = 2


def dn_scan_fwd(qkv, bg, name):
    Tp = qkv.shape[0]
    n = Tp // DN_CHUNK
    G = DN_STEP
    tokspec = lambda w, c=0: pl.BlockSpec((G * DN_CHUNK, w), lambda i: (i, c))

    def body(q_ref, k_ref, v_ref, bg_ref, o_ref, s_ref, n_ref, state):
        @pl.when(pl.program_id(0) == 0)
        def _():
            state[...] = jnp.zeros_like(state)

        rows = [slice(g * DN_CHUNK, (g + 1) * DN_CHUNK) for g in range(G)]
        gates = [_dn_gates(bg_ref[r, :]) for r in rows]
        N_all = dn_inverse(jnp.concatenate([_dn_heads(k_ref[r, :].astype(F32)) for r in rows], axis=0),
                           jnp.concatenate([b for b, _ in gates], axis=0), jnp.concatenate([c for _, c in gates], axis=0))
        S = state[...]
        for g, r in enumerate(rows):
            N = N_all[DN_HEADS * g:DN_HEADS * (g + 1)]
            s_ref[g] = S
            n_ref[g] = N
            O, S = dn_chunk(S, q_ref[r, :].astype(F32), k_ref[r, :].astype(F32), v_ref[r, :].astype(F32), bg_ref[r, :], N)
            o_ref[r, :] = O.astype(o_ref.dtype)
        state[...] = S

    return pl.pallas_call(
        body, name=name, grid=(n // G,),
        in_specs=[tokspec(DN_HD, 0), tokspec(DN_HD, 1), tokspec(DN_HD, 2), tokspec(AUG)],
        out_specs=[tokspec(DN_HD), pl.BlockSpec((G, DN_HEADS, DN_DH, DN_DH), lambda i: (i, 0, 0, 0)),
                   pl.BlockSpec((G, DN_HEADS, DN_CHUNK, DN_CHUNK), lambda i: (i, 0, 0, 0))],
        out_shape=[jax.ShapeDtypeStruct((Tp, DN_HD), BF16), jax.ShapeDtypeStruct((n, DN_HEADS, DN_DH, DN_DH), F32),
                   jax.ShapeDtypeStruct((n, DN_HEADS, DN_CHUNK, DN_CHUNK), F32)],
        scratch_shapes=[pltpu.VMEM((DN_HEADS, DN_DH, DN_DH), F32)],
        compiler_params=_cp(("arbitrary",)),
    )(qkv, qkv, qkv, bg)


def dn_scan_bwd(qkv, bg, states, inverses, dO, name):
    Tp = qkv.shape[0]
    n = Tp // DN_CHUNK
    tokspec = lambda w, c=0: pl.BlockSpec((DN_CHUNK, w), lambda i: (n - 1 - i, c))

    def body(q_ref, k_ref, v_ref, bg_ref, s_ref, n_ref, do_ref, dqkv_ref, dbg_ref, dstate):
        @pl.when(pl.program_id(0) == 0)
        def _():
            dstate[...] = jnp.zeros_like(dstate)

        N = n_ref[0]
        _, vjp = jax.vjp(lambda *xs: dn_chunk(*xs, N),
                         s_ref[0], q_ref[...].astype(F32), k_ref[...].astype(F32), v_ref[...].astype(F32), bg_ref[...])
        dS, dq, dk, dv, dbg = vjp((do_ref[...].astype(F32), dstate[...]))
        dqkv_ref[...] = jnp.concatenate([dq, dk, dv], axis=1).astype(dqkv_ref.dtype)
        dbg_ref[...] = dbg
        dstate[...] = dS

    return pl.pallas_call(
        body, name=name, grid=(n,),
        in_specs=[tokspec(DN_HD, 0), tokspec(DN_HD, 1), tokspec(DN_HD, 2), tokspec(AUG),
                  pl.BlockSpec((1, DN_HEADS, DN_DH, DN_DH), lambda i: (n - 1 - i, 0, 0, 0)),
                  pl.BlockSpec((1, DN_HEADS, DN_CHUNK, DN_CHUNK), lambda i: (n - 1 - i, 0, 0, 0)), tokspec(DN_HD)],
        out_specs=[tokspec(3 * DN_HD), tokspec(AUG)],
        out_shape=[jax.ShapeDtypeStruct((Tp, 3 * DN_HD), BF16), jax.ShapeDtypeStruct((Tp, AUG), F32)],
        scratch_shapes=[pltpu.VMEM((DN_HEADS, DN_DH, DN_DH), F32)],
        compiler_params=_cp(("arbitrary",)),
    )(qkv, qkv, qkv, bg, states, inverses, dO)


def loss_head(h, tgt, n_tok, name):
    Tp, Dm = h.shape

    def body(h_ref, t_ref, l_ref, dh_ref):
        i = pl.program_id(0)
        row = i * TM + lax.broadcasted_iota(jnp.int32, (TM, Dm), 0)
        e = jnp.where((row >= N_META) & (row < N_META + n_tok), h_ref[...] - t_ref[...], 0.0)
        dh_ref[...] = e * (1.0 / Dm)
        s = jnp.sum(e * e, axis=0, keepdims=True) * (0.5 / Dm)

        @pl.when(i == 0)
        def _():
            l_ref[...] = s

        @pl.when(i > 0)
        def _():
            l_ref[...] += s

    tile = pl.BlockSpec((TM, Dm), lambda i: (i, 0))
    return pl.pallas_call(
        body, name=name, grid=(Tp // TM,), in_specs=[tile, tile],
        out_specs=[pl.BlockSpec((1, Dm), lambda i: (0, 0)), tile],
        out_shape=[jax.ShapeDtypeStruct((1, Dm), F32), jax.ShapeDtypeStruct((Tp, Dm), F32)],
        compiler_params=_cp(("arbitrary",)),
    )(h, tgt)


def _row_block(R):
    return _pick(R, (256, 176, 128, 64, 32, 16, 8))


def sum_slots(parts, name):
    P, R, C = parts.shape
    rb = _row_block(R)

    def body(p_ref, o_ref):
        acc = p_ref[0].astype(F32)
        for s in range(1, P):
            acc = acc + p_ref[s].astype(F32)
        o_ref[...] = acc

    return pl.pallas_call(
        body, name=name, grid=(R // rb,), in_specs=[pl.BlockSpec((P, rb, C), lambda i: (0, i, 0))],
        out_specs=pl.BlockSpec((rb, C), lambda i: (i, 0)), out_shape=jax.ShapeDtypeStruct((R, C), F32),
        compiler_params=_cp(("parallel",)),
    )(parts)


def _adamw_math(w, g, m, v):
    m2 = ADAM_B1 * m + (1.0 - ADAM_B1) * g
    v2 = ADAM_B2 * v + (1.0 - ADAM_B2) * (g * g)
    m_hat = m2 / (1.0 - ADAM_B1 ** ADAM_STEP)
    v_hat = v2 / (1.0 - ADAM_B2 ** ADAM_STEP)
    delta = -ADAM_LR * (m_hat / (jnp.sqrt(v_hat) + ADAM_EPS) + ADAM_WD * w)
    return delta, m2, v2


def adamw_big(w, m, v, g_a, g_b, name):
    R, C = w.shape
    rb = _row_block(R)

    def body(w_ref, m_ref, v_ref, ga_ref, gb_ref, g_ref, d_ref, m2_ref, v2_ref):
        g = ga_ref[...] + gb_ref[...]
        delta, m2, v2 = _adamw_math(w_ref[...], g, m_ref[...], v_ref[...])
        g_ref[...], d_ref[...], m2_ref[...], v2_ref[...] = g, delta, m2, v2

    spec = pl.BlockSpec((rb, C), lambda i: (i, 0))
    return pl.pallas_call(
        body, name=name, grid=(R // rb,), in_specs=[spec] * 5, out_specs=[spec] * 4,
        out_shape=[jax.ShapeDtypeStruct((R, C), F32)] * 4, compiler_params=_cp(("parallel",)),
    )(w, m, v, g_a, g_b)


def adamw_small(ws, ms, vs, gs, name):
    n = len(ws)

    def body(*refs):
        w_r, m_r, v_r, g_r = refs[:n], refs[n:2 * n], refs[2 * n:3 * n], refs[3 * n:4 * n]
        d_o, m_o, v_o = refs[4 * n:5 * n], refs[5 * n:6 * n], refs[6 * n:7 * n]
        for k in range(n):
            delta, m2, v2 = _adamw_math(w_r[k][...], g_r[k][...], m_r[k][...], v_r[k][...])
            d_o[k][...], m_o[k][...], v_o[k][...] = delta, m2, v2

    shapes = [jax.ShapeDtypeStruct(w.shape, F32) for w in ws]
    res = pl.pallas_call(body, name=name, out_shape=shapes * 3, compiler_params=pltpu.CompilerParams(vmem_limit_bytes=VMEM_LIMIT))(
        *ws, *ms, *vs, *gs)
    return res[:n], res[n:2 * n], res[2 * n:]


def sum8(own4, sib4, name):
    _, R, C = own4.shape

    def body(a_ref, b_ref, o_ref):
        a = ((a_ref[0] + a_ref[1]) + a_ref[2]) + a_ref[3]
        b = ((b_ref[0] + b_ref[1]) + b_ref[2]) + b_ref[3]
        o_ref[...] = a + b

    return pl.pallas_call(body, name=name, out_shape=jax.ShapeDtypeStruct((R, C), F32),
                          compiler_params=pltpu.CompilerParams(vmem_limit_bytes=VMEM_LIMIT))(own4, sib4)


def _il(w):
    lead = w.shape[:-1]
    return w.reshape(lead + (2, FFN_DIM // 256, 256)).swapaxes(-3, -2).reshape(lead + (2 * FFN_DIM,))


def _unil(w):
    lead = w.shape[:-1]
    return w.reshape(lead + (FFN_DIM // 256, 2, 256)).swapaxes(-3, -2).reshape(lead + (2 * FFN_DIM,))


def _pad_cols(w, n=AUG):
    return jnp.pad(w, ((0, 0), (0, n - w.shape[1])))


ATT_SPLIT = ATT_HEADS // 2


def attn_fwd(a, a_t, W, tag, cargo=()):
    proj = matmul(a, W["main"], "nn", BF16, f"{tag}_proj", ncols=4 * D_MODEL)
    fl = matmul(a, W["small"], "nn", F32, f"{tag}_fl")
    c = forget_cumsum(fl, W["bias"], f"{tag}_cumsum")
    cfix = tok(c, fixed=True)
    q_aug = rowwise(fn_attn_prep_q, [tok(proj, AUG, 0), cfix], [par(W["qg"])], [(2 * AUG, BF16)], f"{tag}_prepq", ATT_SPLIT, True)[0]
    k_aug = rowwise(fn_attn_prep_k, [tok(proj, AUG, ATT_SPLIT), cfix], [par(W["kg"])], [(2 * AUG, BF16)], f"{tag}_prepk", ATT_SPLIT, True)[0]
    v_aug = rowwise(fn_attn_prep_v, [tok(proj, AUG, 2 * ATT_SPLIT)], [], [(2 * AUG, BF16)], f"{tag}_prepv", ATT_SPLIT)[0]
    o_aug, landed = flash_fwd(q_aug, k_aug, v_aug, f"{tag}_flash", cargo)
    go, go_t = rowwise(fn_attn_gate, [tok(o_aug, 2 * AUG, 0), tok(proj, AUG, 3 * ATT_SPLIT)], [], [(AUG, BF16), (AUG, BF16, 0)],
                       f"{tag}_gate", ATT_SPLIT)
    return (a_t, proj, fl, c, q_aug, k_aug, v_aug, o_aug, go_t, go), landed


def attn_out(saved, W, tag):
    return matmul(saved[-1], W["out"], "nn", BF16, f"{tag}_out")


def attn_bwd(dm, saved, W, tag, cargo=None):
    a_t, proj, fl, c, q_aug, k_aug, v_aug, o_aug, go_t, _ = saved
    dgo = matmul(dm, W["out"], "nt", BF16, f"{tag}_dgo")
    d_out = matmul(go_t, dm, "nn", BF16, f"{tag}_dwout")
    do_aug, dog, q_aug2 = rowwise(
        fn_attn_bwd_prep, [tok(dgo, AUG, 0), tok(o_aug, 2 * AUG, 0), tok(proj, AUG, 3 * ATT_SPLIT), tok(q_aug, 2 * AUG, 0)], [],
        [(2 * AUG, BF16), (AUG, BF16), (2 * AUG, BF16)], f"{tag}_bprep", ATT_SPLIT)
    (dq_aug, dk_aug, dv_aug), landed = flash_bwd(q_aug2, k_aug, v_aug, do_aug, f"{tag}_flashb", cargo(d_out) if cargo else ())
    dc = rowwise(fn_attn_dc, [tok(dq_aug), tok(dk_aug)], [], [(AUG, F32)], f"{tag}_dc")[0]
    cfix = tok(c, fixed=True)
    (dq,), (d_qg,) = rowwise_bwd(fn_attn_prep_q, [tok(proj, AUG, 0, grad=BF16), cfix], [par(W["qg"])], [dq_aug],
                                 f"{tag}_prepqb", nsplit=ATT_SPLIT, with_j=True)
    (dk,), (d_kg,) = rowwise_bwd(fn_attn_prep_k, [tok(proj, AUG, ATT_SPLIT, grad=BF16), cfix], [par(W["kg"])], [dk_aug],
                                 f"{tag}_prepkb", nsplit=ATT_SPLIT, with_j=True)
    (dv,), _ = rowwise_bwd(fn_attn_prep_v, [tok(proj, AUG, 2 * ATT_SPLIT, grad=BF16)], [], [dv_aug], f"{tag}_prepvb", nsplit=ATT_SPLIT)
    dfl, d_bias = forget_cumsum_bwd(dc, fl, W["bias"], f"{tag}_cumsumb")
    dproj = jnp.concatenate([dq, dk, dv, dog], axis=1)
    da = matmul(dproj, W["main"], "nt", BF16, f"{tag}_da", add=matmul(dfl, W["small"], "nt", F32, f"{tag}_da0"))
    d_in = jnp.concatenate([matmul(a_t, dproj, "nn", BF16, f"{tag}_dw"), matmul(a_t, dfl, "nn", BF16, f"{tag}_dw2")[:, :ATT_HEADS]], axis=1)
    return da, dict(w_in=d_in, w_out=d_out, b_forget=d_bias[0, :ATT_HEADS], q_norm=d_qg[0], k_norm=d_kg[0]), landed


DN_SPLIT = 3 * DN_HD // 256


def dn_fwd(a, a_t, W, tag):
    proj = matmul(a, W["main"], "nn", BF16, f"{tag}_proj", ncols=4 * D_MODEL)
    lg = matmul(a, W["small"], "nn", F32, f"{tag}_lg")
    qkv, y = rowwise(fn_dn_prep_keep, [tok(proj, 256, 0, conv=(W["conv"], 0))], [], [(256, BF16), (256, BF16)], f"{tag}_prep", DN_SPLIT, True)
    bg = rowwise(fn_dn_gates, [tok(lg)], [par(W["alog"]), par(W["dtb"])], [(AUG, F32)], f"{tag}_gates")[0]
    O, states, inverses = dn_scan_fwd(qkv, bg, f"{tag}_scan")
    go, go_t = rowwise(fn_dn_post, [tok(O), tok(proj, DN_HD, 3)], [par(W["ogain"])], [(DN_HD, BF16), (DN_HD, BF16, 0)], f"{tag}_post")
    m = matmul(go, W["out"], "nn", BF16, f"{tag}_out")
    return m, (a_t, proj, lg, qkv, y, bg, O, states, inverses, go_t)


def dn_bwd(dm, saved, W, tag):
    a_t, proj, lg, qkv, y, bg, O, states, inverses, go_t = saved
    dgo = matmul(dm, W["out"], "nt", BF16, f"{tag}_dgo")
    d_out = matmul(go_t, dm, "nn", BF16, f"{tag}_dwout")
    (dO, dog), (d_ogain,) = rowwise_bwd(fn_dn_post, [tok(O, grad=BF16), tok(proj, DN_HD, 3, grad=BF16)], [par(W["ogain"])],
                                        [dgo], f"{tag}_postb")
    dqkv_n, dbg = dn_scan_bwd(qkv, bg, states, inverses, dO, f"{tag}_scanb")
    (dy,), _ = rowwise_bwd(fn_dn_prep, [tok(y, 256, 0, grad=BF16)], [], [dqkv_n], f"{tag}_prepb", nsplit=DN_SPLIT, with_j=True)
    dqkv, d_conv = conv_bwd(dy, proj, W["conv"], f"{tag}_convb")
    (dlg,), (d_alog, d_dtb) = rowwise_bwd(fn_dn_gates, [tok(lg, grad=BF16)], [par(W["alog"]), par(W["dtb"])], [dbg], f"{tag}_gatesb")
    dproj = jnp.concatenate([dqkv, dog], axis=1)
    da = matmul(dproj, W["main"], "nt", BF16, f"{tag}_da", add=matmul(dlg, W["small"], "nt", F32, f"{tag}_da0"))
    d_in = jnp.concatenate([matmul(a_t, dproj, "nn", BF16, f"{tag}_dw"), matmul(a_t, dlg, "nn", BF16, f"{tag}_dw2")[:, :2 * DN_HEADS]], axis=1)
    return da, dict(w_in=d_in, w_out=d_out, conv=d_conv, a_log=d_alog[0, DN_HEADS:2 * DN_HEADS],
                    dt_bias=d_dtb[0, DN_HEADS:2 * DN_HEADS], o_norm=d_ogain[0])


FFN_SPLIT = FFN_DIM // 256


def ffn_fwd(b, b_t, W, tag):
    u0 = matmul(b, W["up"], "nn", BF16, f"{tag}_up")
    act, u, act_t = rowwise(fn_geglu_keep, [tok(u0, 512, 0, conv=(W["conv"], 0))], [], [(256, BF16), (512, BF16), (256, BF16, 0)],
                            f"{tag}_act", nsplit=FFN_SPLIT)
    f = matmul(act, W["down"], "nn", BF16, f"{tag}_down")
    return f, (b_t, u0, u, act_t)


def ffn_bwd(df, saved, W, tag):
    b_t, u0, u, act_t = saved
    dact = matmul(df, W["down"], "nt", BF16, f"{tag}_dact")
    d_down = matmul(act_t, df, "nn", BF16, f"{tag}_dwdown")
    (du,), _ = rowwise_bwd(fn_geglu, [tok(u, 512, 0, grad=BF16)], [], [dact], f"{tag}_actb", nsplit=FFN_SPLIT)
    du0, d_conv = conv_bwd(du, u0, W["conv"], f"{tag}_convb")
    db = matmul(du0, W["up"], "nt", BF16, f"{tag}_db")
    d_up = matmul(b_t, du0, "nn", BF16, f"{tag}_dwup")
    return db, dict(up=_unil(d_up), conv=_unil(d_conv), down=d_down)


def _cols_from_slots(g, L):
    K, Cs = g.shape[1] // L, g.shape[2]
    return g.reshape(4, L, K, Cs).transpose(1, 2, 0, 3).reshape(L, K, 4 * Cs)


def _rows_from_slots(g, L):
    Rs, C = g.shape[1] // L, g.shape[2]
    return g.reshape(4, L, Rs, C).transpose(1, 0, 2, 3).reshape(L, 4 * Rs, C)


def _cols_to_slots(w):
    L, K, C = w.shape
    return w.reshape(L, K, 4, C // 4).transpose(2, 0, 1, 3).reshape(4, L * K, C // 4)


def _rows_to_slots(w):
    L, R, C = w.shape
    return w.reshape(L, 4, R // 4, C).transpose(1, 0, 2, 3).reshape(4, L * (R // 4), C)


SMALL_NAMES = ["meta_tokens", "norm_mix_pre", "norm_mix_post", "norm_ffn_pre", "norm_ffn_post", "attn_b_forget", "attn_q_norm",
               "attn_k_norm", "dn_conv", "dn_a_log", "dn_dt_bias", "dn_o_norm", "ffn_conv"]
BIG_NAMES = ["attn_w_in", "attn_w_out", "dn_w_in", "dn_w_out", "ffn_w_up", "ffn_w_down"]
WEIGHT_NAMES = ["meta_tokens", "norm_mix_pre", "norm_mix_post", "norm_ffn_pre", "norm_ffn_post", "attn_w_in", "attn_b_forget",
                "attn_q_norm", "attn_k_norm", "attn_w_out", "dn_w_in", "dn_conv", "dn_a_log", "dn_dt_bias", "dn_o_norm", "dn_w_out",
                "ffn_w_up", "ffn_conv", "ffn_w_down"]
PACK_COLS = 1024


def train_step(x, loss_target, w, m, v):
    L = x.shape[1]
    T = L + N_META
    Tp = -(-T // TM) * TM
    xi, yi = lax.axis_index("x"), lax.axis_index("y")
    slot = 2 * xi + yi
    as2d = lambda t: t.reshape(-1, t.shape[-1])

    bf2d = lambda n: as2d(w[n]).astype(BF16)
    attn_in_shard = w["attn_w_in"].astype(BF16)
    first = exchange_chips(False, [attn_in_shard[0], w["meta_tokens"], as2d(w["dn_conv"]), as2d(w["ffn_conv"])], "gather_first")
    later = [attn_in_shard[1], bf2d("attn_w_out"), bf2d("dn_w_in"), bf2d("dn_w_out"), bf2d("ffn_w_up"), bf2d("ffn_w_down")]
    meta = first[1].transpose(1, 0, 2).reshape(N_META, D_MODEL)
    dn_conv = _cols_from_slots(first[2], 2)
    ffn_conv = _il(_cols_from_slots(first[3], DEPTH))
    lane8 = lambda t: jnp.pad(t[None, :], ((0, 0), (DN_HEADS, AUG - 2 * DN_HEADS)))

    def attn_weights(j, w_in, w_out):
        return dict(main=(w_in, 0), small=_pad_cols(w_in[0, :, 4 * ATT_HD:]), bias=_pad_cols(w["attn_b_forget"][j][None, :]),
                    qg=w["attn_q_norm"][j][None, :], kg=w["attn_k_norm"][j][None, :], out=w_out)

    WA = [attn_weights(0, _cols_from_slots(first[0], 1), None), None]
    WD, WF = None, None
    row = lambda t, i: t[i][None, :]

    h = jnp.concatenate([meta, x[0], jnp.zeros((Tp - T, D_MODEL), F32)], axis=0)
    tgt = jnp.concatenate([jnp.zeros((N_META, D_MODEL), F32), loss_target[0], jnp.zeros((Tp - T, D_MODEL), F32)], axis=0)
    h0 = h
    a, a_t = rowwise(fn_norm, [tok(h)], [par(row(w["norm_mix_pre"], 0))], [(D_MODEL, BF16), (D_MODEL, BF16, 0)], "norm0")
    saved = []
    for i in range(DEPTH):
        j = i // 2
        if i == 0:
            sv_mix, landed = attn_fwd(a, a_t, WA[0], "attn0", later)
            attn_out_w, dn_in, dn_out = _rows_from_slots(landed[1], 2), _cols_from_slots(landed[2], 2), _rows_from_slots(landed[3], 2)
            ffn_up, ffn_down = _il(_cols_from_slots(landed[4], DEPTH)), _rows_from_slots(landed[5], DEPTH)
            WA[0]["out"] = (attn_out_w, 0)
            WA[1] = attn_weights(1, _cols_from_slots(landed[0], 1), (attn_out_w, 1))
            WD = [dict(main=(dn_in, k), small=_pad_cols(dn_in[k, :, 4 * DN_HD:]), conv=dn_conv[k], alog=lane8(w["dn_a_log"][k]),
                       dtb=lane8(w["dn_dt_bias"][k]), ogain=w["dn_o_norm"][k][None, :], out=(dn_out, k)) for k in range(2)]
            WF = [dict(up=(ffn_up, k), conv=ffn_conv[k], down=(ffn_down, k)) for k in range(DEPTH)]
            mix = attn_out(sv_mix, WA[0], "attn0")
        elif i % 2 == 0:
            sv_mix, _ = attn_fwd(a, a_t, WA[j], f"attn{j}")
            mix = attn_out(sv_mix, WA[j], f"attn{j}")
        else:
            mix, sv_mix = dn_fwd(a, a_t, WD[j], f"dn{j}")
        norm_outs = [(D_MODEL, F32), (D_MODEL, BF16), (D_MODEL, BF16, 1)]
        h_mid, b, b_t = rowwise(fn_resid_norm, [tok(h), tok(mix)], [par(row(w["norm_mix_post"], i)), par(row(w["norm_ffn_pre"], i))],
                                norm_outs, f"resid_mix{i}")
        f, sv_ffn = ffn_fwd(b, b_t, WF[i], f"ffn{i}")
        if i < DEPTH - 1:
            h_out, a, a_t = rowwise(fn_resid_norm, [tok(h_mid), tok(f)], [par(row(w["norm_ffn_post"], i)), par(row(w["norm_mix_pre"], i + 1))],
                                    norm_outs, f"resid_ffn{i}")
        else:
            h_out = rowwise(fn_resid, [tok(h_mid), tok(f)], [par(row(w["norm_ffn_post"], i))], [(D_MODEL, F32)], f"resid_ffn{i}")[0]
        saved.append((h, mix, sv_mix, h_mid, f, sv_ffn))
        h = h_out
    lvec, dh = loss_head(h, tgt, L, "loss_head")
    loss = lax.psum(jnp.sum(lvec), ("x", "y", "c"))

    gn = {n: [None] * DEPTH for n in ("norm_mix_pre", "norm_mix_post", "norm_ffn_pre", "norm_ffn_post")}
    g_attn, g_dn, g_ffn = [None, None], [None, None], [None] * DEPTH
    da = None
    for i in reversed(range(DEPTH)):
        j = i // 2
        h_in, mix, sv_mix, h_mid, f, sv_ffn = saved[i]
        if i == DEPTH - 1:
            (dh, df), (gn["norm_ffn_post"][i],) = rowwise_bwd(
                fn_resid, [tok(h_mid, grad=F32), tok(f, grad=BF16)], [par(row(w["norm_ffn_post"], i))], [dh], f"resid_ffn{i}_b")
        else:
            (dh, df), (gn["norm_ffn_post"][i], gn["norm_mix_pre"][i + 1]) = rowwise_bwd(
                fn_resid_norm, [tok(h_mid, grad=F32), tok(f, grad=BF16)],
                [par(row(w["norm_ffn_post"], i)), par(row(w["norm_mix_pre"], i + 1))], [dh, da], f"resid_ffn{i}_b")
        db, g_ffn[i] = ffn_bwd(df, sv_ffn, WF[i], f"ffn{i}")
        (dh, dm), (gn["norm_mix_post"][i], gn["norm_ffn_pre"][i]) = rowwise_bwd(
            fn_resid_norm, [tok(h_in, grad=F32), tok(mix, grad=BF16)],
            [par(row(w["norm_mix_post"], i)), par(row(w["norm_ffn_pre"], i))], [dh, db], f"resid_mix{i}_b")
        if i == 0:
            def riders(d_out0):
                return [_cols_to_slots(g_attn[1]["w_in"][None]), _rows_to_slots(jnp.stack([d_out0, g_attn[1]["w_out"]])),
                        _cols_to_slots(jnp.stack([g["w_in"] for g in g_dn])), _rows_to_slots(jnp.stack([g["w_out"] for g in g_dn])),
                        _cols_to_slots(jnp.stack([g["up"] for g in g_ffn])), _rows_to_slots(jnp.stack([g["down"] for g in g_ffn]))]
            da, g_attn[0], landed_grads = attn_bwd(dm, sv_mix, WA[0], "attn0", riders)
        elif i % 2 == 0:
            da, g_attn[j], _ = attn_bwd(dm, sv_mix, WA[j], f"attn{j}")
        else:
            da, g_dn[j] = dn_bwd(dm, sv_mix, WD[j], f"dn{j}")
    (dh0,), (gn["norm_mix_pre"][0],) = rowwise_bwd(fn_id_norm, [tok(h0, grad=F32)], [par(row(w["norm_mix_pre"], 0))], [dh, da], "norm0_b")
    grad_x = dh0[N_META:T][None]

    full = dict(
        meta_tokens=dh0[:N_META],
        **{n: jnp.concatenate(gn[n], axis=0) for n in gn},
        attn_b_forget=jnp.stack([g["b_forget"] for g in g_attn]), attn_q_norm=jnp.stack([g["q_norm"] for g in g_attn]),
        attn_k_norm=jnp.stack([g["k_norm"] for g in g_attn]),
        dn_conv=jnp.stack([g["conv"] for g in g_dn]), dn_a_log=jnp.stack([g["a_log"] for g in g_dn]),
        dn_dt_bias=jnp.stack([g["dt_bias"] for g in g_dn]), dn_o_norm=jnp.stack([g["o_norm"] for g in g_dn]),
        ffn_conv=jnp.stack([g["conv"] for g in g_ffn]))
    flat = jnp.concatenate([full[n].reshape(-1) for n in SMALL_NAMES])
    rows = -(-flat.shape[0] // PACK_COLS)
    rows = -(-rows // 8) * 8
    pack = jnp.pad(flat, (0, rows * PACK_COLS - flat.shape[0])).reshape(rows, PACK_COLS)
    own4 = exchange_chips(False, [pack], "gather_small_grads")[0]
    sib4 = swap_cores([own4], "swap_small_grads")[0]
    tot = sum8(own4, sib4, "sum_small_grads").reshape(-1)
    grads, off = {}, 0
    for n in SMALL_NAMES:
        size = full[n].size
        g = tot[off:off + size].reshape(full[n].shape)
        off += size
        if g.shape != w[n].shape:
            g = lax.dynamic_slice_in_dim(g, slot * w[n].shape[-1], w[n].shape[-1], axis=g.ndim - 1)
        grads[n] = g
    d_s, m_s, v_s = adamw_small([as2d(w[n]) for n in SMALL_NAMES], [as2d(m[n]) for n in SMALL_NAMES],
                                [as2d(v[n]) for n in SMALL_NAMES], [as2d(grads[n]) for n in SMALL_NAMES], "adamw_small")
    deltas = {n: d.reshape(w[n].shape) for n, d in zip(SMALL_NAMES, d_s, strict=True)}
    new_m = {n: d.reshape(w[n].shape) for n, d in zip(SMALL_NAMES, m_s, strict=True)}
    new_v = {n: d.reshape(w[n].shape) for n, d in zip(SMALL_NAMES, v_s, strict=True)}

    last = exchange_chips(True, [_cols_to_slots(g_attn[0]["w_in"][None])], "scatter_last")[0]
    recv = [jnp.concatenate([last, landed_grads[0]], axis=1)] + list(landed_grads[1:])
    part = [sum_slots(r, f"sum_{n}") for n, r in zip(BIG_NAMES, recv, strict=True)]
    other = swap_cores(part, "swap_grads")
    for n, pa, pb in zip(BIG_NAMES, part, other, strict=True):
        g, d, m2, v2 = adamw_big(as2d(w[n]), as2d(m[n]), as2d(v[n]), pa, pb, f"adamw_{n}")
        grads[n], deltas[n], new_m[n], new_v[n] = (t.reshape(w[n].shape) for t in (g, d, m2, v2))
    return loss, grad_x, grads, deltas, new_m, new_v


def kernel(x, meta_tokens, norm_mix_pre, norm_mix_post, norm_ffn_pre, norm_ffn_post, attn_w_in, attn_b_forget, attn_q_norm, attn_k_norm, attn_w_out, dn_w_in, dn_conv, dn_a_log, dn_dt_bias, dn_o_norm, dn_w_out, ffn_w_up, ffn_conv, ffn_w_down, loss_target, m_meta_tokens, m_norm_mix_pre, m_norm_mix_post, m_norm_ffn_pre, m_norm_ffn_post, m_attn_w_in, m_attn_b_forget, m_attn_q_norm, m_attn_k_norm, m_attn_w_out, m_dn_w_in, m_dn_conv, m_dn_a_log, m_dn_dt_bias, m_dn_o_norm, m_dn_w_out, m_ffn_w_up, m_ffn_conv, m_ffn_w_down, v_meta_tokens, v_norm_mix_pre, v_norm_mix_post, v_norm_ffn_pre, v_norm_ffn_post, v_attn_w_in, v_attn_b_forget, v_attn_q_norm, v_attn_k_norm, v_attn_w_out, v_dn_w_in, v_dn_conv, v_dn_a_log, v_dn_dt_bias, v_dn_o_norm, v_dn_w_out, v_ffn_w_up, v_ffn_conv, v_ffn_w_down):
    w = dict(meta_tokens=meta_tokens, norm_mix_pre=norm_mix_pre, norm_mix_post=norm_mix_post, norm_ffn_pre=norm_ffn_pre, norm_ffn_post=norm_ffn_post, attn_w_in=attn_w_in, attn_b_forget=attn_b_forget, attn_q_norm=attn_q_norm, attn_k_norm=attn_k_norm, attn_w_out=attn_w_out, dn_w_in=dn_w_in, dn_conv=dn_conv, dn_a_log=dn_a_log, dn_dt_bias=dn_dt_bias, dn_o_norm=dn_o_norm, dn_w_out=dn_w_out, ffn_w_up=ffn_w_up, ffn_conv=ffn_conv, ffn_w_down=ffn_w_down)
    m = dict(meta_tokens=m_meta_tokens, norm_mix_pre=m_norm_mix_pre, norm_mix_post=m_norm_mix_post, norm_ffn_pre=m_norm_ffn_pre, norm_ffn_post=m_norm_ffn_post, attn_w_in=m_attn_w_in, attn_b_forget=m_attn_b_forget, attn_q_norm=m_attn_q_norm, attn_k_norm=m_attn_k_norm, attn_w_out=m_attn_w_out, dn_w_in=m_dn_w_in, dn_conv=m_dn_conv, dn_a_log=m_dn_a_log, dn_dt_bias=m_dn_dt_bias, dn_o_norm=m_dn_o_norm, dn_w_out=m_dn_w_out, ffn_w_up=m_ffn_w_up, ffn_conv=m_ffn_conv, ffn_w_down=m_ffn_w_down)
    v = dict(meta_tokens=v_meta_tokens, norm_mix_pre=v_norm_mix_pre, norm_mix_post=v_norm_mix_post, norm_ffn_pre=v_norm_ffn_pre, norm_ffn_post=v_norm_ffn_post, attn_w_in=v_attn_w_in, attn_b_forget=v_attn_b_forget, attn_q_norm=v_attn_q_norm, attn_k_norm=v_attn_k_norm, attn_w_out=v_attn_w_out, dn_w_in=v_dn_w_in, dn_conv=v_dn_conv, dn_a_log=v_dn_a_log, dn_dt_bias=v_dn_dt_bias, dn_o_norm=v_dn_o_norm, dn_w_out=v_dn_w_out, ffn_w_up=v_ffn_w_up, ffn_conv=v_ffn_conv, ffn_w_down=v_ffn_w_down)
    loss, grad_x, grads, deltas, new_m, new_v = train_step(x, loss_target, w, m, v)
    return (loss, grad_x, *[grads[n] for n in WEIGHT_NAMES], *[deltas[n] for n in WEIGHT_NAMES],
            *[new_m[n] for n in WEIGHT_NAMES], *[new_v[n] for n in WEIGHT_NAMES])
```

```python
import functools

import jax
import jax.numpy as jnp
from jax import lax
from jax.experimental import pallas as pl
from jax.experimental.pallas import tpu as pltpu

F32, BF16 = jnp.float32, jnp.bfloat16

D_MODEL = 1024
N_META = 16
ATT_HEADS, ATT_DH = 16, 64
ATT_HD = ATT_HEADS * ATT_DH
AUG = 128
ATT_AUG = ATT_HEADS * AUG
DN_HEADS, DN_DH = 8, 128
DN_HD = DN_HEADS * DN_DH
DN_CHUNK = 64
FFN_DIM = 2816
DEPTH = 4
EPS = 1e-6
NEG = -1e30

ADAM_LR, ADAM_B1, ADAM_B2, ADAM_EPS, ADAM_WD, ADAM_STEP = 0.001, 0.9, 0.999, 1e-08, 0.01, 10

TM = 640
HALO = 8
VMEM_LIMIT = 52 * 1024 * 1024
MESH_ID = pl.DeviceIdType.MESH


def _cp(sem):
    return pltpu.CompilerParams(dimension_semantics=sem, vmem_limit_bytes=VMEM_LIMIT)


def _pick(n, cands):
    for c in cands:
        if n % c == 0:
            return c
    return n


def matmul(a, b, mode, out_dtype, name, add=None, ncols=None):
    layer = None
    if isinstance(b, tuple):
        b, layer = b
    bshape = b.shape[-2:]
    if mode == "nn":
        (M, K), N = a.shape, ncols or bshape[1]
    elif mode == "nt":
        (M, K), N = a.shape, bshape[0]
    else:
        (K, M), N = a.shape, bshape[1]
    tm = _pick(M, (1664, TM, 1024, 1408, 512, 256, 128))
    tn = _pick(N, (1024, 1408, 512, 256, 128))
    tk = _pick(K, (1664, TM, 1024, 1408, 512, 256, 128))
    nk = K // tk
    if mode == "nn":
        a_spec = pl.BlockSpec((tm, tk), lambda i, j, k: (i, k))
        b_spec = pl.BlockSpec((tk, tn), lambda i, j, k: (k, j))
        dims = (((1,), (0,)), ((), ()))
    elif mode == "nt":
        a_spec = pl.BlockSpec((tm, tk), lambda i, j, k: (i, k))
        b_spec = pl.BlockSpec((tn, tk), lambda i, j, k: (j, k))
        dims = (((1,), (1,)), ((), ()))
    else:
        a_spec = pl.BlockSpec((tk, tm), lambda i, j, k: (k, i))
        b_spec = pl.BlockSpec((tk, tn), lambda i, j, k: (k, j))
        dims = (((0,), (0,)), ((), ()))
    if layer is not None:
        b_spec = pl.BlockSpec((None,) + b_spec.block_shape, functools.partial(lambda i, j, k, f: (layer,) + f(i, j, k), f=b_spec.index_map))
    o_spec = pl.BlockSpec((tm, tn), lambda i, j, k: (i, j))
    has_add = add is not None

    def body(*refs):
        if has_add:
            a_ref, b_ref, add_ref, o_ref, acc_ref = refs
        else:
            a_ref, b_ref, o_ref, acc_ref = refs
        k = pl.program_id(2)
        part = lax.dot_general(a_ref[...], b_ref[...], dims, preferred_element_type=F32)

        @pl.when(k == 0)
        def _():
            acc_ref[...] = part

        @pl.when(k > 0)
        def _():
            acc_ref[...] += part

        @pl.when(k == nk - 1)
        def _():
            r = acc_ref[...]
            if has_add:
                r = r + add_ref[...].astype(F32)
            o_ref[...] = r.astype(out_dtype)

    in_specs = [a_spec, b_spec] + ([o_spec] if has_add else [])
    args = (a, b) + ((add,) if has_add else ())
    return pl.pallas_call(
        body, name=name, grid=(M // tm, N // tn, nk), in_specs=in_specs, out_specs=o_spec,
        out_shape=jax.ShapeDtypeStruct((M, N), out_dtype),
        scratch_shapes=[pltpu.VMEM((tm, tn), F32)],
        compiler_params=_cp(("parallel", "parallel", "arbitrary")),
    )(*args)


def tok(arr, width=None, col=0, conv=None, grad=None, fixed=False):
    return dict(arr=arr, w=arr.shape[1] if width is None else width, col=col, conv=conv, grad=grad, step=0 if fixed else 1)


def par(arr, width=None, col=0):
    return dict(arr=arr, w=arr.shape[1] if width is None else width, col=col)


def _conv_apply(x, halo, w, ext_ref):
    K = w.shape[0]
    rows = x.shape[0]
    ext_ref[0:HALO, :] = halo
    ext_ref[HALO:, :] = x
    y = w[K - 1:K, :] * x
    for k in range(K - 1):
        y = y + w[k:k + 1, :] * ext_ref[pl.ds(HALO - (K - 1) + k, rows), :]
    return y


WIDE_TILE = 1664


def _row_tile(Tp, wide):
    return WIDE_TILE if wide and Tp % WIDE_TILE == 0 else TM


def _row_specs(toks, pars, tile):
    specs, args = [], []
    for t in toks:
        specs.append(pl.BlockSpec((tile, t["w"]), functools.partial(lambda i, j, c, st: (i, c + st * j), c=t["col"], st=t["step"])))
        args.append(t["arr"])
        if t["conv"] is not None:
            specs.append(pl.BlockSpec((HALO, t["w"]), functools.partial(
                lambda i, j, c: (jnp.maximum(i * (tile // HALO) - 1, 0), c + j), c=t["col"])))
            args.append(t["arr"])
            cw, ccol = t["conv"]
            specs.append(pl.BlockSpec((cw.shape[0], t["w"]), functools.partial(lambda i, j, c: (0, c + j), c=ccol)))
            args.append(cw)
    for p in pars:
        specs.append(pl.BlockSpec((p["arr"].shape[0], p["w"]), functools.partial(lambda i, j, c: (0, c), c=p["col"])))
        args.append(p["arr"])
    return specs, args


def _row_load(toks, pars, refs, ext_refs):
    i = pl.program_id(0)
    vals, n, e = [], 0, 0
    for t in toks:
        x = refs[n][...].astype(F32)
        n += 1
        if t["conv"] is not None:
            halo = jnp.where(i == 0, 0.0, refs[n][...].astype(F32))
            w = refs[n + 1][...]
            n += 2
            x = _conv_apply(x, halo, w, ext_refs[e])
            e += 1
        vals.append(x)
    for _ in pars:
        vals.append(refs[n][...])
        n += 1
    return vals, n


def rowwise(fn, toks, pars, outs, name, nsplit=1, with_j=False, wide=False):
    Tp = toks[0]["arr"].shape[0]
    tile = _row_tile(Tp, wide)
    specs, args = _row_specs(toks, pars, tile)
    n_ext = sum(t["conv"] is not None for t in toks)

    def body(*refs):
        ext_refs = refs[len(refs) - n_ext:]
        vals, n = _row_load(toks, pars, refs, ext_refs)
        res = fn(pl.program_id(1), *vals) if with_j else fn(*vals)
        plain = 0
        for o, o_ref in zip(outs, refs[n:n + len(outs)], strict=True):
            if len(o) == 2:
                o_ref[...] = res[plain].astype(o_ref.dtype)
                plain += 1
            else:
                o_ref[...] = res[o[2]].T.astype(o_ref.dtype)

    return pl.pallas_call(
        body, name=name, grid=(Tp // tile, nsplit), in_specs=specs,
        out_specs=[pl.BlockSpec((tile, o[0]), lambda i, j: (i, j)) if len(o) == 2 else pl.BlockSpec((o[0], tile), lambda i, j: (j, i))
                   for o in outs],
        out_shape=[jax.ShapeDtypeStruct((Tp, o[0] * nsplit) if len(o) == 2 else (o[0] * nsplit, Tp), o[1]) for o in outs],
        scratch_shapes=[pltpu.VMEM((tile + HALO, t["w"]), F32) for t in toks if t["conv"] is not None],
        compiler_params=_cp(("parallel", "parallel")),
    )(*args)


def rowwise_bwd(fn, toks, pars, cts, name, groups=None, par_grads=None, nsplit=1, with_j=False, wide=False):
    Tp = toks[0]["arr"].shape[0]
    tile = _row_tile(Tp, wide)
    specs, args = _row_specs(toks, pars, tile)
    n_ext = sum(t["conv"] is not None for t in toks)
    gidx = [k for k, t in enumerate(toks) if t["grad"] is not None]
    if groups is None:
        groups = [[k] for k in range(len(gidx))]
    par_grads = list(range(len(pars))) if par_grads is None else par_grads
    ct_specs, ct_args = [], []
    for c in cts:
        if c is not None:
            ct_specs.append(pl.BlockSpec((tile, c.shape[1] // nsplit), lambda i, j: (i, j)))
            ct_args.append(c)
    out_shapes, out_specs = [], []
    for g in groups:
        w = sum(toks[gidx[k]]["w"] for k in g)
        out_shapes.append(jax.ShapeDtypeStruct((Tp, w * nsplit), toks[gidx[g[0]]]["grad"]))
        out_specs.append(pl.BlockSpec((tile, w), lambda i, j: (i, j)))
    for k in par_grads:
        shp = (pars[k]["arr"].shape[0], pars[k]["w"])
        out_shapes.append(jax.ShapeDtypeStruct(shp, F32))
        out_specs.append(pl.BlockSpec(shp, lambda i, j: (0, 0)))
    n_ct = len(ct_args)

    def body(*refs):
        first = (pl.program_id(0) == 0) & (pl.program_id(1) == 0)
        ext_refs = refs[len(refs) - n_ext:]
        vals, n = _row_load(toks, pars, refs, ext_refs)
        ct_refs = refs[n:n + n_ct]
        o_refs = refs[n + n_ct:len(refs) - n_ext]
        res, vjp = jax.vjp(functools.partial(fn, pl.program_id(1)) if with_j else fn, *vals)
        ct_vals, c = [], 0
        for r, ct in zip(res, cts, strict=True):
            if ct is None:
                ct_vals.append(jnp.zeros_like(r))
            else:
                ct_vals.append(ct_refs[c][...].astype(F32))
                c += 1
        grads = vjp(tuple(ct_vals))
        for g, o_ref in zip(groups, o_refs[:len(groups)], strict=True):
            pieces = [grads[gidx[k]] for k in g]
            val = pieces[0] if len(pieces) == 1 else jnp.concatenate(pieces, axis=1)
            o_ref[...] = val.astype(o_ref.dtype)
        for k, o_ref in zip(par_grads, o_refs[len(groups):], strict=True):
            gk = grads[len(toks) + k]

            @pl.when(first)
            def _(o_ref=o_ref, gk=gk):
                o_ref[...] = gk

            @pl.when(jnp.logical_not(first))
            def _(o_ref=o_ref, gk=gk):
                o_ref[...] += gk

    res = pl.pallas_call(
        body, name=name, grid=(Tp // tile, nsplit), in_specs=specs + ct_specs, out_specs=out_specs, out_shape=out_shapes,
        scratch_shapes=[pltpu.VMEM((tile + HALO, t["w"]), F32) for t in toks if t["conv"] is not None],
        compiler_params=_cp(("arbitrary", "arbitrary")),
    )(*args, *ct_args)
    return res[:len(groups)], res[len(groups):]


ROWS = 16


def _shift_rows(win, s, lo, rows):
    return pltpu.roll(win, (-s) % win.shape[0], 0)[lo:lo + rows, :] if s else win[lo:lo + rows, :]


def conv_bwd(dy, x, w, name):
    Tp, W = dy.shape
    K = w.shape[0]
    wb = _pick(W, (512, 256, 128))
    tile = _row_tile(Tp, True)
    nt = Tp // tile

    def body(dy_ref, dyn_ref, x_ref, w_ref, dx_ref, dw_ref, ext_dy):
        i = pl.program_id(1)
        ext_dy[0:tile, :] = dy_ref[...].astype(F32)
        ext_dy[tile:, :] = jnp.where(i == nt - 1, 0.0, dyn_ref[...].astype(F32))
        wv = w_ref[...]

        def chunk(t, sums):
            r0 = pl.multiple_of(t * ROWS, ROWS)
            dyw = ext_dy[pl.ds(r0, ROWS + HALO), :]
            xv = x_ref[pl.ds(r0, ROWS), :].astype(F32)
            dx, new = None, []
            for k in range(K):
                dys = _shift_rows(dyw, K - 1 - k, 0, ROWS)
                term = wv[k:k + 1, :] * dys
                dx = term if dx is None else dx + term
                prod = dys * xv
                new.append(sums[k] + (prod[0:HALO, :] + prod[HALO:ROWS, :]))
            dx_ref[pl.ds(r0, ROWS), :] = dx.astype(dx_ref.dtype)
            return tuple(new)

        sums = lax.fori_loop(0, tile // ROWS, chunk, tuple(jnp.zeros((HALO, wb), F32) for _ in range(K)))
        dwv = jnp.concatenate([jnp.sum(sm, axis=0, keepdims=True) for sm in sums], axis=0)

        @pl.when(i == 0)
        def _():
            dw_ref[...] = dwv

        @pl.when(i > 0)
        def _():
            dw_ref[...] += dwv

    return pl.pallas_call(
        body, name=name, grid=(W // wb, nt),
        in_specs=[
            pl.BlockSpec((tile, wb), lambda j, i: (i, j)),
            pl.BlockSpec((HALO, wb), lambda j, i: (jnp.minimum((i + 1) * (tile // HALO), Tp // HALO - 1), j)),
            pl.BlockSpec((tile, wb), lambda j, i: (i, j)),
            pl.BlockSpec((K, wb), lambda j, i: (0, j)),
        ],
        out_specs=[pl.BlockSpec((tile, wb), lambda j, i: (i, j)), pl.BlockSpec((K, wb), lambda j, i: (0, j))],
        out_shape=[jax.ShapeDtypeStruct((Tp, W), BF16), jax.ShapeDtypeStruct((K, W), F32)],
        scratch_shapes=[pltpu.VMEM((tile + HALO, wb), F32)],
        compiler_params=_cp(("parallel", "arbitrary")),
    )(dy, dy, x, w)


def _rms(x, g):
    return x * lax.rsqrt(jnp.mean(x * x, axis=-1, keepdims=True) + EPS) * g


def fn_norm(h, g):
    return (_rms(h, g),)


def fn_id_norm(h, g):
    return h, _rms(h, g)


def fn_resid_norm(h, m, g_post, g_next):
    h2 = h + _rms(m, g_post)
    return h2, _rms(h2, g_next)


def fn_resid(h, m, g_post):
    return (h + _rms(m, g_post),)


def fn_geglu(y):
    w = y.shape[1] // 2
    return (jax.nn.gelu(y[:, :w], approximate=True) * y[:, w:],)


def fn_geglu_keep(y):
    return fn_geglu(y) + (y,)


def _split3(c):
    hi = c.astype(BF16).astype(F32)
    r = c - hi
    mid = r.astype(BF16).astype(F32)
    lo = (r - mid).astype(BF16).astype(F32)
    return hi, mid, lo


def _aug_cols(rows, entries):
    lane = lax.broadcasted_iota(jnp.int32, (rows, ATT_DH), 1)
    out = jnp.zeros((rows, ATT_DH), F32)
    for ln, val in entries:
        out = jnp.where(lane == ln, val, out)
    return out


def _head_col(c, idx):
    lane = lax.broadcasted_iota(jnp.int32, c.shape, 1)
    return jnp.sum(jnp.where(lane == idx, c, 0.0), axis=1, keepdims=True)


def fn_attn_prep_q(j, xp, c, gain):
    rows = xp.shape[0]
    out = []
    for hh in range(2):
        hi, mid, lo = _split3(lax.stop_gradient(_head_col(c, 2 * j + hh)))
        out += [_rms(xp[:, ATT_DH * hh:ATT_DH * (hh + 1)], gain) * (ATT_DH ** -0.5),
                _aug_cols(rows, [(0, hi), (1, mid), (2, lo), (3, 1.0), (4, 1.0), (5, 1.0)])]
    return (jnp.concatenate(out, axis=1),)


def fn_attn_prep_k(j, xp, c, gain):
    rows = xp.shape[0]
    out = []
    for hh in range(2):
        hi, mid, lo = _split3(lax.stop_gradient(_head_col(c, 2 * j + hh)))
        out += [_rms(xp[:, ATT_DH * hh:ATT_DH * (hh + 1)], gain),
                _aug_cols(rows, [(0, 1.0), (1, 1.0), (2, 1.0), (3, -hi), (4, -mid), (5, -lo), (6, 1.0), (7, 1.0), (8, 1.0)])]
    return (jnp.concatenate(out, axis=1),)


def fn_attn_prep_v(xp):
    rows = xp.shape[0]
    out = []
    for hh in range(2):
        out += [xp[:, ATT_DH * hh:ATT_DH * (hh + 1)], _aug_cols(rows, [(0, 1.0), (1, 1.0), (2, 1.0)])]
    return (jnp.concatenate(out, axis=1),)


def fn_attn_gate(o_aug, og):
    outs = []
    for h in range(og.shape[1] // ATT_DH):
        o = o_aug[:, AUG * h:AUG * h + ATT_DH]
        outs.append(o * jax.nn.sigmoid(og[:, ATT_DH * h:ATT_DH * (h + 1)]))
    return (jnp.concatenate(outs, axis=1),)


def fn_attn_bwd_prep(dgo, o_aug, og, q_aug):
    rows = dgo.shape[0]
    lane = lax.broadcasted_iota(jnp.int32, (rows, ATT_DH), 1)
    dos, dogs, qs = [], [], []
    for h in range(og.shape[1] // ATT_DH):
        sl = slice(ATT_DH * h, ATT_DH * (h + 1))
        o = o_aug[:, AUG * h:AUG * h + ATT_DH]
        lse = o_aug[:, AUG * h + ATT_DH:AUG * h + ATT_DH + 1]
        sig = jax.nn.sigmoid(og[:, sl])
        do = dgo[:, sl] * sig
        dogs.append(dgo[:, sl] * o * sig * (1.0 - sig))
        dhi, dmid, dlo = _split3(-jnp.sum(do * o, axis=-1, keepdims=True))
        dos += [do, _aug_cols(rows, [(0, dhi), (1, dmid), (2, dlo)])]
        lhi, lmid, llo = _split3(-lse)
        qa = q_aug[:, AUG * h + ATT_DH:AUG * (h + 1)]
        qa = jnp.where(lane == 6, lhi, jnp.where(lane == 7, lmid, jnp.where(lane == 8, llo, qa)))
        qs += [q_aug[:, AUG * h:AUG * h + ATT_DH], qa]
    return jnp.concatenate(dos, axis=1), jnp.concatenate(dogs, axis=1), jnp.concatenate(qs, axis=1)


def fn_attn_dc(dq_aug, dk_aug):
    lane = lax.broadcasted_iota(jnp.int32, (dk_aug.shape[0], AUG), 1)
    dc = jnp.zeros((dk_aug.shape[0], AUG), F32)
    for h in range(ATT_HEADS):
        col = AUG * h + ATT_DH
        dc = jnp.where(lane == h, dq_aug[:, col:col + 1] - dk_aug[:, col + 3:col + 4], dc)
    return (dc,)


def fn_dn_prep_keep(j, y):
    return fn_dn_prep(j, y) + (y,)


def fn_dn_prep(j, y):
    s = jax.nn.silu(y)
    scale = jnp.where(j < DN_HD // 256, DN_DH ** -0.5, 1.0)
    out = []
    for hh in range(2):
        sh = s[:, DN_DH * hh:DN_DH * (hh + 1)]
        n = sh * lax.rsqrt(jnp.sum(sh * sh, axis=-1, keepdims=True) + EPS) * scale
        out.append(jnp.where(j < 2 * (DN_HD // 256), n, sh))
    return (jnp.concatenate(out, axis=1),)


def fn_dn_gates(lg, alog, dtb):
    lane = lax.broadcasted_iota(jnp.int32, lg.shape, 1)
    beta = jax.nn.sigmoid(lg)
    g = -jnp.exp(alog) * jax.nn.softplus(lg + dtb)
    return (jnp.where(lane < DN_HEADS, beta, jnp.where(lane < 2 * DN_HEADS, g, 0.0)),)


def fn_dn_post(o, og, gain):
    outs = []
    for h in range(DN_HEADS):
        sl = slice(DN_DH * h, DN_DH * (h + 1))
        outs.append(_rms(o[:, sl], gain) * jax.nn.silu(og[:, sl]))
    return (jnp.concatenate(outs, axis=1),)


def _tri(n, lower):
    r = lax.broadcasted_iota(jnp.int32, (n, n), 0)
    c = lax.broadcasted_iota(jnp.int32, (n, n), 1)
    return jnp.where((r >= c) if lower else (r <= c), 1.0, 0.0).astype(F32)


def forget_cumsum(fl, bias, name):
    Tp, W = fl.shape

    def body(fl_ref, b_ref, c_ref, carry):
        i = pl.program_id(0)

        @pl.when(i == 0)
        def _():
            carry[...] = jnp.zeros_like(carry)

        logf = jax.nn.log_sigmoid(fl_ref[...] + b_ref[...])
        c = jnp.dot(_tri(TM, True), logf, precision=lax.Precision.HIGHEST, preferred_element_type=F32) + carry[...]
        c_ref[...] = c
        carry[...] = c[TM - 1:TM, :]

    return pl.pallas_call(
        body, name=name, grid=(Tp // TM,),
        in_specs=[pl.BlockSpec((TM, W), lambda i: (i, 0)), pl.BlockSpec((1, W), lambda i: (0, 0))],
        out_specs=pl.BlockSpec((TM, W), lambda i: (i, 0)), out_shape=jax.ShapeDtypeStruct((Tp, W), F32),
        scratch_shapes=[pltpu.VMEM((1, W), F32)], compiler_params=_cp(("arbitrary",)),
    )(fl, bias)


def forget_cumsum_bwd(dc, fl, bias, name):
    Tp, W = fl.shape
    nt = Tp // TM

    def body(dc_ref, fl_ref, b_ref, dfl_ref, db_ref, carry):
        i = pl.program_id(0)

        @pl.when(i == 0)
        def _():
            carry[...] = jnp.zeros_like(carry)

        dlogf = jnp.dot(_tri(TM, False), dc_ref[...], precision=lax.Precision.HIGHEST, preferred_element_type=F32) + carry[...]
        carry[...] = dlogf[0:1, :]
        dfl = dlogf * jax.nn.sigmoid(-(fl_ref[...] + b_ref[...]))
        dfl_ref[...] = dfl.astype(dfl_ref.dtype)
        s = jnp.sum(dfl, axis=0, keepdims=True)

        @pl.when(i == 0)
        def _():
            db_ref[...] = s

        @pl.when(i > 0)
        def _():
            db_ref[...] += s

    rev = lambda i: (nt - 1 - i, 0)
    return pl.pallas_call(
        body, name=name, grid=(nt,),
        in_specs=[pl.BlockSpec((TM, W), rev), pl.BlockSpec((TM, W), rev), pl.BlockSpec((1, W), lambda i: (0, 0))],
        out_specs=[pl.BlockSpec((TM, W), rev), pl.BlockSpec((1, W), lambda i: (0, 0))],
        out_shape=[jax.ShapeDtypeStruct((Tp, W), BF16), jax.ShapeDtypeStruct((1, W), F32)],
        scratch_shapes=[pltpu.VMEM((1, W), F32)], compiler_params=_cp(("arbitrary",)),
    )(dc, fl, bias)


_HBM = pl.BlockSpec(memory_space=pltpu.HBM)


def _place():
    x, y, c = lax.axis_index("x"), lax.axis_index("y"), lax.axis_index("c")
    return x, y, c, [(1 - x, y), (x, 1 - y), (1 - x, 1 - y)]


def _chip_copies(scatter, ins, outs, send_sems, recv_sems, local_sems):
    x, y, c, chips = _place()
    s = 2 * x + y
    copies = []
    for a in range(len(ins)):
        copies.append(pltpu.make_async_copy(ins[a].at[s] if scatter else ins[a], outs[a].at[s], local_sems.at[a]))
        for p, (px, py) in enumerate(chips):
            copies.append(pltpu.make_async_remote_copy(
                src_ref=ins[a].at[2 * px + py] if scatter else ins[a], dst_ref=outs[a].at[s], send_sem=send_sems.at[a, p],
                recv_sem=recv_sems.at[a, p], device_id=(px, py, c), device_id_type=MESH_ID))
    return copies


def _chip_exchange_shapes(scatter, arrs):
    n = len(arrs)
    out_shape = [jax.ShapeDtypeStruct(a.shape if scatter else (4,) + a.shape, a.dtype) for a in arrs]
    sems = [pltpu.SemaphoreType.DMA((n, 3)), pltpu.SemaphoreType.DMA((n, 3)), pltpu.SemaphoreType.DMA((n,))]
    return out_shape, sems


def exchange_chips(scatter, arrs, name):
    n = len(arrs)

    def body(*refs):
        copies = _chip_copies(scatter, refs[:n], refs[n:2 * n], *refs[2 * n:])
        for cp in copies:
            cp.start()
        for cp in copies:
            cp.wait()

    out_shape, sems = _chip_exchange_shapes(scatter, arrs)
    return pl.pallas_call(body, name=name, in_specs=[_HBM] * n, out_specs=[_HBM] * n, out_shape=out_shape, scratch_shapes=sems)(*arrs)


def swap_cores(arrs, name):
    n = len(arrs)

    def body(*refs):
        ins, outs = refs[:n], refs[n:2 * n]
        send_sems, recv_sems = refs[2 * n:]
        x, y, c, _ = _place()
        copies = []
        for a in range(n):
            cp = pltpu.make_async_remote_copy(
                src_ref=ins[a], dst_ref=outs[a], send_sem=send_sems.at[a], recv_sem=recv_sems.at[a],
                device_id=(x, y, 1 - c), device_id_type=MESH_ID)
            cp.start()
            copies.append(cp)
        for cp in copies:
            cp.wait()

    return pl.pallas_call(
        body, name=name, in_specs=[_HBM] * n, out_specs=[_HBM] * n,
        out_shape=[jax.ShapeDtypeStruct(a.shape, a.dtype) for a in arrs],
        scratch_shapes=[pltpu.SemaphoreType.DMA((n,)), pltpu.SemaphoreType.DMA((n,))],
    )(*arrs)


_NT = (((1,), (1,)), ((), ()))
_TN = (((0,), (0,)), ((), ()))


def _cargo_edges(scatter, n, refs, nb):
    first = (pl.program_id(0) == 0) & (pl.program_id(1) == 0)
    last = (pl.program_id(0) == ATT_HEADS - 1) & (pl.program_id(1) == nb - 1)

    @pl.when(first)
    def _():
        for cp in _chip_copies(scatter, refs[:n], refs[n:2 * n], *refs[2 * n:]):
            cp.start()

    @pl.when(last)
    def _():
        for cp in _chip_copies(scatter, refs[:n], refs[n:2 * n], *refs[2 * n:]):
            cp.wait()


def flash_fwd(q_aug, k_aug, v_aug, name, cargo=()):
    Tp = q_aug.shape[0]
    nb = Tp // TM
    nc = len(cargo)

    def body(*refs):
        q_ref, k_ref, v_ref = refs[:3]
        o_ref = refs[3 + nc]
        if nc:
            _cargo_edges(False, nc, refs[3:3 + nc] + refs[4 + nc:], nb)
        i = pl.program_id(1)
        q = q_ref[...]

        def rows(j):
            return pl.ds(pl.multiple_of(j * TM, TM), TM)

        def scores(j):
            return lax.dot_general(q, k_ref[rows(j), :], _NT, preferred_element_type=F32)

        def absorb(j, s, m_old, acc):
            m_new = jnp.maximum(m_old, jnp.max(s, axis=-1, keepdims=True))
            p = jnp.exp(s - m_new)
            acc = jnp.exp(m_old - m_new) * acc + jnp.dot(p.astype(BF16), v_ref[rows(j), :], preferred_element_type=F32)
            return m_new, acc

        def step(j, carry):
            m_old, acc, s = carry
            s_next = scores(j + 1)
            return absorb(j, s, m_old, acc) + (s_next,)

        m, acc, s = lax.fori_loop(0, i, step, (jnp.full((TM, 1), NEG, F32), jnp.zeros((TM, AUG), F32), scores(0)))
        r = lax.broadcasted_iota(jnp.int32, (TM, TM), 0)
        c = lax.broadcasted_iota(jnp.int32, (TM, TM), 1)
        m, acc = absorb(i, jnp.where(c <= r, s, NEG), m, acc)
        l = acc[:, ATT_DH:ATT_DH + 1]
        lane = lax.broadcasted_iota(jnp.int32, (TM, AUG), 1)
        o_ref[...] = jnp.where(lane < ATT_DH, acc / l, m + jnp.log(l))

    head = pl.BlockSpec((Tp, AUG), lambda h, i: (0, h))
    cargo_shape, sems = _chip_exchange_shapes(False, cargo) if nc else ([], [])
    res = pl.pallas_call(
        body, name=name, grid=(ATT_HEADS, nb),
        in_specs=[pl.BlockSpec((TM, AUG), lambda h, i: (i, h)), head, head] + [_HBM] * nc,
        out_specs=[pl.BlockSpec((TM, AUG), lambda h, i: (i, h))] + [_HBM] * nc,
        out_shape=[jax.ShapeDtypeStruct((Tp, ATT_AUG), F32)] + cargo_shape, scratch_shapes=sems,
        compiler_params=_cp(("arbitrary", "arbitrary")),
    )(q_aug, k_aug, v_aug, *cargo)
    return res[0], res[1:]


def flash_bwd(q_aug2, k_aug, v_aug, do_aug, name, cargo=()):
    Tp = q_aug2.shape[0]
    nb = Tp // TM
    nc = len(cargo)

    def body(*refs):
        q_ref, do_ref, k_ref, v_ref = refs[:4]
        dq_ref, dk_ref, dv_ref = refs[4 + nc:7 + nc]
        if nc:
            _cargo_edges(True, nc, refs[4:4 + nc] + refs[7 + nc:], nb)
        j = pl.program_id(1)
        k, v = k_ref[...], v_ref[...]

        @pl.when(j == 0)
        def _():
            dq_ref[...] = jnp.zeros_like(dq_ref)

        def block(i, carry, masked):
            dk, dv = carry
            rows = pl.ds(pl.multiple_of(i * TM, TM), TM)
            q, do = q_ref[rows, :], do_ref[rows, :]
            st = lax.dot_general(k, q, _NT, preferred_element_type=F32)
            if masked:
                r = lax.broadcasted_iota(jnp.int32, (TM, TM), 0)
                c = lax.broadcasted_iota(jnp.int32, (TM, TM), 1)
                st = jnp.where(r <= c, st, NEG)
            pt = jnp.exp(st)
            dpt = lax.dot_general(v, do, _NT, preferred_element_type=F32)
            dst = (pt * dpt).astype(BF16)
            dv = dv + jnp.dot(pt.astype(BF16), do, preferred_element_type=F32)
            dk = dk + jnp.dot(dst, q, preferred_element_type=F32)
            dq_ref[rows, :] += lax.dot_general(dst, k, _TN, preferred_element_type=F32)
            return dk, dv

        carry = block(j, (jnp.zeros((TM, AUG), F32), jnp.zeros((TM, AUG), F32)), True)
        dk, dv = lax.fori_loop(j + 1, nb, lambda i, cr: block(i, cr, False), carry)
        dk_ref[...] = dk
        dv_ref[...] = dv

    head = pl.BlockSpec((Tp, AUG), lambda h, j: (0, h))
    blk = pl.BlockSpec((TM, AUG), lambda h, j: (j, h))
    cargo_shape, sems = _chip_exchange_shapes(True, cargo) if nc else ([], [])
    res = pl.pallas_call(
        body, name=name, grid=(ATT_HEADS, nb),
        in_specs=[head, head, blk, blk] + [_HBM] * nc, out_specs=[head, blk, blk] + [_HBM] * nc,
        out_shape=[jax.ShapeDtypeStruct((Tp, ATT_AUG), F32)] * 3 + cargo_shape, scratch_shapes=sems,
        compiler_params=_cp(("arbitrary", "arbitrary")),
    )(q_aug2, do_aug, k_aug, v_aug, *cargo)
    return res[:3], res[3:]


_BATCH = ((0,), (0,))


def _dg(x, y, cx, cy):
    return lax.dot_general(x.astype(BF16), y.astype(BF16), (((cx + 1,), (cy + 1,)), _BATCH), preferred_element_type=F32)


def _split2(x):
    hi = x.astype(BF16)
    return hi, (x - hi.astype(F32)).astype(BF16)


def _dg3(x, y, cx, cy):
    (xh, xl), (yh, yl) = _split2(x), _split2(y)
    d = lambda a, b: lax.dot_general(a, b, (((cx + 1,), (cy + 1,)), _BATCH), preferred_element_type=F32)
    return d(xh, yh) + (d(xl, yh) + d(xh, yl))


def _make_bdot(ca, cb, dg):
    @jax.custom_vjp
    def f(a, b):
        return dg(a, b, ca, cb)

    def fwd(a, b):
        return dg(a, b, ca, cb), (a, b)

    def bwd(res, ct):
        a, b = res
        da = dg(ct, b, 1, 1 if cb == 0 else 0) if ca == 1 else dg(b, ct, 1 if cb == 0 else 0, 1)
        db = dg(a, ct, 0 if ca == 1 else 1, 0) if cb == 0 else dg(ct, a, 0, 0 if ca == 1 else 1)
        return da, db

    f.defvjp(fwd, bwd)
    return f


_bd_nn, _bd_nt, _bd_tn = _make_bdot(1, 0, _dg), _make_bdot(1, 1, _dg), _make_bdot(0, 0, _dg)
_bd3_nn = _make_bdot(1, 0, _dg3)


@jax.custom_vjp
def _chunk_cumsum(x):
    return jnp.dot(_tri(x.shape[0], True), x, precision=lax.Precision.HIGHEST, preferred_element_type=F32)


def _chunk_cumsum_fwd(x):
    return _chunk_cumsum(x), None


def _chunk_cumsum_bwd(_, ct):
    return (jnp.dot(_tri(ct.shape[0], False), ct, precision=lax.Precision.HIGHEST, preferred_element_type=F32),)


_chunk_cumsum.defvjp(_chunk_cumsum_fwd, _chunk_cumsum_bwd)


@jax.custom_vjp
def _inverse_given(A, N):
    return N


def _inverse_given_fwd(A, N):
    return N, N


def _inverse_given_bwd(N, ct):
    M = ct + _dg3(N, ct, 0, 0)
    return -(M + _dg3(M, N, 1, 1)), jnp.zeros_like(N)


_inverse_given.defvjp(_inverse_given_fwd, _inverse_given_bwd)


def _dn_heads(t):
    return jnp.stack([t[:, DN_DH * h:DN_DH * (h + 1)] for h in range(DN_HEADS)])


def _dn_gates(bg):
    cols = lambda t, o: jnp.stack([t[:, o + h:o + h + 1] for h in range(DN_HEADS)])
    return cols(bg, 0), cols(_chunk_cumsum(bg), DN_HEADS)


def _dn_decay(gc):
    H, C = gc.shape[0], gc.shape[1]
    r = lax.broadcasted_iota(jnp.int32, (H, C, C), 1)
    c = lax.broadcasted_iota(jnp.int32, (H, C, C), 2)
    gcb = jnp.broadcast_to(gc, (H, C, C))
    return jnp.exp(jnp.where(r >= c, gcb - jnp.swapaxes(gcb, 1, 2), NEG)), r > c


def dn_inverse(kh, beta, gc):
    C = kh.shape[1]
    dec, strict = _dn_decay(gc)
    X = -jnp.where(strict, _bd_nt(kh, kh) * dec * beta, 0.0)
    P = _bd3_nn(X, X)
    N = X
    for i in range(5):
        NP = _bd3_nn(jnp.concatenate([N, P], axis=1) if i < 4 else N, P)
        N = N + P + NP[:, :C]
        if i < 4:
            P = NP[:, C:]
    return N


def dn_chunk(S, q, k, v, bg, N_given):
    C = q.shape[0]
    H = DN_HEADS
    qh, kh, vh = _dn_heads(q), _dn_heads(k), _dn_heads(v)
    beta, gc = _dn_gates(bg)
    dec, strict = _dn_decay(gc)
    e_gc = jnp.exp(gc)
    kq = _bd_nt(jnp.concatenate([kh, qh], axis=1), kh)
    A = jnp.where(strict, kq[:, :C] * dec * beta, 0.0)
    qk = kq[:, C:] * dec
    N = _inverse_given(A, N_given)
    R = jnp.concatenate([kh * (beta * e_gc), vh * beta], axis=2)
    WU = R + _bd3_nn(N, R)
    W, U0 = WU[:, :, :DN_DH], WU[:, :, DN_DH:]
    gl = gc[:, C - 1:C, :]
    WqS = _bd_nt(jnp.concatenate([W, qh * e_gc], axis=1), S)
    U = U0 - WqS[:, :C]
    O = WqS[:, C:] + _bd_nn(qk, U)
    S_new = jnp.exp(gl) * S + _bd_tn(U, kh * jnp.exp(gl - gc))
    return jnp.concatenate([O[h] for h in range(H)], axis=1), S_new


DN_STEP = 2


def dn_scan_fwd(qkv, bg, name):
    Tp = qkv.shape[0]
    n = Tp // DN_CHUNK
    G = DN_STEP
    tokspec = lambda w, c=0: pl.BlockSpec((G * DN_CHUNK, w), lambda i: (i, c))

    def body(q_ref, k_ref, v_ref, bg_ref, o_ref, s_ref, n_ref, state):
        @pl.when(pl.program_id(0) == 0)
        def _():
            state[...] = jnp.zeros_like(state)

        rows = [slice(g * DN_CHUNK, (g + 1) * DN_CHUNK) for g in range(G)]
        gates = [_dn_gates(bg_ref[r, :]) for r in rows]
        N_all = dn_inverse(jnp.concatenate([_dn_heads(k_ref[r, :].astype(F32)) for r in rows], axis=0),
                           jnp.concatenate([b for b, _ in gates], axis=0), jnp.concatenate([c for _, c in gates], axis=0))
        S = state[...]
        for g, r in enumerate(rows):
            N = N_all[DN_HEADS * g:DN_HEADS * (g + 1)]
            s_ref[g] = S
            n_ref[g] = N
            O, S = dn_chunk(S, q_ref[r, :].astype(F32), k_ref[r, :].astype(F32), v_ref[r, :].astype(F32), bg_ref[r, :], N)
            o_ref[r, :] = O.astype(o_ref.dtype)
        state[...] = S

    return pl.pallas_call(
        body, name=name, grid=(n // G,),
        in_specs=[tokspec(DN_HD, 0), tokspec(DN_HD, 1), tokspec(DN_HD, 2), tokspec(AUG)],
        out_specs=[tokspec(DN_HD), pl.BlockSpec((G, DN_HEADS, DN_DH, DN_DH), lambda i: (i, 0, 0, 0)),
                   pl.BlockSpec((G, DN_HEADS, DN_CHUNK, DN_CHUNK), lambda i: (i, 0, 0, 0))],
        out_shape=[jax.ShapeDtypeStruct((Tp, DN_HD), BF16), jax.ShapeDtypeStruct((n, DN_HEADS, DN_DH, DN_DH), F32),
                   jax.ShapeDtypeStruct((n, DN_HEADS, DN_CHUNK, DN_CHUNK), F32)],
        scratch_shapes=[pltpu.VMEM((DN_HEADS, DN_DH, DN_DH), F32)],
        compiler_params=_cp(("arbitrary",)),
    )(qkv, qkv, qkv, bg)


def dn_scan_bwd(qkv, bg, states, inverses, dO, name):
    Tp = qkv.shape[0]
    n = Tp // DN_CHUNK
    tokspec = lambda w, c=0: pl.BlockSpec((DN_CHUNK, w), lambda i: (n - 1 - i, c))

    def body(q_ref, k_ref, v_ref, bg_ref, s_ref, n_ref, do_ref, dqkv_ref, dbg_ref, dstate):
        @pl.when(pl.program_id(0) == 0)
        def _():
            dstate[...] = jnp.zeros_like(dstate)

        N = n_ref[0]
        _, vjp = jax.vjp(lambda *xs: dn_chunk(*xs, N),
                         s_ref[0], q_ref[...].astype(F32), k_ref[...].astype(F32), v_ref[...].astype(F32), bg_ref[...])
        dS, dq, dk, dv, dbg = vjp((do_ref[...].astype(F32), dstate[...]))
        dqkv_ref[...] = jnp.concatenate([dq, dk, dv], axis=1).astype(dqkv_ref.dtype)
        dbg_ref[...] = dbg
        dstate[...] = dS

    return pl.pallas_call(
        body, name=name, grid=(n,),
        in_specs=[tokspec(DN_HD, 0), tokspec(DN_HD, 1), tokspec(DN_HD, 2), tokspec(AUG),
                  pl.BlockSpec((1, DN_HEADS, DN_DH, DN_DH), lambda i: (n - 1 - i, 0, 0, 0)),
                  pl.BlockSpec((1, DN_HEADS, DN_CHUNK, DN_CHUNK), lambda i: (n - 1 - i, 0, 0, 0)), tokspec(DN_HD)],
        out_specs=[tokspec(3 * DN_HD), tokspec(AUG)],
        out_shape=[jax.ShapeDtypeStruct((Tp, 3 * DN_HD), BF16), jax.ShapeDtypeStruct((Tp, AUG), F32)],
        scratch_shapes=[pltpu.VMEM((DN_HEADS, DN_DH, DN_DH), F32)],
        compiler_params=_cp(("arbitrary",)),
    )(qkv, qkv, qkv, bg, states, inverses, dO)


def loss_head(h, tgt, n_tok, name):
    Tp, Dm = h.shape

    def body(h_ref, t_ref, l_ref, dh_ref):
        i = pl.program_id(0)
        row = i * TM + lax.broadcasted_iota(jnp.int32, (TM, Dm), 0)
        e = jnp.where((row >= N_META) & (row < N_META + n_tok), h_ref[...] - t_ref[...], 0.0)
        dh_ref[...] = e * (1.0 / Dm)
        s = jnp.sum(e * e, axis=0, keepdims=True) * (0.5 / Dm)

        @pl.when(i == 0)
        def _():
            l_ref[...] = s

        @pl.when(i > 0)
        def _():
            l_ref[...] += s

    tile = pl.BlockSpec((TM, Dm), lambda i: (i, 0))
    return pl.pallas_call(
        body, name=name, grid=(Tp // TM,), in_specs=[tile, tile],
        out_specs=[pl.BlockSpec((1, Dm), lambda i: (0, 0)), tile],
        out_shape=[jax.ShapeDtypeStruct((1, Dm), F32), jax.ShapeDtypeStruct((Tp, Dm), F32)],
        compiler_params=_cp(("arbitrary",)),
    )(h, tgt)


def _row_block(R):
    return _pick(R, (256, 176, 128, 64, 32, 16, 8))


def sum_slots(parts, name):
    P, R, C = parts.shape
    rb = _row_block(R)

    def body(p_ref, o_ref):
        acc = p_ref[0].astype(F32)
        for s in range(1, P):
            acc = acc + p_ref[s].astype(F32)
        o_ref[...] = acc

    return pl.pallas_call(
        body, name=name, grid=(R // rb,), in_specs=[pl.BlockSpec((P, rb, C), lambda i: (0, i, 0))],
        out_specs=pl.BlockSpec((rb, C), lambda i: (i, 0)), out_shape=jax.ShapeDtypeStruct((R, C), F32),
        compiler_params=_cp(("parallel",)),
    )(parts)


def _adamw_math(w, g, m, v):
    m2 = ADAM_B1 * m + (1.0 - ADAM_B1) * g
    v2 = ADAM_B2 * v + (1.0 - ADAM_B2) * (g * g)
    m_hat = m2 / (1.0 - ADAM_B1 ** ADAM_STEP)
    v_hat = v2 / (1.0 - ADAM_B2 ** ADAM_STEP)
    delta = -ADAM_LR * (m_hat / (jnp.sqrt(v_hat) + ADAM_EPS) + ADAM_WD * w)
    return delta, m2, v2


def adamw_big(w, m, v, g_a, g_b, name):
    R, C = w.shape
    rb = _row_block(R)

    def body(w_ref, m_ref, v_ref, ga_ref, gb_ref, g_ref, d_ref, m2_ref, v2_ref):
        g = ga_ref[...] + gb_ref[...]
        delta, m2, v2 = _adamw_math(w_ref[...], g, m_ref[...], v_ref[...])
        g_ref[...], d_ref[...], m2_ref[...], v2_ref[...] = g, delta, m2, v2

    spec = pl.BlockSpec((rb, C), lambda i: (i, 0))
    return pl.pallas_call(
        body, name=name, grid=(R // rb,), in_specs=[spec] * 5, out_specs=[spec] * 4,
        out_shape=[jax.ShapeDtypeStruct((R, C), F32)] * 4, compiler_params=_cp(("parallel",)),
    )(w, m, v, g_a, g_b)


def adamw_small(ws, ms, vs, gs, name):
    n = len(ws)

    def body(*refs):
        w_r, m_r, v_r, g_r = refs[:n], refs[n:2 * n], refs[2 * n:3 * n], refs[3 * n:4 * n]
        d_o, m_o, v_o = refs[4 * n:5 * n], refs[5 * n:6 * n], refs[6 * n:7 * n]
        for k in range(n):
            delta, m2, v2 = _adamw_math(w_r[k][...], g_r[k][...], m_r[k][...], v_r[k][...])
            d_o[k][...], m_o[k][...], v_o[k][...] = delta, m2, v2

    shapes = [jax.ShapeDtypeStruct(w.shape, F32) for w in ws]
    res = pl.pallas_call(body, name=name, out_shape=shapes * 3, compiler_params=pltpu.CompilerParams(vmem_limit_bytes=VMEM_LIMIT))(
        *ws, *ms, *vs, *gs)
    return res[:n], res[n:2 * n], res[2 * n:]


def sum8(own4, sib4, name):
    _, R, C = own4.shape

    def body(a_ref, b_ref, o_ref):
        a = ((a_ref[0] + a_ref[1]) + a_ref[2]) + a_ref[3]
        b = ((b_ref[0] + b_ref[1]) + b_ref[2]) + b_ref[3]
        o_ref[...] = a + b

    return pl.pallas_call(body, name=name, out_shape=jax.ShapeDtypeStruct((R, C), F32),
                          compiler_params=pltpu.CompilerParams(vmem_limit_bytes=VMEM_LIMIT))(own4, sib4)


def _il(w):
    lead = w.shape[:-1]
    return w.reshape(lead + (2, FFN_DIM // 256, 256)).swapaxes(-3, -2).reshape(lead + (2 * FFN_DIM,))


def _unil(w):
    lead = w.shape[:-1]
    return w.reshape(lead + (FFN_DIM // 256, 2, 256)).swapaxes(-3, -2).reshape(lead + (2 * FFN_DIM,))


def _pad_cols(w, n=AUG):
    return jnp.pad(w, ((0, 0), (0, n - w.shape[1])))


ATT_SPLIT = ATT_HEADS // 2


def attn_fwd(a, a_t, W, tag, cargo=()):
    proj = matmul(a, W["main"], "nn", BF16, f"{tag}_proj", ncols=4 * D_MODEL)
    fl = matmul(a, W["small"], "nn", F32, f"{tag}_fl")
    c = forget_cumsum(fl, W["bias"], f"{tag}_cumsum")
    cfix = tok(c, fixed=True)
    q_aug = rowwise(fn_attn_prep_q, [tok(proj, AUG, 0), cfix], [par(W["qg"])], [(2 * AUG, BF16)], f"{tag}_prepq", ATT_SPLIT, True, wide=True)[0]
    k_aug = rowwise(fn_attn_prep_k, [tok(proj, AUG, ATT_SPLIT), cfix], [par(W["kg"])], [(2 * AUG, BF16)], f"{tag}_prepk", ATT_SPLIT, True, wide=True)[0]
    v_aug = rowwise(fn_attn_prep_v, [tok(proj, AUG, 2 * ATT_SPLIT)], [], [(2 * AUG, BF16)], f"{tag}_prepv", ATT_SPLIT, wide=True)[0]
    o_aug, landed = flash_fwd(q_aug, k_aug, v_aug, f"{tag}_flash", cargo)
    go, go_t = rowwise(fn_attn_gate, [tok(o_aug, 2 * AUG, 0), tok(proj, AUG, 3 * ATT_SPLIT)], [], [(AUG, BF16), (AUG, BF16, 0)],
                       f"{tag}_gate", ATT_SPLIT, wide=True)
    return (a_t, proj, fl, c, q_aug, k_aug, v_aug, o_aug, go_t, go), landed


def attn_out(saved, W, tag):
    return matmul(saved[-1], W["out"], "nn", BF16, f"{tag}_out")


def attn_bwd(dm, saved, W, tag, cargo=None):
    a_t, proj, fl, c, q_aug, k_aug, v_aug, o_aug, go_t, _ = saved
    dgo = matmul(dm, W["out"], "nt", BF16, f"{tag}_dgo")
    d_out = matmul(go_t, dm, "nn", BF16, f"{tag}_dwout")
    do_aug, dog, q_aug2 = rowwise(
        fn_attn_bwd_prep, [tok(dgo, AUG, 0), tok(o_aug, 2 * AUG, 0), tok(proj, AUG, 3 * ATT_SPLIT), tok(q_aug, 2 * AUG, 0)], [],
        [(2 * AUG, BF16), (AUG, BF16), (2 * AUG, BF16)], f"{tag}_bprep", ATT_SPLIT, wide=True)
    (dq_aug, dk_aug, dv_aug), landed = flash_bwd(q_aug2, k_aug, v_aug, do_aug, f"{tag}_flashb", cargo(d_out) if cargo else ())
    dc = rowwise(fn_attn_dc, [tok(dq_aug), tok(dk_aug)], [], [(AUG, F32)], f"{tag}_dc")[0]
    cfix = tok(c, fixed=True)
    (dq,), (d_qg,) = rowwise_bwd(fn_attn_prep_q, [tok(proj, AUG, 0, grad=BF16), cfix], [par(W["qg"])], [dq_aug],
                                 f"{tag}_prepqb", nsplit=ATT_SPLIT, with_j=True, wide=True)
    (dk,), (d_kg,) = rowwise_bwd(fn_attn_prep_k, [tok(proj, AUG, ATT_SPLIT, grad=BF16), cfix], [par(W["kg"])], [dk_aug],
                                 f"{tag}_prepkb", nsplit=ATT_SPLIT, with_j=True, wide=True)
    (dv,), _ = rowwise_bwd(fn_attn_prep_v, [tok(proj, AUG, 2 * ATT_SPLIT, grad=BF16)], [], [dv_aug], f"{tag}_prepvb", nsplit=ATT_SPLIT, wide=True)
    dfl, d_bias = forget_cumsum_bwd(dc, fl, W["bias"], f"{tag}_cumsumb")
    dproj = jnp.concatenate([dq, dk, dv, dog], axis=1)
    da = matmul(dproj, W["main"], "nt", BF16, f"{tag}_da", add=matmul(dfl, W["small"], "nt", F32, f"{tag}_da0"))
    d_in = jnp.concatenate([matmul(a_t, dproj, "nn", BF16, f"{tag}_dw"), matmul(a_t, dfl, "nn", BF16, f"{tag}_dw2")[:, :ATT_HEADS]], axis=1)
    return da, dict(w_in=d_in, w_out=d_out, b_forget=d_bias[0, :ATT_HEADS], q_norm=d_qg[0], k_norm=d_kg[0]), landed


DN_SPLIT = 3 * DN_HD // 256


def dn_fwd(a, a_t, W, tag):
    proj = matmul(a, W["main"], "nn", BF16, f"{tag}_proj", ncols=4 * D_MODEL)
    lg = matmul(a, W["small"], "nn", F32, f"{tag}_lg")
    qkv, y = rowwise(fn_dn_prep_keep, [tok(proj, 256, 0, conv=(W["conv"], 0))], [], [(256, BF16), (256, BF16)], f"{tag}_prep", DN_SPLIT, True, wide=True)
    bg = rowwise(fn_dn_gates, [tok(lg)], [par(W["alog"]), par(W["dtb"])], [(AUG, F32)], f"{tag}_gates")[0]
    O, states, inverses = dn_scan_fwd(qkv, bg, f"{tag}_scan")
    go, go_t = rowwise(fn_dn_post, [tok(O), tok(proj, DN_HD, 3)], [par(W["ogain"])], [(DN_HD, BF16), (DN_HD, BF16, 0)], f"{tag}_post")
    m = matmul(go, W["out"], "nn", BF16, f"{tag}_out")
    return m, (a_t, proj, lg, qkv, y, bg, O, states, inverses, go_t)


def dn_bwd(dm, saved, W, tag):
    a_t, proj, lg, qkv, y, bg, O, states, inverses, go_t = saved
    dgo = matmul(dm, W["out"], "nt", BF16, f"{tag}_dgo")
    d_out = matmul(go_t, dm, "nn", BF16, f"{tag}_dwout")
    (dO, dog), (d_ogain,) = rowwise_bwd(fn_dn_post, [tok(O, grad=BF16), tok(proj, DN_HD, 3, grad=BF16)], [par(W["ogain"])],
                                        [dgo], f"{tag}_postb")
    dqkv_n, dbg = dn_scan_bwd(qkv, bg, states, inverses, dO, f"{tag}_scanb")
    (dy,), _ = rowwise_bwd(fn_dn_prep, [tok(y, 256, 0, grad=BF16)], [], [dqkv_n], f"{tag}_prepb", nsplit=DN_SPLIT, with_j=True, wide=True)
    dqkv, d_conv = conv_bwd(dy, proj, W["conv"], f"{tag}_convb")
    (dlg,), (d_alog, d_dtb) = rowwise_bwd(fn_dn_gates, [tok(lg, grad=BF16)], [par(W["alog"]), par(W["dtb"])], [dbg], f"{tag}_gatesb")
    dproj = jnp.concatenate([dqkv, dog], axis=1)
    da = matmul(dproj, W["main"], "nt", BF16, f"{tag}_da", add=matmul(dlg, W["small"], "nt", F32, f"{tag}_da0"))
    d_in = jnp.concatenate([matmul(a_t, dproj, "nn", BF16, f"{tag}_dw"), matmul(a_t, dlg, "nn", BF16, f"{tag}_dw2")[:, :2 * DN_HEADS]], axis=1)
    return da, dict(w_in=d_in, w_out=d_out, conv=d_conv, a_log=d_alog[0, DN_HEADS:2 * DN_HEADS],
                    dt_bias=d_dtb[0, DN_HEADS:2 * DN_HEADS], o_norm=d_ogain[0])


FFN_SPLIT = FFN_DIM // 256


def ffn_fwd(b, b_t, W, tag):
    u0 = matmul(b, W["up"], "nn", BF16, f"{tag}_up")
    act, u, act_t = rowwise(fn_geglu_keep, [tok(u0, 512, 0, conv=(W["conv"], 0))], [], [(256, BF16), (512, BF16), (256, BF16, 0)],
                            f"{tag}_act", nsplit=FFN_SPLIT, wide=True)
    f = matmul(act, W["down"], "nn", BF16, f"{tag}_down")
    return f, (b_t, u0, u, act_t)


def ffn_bwd(df, saved, W, tag):
    b_t, u0, u, act_t = saved
    dact = matmul(df, W["down"], "nt", BF16, f"{tag}_dact")
    d_down = matmul(act_t, df, "nn", BF16, f"{tag}_dwdown")
    (du,), _ = rowwise_bwd(fn_geglu, [tok(u, 512, 0, grad=BF16)], [], [dact], f"{tag}_actb", nsplit=FFN_SPLIT, wide=True)
    du0, d_conv = conv_bwd(du, u0, W["conv"], f"{tag}_convb")
    db = matmul(du0, W["up"], "nt", BF16, f"{tag}_db")
    d_up = matmul(b_t, du0, "nn", BF16, f"{tag}_dwup")
    return db, dict(up=_unil(d_up), conv=_unil(d_conv), down=d_down)


def _cols_from_slots(g, L):
    K, Cs = g.shape[1] // L, g.shape[2]
    return g.reshape(4, L, K, Cs).transpose(1, 2, 0, 3).reshape(L, K, 4 * Cs)


def _rows_from_slots(g, L):
    Rs, C = g.shape[1] // L, g.shape[2]
    return g.reshape(4, L, Rs, C).transpose(1, 0, 2, 3).reshape(L, 4 * Rs, C)


def _cols_to_slots(w):
    L, K, C = w.shape
    return w.reshape(L, K, 4, C // 4).transpose(2, 0, 1, 3).reshape(4, L * K, C // 4)


def _rows_to_slots(w):
    L, R, C = w.shape
    return w.reshape(L, 4, R // 4, C).transpose(1, 0, 2, 3).reshape(4, L * (R // 4), C)


SMALL_NAMES = ["meta_tokens", "norm_mix_pre", "norm_mix_post", "norm_ffn_pre", "norm_ffn_post", "attn_b_forget", "attn_q_norm",
               "attn_k_norm", "dn_conv", "dn_a_log", "dn_dt_bias", "dn_o_norm", "ffn_conv"]
BIG_NAMES = ["attn_w_in", "attn_w_out", "dn_w_in", "dn_w_out", "ffn_w_up", "ffn_w_down"]
WEIGHT_NAMES = ["meta_tokens", "norm_mix_pre", "norm_mix_post", "norm_ffn_pre", "norm_ffn_post", "attn_w_in", "attn_b_forget",
                "attn_q_norm", "attn_k_norm", "attn_w_out", "dn_w_in", "dn_conv", "dn_a_log", "dn_dt_bias", "dn_o_norm", "dn_w_out",
                "ffn_w_up", "ffn_conv", "ffn_w_down"]
PACK_COLS = 1024


def train_step(x, loss_target, w, m, v):
    L = x.shape[1]
    T = L + N_META
    Tp = -(-T // TM) * TM
    xi, yi = lax.axis_index("x"), lax.axis_index("y")
    slot = 2 * xi + yi
    as2d = lambda t: t.reshape(-1, t.shape[-1])

    bf2d = lambda n: as2d(w[n]).astype(BF16)
    attn_in_shard = w["attn_w_in"].astype(BF16)
    first = exchange_chips(False, [attn_in_shard[0], w["meta_tokens"], as2d(w["dn_conv"]), as2d(w["ffn_conv"])], "gather_first")
    later = [attn_in_shard[1], bf2d("attn_w_out"), bf2d("dn_w_in"), bf2d("dn_w_out"), bf2d("ffn_w_up"), bf2d("ffn_w_down")]
    meta = first[1].transpose(1, 0, 2).reshape(N_META, D_MODEL)
    dn_conv = _cols_from_slots(first[2], 2)
    ffn_conv = _il(_cols_from_slots(first[3], DEPTH))
    lane8 = lambda t: jnp.pad(t[None, :], ((0, 0), (DN_HEADS, AUG - 2 * DN_HEADS)))

    def attn_weights(j, w_in, w_out):
        return dict(main=(w_in, 0), small=_pad_cols(w_in[0, :, 4 * ATT_HD:]), bias=_pad_cols(w["attn_b_forget"][j][None, :]),
                    qg=w["attn_q_norm"][j][None, :], kg=w["attn_k_norm"][j][None, :], out=w_out)

    WA = [attn_weights(0, _cols_from_slots(first[0], 1), None), None]
    WD, WF = None, None
    row = lambda t, i: t[i][None, :]

    h = jnp.concatenate([meta, x[0], jnp.zeros((Tp - T, D_MODEL), F32)], axis=0)
    tgt = jnp.concatenate([jnp.zeros((N_META, D_MODEL), F32), loss_target[0], jnp.zeros((Tp - T, D_MODEL), F32)], axis=0)
    h0 = h
    a, a_t = rowwise(fn_norm, [tok(h)], [par(row(w["norm_mix_pre"], 0))], [(D_MODEL, BF16), (D_MODEL, BF16, 0)], "norm0")
    saved = []
    for i in range(DEPTH):
        j = i // 2
        if i == 0:
            sv_mix, landed = attn_fwd(a, a_t, WA[0], "attn0", later)
            attn_out_w, dn_in, dn_out = _rows_from_slots(landed[1], 2), _cols_from_slots(landed[2], 2), _rows_from_slots(landed[3], 2)
            ffn_up, ffn_down = _il(_cols_from_slots(landed[4], DEPTH)), _rows_from_slots(landed[5], DEPTH)
            WA[0]["out"] = (attn_out_w, 0)
            WA[1] = attn_weights(1, _cols_from_slots(landed[0], 1), (attn_out_w, 1))
            WD = [dict(main=(dn_in, k), small=_pad_cols(dn_in[k, :, 4 * DN_HD:]), conv=dn_conv[k], alog=lane8(w["dn_a_log"][k]),
                       dtb=lane8(w["dn_dt_bias"][k]), ogain=w["dn_o_norm"][k][None, :], out=(dn_out, k)) for k in range(2)]
            WF = [dict(up=(ffn_up, k), conv=ffn_conv[k], down=(ffn_down, k)) for k in range(DEPTH)]
            mix = attn_out(sv_mix, WA[0], "attn0")
        elif i % 2 == 0:
            sv_mix, _ = attn_fwd(a, a_t, WA[j], f"attn{j}")
            mix = attn_out(sv_mix, WA[j], f"attn{j}")
        else:
            mix, sv_mix = dn_fwd(a, a_t, WD[j], f"dn{j}")
        norm_outs = [(D_MODEL, F32), (D_MODEL, BF16), (D_MODEL, BF16, 1)]
        h_mid, b, b_t = rowwise(fn_resid_norm, [tok(h), tok(mix)], [par(row(w["norm_mix_post"], i)), par(row(w["norm_ffn_pre"], i))],
                                norm_outs, f"resid_mix{i}")
        f, sv_ffn = ffn_fwd(b, b_t, WF[i], f"ffn{i}")
        if i < DEPTH - 1:
            h_out, a, a_t = rowwise(fn_resid_norm, [tok(h_mid), tok(f)], [par(row(w["norm_ffn_post"], i)), par(row(w["norm_mix_pre"], i + 1))],
                                    norm_outs, f"resid_ffn{i}")
        else:
            h_out = rowwise(fn_resid, [tok(h_mid), tok(f)], [par(row(w["norm_ffn_post"], i))], [(D_MODEL, F32)], f"resid_ffn{i}")[0]
        saved.append((h, mix, sv_mix, h_mid, f, sv_ffn))
        h = h_out
    lvec, dh = loss_head(h, tgt, L, "loss_head")
    loss = lax.psum(jnp.sum(lvec), ("x", "y", "c"))

    gn = {n: [None] * DEPTH for n in ("norm_mix_pre", "norm_mix_post", "norm_ffn_pre", "norm_ffn_post")}
    g_attn, g_dn, g_ffn = [None, None], [None, None], [None] * DEPTH
    da = None
    for i in reversed(range(DEPTH)):
        j = i // 2
        h_in, mix, sv_mix, h_mid, f, sv_ffn = saved[i]
        if i == DEPTH - 1:
            (dh, df), (gn["norm_ffn_post"][i],) = rowwise_bwd(
                fn_resid, [tok(h_mid, grad=F32), tok(f, grad=BF16)], [par(row(w["norm_ffn_post"], i))], [dh], f"resid_ffn{i}_b")
        else:
            (dh, df), (gn["norm_ffn_post"][i], gn["norm_mix_pre"][i + 1]) = rowwise_bwd(
                fn_resid_norm, [tok(h_mid, grad=F32), tok(f, grad=BF16)],
                [par(row(w["norm_ffn_post"], i)), par(row(w["norm_mix_pre"], i + 1))], [dh, da], f"resid_ffn{i}_b")
        db, g_ffn[i] = ffn_bwd(df, sv_ffn, WF[i], f"ffn{i}")
        (dh, dm), (gn["norm_mix_post"][i], gn["norm_ffn_pre"][i]) = rowwise_bwd(
            fn_resid_norm, [tok(h_in, grad=F32), tok(mix, grad=BF16)],
            [par(row(w["norm_mix_post"], i)), par(row(w["norm_ffn_pre"], i))], [dh, db], f"resid_mix{i}_b")
        if i == 0:
            def riders(d_out0):
                return [_cols_to_slots(g_attn[1]["w_in"][None]), _rows_to_slots(jnp.stack([d_out0, g_attn[1]["w_out"]])),
                        _cols_to_slots(jnp.stack([g["w_in"] for g in g_dn])), _rows_to_slots(jnp.stack([g["w_out"] for g in g_dn])),
                        _cols_to_slots(jnp.stack([g["up"] for g in g_ffn])), _rows_to_slots(jnp.stack([g["down"] for g in g_ffn]))]
            da, g_attn[0], landed_grads = attn_bwd(dm, sv_mix, WA[0], "attn0", riders)
        elif i % 2 == 0:
            da, g_attn[j], _ = attn_bwd(dm, sv_mix, WA[j], f"attn{j}")
        else:
            da, g_dn[j] = dn_bwd(dm, sv_mix, WD[j], f"dn{j}")
    (dh0,), (gn["norm_mix_pre"][0],) = rowwise_bwd(fn_id_norm, [tok(h0, grad=F32)], [par(row(w["norm_mix_pre"], 0))], [dh, da], "norm0_b")
    grad_x = dh0[N_META:T][None]

    full = dict(
        meta_tokens=dh0[:N_META],
        **{n: jnp.concatenate(gn[n], axis=0) for n in gn},
        attn_b_forget=jnp.stack([g["b_forget"] for g in g_attn]), attn_q_norm=jnp.stack([g["q_norm"] for g in g_attn]),
        attn_k_norm=jnp.stack([g["k_norm"] for g in g_attn]),
        dn_conv=jnp.stack([g["conv"] for g in g_dn]), dn_a_log=jnp.stack([g["a_log"] for g in g_dn]),
        dn_dt_bias=jnp.stack([g["dt_bias"] for g in g_dn]), dn_o_norm=jnp.stack([g["o_norm"] for g in g_dn]),
        ffn_conv=jnp.stack([g["conv"] for g in g_ffn]))
    flat = jnp.concatenate([full[n].reshape(-1) for n in SMALL_NAMES])
    rows = -(-flat.shape[0] // PACK_COLS)
    rows = -(-rows // 8) * 8
    pack = jnp.pad(flat, (0, rows * PACK_COLS - flat.shape[0])).reshape(rows, PACK_COLS)
    own4 = exchange_chips(False, [pack], "gather_small_grads")[0]
    sib4 = swap_cores([own4], "swap_small_grads")[0]
    tot = sum8(own4, sib4, "sum_small_grads").reshape(-1)
    grads, off = {}, 0
    for n in SMALL_NAMES:
        size = full[n].size
        g = tot[off:off + size].reshape(full[n].shape)
        off += size
        if g.shape != w[n].shape:
            g = lax.dynamic_slice_in_dim(g, slot * w[n].shape[-1], w[n].shape[-1], axis=g.ndim - 1)
        grads[n] = g
    d_s, m_s, v_s = adamw_small([as2d(w[n]) for n in SMALL_NAMES], [as2d(m[n]) for n in SMALL_NAMES],
                                [as2d(v[n]) for n in SMALL_NAMES], [as2d(grads[n]) for n in SMALL_NAMES], "adamw_small")
    deltas = {n: d.reshape(w[n].shape) for n, d in zip(SMALL_NAMES, d_s, strict=True)}
    new_m = {n: d.reshape(w[n].shape) for n, d in zip(SMALL_NAMES, m_s, strict=True)}
    new_v = {n: d.reshape(w[n].shape) for n, d in zip(SMALL_NAMES, v_s, strict=True)}

    last = exchange_chips(True, [_cols_to_slots(g_attn[0]["w_in"][None])], "scatter_last")[0]
    recv = [jnp.concatenate([last, landed_grads[0]], axis=1)] + list(landed_grads[1:])
    part = [sum_slots(r, f"sum_{n}") for n, r in zip(BIG_NAMES, recv, strict=True)]
    other = swap_cores(part, "swap_grads")
    for n, pa, pb in zip(BIG_NAMES, part, other, strict=True):
        g, d, m2, v2 = adamw_big(as2d(w[n]), as2d(m[n]), as2d(v[n]), pa, pb, f"adamw_{n}")
        grads[n], deltas[n], new_m[n], new_v[n] = (t.reshape(w[n].shape) for t in (g, d, m2, v2))
    return loss, grad_x, grads, deltas, new_m, new_v


def kernel(x, meta_tokens, norm_mix_pre, norm_mix_post, norm_ffn_pre, norm_ffn_post, attn_w_in, attn_b_forget, attn_q_norm, attn_k_norm, attn_w_out, dn_w_in, dn_conv, dn_a_log, dn_dt_bias, dn_o_norm, dn_w_out, ffn_w_up, ffn_conv, ffn_w_down, loss_target, m_meta_tokens, m_norm_mix_pre, m_norm_mix_post, m_norm_ffn_pre, m_norm_ffn_post, m_attn_w_in, m_attn_b_forget, m_attn_q_norm, m_attn_k_norm, m_attn_w_out, m_dn_w_in, m_dn_conv, m_dn_a_log, m_dn_dt_bias, m_dn_o_norm, m_dn_w_out, m_ffn_w_up, m_ffn_conv, m_ffn_w_down, v_meta_tokens, v_norm_mix_pre, v_norm_mix_post, v_norm_ffn_pre, v_norm_ffn_post, v_attn_w_in, v_attn_b_forget, v_attn_q_norm, v_attn_k_norm, v_attn_w_out, v_dn_w_in, v_dn_conv, v_dn_a_log, v_dn_dt_bias, v_dn_o_norm, v_dn_w_out, v_ffn_w_up, v_ffn_conv, v_ffn_w_down):
    w = dict(meta_tokens=meta_tokens, norm_mix_pre=norm_mix_pre, norm_mix_post=norm_mix_post, norm_ffn_pre=norm_ffn_pre, norm_ffn_post=norm_ffn_post, attn_w_in=attn_w_in, attn_b_forget=attn_b_forget, attn_q_norm=attn_q_norm, attn_k_norm=attn_k_norm, attn_w_out=attn_w_out, dn_w_in=dn_w_in, dn_conv=dn_conv, dn_a_log=dn_a_log, dn_dt_bias=dn_dt_bias, dn_o_norm=dn_o_norm, dn_w_out=dn_w_out, ffn_w_up=ffn_w_up, ffn_conv=ffn_conv, ffn_w_down=ffn_w_down)
    m = dict(meta_tokens=m_meta_tokens, norm_mix_pre=m_norm_mix_pre, norm_mix_post=m_norm_mix_post, norm_ffn_pre=m_norm_ffn_pre, norm_ffn_post=m_norm_ffn_post, attn_w_in=m_attn_w_in, attn_b_forget=m_attn_b_forget, attn_q_norm=m_attn_q_norm, attn_k_norm=m_attn_k_norm, attn_w_out=m_attn_w_out, dn_w_in=m_dn_w_in, dn_conv=m_dn_conv, dn_a_log=m_dn_a_log, dn_dt_bias=m_dn_dt_bias, dn_o_norm=m_dn_o_norm, dn_w_out=m_dn_w_out, ffn_w_up=m_ffn_w_up, ffn_conv=m_ffn_conv, ffn_w_down=m_ffn_w_down)
    v = dict(meta_tokens=v_meta_tokens, norm_mix_pre=v_norm_mix_pre, norm_mix_post=v_norm_mix_post, norm_ffn_pre=v_norm_ffn_pre, norm_ffn_post=v_norm_ffn_post, attn_w_in=v_attn_w_in, attn_b_forget=v_attn_b_forget, attn_q_norm=v_attn_q_norm, attn_k_norm=v_attn_k_norm, attn_w_out=v_attn_w_out, dn_w_in=v_dn_w_in, dn_conv=v_dn_conv, dn_a_log=v_dn_a_log, dn_dt_bias=v_dn_dt_bias, dn_o_norm=v_dn_o_norm, dn_w_out=v_dn_w_out, ffn_w_up=v_ffn_w_up, ffn_conv=v_ffn_conv, ffn_w_down=v_ffn_w_down)
    loss, grad_x, grads, deltas, new_m, new_v = train_step(x, loss_target, w, m, v)
    return (loss, grad_x, *[grads[n] for n in WEIGHT_NAMES], *[deltas[n] for n in WEIGHT_NAMES],
            *[new_m[n] for n in WEIGHT_NAMES], *[new_v[n] for n in WEIGHT_NAMES])
```
